```python
import jax
import jax.numpy as jnp
from jax import lax
import numpy as np

D_MODEL = 1024
BATCH = 8
SEQ = 8192
DEPTH = 2

N_META = 16
NORM_EPS = 1e-6
SSM_D_INNER = 2 * D_MODEL
SSM_HEAD_DIM = 64
SSM_HEADS = SSM_D_INNER // SSM_HEAD_DIM
SSM_GROUPS = 4
SSM_HEADS_PER_GROUP = SSM_HEADS // SSM_GROUPS
SSM_STATE = 128
SSM_CONV = 4
SSM_CHUNK = 256
SSM_CONV_DIM = SSM_D_INNER + 2 * SSM_GROUPS * SSM_STATE
SSM_IN_DIM = SSM_D_INNER + SSM_CONV_DIM + SSM_HEADS
SB_HEAD_DIM = 64
SB_HEADS = D_MODEL // SB_HEAD_DIM
SB_WIDTH = SB_HEADS * SB_HEAD_DIM
SB_Q_BLOCK = 128
D_FF = 256 * ((8 * D_MODEL // 3 + 255) // 256)
FFN_CONV = 3

kernel_name = 'hybrid_ssd_stickbreaking_yoco'


def _rmsnorm(x, g):
    x32 = x.astype(jnp.float32)
    y = x32 * lax.rsqrt(jnp.mean(x32 * x32, axis=-1, keepdims=True) + NORM_EPS)
    return (y * g.astype(jnp.float32)).astype(x.dtype)


def _causal_dwconv(x, w, bias):
    width = w.shape[0]
    L = x.shape[1]
    xp = jnp.pad(x, ((0, 0), (width - 1, 0), (0, 0)))
    y = xp[:, 0:L] * w[0] + bias
    for k in range(1, width):
        y = y + xp[:, k:k + L] * w[k]
    return y


def _ssd_mixer(u, w_in, conv_w, conv_b, dt_bias, a_log, d_skip, gate_g, w_out):
    b, L, _ = u.shape
    G, E, P, N, Q = SSM_GROUPS, SSM_HEADS_PER_GROUP, SSM_HEAD_DIM, SSM_STATE, SSM_CHUNK
    f32 = jnp.float32
    z, xbc, dt_raw = jnp.split(u @ w_in, [SSM_D_INNER, SSM_D_INNER + SSM_CONV_DIM], axis=-1)
    xbc = jax.nn.silu(_causal_dwconv(xbc, conv_w, conv_b))
    xs, b_in, c_in = jnp.split(xbc, [SSM_D_INNER, SSM_D_INNER + G * N], axis=-1)
    dt = jax.nn.softplus(dt_raw.astype(f32) + dt_bias.astype(f32))
    a = -jnp.exp(a_log.astype(f32))
    pf = (-N_META) % Q
    pe = (-(pf + L)) % Q
    nc = (pf + L + pe) // Q

    def to_chunks(t, tail):
        return jnp.pad(t, ((0, 0), (pf, pe), (0, 0))).reshape((b, nc, Q) + tail)

    x_c = to_chunks(xs, (G, E, P)).astype(f32)
    b_c = to_chunks(b_in, (G, N)).astype(f32)
    c_c = to_chunks(c_in, (G, N)).astype(f32)
    dt_c = to_chunks(dt, (G, E))
    xdt = x_c * dt_c[..., None]
    a_cs = jnp.cumsum(jnp.transpose(dt_c * a.reshape(G, E), (0, 3, 4, 1, 2)), axis=-1)
    causal = jnp.tril(jnp.ones((Q, Q), dtype=bool))
    decay_in = jnp.exp(jnp.where(causal, a_cs[..., :, None] - a_cs[..., None, :], -jnp.inf))
    cb = jnp.einsum('bclgn,bcsgn->bgcls', c_c, b_c)
    y_diag = jnp.einsum('bgcls,bgecls,bcsgep->bclgep', cb, decay_in, xdt)
    decay_to_end = jnp.exp(a_cs[..., -1:] - a_cs)
    chunk_states = jnp.einsum('bclgn,bgecl,bclgep->cbgepn', b_c, decay_to_end, xdt)
    chunk_decay = jnp.moveaxis(jnp.exp(a_cs[..., -1]), -1, 0)

    def step(state, inp):
        s_new, d = inp
        return state * d[..., None, None] + s_new, state

    _, prev_states = lax.scan(step, jnp.zeros((b, G, E, P, N), f32), (chunk_states, chunk_decay))
    y_off = jnp.einsum('bclgn,cbgepn,bgecl->bclgep', c_c, prev_states, jnp.exp(a_cs))
    y = (y_diag + y_off).reshape(b, nc * Q, SSM_D_INNER)[:, pf:pf + L]
    y = y + (xs.reshape(b, L, SSM_HEADS, P).astype(f32) * d_skip.astype(f32)[:, None]).reshape(b, L, SSM_D_INNER)
    hg = (y * jax.nn.silu(z.astype(f32))).reshape(b, L, G, SSM_D_INNER // G)
    hg = hg * lax.rsqrt(jnp.mean(hg * hg, axis=-1, keepdims=True) + NORM_EPS)
    hg = hg.reshape(b, L, SSM_D_INNER) * gate_g.astype(f32)
    return hg.astype(u.dtype) @ w_out


def _stick_breaking_attention(q, k, v):
    b, L, H, Dh = q.shape
    lp = -(-L // SB_Q_BLOCK) * SB_Q_BLOCK
    pad = ((0, 0), (0, lp - L), (0, 0), (0, 0))
    q, k, v = jnp.pad(q, pad), jnp.pad(k, pad), jnp.pad(v, pad)
    scale = Dh ** -0.5
    outs = []
    for i in range(lp // SB_Q_BLOCK):
        t0, t1 = i * SB_Q_BLOCK, (i + 1) * SB_Q_BLOCK
        logits = jnp.einsum('bthd,bshd->bhts', q[:, t0:t1], k[:, :t1]).astype(jnp.float32) * scale
        t_idx = t0 + jnp.arange(SB_Q_BLOCK)[:, None]
        s_idx = jnp.arange(t1)[None, :]
        visible = s_idx < t_idx
        log_keep = jnp.where(visible, jax.nn.log_sigmoid(-logits), 0.0)
        later = lax.cumsum(log_keep, axis=3, reverse=True) - log_keep
        log_w = jnp.where(visible, jax.nn.log_sigmoid(logits) + later, -jnp.inf)
        w = jnp.exp(log_w).astype(v.dtype)
        outs.append(jnp.einsum('bhts,bshd->bthd', w, v[:, :t1]))
    return jnp.concatenate(outs, axis=1)[:, :L]


def _conv_ffn(u, w_up, conv_w, conv_b, w_down):
    h = _causal_dwconv(u @ w_up, conv_w, conv_b)
    g, val = jnp.split(h, 2, axis=-1)
    return (jax.nn.silu(g) * val) @ w_down


def _fwd_setup_inputs(seed: int = 0) -> dict:
    key = jax.random.key(seed)
    ks = jax.random.split(key, 32)
    n_a = DEPTH // 2
    n_b = DEPTH - n_a
    f32 = jnp.float32

    def nrm(k, shape, scale):
        return jax.random.normal(k, shape, f32) * scale

    def gain(k, shape):
        return 1.0 + 0.02 * jax.random.normal(k, shape, f32)

    dt0 = jnp.exp(jax.random.uniform(ks[6], (n_a, SSM_HEADS), f32, np.log(1e-3), np.log(1e-1)))
    dt_bias = dt0 + jnp.log(-jnp.expm1(-dt0))
    return {
        'x': jax.random.normal(ks[0], (BATCH, SEQ, D_MODEL), f32),
        'meta_tokens': nrm(ks[1], (N_META, D_MODEL), 1.0),
        'ssd_norm': gain(ks[2], (n_a, D_MODEL)),
        'ssd_w_in': nrm(ks[3], (n_a, D_MODEL, SSM_IN_DIM), D_MODEL ** -0.5),
        'ssd_conv_w': nrm(ks[4], (n_a, SSM_CONV, SSM_CONV_DIM), SSM_CONV ** -0.5),
        'ssd_conv_b': nrm(ks[5], (n_a, SSM_CONV_DIM), 0.02),
        'ssd_dt_bias': dt_bias,
        'ssd_a_log': jnp.log(jax.random.uniform(ks[7], (n_a, SSM_HEADS), f32, 1.0, 16.0)),
        'ssd_d_skip': jax.random.uniform(ks[8], (n_a, SSM_HEADS), f32, 0.5, 1.5),
        'ssd_gate_norm': gain(ks[9], (n_a, SSM_D_INNER)),
        'ssd_w_out': nrm(ks[10], (n_a, SSM_D_INNER, D_MODEL), SSM_D_INNER ** -0.5),
        'kv_norm': gain(ks[11], (D_MODEL,)),
        'w_kv': nrm(ks[12], (D_MODEL, 2 * SB_WIDTH), D_MODEL ** -0.5),
        'sb_norm': gain(ks[13], (n_b, D_MODEL)),
        'sb_w_q': nrm(ks[14], (n_b, D_MODEL, SB_WIDTH), D_MODEL ** -0.5),
        'sb_w_o': nrm(ks[15], (n_b, SB_WIDTH, D_MODEL), SB_WIDTH ** -0.5),
        'ffn_norm': gain(ks[16], (DEPTH, D_MODEL)),
        'ffn_w_up': nrm(ks[17], (DEPTH, D_MODEL, 2 * D_FF), D_MODEL ** -0.5),
        'ffn_conv_w': nrm(ks[18], (DEPTH, FFN_CONV, 2 * D_FF), FFN_CONV ** -0.5),
        'ffn_conv_b': nrm(ks[19], (DEPTH, 2 * D_FF), 0.02),
        'ffn_w_down': nrm(ks[20], (DEPTH, D_FF, D_MODEL), D_FF ** -0.5),
        'final_norm': gain(ks[21], (D_MODEL,)),
    }


def _fwd_reference(x, meta_tokens, ssd_norm, ssd_w_in, ssd_conv_w, ssd_conv_b, ssd_dt_bias, ssd_a_log,
              ssd_d_skip, ssd_gate_norm, ssd_w_out, kv_norm, w_kv, sb_norm, sb_w_q, sb_w_o,
              ffn_norm, ffn_w_up, ffn_conv_w, ffn_conv_b, ffn_w_down, final_norm):
    b = x.shape[0]
    n_a = DEPTH // 2
    h = jnp.concatenate([jnp.broadcast_to(meta_tokens[None], (b, N_META, D_MODEL)).astype(x.dtype), x], axis=1)
    L = h.shape[1]
    k_shared = None
    v_shared = None
    for layer in range(DEPTH):
        if layer < n_a:
            h = h + _ssd_mixer(_rmsnorm(h, ssd_norm[layer]), ssd_w_in[layer], ssd_conv_w[layer],
                               ssd_conv_b[layer], ssd_dt_bias[layer], ssd_a_log[layer],
                               ssd_d_skip[layer], ssd_gate_norm[layer], ssd_w_out[layer])
        else:
            if layer == n_a:
                kv = _rmsnorm(h, kv_norm) @ w_kv
                k_shared, v_shared = jnp.split(kv.reshape(b, L, 2, SB_HEADS, SB_HEAD_DIM), 2, axis=2)
                k_shared, v_shared = k_shared[:, :, 0], v_shared[:, :, 0]
            j = layer - n_a
            q = (_rmsnorm(h, sb_norm[j]) @ sb_w_q[j]).reshape(b, L, SB_HEADS, SB_HEAD_DIM)
            o = _stick_breaking_attention(q, k_shared, v_shared).reshape(b, L, SB_WIDTH)
            h = h + o @ sb_w_o[j]
        h = h + _conv_ffn(_rmsnorm(h, ffn_norm[layer]), ffn_w_up[layer], ffn_conv_w[layer],
                          ffn_conv_b[layer], ffn_w_down[layer])
    return _rmsnorm(h, final_norm)[:, N_META:]


import jax as _jax
import jax.numpy as _jnp

TWIN_FORMAT = 'train_step'
FWD_PARAMS = ['x', 'meta_tokens', 'ssd_norm', 'ssd_w_in', 'ssd_conv_w', 'ssd_conv_b', 'ssd_dt_bias', 'ssd_a_log', 'ssd_d_skip', 'ssd_gate_norm', 'ssd_w_out', 'kv_norm', 'w_kv', 'sb_norm', 'sb_w_q', 'sb_w_o', 'ffn_norm', 'ffn_w_up', 'ffn_conv_w', 'ffn_conv_b', 'ffn_w_down', 'final_norm']
TWIN_WEIGHTS = ['meta_tokens', 'ssd_norm', 'ssd_w_in', 'ssd_conv_w', 'ssd_conv_b', 'ssd_dt_bias', 'ssd_a_log', 'ssd_d_skip', 'ssd_gate_norm', 'ssd_w_out', 'kv_norm', 'w_kv', 'sb_norm', 'sb_w_q', 'sb_w_o', 'ffn_norm', 'ffn_w_up', 'ffn_conv_w', 'ffn_conv_b', 'ffn_w_down', 'final_norm']
TWIN_DIFF_INPUT = 'x'
TWIN_INPUTS = ['x', 'meta_tokens', 'ssd_norm', 'ssd_w_in', 'ssd_conv_w', 'ssd_conv_b', 'ssd_dt_bias', 'ssd_a_log', 'ssd_d_skip', 'ssd_gate_norm', 'ssd_w_out', 'kv_norm', 'w_kv', 'sb_norm', 'sb_w_q', 'sb_w_o', 'ffn_norm', 'ffn_w_up', 'ffn_conv_w', 'ffn_conv_b', 'ffn_w_down', 'final_norm', 'loss_target', 'm_meta_tokens', 'm_ssd_norm', 'm_ssd_w_in', 'm_ssd_conv_w', 'm_ssd_conv_b', 'm_ssd_dt_bias', 'm_ssd_a_log', 'm_ssd_d_skip', 'm_ssd_gate_norm', 'm_ssd_w_out', 'm_kv_norm', 'm_w_kv', 'm_sb_norm', 'm_sb_w_q', 'm_sb_w_o', 'm_ffn_norm', 'm_ffn_w_up', 'm_ffn_conv_w', 'm_ffn_conv_b', 'm_ffn_w_down', 'm_final_norm', 'v_meta_tokens', 'v_ssd_norm', 'v_ssd_w_in', 'v_ssd_conv_w', 'v_ssd_conv_b', 'v_ssd_dt_bias', 'v_ssd_a_log', 'v_ssd_d_skip', 'v_ssd_gate_norm', 'v_ssd_w_out', 'v_kv_norm', 'v_w_kv', 'v_sb_norm', 'v_sb_w_q', 'v_sb_w_o', 'v_ffn_norm', 'v_ffn_w_up', 'v_ffn_conv_w', 'v_ffn_conv_b', 'v_ffn_w_down', 'v_final_norm']
TWIN_OUTPUTS = ['loss', 'grad_x', 'grad_meta_tokens', 'grad_ssd_norm', 'grad_ssd_w_in', 'grad_ssd_conv_w', 'grad_ssd_conv_b', 'grad_ssd_dt_bias', 'grad_ssd_a_log', 'grad_ssd_d_skip', 'grad_ssd_gate_norm', 'grad_ssd_w_out', 'grad_kv_norm', 'grad_w_kv', 'grad_sb_norm', 'grad_sb_w_q', 'grad_sb_w_o', 'grad_ffn_norm', 'grad_ffn_w_up', 'grad_ffn_conv_w', 'grad_ffn_conv_b', 'grad_ffn_w_down', 'grad_final_norm', 'delta_meta_tokens', 'delta_ssd_norm', 'delta_ssd_w_in', 'delta_ssd_conv_w', 'delta_ssd_conv_b', 'delta_ssd_dt_bias', 'delta_ssd_a_log', 'delta_ssd_d_skip', 'delta_ssd_gate_norm', 'delta_ssd_w_out', 'delta_kv_norm', 'delta_w_kv', 'delta_sb_norm', 'delta_sb_w_q', 'delta_sb_w_o', 'delta_ffn_norm', 'delta_ffn_w_up', 'delta_ffn_conv_w', 'delta_ffn_conv_b', 'delta_ffn_w_down', 'delta_final_norm', 'new_m_meta_tokens', 'new_m_ssd_norm', 'new_m_ssd_w_in', 'new_m_ssd_conv_w', 'new_m_ssd_conv_b', 'new_m_ssd_dt_bias', 'new_m_ssd_a_log', 'new_m_ssd_d_skip', 'new_m_ssd_gate_norm', 'new_m_ssd_w_out', 'new_m_kv_norm', 'new_m_w_kv', 'new_m_sb_norm', 'new_m_sb_w_q', 'new_m_sb_w_o', 'new_m_ffn_norm', 'new_m_ffn_w_up', 'new_m_ffn_conv_w', 'new_m_ffn_conv_b', 'new_m_ffn_w_down', 'new_m_final_norm', 'new_v_meta_tokens', 'new_v_ssd_norm', 'new_v_ssd_w_in', 'new_v_ssd_conv_w', 'new_v_ssd_conv_b', 'new_v_ssd_dt_bias', 'new_v_ssd_a_log', 'new_v_ssd_d_skip', 'new_v_ssd_gate_norm', 'new_v_ssd_w_out', 'new_v_kv_norm', 'new_v_w_kv', 'new_v_sb_norm', 'new_v_sb_w_q', 'new_v_sb_w_o', 'new_v_ffn_norm', 'new_v_ffn_w_up', 'new_v_ffn_conv_w', 'new_v_ffn_conv_b', 'new_v_ffn_w_down', 'new_v_final_norm']
TWIN_LEAF_KINDS = {'loss': 'loss', 'grad_x': 'grad_x', 'grad_meta_tokens': 'grad_w', 'grad_ssd_norm': 'grad_w', 'grad_ssd_w_in': 'grad_w', 'grad_ssd_conv_w': 'grad_w', 'grad_ssd_conv_b': 'grad_w', 'grad_ssd_dt_bias': 'grad_w', 'grad_ssd_a_log': 'grad_w', 'grad_ssd_d_skip': 'grad_w', 'grad_ssd_gate_norm': 'grad_w', 'grad_ssd_w_out': 'grad_w', 'grad_kv_norm': 'grad_w', 'grad_w_kv': 'grad_w', 'grad_sb_norm': 'grad_w', 'grad_sb_w_q': 'grad_w', 'grad_sb_w_o': 'grad_w', 'grad_ffn_norm': 'grad_w', 'grad_ffn_w_up': 'grad_w', 'grad_ffn_conv_w': 'grad_w', 'grad_ffn_conv_b': 'grad_w', 'grad_ffn_w_down': 'grad_w', 'grad_final_norm': 'grad_w', 'delta_meta_tokens': 'delta_w', 'delta_ssd_norm': 'delta_w', 'delta_ssd_w_in': 'delta_w', 'delta_ssd_conv_w': 'delta_w', 'delta_ssd_conv_b': 'delta_w', 'delta_ssd_dt_bias': 'delta_w', 'delta_ssd_a_log': 'delta_w', 'delta_ssd_d_skip': 'delta_w', 'delta_ssd_gate_norm': 'delta_w', 'delta_ssd_w_out': 'delta_w', 'delta_kv_norm': 'delta_w', 'delta_w_kv': 'delta_w', 'delta_sb_norm': 'delta_w', 'delta_sb_w_q': 'delta_w', 'delta_sb_w_o': 'delta_w', 'delta_ffn_norm': 'delta_w', 'delta_ffn_w_up': 'delta_w', 'delta_ffn_conv_w': 'delta_w', 'delta_ffn_conv_b': 'delta_w', 'delta_ffn_w_down': 'delta_w', 'delta_final_norm': 'delta_w', 'new_m_meta_tokens': 'new_m', 'new_m_ssd_norm': 'new_m', 'new_m_ssd_w_in': 'new_m', 'new_m_ssd_conv_w': 'new_m', 'new_m_ssd_conv_b': 'new_m', 'new_m_ssd_dt_bias': 'new_m', 'new_m_ssd_a_log': 'new_m', 'new_m_ssd_d_skip': 'new_m', 'new_m_ssd_gate_norm': 'new_m', 'new_m_ssd_w_out': 'new_m', 'new_m_kv_norm': 'new_m', 'new_m_w_kv': 'new_m', 'new_m_sb_norm': 'new_m', 'new_m_sb_w_q': 'new_m', 'new_m_sb_w_o': 'new_m', 'new_m_ffn_norm': 'new_m', 'new_m_ffn_w_up': 'new_m', 'new_m_ffn_conv_w': 'new_m', 'new_m_ffn_conv_b': 'new_m', 'new_m_ffn_w_down': 'new_m', 'new_m_final_norm': 'new_m', 'new_v_meta_tokens': 'new_v', 'new_v_ssd_norm': 'new_v', 'new_v_ssd_w_in': 'new_v', 'new_v_ssd_conv_w': 'new_v', 'new_v_ssd_conv_b': 'new_v', 'new_v_ssd_dt_bias': 'new_v', 'new_v_ssd_a_log': 'new_v', 'new_v_ssd_d_skip': 'new_v', 'new_v_ssd_gate_norm': 'new_v', 'new_v_ssd_w_out': 'new_v', 'new_v_kv_norm': 'new_v', 'new_v_w_kv': 'new_v', 'new_v_sb_norm': 'new_v', 'new_v_sb_w_q': 'new_v', 'new_v_sb_w_o': 'new_v', 'new_v_ffn_norm': 'new_v', 'new_v_ffn_w_up': 'new_v', 'new_v_ffn_conv_w': 'new_v', 'new_v_ffn_conv_b': 'new_v', 'new_v_ffn_w_down': 'new_v', 'new_v_final_norm': 'new_v'}


def _forward(args):
    return _fwd_reference(*[args[k] for k in FWD_PARAMS])


def _output_shape():
    def fwd():
        inp = _fwd_setup_inputs(0)
        return _fwd_reference(*[inp[k] for k in FWD_PARAMS])
    out = _jax.eval_shape(fwd)
    return out.shape, out.dtype

N_MICROBATCH = 1
ADAM_LR = 0.001
ADAM_B1 = 0.9
ADAM_B2 = 0.999
ADAM_EPS = 1e-08
ADAM_WD = 0.01
ADAM_STEP = 10
PER_EXAMPLE_BATCH_AXIS = {'x': 0, 'loss_target': 0}
SHARED_INPUTS = []
_WEIGHT_DTYPES = {'meta_tokens': _jnp.float32, 'ssd_norm': _jnp.float32, 'ssd_w_in': _jnp.float32, 'ssd_conv_w': _jnp.float32, 'ssd_conv_b': _jnp.float32, 'ssd_dt_bias': _jnp.float32, 'ssd_a_log': _jnp.float32, 'ssd_d_skip': _jnp.float32, 'ssd_gate_norm': _jnp.float32, 'ssd_w_out': _jnp.float32, 'kv_norm': _jnp.float32, 'w_kv': _jnp.float32, 'sb_norm': _jnp.float32, 'sb_w_q': _jnp.float32, 'sb_w_o': _jnp.float32, 'ffn_norm': _jnp.float32, 'ffn_w_up': _jnp.float32, 'ffn_conv_w': _jnp.float32, 'ffn_conv_b': _jnp.float32, 'ffn_w_down': _jnp.float32, 'final_norm': _jnp.float32}
MOMENT_SCALE = {'meta_tokens': 8.060319e-03, 'ssd_norm': 3.143738e-01, 'ssd_w_in': 1.399858e-01, 'ssd_conv_w': 1.273595e-01, 'ssd_conv_b': 1.769779e-01, 'ssd_dt_bias': 2.416062e-01, 'ssd_a_log': 4.560534e-01, 'ssd_d_skip': 7.194307e-01, 'ssd_gate_norm': 1.504880e-01, 'ssd_w_out': 2.128212e-01, 'kv_norm': 1.221080e-01, 'w_kv': 8.127627e-02, 'sb_norm': 5.247747e-02, 'sb_w_q': 4.729015e-02, 'sb_w_o': 1.052215e-01, 'ffn_norm': 1.433646e-01, 'ffn_w_up': 5.928158e-02, 'ffn_conv_w': 6.150111e-02, 'ffn_conv_b': 5.985390e-02, 'ffn_w_down': 9.678046e-02, 'final_norm': 6.410568e+01}


def _to_microbatches(a, axis):
    t = _jnp.moveaxis(a, axis, 0)
    t = t.reshape((N_MICROBATCH, t.shape[0] // N_MICROBATCH) + t.shape[1:])
    return _jnp.moveaxis(t, 1, axis + 1)


def setup_inputs(seed: int = 0) -> dict:
    inp = _fwd_setup_inputs(seed)
    key = _jax.random.fold_in(_jax.random.key(seed), 7919)
    shape, _ = _output_shape()
    out = dict(inp)
    out["loss_target"] = _jax.random.normal(_jax.random.fold_in(key, 0), shape, _jnp.float32)
    for i, name in enumerate(TWIN_WEIGHTS):
        w = inp[name].astype(_jnp.float32)
        if MOMENT_SCALE is None:
            s = _jnp.sqrt(_jnp.mean(_jnp.square(w)) + 1e-30)
        else:
            s = MOMENT_SCALE[name]
        km, kv = _jax.random.split(_jax.random.fold_in(key, i + 1))
        out[name] = w
        out["m_" + name] = s * _jax.random.normal(km, w.shape, _jnp.float32)
        out["v_" + name] = (s * s) * _jax.random.uniform(kv, w.shape, _jnp.float32, 0.5, 1.5)
    if N_MICROBATCH > 1:
        for name, axis in PER_EXAMPLE_BATCH_AXIS.items():
            out[name] = _to_microbatches(out[name], axis)
    return {'x': out['x'], 'meta_tokens': out['meta_tokens'], 'ssd_norm': out['ssd_norm'], 'ssd_w_in': out['ssd_w_in'], 'ssd_conv_w': out['ssd_conv_w'], 'ssd_conv_b': out['ssd_conv_b'], 'ssd_dt_bias': out['ssd_dt_bias'], 'ssd_a_log': out['ssd_a_log'], 'ssd_d_skip': out['ssd_d_skip'], 'ssd_gate_norm': out['ssd_gate_norm'], 'ssd_w_out': out['ssd_w_out'], 'kv_norm': out['kv_norm'], 'w_kv': out['w_kv'], 'sb_norm': out['sb_norm'], 'sb_w_q': out['sb_w_q'], 'sb_w_o': out['sb_w_o'], 'ffn_norm': out['ffn_norm'], 'ffn_w_up': out['ffn_w_up'], 'ffn_conv_w': out['ffn_conv_w'], 'ffn_conv_b': out['ffn_conv_b'], 'ffn_w_down': out['ffn_w_down'], 'final_norm': out['final_norm'], 'loss_target': out['loss_target'], 'm_meta_tokens': out['m_meta_tokens'], 'm_ssd_norm': out['m_ssd_norm'], 'm_ssd_w_in': out['m_ssd_w_in'], 'm_ssd_conv_w': out['m_ssd_conv_w'], 'm_ssd_conv_b': out['m_ssd_conv_b'], 'm_ssd_dt_bias': out['m_ssd_dt_bias'], 'm_ssd_a_log': out['m_ssd_a_log'], 'm_ssd_d_skip': out['m_ssd_d_skip'], 'm_ssd_gate_norm': out['m_ssd_gate_norm'], 'm_ssd_w_out': out['m_ssd_w_out'], 'm_kv_norm': out['m_kv_norm'], 'm_w_kv': out['m_w_kv'], 'm_sb_norm': out['m_sb_norm'], 'm_sb_w_q': out['m_sb_w_q'], 'm_sb_w_o': out['m_sb_w_o'], 'm_ffn_norm': out['m_ffn_norm'], 'm_ffn_w_up': out['m_ffn_w_up'], 'm_ffn_conv_w': out['m_ffn_conv_w'], 'm_ffn_conv_b': out['m_ffn_conv_b'], 'm_ffn_w_down': out['m_ffn_w_down'], 'm_final_norm': out['m_final_norm'], 'v_meta_tokens': out['v_meta_tokens'], 'v_ssd_norm': out['v_ssd_norm'], 'v_ssd_w_in': out['v_ssd_w_in'], 'v_ssd_conv_w': out['v_ssd_conv_w'], 'v_ssd_conv_b': out['v_ssd_conv_b'], 'v_ssd_dt_bias': out['v_ssd_dt_bias'], 'v_ssd_a_log': out['v_ssd_a_log'], 'v_ssd_d_skip': out['v_ssd_d_skip'], 'v_ssd_gate_norm': out['v_ssd_gate_norm'], 'v_ssd_w_out': out['v_ssd_w_out'], 'v_kv_norm': out['v_kv_norm'], 'v_w_kv': out['v_w_kv'], 'v_sb_norm': out['v_sb_norm'], 'v_sb_w_q': out['v_sb_w_q'], 'v_sb_w_o': out['v_sb_w_o'], 'v_ffn_norm': out['v_ffn_norm'], 'v_ffn_w_up': out['v_ffn_w_up'], 'v_ffn_conv_w': out['v_ffn_conv_w'], 'v_ffn_conv_b': out['v_ffn_conv_b'], 'v_ffn_w_down': out['v_ffn_w_down'], 'v_final_norm': out['v_final_norm']}


def _loss(weights, diff, rest, loss_target):
    with _jax.named_scope("forward"):
        args = {**rest, TWIN_DIFF_INPUT: diff, **{k: w.astype(_WEIGHT_DTYPES[k]) for k, w in weights.items()}}
        y = _forward(args)
    with _jax.named_scope("loss_head"):
        err = _jnp.square(y.astype(_jnp.float32) - loss_target)
        return 0.5 * _jnp.sum(_jnp.mean(err, axis=-1)) if err.ndim else 0.5 * err


def _adamw(w, g, m, v):
    m = ADAM_B1 * m + (1.0 - ADAM_B1) * g
    v = ADAM_B2 * v + (1.0 - ADAM_B2) * _jnp.square(g)
    m_hat = m / (1.0 - ADAM_B1 ** ADAM_STEP)
    v_hat = v / (1.0 - ADAM_B2 ** ADAM_STEP)
    delta = -ADAM_LR * (m_hat / (_jnp.sqrt(v_hat) + ADAM_EPS) + ADAM_WD * w)
    return delta, m, v


def reference(x, meta_tokens, ssd_norm, ssd_w_in, ssd_conv_w, ssd_conv_b, ssd_dt_bias, ssd_a_log, ssd_d_skip, ssd_gate_norm, ssd_w_out, kv_norm, w_kv, sb_norm, sb_w_q, sb_w_o, ffn_norm, ffn_w_up, ffn_conv_w, ffn_conv_b, ffn_w_down, final_norm, loss_target, m_meta_tokens, m_ssd_norm, m_ssd_w_in, m_ssd_conv_w, m_ssd_conv_b, m_ssd_dt_bias, m_ssd_a_log, m_ssd_d_skip, m_ssd_gate_norm, m_ssd_w_out, m_kv_norm, m_w_kv, m_sb_norm, m_sb_w_q, m_sb_w_o, m_ffn_norm, m_ffn_w_up, m_ffn_conv_w, m_ffn_conv_b, m_ffn_w_down, m_final_norm, v_meta_tokens, v_ssd_norm, v_ssd_w_in, v_ssd_conv_w, v_ssd_conv_b, v_ssd_dt_bias, v_ssd_a_log, v_ssd_d_skip, v_ssd_gate_norm, v_ssd_w_out, v_kv_norm, v_w_kv, v_sb_norm, v_sb_w_q, v_sb_w_o, v_ffn_norm, v_ffn_w_up, v_ffn_conv_w, v_ffn_conv_b, v_ffn_w_down, v_final_norm):
    given = dict(x=x, meta_tokens=meta_tokens, ssd_norm=ssd_norm, ssd_w_in=ssd_w_in, ssd_conv_w=ssd_conv_w, ssd_conv_b=ssd_conv_b, ssd_dt_bias=ssd_dt_bias, ssd_a_log=ssd_a_log, ssd_d_skip=ssd_d_skip, ssd_gate_norm=ssd_gate_norm, ssd_w_out=ssd_w_out, kv_norm=kv_norm, w_kv=w_kv, sb_norm=sb_norm, sb_w_q=sb_w_q, sb_w_o=sb_w_o, ffn_norm=ffn_norm, ffn_w_up=ffn_w_up, ffn_conv_w=ffn_conv_w, ffn_conv_b=ffn_conv_b, ffn_w_down=ffn_w_down, final_norm=final_norm, loss_target=loss_target, m_meta_tokens=m_meta_tokens, m_ssd_norm=m_ssd_norm, m_ssd_w_in=m_ssd_w_in, m_ssd_conv_w=m_ssd_conv_w, m_ssd_conv_b=m_ssd_conv_b, m_ssd_dt_bias=m_ssd_dt_bias, m_ssd_a_log=m_ssd_a_log, m_ssd_d_skip=m_ssd_d_skip, m_ssd_gate_norm=m_ssd_gate_norm, m_ssd_w_out=m_ssd_w_out, m_kv_norm=m_kv_norm, m_w_kv=m_w_kv, m_sb_norm=m_sb_norm, m_sb_w_q=m_sb_w_q, m_sb_w_o=m_sb_w_o, m_ffn_norm=m_ffn_norm, m_ffn_w_up=m_ffn_w_up, m_ffn_conv_w=m_ffn_conv_w, m_ffn_conv_b=m_ffn_conv_b, m_ffn_w_down=m_ffn_w_down, m_final_norm=m_final_norm, v_meta_tokens=v_meta_tokens, v_ssd_norm=v_ssd_norm, v_ssd_w_in=v_ssd_w_in, v_ssd_conv_w=v_ssd_conv_w, v_ssd_conv_b=v_ssd_conv_b, v_ssd_dt_bias=v_ssd_dt_bias, v_ssd_a_log=v_ssd_a_log, v_ssd_d_skip=v_ssd_d_skip, v_ssd_gate_norm=v_ssd_gate_norm, v_ssd_w_out=v_ssd_w_out, v_kv_norm=v_kv_norm, v_w_kv=v_w_kv, v_sb_norm=v_sb_norm, v_sb_w_q=v_sb_w_q, v_sb_w_o=v_sb_w_o, v_ffn_norm=v_ffn_norm, v_ffn_w_up=v_ffn_w_up, v_ffn_conv_w=v_ffn_conv_w, v_ffn_conv_b=v_ffn_conv_b, v_ffn_w_down=v_ffn_w_down, v_final_norm=v_final_norm)
    weights = {n: given[n] for n in TWIN_WEIGHTS}
    shared = {n: given[n] for n in SHARED_INPUTS}
    per_example = {n: given[n] for n in ['x']}
    grad_fn = _jax.value_and_grad(_loss, argnums=(0, 1))

    def one_microbatch(ex, loss_target):
        ex = dict(ex)
        diff = ex.pop(TWIN_DIFF_INPUT)
        return grad_fn(weights, diff, {**shared, **ex}, loss_target)

    if N_MICROBATCH == 1:
        loss, (grad_w, grad_x) = one_microbatch(per_example, given["loss_target"])
    else:
        def body(carry, xs):
            loss_sum, grad_sum = carry
            l_k, (gw_k, gx_k) = one_microbatch(xs[0], xs[1])
            with _jax.named_scope("update"):
                return (loss_sum + l_k, _jax.tree.map(_jnp.add, grad_sum, gw_k)), gx_k

        init = (_jnp.zeros((), _jnp.float32), _jax.tree.map(_jnp.zeros_like, weights))
        (loss, grad_w), grad_x = _jax.lax.scan(body, init, (per_example, given["loss_target"]))
    with _jax.named_scope("update"):
        delta_w, new_m, new_v = {}, {}, {}
        for n in TWIN_WEIGHTS:
            delta_w[n], new_m[n], new_v[n] = _adamw(weights[n], grad_w[n], given["m_" + n], given["v_" + n])
    return (loss, grad_x, *[grad_w[n] for n in TWIN_WEIGHTS], *[delta_w[n] for n in TWIN_WEIGHTS],
            *[new_m[n] for n in TWIN_WEIGHTS], *[new_v[n] for n in TWIN_WEIGHTS])
```

```python
import functools

import jax
import jax.numpy as jnp
from jax import lax
from jax.experimental import pallas as pl
from jax.experimental.pallas import tpu as pltpu

D = 1024
SEQ = 8192
N_META = 16
EPS = 1e-6
P = 64
G = 4
N = 128
CONVW = 4
Q = 256
FC = 3
DFF = 256 * ((8 * D // 3 + 255) // 256)
DI = 2 * D
H = DI // P
E = H // G
GW = E * P
CD = DI + 2 * G * N
IN = DI + CD + H
SBH = D // 64
HP = 128
LANES = 128
PF = Q - N_META
LP = PF + N_META + SEQ
NC = LP // Q
TQ = 256
NCHIP = 4
ADAM_LR, ADAM_B1, ADAM_B2, ADAM_EPS, ADAM_WD, ADAM_STEP = 0.001, 0.9, 0.999, 1e-08, 0.01, 10

F32 = jnp.float32
BF16 = jnp.bfloat16
HI = lax.Precision.HIGHEST
MESH = pl.DeviceIdType.MESH
VMEM_LIMIT = 48 * 1024 * 1024


def _pick(n, cands):
    for c in cands:
        if n % c == 0:
            return c
    raise ValueError((n, cands))


def _cparams(sem):
    return pltpu.CompilerParams(dimension_semantics=sem, vmem_limit_bytes=VMEM_LIMIT)


def _valid_rows(block, rows):
    r = block * rows + lax.broadcasted_iota(jnp.int32, (rows, 1), 0)
    return r >= PF


def _sigmoid(x):
    return 1.0 / (1.0 + jnp.exp(-x))


def _softplus(x):
    return jnp.maximum(x, 0.0) + jnp.log(1.0 + jnp.exp(-jnp.abs(x)))


def _sum_all(x):
    return jnp.sum(jnp.sum(x, axis=1, keepdims=True), axis=0, keepdims=True)


def _dsilu(x):
    s = _sigmoid(x)
    return s * (1.0 + x * (1.0 - s))


def _mm(a, b, *, ta=False, tb=False, out_dtype=F32, add=None, mask_rows=False, scale=None, name):
    if ta:
        K, M = a.shape
    else:
        M, K = a.shape
    if tb:
        Nn, K2 = b.shape
    else:
        K2, Nn = b.shape
    assert K == K2, (a.shape, b.shape, ta, tb)
    tm = _pick(M, (768, 512, 256, 128))
    tn = _pick(Nn, (512, 256, 128))
    tk = _pick(K, (1024, 768, 512, 256, 128))
    nk = K // tk
    dims = (((0 if ta else 1,), (1 if tb else 0,)), ((), ()))

    def body(*refs):
        if add is not None:
            a_ref, b_ref, add_ref, o_ref, acc = refs
        else:
            a_ref, b_ref, o_ref, acc = refs
        k = pl.program_id(2)

        @pl.when(k == 0)
        def _():
            acc[...] = jnp.zeros_like(acc)

        acc[...] += lax.dot_general(a_ref[...].astype(BF16), b_ref[...].astype(BF16), dims, preferred_element_type=F32)

        @pl.when(k == nk - 1)
        def _():
            r = acc[...]
            if scale is not None:
                r = r * scale
            if mask_rows:
                r = jnp.where(_valid_rows(pl.program_id(0), tm), r, 0.0)
            if add is not None:
                r = r + add_ref[...]
            o_ref[...] = r.astype(out_dtype)

    a_spec = pl.BlockSpec((tk, tm), lambda i, j, k: (k, i)) if ta else pl.BlockSpec((tm, tk), lambda i, j, k: (i, k))
    b_spec = pl.BlockSpec((tn, tk), lambda i, j, k: (j, k)) if tb else pl.BlockSpec((tk, tn), lambda i, j, k: (k, j))
    o_spec = pl.BlockSpec((tm, tn), lambda i, j, k: (i, j))
    in_specs = [a_spec, b_spec] + ([o_spec] if add is not None else [])
    args = (a, b) + ((add,) if add is not None else ())
    return pl.pallas_call(
        body,
        name=name,
        grid=(M // tm, Nn // tn, nk),
        in_specs=in_specs,
        out_specs=o_spec,
        out_shape=jax.ShapeDtypeStruct((M, Nn), out_dtype),
        scratch_shapes=[pltpu.VMEM((tm, tn), F32)],
        compiler_params=_cparams(("parallel", "parallel", "arbitrary")),
    )(*args)


def _rms_fwd(h, gains, name):
    tr = _pick(LP, (768, 256))
    ng = len(gains)

    def body(*refs):
        h_ref = refs[0]
        g_refs = refs[1 : 1 + ng]
        o_refs = refs[1 + ng :]
        x = h_ref[...]
        xh = x * lax.rsqrt(jnp.mean(x * x, axis=-1, keepdims=True) + EPS)
        for g_ref, o_ref in zip(g_refs, o_refs):
            o_ref[...] = (xh * g_ref[...]).astype(BF16)

    row = pl.BlockSpec((tr, D), lambda i: (i, 0))
    vec = pl.BlockSpec((1, D), lambda i: (0, 0))
    outs = pl.pallas_call(
        body,
        name=name,
        grid=(LP // tr,),
        in_specs=[row] + [vec] * ng,
        out_specs=[row] * ng,
        out_shape=[jax.ShapeDtypeStruct((LP, D), BF16)] * ng,
        compiler_params=_cparams(("parallel",)),
    )(h, *gains)
    return outs


def _rms_bwd(dh_in, h, dus, gains, name):
    tr = _pick(LP, (256,))
    ng = len(gains)

    def body(*refs):
        dh_ref, h_ref = refs[0], refs[1]
        du_refs = refs[2 : 2 + ng]
        g_refs = refs[2 + ng : 2 + 2 * ng]
        o_ref = refs[2 + 2 * ng]
        dg_refs = refs[3 + 2 * ng :]
        i = pl.program_id(0)
        x = h_ref[...]
        r = lax.rsqrt(jnp.mean(x * x, axis=-1, keepdims=True) + EPS)
        xh = x * r
        tot = dh_ref[...]
        for du_ref, g_ref, dg_ref in zip(du_refs, g_refs, dg_refs):
            du = du_ref[...]
            dxh = du * g_ref[...]
            tot = tot + r * (dxh - xh * jnp.mean(dxh * xh, axis=-1, keepdims=True))

            @pl.when(i == 0)
            def _():
                dg_ref[...] = jnp.zeros_like(dg_ref)

            dg_ref[...] += jnp.sum(du * xh, axis=0, keepdims=True)
        o_ref[...] = jnp.where(_valid_rows(i, tr), tot, 0.0)

    row = pl.BlockSpec((tr, D), lambda i: (i, 0))
    vec = pl.BlockSpec((1, D), lambda i: (0, 0))
    outs = pl.pallas_call(
        body,
        name=name,
        grid=(LP // tr,),
        in_specs=[row, row] + [row] * ng + [vec] * ng,
        out_specs=[row] + [vec] * ng,
        out_shape=[jax.ShapeDtypeStruct((LP, D), F32)] + [jax.ShapeDtypeStruct((1, D), F32)] * ng,
        compiler_params=_cparams(("arbitrary",)),
    )(dh_in, h, *dus, *gains)
    return outs[0], outs[1:]


def _loss_head(h, gain, target, name):
    tr = Q

    def body(h_ref, g_ref, t_ref, dh_ref, dg_ref, ls_ref):
        i = pl.program_id(0)

        @pl.when(i == 0)
        def _():
            dg_ref[...] = jnp.zeros_like(dg_ref)
            ls_ref[...] = jnp.zeros_like(ls_ref)
            dh_ref[...] = jnp.zeros_like(dh_ref)

        @pl.when(i > 0)
        def _():
            x = h_ref[...]
            g = g_ref[...]
            r = lax.rsqrt(jnp.mean(x * x, axis=-1, keepdims=True) + EPS)
            xh = x * r
            e = xh * g - t_ref[...]
            ls_ref[...] += jnp.sum(e * e, axis=0, keepdims=True)
            dy = e * (1.0 / D)
            dg_ref[...] += jnp.sum(dy * xh, axis=0, keepdims=True)
            dxh = dy * g
            dh_ref[...] = r * (dxh - xh * jnp.mean(dxh * xh, axis=-1, keepdims=True))

    row = pl.BlockSpec((tr, D), lambda i: (i, 0))
    vec = pl.BlockSpec((1, D), lambda i: (0, 0))
    return pl.pallas_call(
        body,
        name=name,
        grid=(LP // tr,),
        in_specs=[row, vec, pl.BlockSpec((tr, D), lambda i: (jnp.maximum(i - 1, 0), 0))],
        out_specs=[row, vec, vec],
        out_shape=[jax.ShapeDtypeStruct((LP, D), F32), jax.ShapeDtypeStruct((1, D), F32), jax.ShapeDtypeStruct((1, D), F32)],
        compiler_params=_cparams(("arbitrary",)),
    )(h, gain, target)


HALO = 8


def _conv_rows(ext, w, b, width):
    n = ext.shape[0]
    acc = b + w[width - 1 : width, :] * ext[HALO:]
    for k in range(width - 1):
        acc = acc + w[k : k + 1, :] * pltpu.roll(ext, width - 1 - k, 0)[HALO:]
    return acc


def _conv_specs(tr, tn, col):
    per = tr // HALO
    last = LP // HALO - 1
    prev = pl.BlockSpec((HALO, tn), lambda j, i: (jnp.maximum(i * per - 1, 0), col(j)))
    cur = pl.BlockSpec((tr, tn), lambda j, i: (i, col(j)))
    nxt = pl.BlockSpec((HALO, tn), lambda j, i: (jnp.minimum((i + 1) * per, last), col(j)))
    return prev, cur, nxt


def _conv_bwd_core(ext, dact_fn, w, b, width, i, nblk, tr):
    pre = _conv_rows(ext, w, b, width)
    dpre = dact_fn(pre)
    rows = i * tr + lax.broadcasted_iota(jnp.int32, (tr + HALO, 1), 0)
    dpre = jnp.where((rows >= PF) & (rows < LP), dpre, 0.0)
    n = tr + HALO
    dx = w[width - 1 : width, :] * dpre[:tr]
    for k in range(width - 1):
        sh = width - 1 - k
        dx = dx + w[k : k + 1, :] * pltpu.roll(dpre, n - sh, 0)[:tr]
    dcur = dpre[:tr]
    dws = []
    for k in range(width):
        sh = width - 1 - k
        xs = ext[HALO : HALO + tr] if sh == 0 else pltpu.roll(ext, sh, 0)[HALO : HALO + tr]
        dws.append(jnp.sum(xs * dcur, axis=0, keepdims=True))
    db = jnp.sum(dcur, axis=0, keepdims=True)
    dx = jnp.where(_valid_rows(i, tr), dx, 0.0)
    return dx, dws, db


def _ssd_conv_fwd(xr, cw, cb, name):
    tr, tn = Q, 256

    def body(p_ref, c_ref, w_ref, b_ref, o_ref):
        i = pl.program_id(1)
        ext = jnp.concatenate([jnp.where(i > 0, p_ref[...], 0.0), c_ref[...]], axis=0)
        pre = _conv_rows(ext, w_ref[...], b_ref[...], CONVW)
        o_ref[...] = jnp.where(_valid_rows(i, tr), pre * _sigmoid(pre), 0.0)

    prev, cur, _ = _conv_specs(tr, tn, lambda j: j)
    return pl.pallas_call(
        body,
        name=name,
        grid=(CD // tn, LP // tr),
        in_specs=[prev, cur, pl.BlockSpec((CONVW, tn), lambda j, i: (0, j)), pl.BlockSpec((1, tn), lambda j, i: (0, j))],
        out_specs=cur,
        out_shape=jax.ShapeDtypeStruct((LP, CD), F32),
        compiler_params=_cparams(("parallel", "arbitrary")),
    )(xr, xr, cw, cb)


def _ssd_conv_bwd(xr, dxbc, cw, cb, name):
    tr, tn = Q, 256
    nblk = LP // tr

    def body(p_ref, c_ref, n_ref, dc_ref, dn_ref, w_ref, b_ref, dx_ref, dw_ref, db_ref):
        i = pl.program_id(1)
        ext = jnp.concatenate([jnp.where(i > 0, p_ref[...], 0.0), c_ref[...], n_ref[...]], axis=0)
        dout = jnp.concatenate([dc_ref[...], dn_ref[...]], axis=0)
        dx, dws, db = _conv_bwd_core(ext, lambda pre: dout * _dsilu(pre), w_ref[...], b_ref[...], CONVW, i, nblk, tr)
        dx_ref[...] = dx

        @pl.when(i == 0)
        def _():
            dw_ref[...] = jnp.zeros_like(dw_ref)
            db_ref[...] = jnp.zeros_like(db_ref)

        for k in range(CONVW):
            dw_ref[k : k + 1, :] += dws[k]
        db_ref[...] += db

    prev, cur, nxt = _conv_specs(tr, tn, lambda j: j)
    wspec = pl.BlockSpec((CONVW, tn), lambda j, i: (0, j))
    bspec = pl.BlockSpec((1, tn), lambda j, i: (0, j))
    return pl.pallas_call(
        body,
        name=name,
        grid=(CD // tn, LP // tr),
        in_specs=[prev, cur, nxt, cur, nxt, wspec, bspec],
        out_specs=[cur, wspec, bspec],
        out_shape=[jax.ShapeDtypeStruct((LP, CD), F32), jax.ShapeDtypeStruct((CONVW, CD), F32), jax.ShapeDtypeStruct((1, CD), F32)],
        compiler_params=_cparams(("parallel", "arbitrary")),
    )(xr, xr, xr, dxbc, dxbc, cw, cb)


def _ffn_act_fwd(hg, hv, cwg, cwv, cbg, cbv, name):
    tr, tn = Q, 256

    def body(pg, cg, pv, cv, wg, wv, bg, bv, o_ref):
        i = pl.program_id(1)
        eg = jnp.concatenate([jnp.where(i > 0, pg[...], 0.0), cg[...]], axis=0)
        ev = jnp.concatenate([jnp.where(i > 0, pv[...], 0.0), cv[...]], axis=0)
        gate = _conv_rows(eg, wg[...], bg[...], FC)
        val = _conv_rows(ev, wv[...], bv[...], FC)
        o_ref[...] = (gate * _sigmoid(gate) * val).astype(BF16)

    prev, cur, _ = _conv_specs(tr, tn, lambda j: j)
    wspec = pl.BlockSpec((FC, tn), lambda j, i: (0, j))
    bspec = pl.BlockSpec((1, tn), lambda j, i: (0, j))
    return pl.pallas_call(
        body,
        name=name,
        grid=(DFF // tn, LP // tr),
        in_specs=[prev, cur, prev, cur, wspec, wspec, bspec, bspec],
        out_specs=cur,
        out_shape=jax.ShapeDtypeStruct((LP, DFF), BF16),
        compiler_params=_cparams(("parallel", "arbitrary")),
    )(hg, hg, hv, hv, cwg, cwv, cbg, cbv)


def _ffn_act_bwd(hg, hv, da, cwg, cwv, cbg, cbv, name):
    tr, tn = Q, 256
    nblk = LP // tr

    def body(pg, cg, ng, pv, cv, nv, dc, dn, wg, wv, bg, bv, dg_ref, dv_ref, dwg, dwv, dbg, dbv):
        i = pl.program_id(1)
        eg = jnp.concatenate([jnp.where(i > 0, pg[...], 0.0), cg[...], ng[...]], axis=0)
        ev = jnp.concatenate([jnp.where(i > 0, pv[...], 0.0), cv[...], nv[...]], axis=0)
        dout = jnp.concatenate([dc[...], dn[...]], axis=0)
        gate = _conv_rows(eg, wg[...], bg[...], FC)
        val = _conv_rows(ev, wv[...], bv[...], FC)
        dxg, dwsg, dbgv = _conv_bwd_core(eg, lambda pre: dout * val * _dsilu(pre), wg[...], bg[...], FC, i, nblk, tr)
        dxv, dwsv, dbvv = _conv_bwd_core(ev, lambda pre: dout * gate * _sigmoid(gate), wv[...], bv[...], FC, i, nblk, tr)
        dg_ref[...] = dxg
        dv_ref[...] = dxv

        @pl.when(i == 0)
        def _():
            dwg[...] = jnp.zeros_like(dwg)
            dwv[...] = jnp.zeros_like(dwv)
            dbg[...] = jnp.zeros_like(dbg)
            dbv[...] = jnp.zeros_like(dbv)

        for k in range(FC):
            dwg[k : k + 1, :] += dwsg[k]
            dwv[k : k + 1, :] += dwsv[k]
        dbg[...] += dbgv
        dbv[...] += dbvv

    prev, cur, nxt = _conv_specs(tr, tn, lambda j: j)
    wspec = pl.BlockSpec((FC, tn), lambda j, i: (0, j))
    bspec = pl.BlockSpec((1, tn), lambda j, i: (0, j))
    big = jax.ShapeDtypeStruct((LP, DFF), F32)
    wsh = jax.ShapeDtypeStruct((FC, DFF), F32)
    bsh = jax.ShapeDtypeStruct((1, DFF), F32)
    return pl.pallas_call(
        body,
        name=name,
        grid=(DFF // tn, LP // tr),
        in_specs=[prev, cur, nxt, prev, cur, nxt, cur, nxt, wspec, wspec, bspec, bspec],
        out_specs=[cur, cur, wspec, wspec, bspec, bspec],
        out_shape=[big, big, wsh, wsh, bsh, bsh],
        compiler_params=_cparams(("parallel", "arbitrary")),
    )(hg, hg, hg, hv, hv, hv, da, da, cwg, cwv, cbg, cbv)


def _gate_fwd(y, z, gg, name):
    tr = Q

    def body(y_ref, z_ref, g_ref, o_ref):
        zv = z_ref[...]
        hg = y_ref[...] * zv * _sigmoid(zv)
        r = lax.rsqrt(jnp.mean(hg * hg, axis=-1, keepdims=True) + EPS)
        o_ref[...] = (hg * r * g_ref[...]).astype(BF16)

    blk = pl.BlockSpec((tr, GW), lambda i, g: (i, g))
    return pl.pallas_call(
        body,
        name=name,
        grid=(LP // tr, G),
        in_specs=[blk, blk, pl.BlockSpec((1, GW), lambda i, g: (0, g))],
        out_specs=blk,
        out_shape=jax.ShapeDtypeStruct((LP, DI), BF16),
        compiler_params=_cparams(("parallel", "parallel")),
    )(y, z, gg)


def _gate_bwd(dout, y, z, gg, name):
    tr = Q

    def body(do_ref, y_ref, z_ref, g_ref, dy_ref, dz_ref, dg_ref):
        i = pl.program_id(1)
        zv = z_ref[...]
        yv = y_ref[...]
        sz = zv * _sigmoid(zv)
        hg = yv * sz
        r = lax.rsqrt(jnp.mean(hg * hg, axis=-1, keepdims=True) + EPS)
        hh = hg * r
        do = do_ref[...]
        dhh = do * g_ref[...]
        dhg = r * (dhh - hh * jnp.mean(dhh * hh, axis=-1, keepdims=True))
        dy_ref[...] = dhg * sz
        dz_ref[...] = dhg * yv * _dsilu(zv)

        @pl.when(i == 0)
        def _():
            dg_ref[...] = jnp.zeros_like(dg_ref)

        dg_ref[...] += jnp.sum(do * hh, axis=0, keepdims=True)

    blk = pl.BlockSpec((tr, GW), lambda g, i: (i, g))
    vec = pl.BlockSpec((1, GW), lambda g, i: (0, g))
    big = jax.ShapeDtypeStruct((LP, DI), F32)
    return pl.pallas_call(
        body,
        name=name,
        grid=(G, LP // tr),
        in_specs=[blk, blk, blk, vec],
        out_specs=[blk, blk, vec],
        out_shape=[big, big, jax.ShapeDtypeStruct((1, DI), F32)],
        compiler_params=_cparams(("parallel", "arbitrary")),
    )(dout, y, z, gg)


def _ssd_common(dtc_ref, dtr_ref, bc_ref, br_ref, ac_ref, ar_ref, c):
    rows = c * Q + lax.broadcasted_iota(jnp.int32, (Q, 1), 0)
    cols = c * Q + lax.broadcasted_iota(jnp.int32, (1, Q), 1)
    prec = dtc_ref[...] + bc_ref[...]
    prer = dtr_ref[...] + br_ref[...]
    dtc = jnp.where(rows >= PF, _softplus(prec), 0.0)
    dtr = jnp.where(cols >= PF, _softplus(prer), 0.0)
    a_c = -jnp.exp(ac_ref[...])
    a_r = -jnp.exp(ar_ref[...])
    li = lax.broadcasted_iota(jnp.int32, (Q, Q), 0)
    si = lax.broadcasted_iota(jnp.int32, (Q, Q), 1)
    tril = si <= li
    trif = tril.astype(F32)
    csc = jnp.dot(trif, dtc * a_c, precision=HI, preferred_element_type=F32)
    csr = lax.dot_general(dtr * a_r, trif, (((1,), (1,)), ((), ())), precision=HI, preferred_element_type=F32)
    return dict(rows=rows, prec=prec, dtc=dtc, a_c=a_c, tril=tril, trif=trif, csc=csc, csr=csr, li=li, si=si)


def _pair_expand(arr, h0, lane_lo):
    return jnp.where(lane_lo, arr[:, h0 : h0 + 1], arr[:, h0 + 1 : h0 + 2])


def _ssd_specs():
    nb = DI // N
    xs = pl.BlockSpec((Q, GW), lambda g, c: (c, g))
    bb = pl.BlockSpec((Q, N), lambda g, c: (c, nb + g))
    cc = pl.BlockSpec((Q, N), lambda g, c: (c, nb + G + g))
    dtc = pl.BlockSpec((None, Q, LANES), lambda g, c: (g, c, 0))
    dtr = pl.BlockSpec((None, 8, Q), lambda g, c: (g, 0, c))
    pc = pl.BlockSpec((None, 1, LANES), lambda g, c: (g, 0, 0))
    pr = pl.BlockSpec((None, 8, 1), lambda g, c: (g, 0, 0))
    return xs, bb, cc, dtc, dtr, pc, pr


def _ssd_fwd(xbc, dtc, dtr, bias_c, bias_r, alog_c, alog_r, dskip_c, name):
    def body(xs_ref, b_ref, c_ref, dtc_ref, dtr_ref, bc_ref, br_ref, ac_ref, ar_ref, dk_ref, y_ref, st_ref, state):
        c = pl.program_id(1)

        @pl.when(c == 0)
        def _():
            state[...] = jnp.zeros_like(state)

        st_ref[...] = state[...]
        cm = _ssd_common(dtc_ref, dtr_ref, bc_ref, br_ref, ac_ref, ar_ref, c)
        Bm = b_ref[...]
        Cm = c_ref[...]
        cb = lax.dot_general(Cm.astype(BF16), Bm.astype(BF16), (((1,), (1,)), ((), ())), preferred_element_type=F32)
        bt = Bm.T.astype(BF16)
        lane_lo = lax.broadcasted_iota(jnp.int32, (1, HP), 1) < P
        csc, csr, dtc_v = cm["csc"], cm["csr"], cm["dtc"]
        ecs = jnp.exp(csc)
        cs_end = csc[Q - 1 : Q, :]
        wdec = jnp.exp(cs_end - csc)
        eend = jnp.exp(cs_end)
        for pp in range(E // 2):
            h0 = 2 * pp
            sl = slice(pp * HP, (pp + 1) * HP)
            xp = xs_ref[:, sl]
            xdt = xp * _pair_expand(dtc_v, h0, lane_lo)
            yacc = xp * _pair_expand(dk_ref[...], h0, lane_lo)
            for e in range(2):
                h = h0 + e
                lm = jnp.where(cm["tril"], jnp.exp(jnp.minimum(csc[:, h : h + 1] - csr[h : h + 1, :], 0.0)), 0.0)
                m = (cb * lm).astype(BF16)
                xm = jnp.where(lane_lo if e == 0 else jnp.logical_not(lane_lo), xdt, 0.0).astype(BF16)
                yacc = yacc + jnp.dot(m, xm, preferred_element_type=F32)
            stp = state[:, sl]
            yoff = jnp.dot(Cm.astype(BF16), stp.astype(BF16), preferred_element_type=F32)
            y_ref[:, sl] = yacc + yoff * _pair_expand(ecs, h0, lane_lo)
            xw = (xdt * _pair_expand(wdec, h0, lane_lo)).astype(BF16)
            state[:, sl] = stp * _pair_expand(eend, h0, lane_lo) + jnp.dot(bt, xw, preferred_element_type=F32)

    xs, bb, cc, dtcs, dtrs, pc, pr = _ssd_specs()
    return pl.pallas_call(
        body,
        name=name,
        grid=(G, NC),
        in_specs=[xs, bb, cc, dtcs, dtrs, pc, pr, pc, pr, pc],
        out_specs=[xs, pl.BlockSpec((None, None, N, GW), lambda g, c: (c, g, 0, 0))],
        out_shape=[jax.ShapeDtypeStruct((LP, DI), F32), jax.ShapeDtypeStruct((NC, G, N, GW), F32)],
        scratch_shapes=[pltpu.VMEM((N, GW), F32)],
        compiler_params=_cparams(("parallel", "arbitrary")),
    )(xbc, xbc, xbc, dtc, dtr, bias_c, bias_r, alog_c, alog_r, dskip_c)


def _ssd_bwd(xbc, dy, states, dtc, dtr, bias_c, bias_r, alog_c, alog_r, dskip_c, name):
    def body(xs_ref, b_ref, c_ref, dy_ref, st_ref, dtc_ref, dtr_ref, bc_ref, br_ref, ac_ref, ar_ref, dk_ref,
             dx_ref, db_ref, dc_ref, ddt_ref, dbias_ref, dalog_ref, ddk_ref, dstate):
        ci = pl.program_id(1)
        c = NC - 1 - ci

        @pl.when(ci == 0)
        def _():
            dstate[...] = jnp.zeros_like(dstate)
            dbias_ref[...] = jnp.zeros_like(dbias_ref)
            dalog_ref[...] = jnp.zeros_like(dalog_ref)
            ddk_ref[...] = jnp.zeros_like(ddk_ref)

        cm = _ssd_common(dtc_ref, dtr_ref, bc_ref, br_ref, ac_ref, ar_ref, c)
        Bm = b_ref[...]
        Cm = c_ref[...]
        Bb = Bm.astype(BF16)
        Cb = Cm.astype(BF16)
        nt = (((1,), (1,)), ((), ()))
        cb = lax.dot_general(Cb, Bb, nt, preferred_element_type=F32)
        cbt = lax.dot_general(Bb, Cb, nt, preferred_element_type=F32)
        ct = Cm.T.astype(BF16)
        lane_lo = lax.broadcasted_iota(jnp.int32, (1, HP), 1) < P
        lane_id = lax.broadcasted_iota(jnp.int32, (1, LANES), 1)
        csc, csr, dtc_v, a_c = cm["csc"], cm["csr"], cm["dtc"], cm["a_c"]
        triu = cm["si"] >= cm["li"]
        ecs = jnp.exp(csc)
        cs_end = csc[Q - 1 : Q, :]
        wdec = jnp.exp(cs_end - csc)
        eend = jnp.exp(cs_end)
        dcb = jnp.zeros((Q, Q), F32)
        dcbt = jnp.zeros((Q, Q), F32)
        dcs = jnp.zeros((Q, LANES), F32)
        dcs_end = jnp.zeros((1, LANES), F32)
        ddt = jnp.zeros((Q, LANES), F32)
        ddk = jnp.zeros((1, LANES), F32)
        dB = jnp.zeros((Q, N), F32)
        dC = jnp.zeros((Q, N), F32)
        for pp in range(E // 2):
            h0 = 2 * pp
            sl = slice(pp * HP, (pp + 1) * HP)
            xp = xs_ref[:, sl]
            dyp = dy_ref[:, sl]
            dtx = _pair_expand(dtc_v, h0, lane_lo)
            xdt = xp * dtx
            dxdt = jnp.zeros((Q, HP), F32)
            stp = st_ref[:, sl]
            stb = stp.astype(BF16)
            dsn = dstate[:, sl]
            dsnb = dsn.astype(BF16)
            ecsx = _pair_expand(ecs, h0, lane_lo)
            wdx = _pair_expand(wdec, h0, lane_lo)
            cs_ = jnp.dot(Cb, stb, preferred_element_type=F32)
            yo = cs_ * ecsx
            dyo = dyp * ecsx
            dyob = dyo.astype(BF16)
            dC = dC + lax.dot_general(dyob, stb, nt, preferred_element_type=F32)
            ds_from_y = jnp.dot(ct, dyob, preferred_element_type=F32)
            dyyo = dyp * yo
            xw = xdt * wdx
            dB = dB + lax.dot_general(xw.astype(BF16), dsnb, nt, preferred_element_type=F32)
            dxw = jnp.dot(Bb, dsnb, preferred_element_type=F32)
            dxdt = dxdt + dxw * wdx
            w2 = dxw * xw
            rs = jnp.sum(dsn * stp, axis=0, keepdims=True) * _pair_expand(eend, h0, lane_lo)
            dstate[:, sl] = dsn * _pair_expand(eend, h0, lane_lo) + ds_from_y
            dyx = dyp * xp
            for e in range(2):
                h = h0 + e
                msk = lane_lo if e == 0 else jnp.logical_not(lane_lo)
                oh = (lane_id == h).astype(F32)
                col = csc[:, h : h + 1]
                row = csr[h : h + 1, :]
                lm = jnp.where(cm["tril"], jnp.exp(jnp.minimum(col - row, 0.0)), 0.0)
                lmt = jnp.where(triu, jnp.exp(jnp.minimum(row - col, 0.0)), 0.0)
                dye = jnp.where(msk, dyp, 0.0).astype(BF16)
                xde = jnp.where(msk, xdt, 0.0).astype(BF16)
                gm = lax.dot_general(dye, xde, nt, preferred_element_type=F32)
                gmt = lax.dot_general(xde, dye, nt, preferred_element_type=F32)
                gl = gm * lm
                glt = gmt * lmt
                dcb = dcb + gl
                dcbt = dcbt + glt
                dcs_h = (
                    jnp.sum(gl * cb, axis=1, keepdims=True)
                    - jnp.sum(glt * cbt, axis=1, keepdims=True)
                    + jnp.sum(jnp.where(msk, dyyo, 0.0), axis=1, keepdims=True)
                    - jnp.sum(jnp.where(msk, w2, 0.0), axis=1, keepdims=True)
                )
                dcs = dcs + dcs_h * oh
                dcs_end = dcs_end + (_sum_all(jnp.where(msk, w2, 0.0)) + _sum_all(jnp.where(msk, rs, 0.0))) * oh
                dxdt = dxdt + jnp.dot((cbt * lmt).astype(BF16), dye, preferred_element_type=F32)
                ddk = ddk + _sum_all(jnp.where(msk, dyx, 0.0)) * oh
            dx_ref[:, sl] = dxdt * dtx + dyp * _pair_expand(dk_ref[...], h0, lane_lo)
            dxx = dxdt * xp
            for e in range(2):
                h = h0 + e
                msk = lane_lo if e == 0 else jnp.logical_not(lane_lo)
                oh = (lane_id == h).astype(F32)
                ddt = ddt + jnp.sum(jnp.where(msk, dxx, 0.0), axis=1, keepdims=True) * oh
        dC = dC + jnp.dot(dcb.astype(BF16), Bb, preferred_element_type=F32)
        dB = dB + jnp.dot(dcbt.astype(BF16), Cb, preferred_element_type=F32)
        db_ref[...] = dB
        dc_ref[...] = dC
        last = (lax.broadcasted_iota(jnp.int32, (Q, 1), 0) == Q - 1).astype(F32)
        dcs = dcs + last * dcs_end
        dda = jnp.dot(triu.astype(F32), dcs, precision=HI, preferred_element_type=F32)
        ddt = ddt + dda * a_c
        da = jnp.sum(dda * dtc_v, axis=0, keepdims=True)
        draw = jnp.where(cm["rows"] >= PF, ddt * _sigmoid(cm["prec"]), 0.0)
        ddt_ref[...] = draw
        dbias_ref[...] += jnp.sum(draw, axis=0, keepdims=True)
        dalog_ref[...] += da * a_c
        ddk_ref[...] += ddk

    xs, bb, cc, dtcs, dtrs, pc, pr = _ssd_specs()

    def rev(spec_fn):
        return lambda g, ci: spec_fn(g, NC - 1 - ci)

    def rspec(spec):
        return pl.BlockSpec(spec.block_shape, rev(spec.index_map))

    xs_r, bb_r, cc_r, dtc_r, dtr_r = rspec(xs), rspec(bb), rspec(cc), rspec(dtcs), rspec(dtrs)
    st_r = pl.BlockSpec((None, None, N, GW), lambda g, ci: (NC - 1 - ci, g, 0, 0))
    gn = pl.BlockSpec((Q, N), lambda g, ci: (NC - 1 - ci, g))
    return pl.pallas_call(
        body,
        name=name,
        grid=(G, NC),
        in_specs=[xs_r, bb_r, cc_r, xs_r, st_r, dtc_r, dtr_r, pc, pr, pc, pr, pc],
        out_specs=[xs_r, gn, gn, dtc_r, pc, pc, pc],
        out_shape=[
            jax.ShapeDtypeStruct((LP, DI), F32),
            jax.ShapeDtypeStruct((LP, G * N), F32),
            jax.ShapeDtypeStruct((LP, G * N), F32),
            jax.ShapeDtypeStruct((G, LP, LANES), F32),
            jax.ShapeDtypeStruct((G, 1, LANES), F32),
            jax.ShapeDtypeStruct((G, 1, LANES), F32),
            jax.ShapeDtypeStruct((G, 1, LANES), F32),
        ],
        scratch_shapes=[pltpu.VMEM((N, GW), F32)],
        compiler_params=_cparams(("parallel", "arbitrary")),
    )(xbc, xbc, xbc, dy, states, dtc, dtr, bias_c, bias_r, alog_c, alog_r, dskip_c)


def _split_dot(x, u):
    hi = x.astype(BF16)
    lo = (x - hi.astype(F32)).astype(BF16)
    return jnp.dot(hi, u, preferred_element_type=F32) + jnp.dot(lo, u, preferred_element_type=F32)


def _sb_block(qe, kblk, vis, a_run, u_gt):
    l = lax.dot_general(qe, kblk, (((1,), (1,)), ((), ())), preferred_element_type=F32)
    sp = _softplus(l)
    lk = jnp.where(vis, -sp, 0.0)
    later = _split_dot(lk, u_gt) + a_run
    return l, sp, lk, later


def _attn_fwd(q, k, v, name):
    nq = LP // TQ

    def body(q_ref, k_ref, v_ref, o_ref):
        i = pl.program_id(1)
        qv = q_ref[...]
        lane_lo = lax.broadcasted_iota(jnp.int32, (1, HP), 1) < 64
        t_idx = i * TQ + lax.broadcasted_iota(jnp.int32, (TQ, 1), 0)
        ji = lax.broadcasted_iota(jnp.int32, (TQ, TQ), 0)
        si = lax.broadcasted_iota(jnp.int32, (TQ, TQ), 1)
        u_gt = (ji > si).astype(BF16)
        qs = [jnp.where(lane_lo, qv, jnp.zeros_like(qv)), jnp.where(lane_lo, jnp.zeros_like(qv), qv)]

        def step(n, carry):
            a0, a1, acc = carry
            kb = i - n
            off = pl.multiple_of(kb * TQ, TQ)
            kblk = k_ref[pl.ds(off, TQ), :]
            vblk = v_ref[pl.ds(off, TQ), :]
            s_idx = kb * TQ + lax.broadcasted_iota(jnp.int32, (1, TQ), 1)
            vis = (s_idx < t_idx) & (s_idx >= PF)
            new_a = []
            for e, a_run in enumerate((a0, a1)):
                l, sp, lk, later = _sb_block(qs[e], kblk, vis, a_run, u_gt)
                w = jnp.where(vis, jnp.exp(l - sp + later), 0.0).astype(BF16)
                msk = lane_lo if e == 0 else jnp.logical_not(lane_lo)
                acc = acc + jnp.dot(w, jnp.where(msk, vblk, jnp.zeros_like(vblk)), preferred_element_type=F32)
                new_a.append(a_run + jnp.sum(lk, axis=1, keepdims=True))
            return new_a[0], new_a[1], acc

        z1 = jnp.zeros((TQ, 1), F32)
        _, _, acc = lax.fori_loop(0, i + 1, step, (z1, z1, jnp.zeros((TQ, HP), F32)))
        o_ref[...] = acc

    return pl.pallas_call(
        body,
        name=name,
        grid=(D // HP, nq),
        in_specs=[
            pl.BlockSpec((TQ, HP), lambda j, i: (i, j)),
            pl.BlockSpec((LP, HP), lambda j, i: (0, j)),
            pl.BlockSpec((LP, HP), lambda j, i: (0, j)),
        ],
        out_specs=pl.BlockSpec((TQ, HP), lambda j, i: (i, j)),
        out_shape=jax.ShapeDtypeStruct((LP, D), F32),
        compiler_params=_cparams(("parallel", "arbitrary")),
    )(q, k, v)


def _attn_bwd(q, k, v, o, do, name):
    nq = LP // TQ

    def body(q_ref, k_ref, v_ref, o_ref, do_ref, dq_ref, dk_ref, dv_ref):
        i = pl.program_id(1)

        @pl.when(i == 0)
        def _():
            dk_ref[...] = jnp.zeros_like(dk_ref)
            dv_ref[...] = jnp.zeros_like(dv_ref)

        qv = q_ref[...]
        dov = do_ref[...]
        lane_lo = lax.broadcasted_iota(jnp.int32, (1, HP), 1) < 64
        t_idx = i * TQ + lax.broadcasted_iota(jnp.int32, (TQ, 1), 0)
        ji = lax.broadcasted_iota(jnp.int32, (TQ, TQ), 0)
        si = lax.broadcasted_iota(jnp.int32, (TQ, TQ), 1)
        u_gt = (ji > si).astype(BF16)
        u_ge = (ji >= si).astype(BF16)
        msks = [lane_lo, jnp.logical_not(lane_lo)]
        qs = [jnp.where(m, qv, jnp.zeros_like(qv)) for m in msks]
        dob = [jnp.where(m, dov, 0.0).astype(BF16) for m in msks]
        ov = o_ref[...]
        deltas = [jnp.sum(d.astype(F32) * ov, axis=1, keepdims=True) for d in dob]
        nt = (((1,), (1,)), ((), ()))
        tn = (((0,), (0,)), ((), ()))

        def step(n, carry):
            a0, a1, p0, p1, dq = carry
            kb = i - n
            off = pl.multiple_of(kb * TQ, TQ)
            kblk = k_ref[pl.ds(off, TQ), :]
            vblk = v_ref[pl.ds(off, TQ), :]
            s_idx = kb * TQ + lax.broadcasted_iota(jnp.int32, (1, TQ), 1)
            vis = (s_idx < t_idx) & (s_idx >= PF)
            new_a, new_p = [], []
            dk_acc = jnp.zeros((TQ, HP), F32)
            dv_acc = jnp.zeros((TQ, HP), F32)
            for e, (a_run, p_run) in enumerate(((a0, p0), (a1, p1))):
                l, sp, lk, later = _sb_block(qs[e], kblk, vis, a_run, u_gt)
                sig = jnp.exp(l - sp)
                wb = jnp.where(vis, jnp.exp(l - sp + later), 0.0).astype(BF16)
                dw = lax.dot_general(dob[e], vblk, nt, preferred_element_type=F32)
                pm = wb.astype(F32) * dw
                cum_p = deltas[e] - (_split_dot(pm, u_ge) + p_run)
                dl = jnp.where(vis, pm * (1.0 - sig) - cum_p * sig, 0.0).astype(BF16)
                km = jnp.where(msks[e], kblk, jnp.zeros_like(kblk))
                dq = dq + jnp.dot(dl, km, preferred_element_type=F32)
                dk_acc = dk_acc + lax.dot_general(dl, qs[e], tn, preferred_element_type=F32)
                dv_acc = dv_acc + lax.dot_general(wb, dob[e], tn, preferred_element_type=F32)
                new_a.append(a_run + jnp.sum(lk, axis=1, keepdims=True))
                new_p.append(p_run + jnp.sum(pm, axis=1, keepdims=True))
            dk_ref[pl.ds(off, TQ), :] += dk_acc
            dv_ref[pl.ds(off, TQ), :] += dv_acc
            return new_a[0], new_a[1], new_p[0], new_p[1], dq

        z1 = jnp.zeros((TQ, 1), F32)
        carry = lax.fori_loop(0, i + 1, step, (z1, z1, z1, z1, jnp.zeros((TQ, HP), F32)))
        dq_ref[...] = carry[4]

    blk = pl.BlockSpec((TQ, HP), lambda j, i: (i, j))
    full = pl.BlockSpec((LP, HP), lambda j, i: (0, j))
    big = jax.ShapeDtypeStruct((LP, D), F32)
    return pl.pallas_call(
        body,
        name=name,
        grid=(D // HP, nq),
        in_specs=[blk, full, full, blk, blk],
        out_specs=[blk, full, full],
        out_shape=[big, big, big],
        compiler_params=_cparams(("parallel", "arbitrary")),
    )(q, k, v, o, do)


def _adamw(w, ga, gb, m, v, name):
    R = w.shape[0]
    tr = _pick(R, (2048, 1024, 512, 256, 128, 64, 32, 16, 8)) if R % 8 == 0 else R

    def body(w_ref, ga_ref, gb_ref, m_ref, v_ref, g_out, d_out, m_out, v_out):
        g = ga_ref[...] + gb_ref[...]
        mn = ADAM_B1 * m_ref[...] + (1.0 - ADAM_B1) * g
        vn = ADAM_B2 * v_ref[...] + (1.0 - ADAM_B2) * (g * g)
        mh = mn / (1.0 - ADAM_B1**ADAM_STEP)
        vh = vn / (1.0 - ADAM_B2**ADAM_STEP)
        g_out[...] = g
        d_out[...] = -ADAM_LR * (mh / (jnp.sqrt(vh) + ADAM_EPS) + ADAM_WD * w_ref[...])
        m_out[...] = mn
        v_out[...] = vn

    blk = pl.BlockSpec((tr, LANES), lambda i: (i, 0))
    sh = jax.ShapeDtypeStruct((R, LANES), F32)
    return pl.pallas_call(
        body,
        name=name,
        grid=(R // tr,),
        in_specs=[blk] * 5,
        out_specs=[blk] * 4,
        out_shape=[sh] * 4,
        compiler_params=_cparams(("parallel",)),
    )(w, ga, gb, m, v)


def _sum4(bufs, name):
    outs = []
    for n, buf in enumerate(bufs):
        R = buf.shape[1]
        tr = _pick(R, (2048, 1024, 512, 256, 128, 64, 32, 16))

        def body(b_ref, o_ref):
            acc = b_ref[0].astype(F32)
            for s in range(1, NCHIP):
                acc = acc + b_ref[s].astype(F32)
            o_ref[...] = acc

        outs.append(
            pl.pallas_call(
                body,
                name=f"{name}_{n}",
                grid=(R // tr,),
                in_specs=[pl.BlockSpec((NCHIP, tr, LANES), lambda i: (0, i, 0))],
                out_specs=pl.BlockSpec((tr, LANES), lambda i: (i, 0)),
                out_shape=jax.ShapeDtypeStruct((R, LANES), F32),
                compiler_params=_cparams(("parallel",)),
            )(buf)
        )
    return outs


ANY = pl.BlockSpec(memory_space=pl.ANY)


def _chip_exchange(bufs, gather, name):
    nb = len(bufs)

    def body(*refs):
        ins = refs[:nb]
        outs = refs[nb : 2 * nb]
        send_sems, recv_sems, loc_sems = refs[2 * nb :]
        x, y, c = lax.axis_index("x"), lax.axis_index("y"), lax.axis_index("c")
        me = 2 * x + y
        peers = [(1 - x, y), (x, 1 - y), (1 - x, 1 - y)]
        locs = []
        for b in range(nb):
            src = ins[b] if gather else ins[b].at[me]
            cp = pltpu.make_async_copy(src, outs[b].at[me], loc_sems.at[b])
            cp.start()
            locs.append(cp)
        sends = []
        for k, (px, py) in enumerate(peers):
            for b in range(nb):
                src = ins[b] if gather else ins[b].at[2 * px + py]
                cp = pltpu.make_async_remote_copy(
                    src_ref=src, dst_ref=outs[b].at[me], send_sem=send_sems.at[k * nb + b], recv_sem=recv_sems.at[k * nb + b],
                    device_id=(px, py, c), device_id_type=MESH)
                cp.start()
                sends.append(cp)
        for k, (px, py) in enumerate(peers):
            for b in range(nb):
                src = ins[b] if gather else ins[b].at[me]
                pltpu.make_async_remote_copy(
                    src_ref=src, dst_ref=outs[b].at[2 * px + py], send_sem=send_sems.at[k * nb + b],
                    recv_sem=recv_sems.at[k * nb + b], device_id=(px, py, c), device_id_type=MESH).wait_recv()
        for cp in sends:
            cp.wait_send()
        for cp in locs:
            cp.wait()

    out_shape = []
    for buf in bufs:
        shp = (NCHIP,) + tuple(buf.shape) if gather else tuple(buf.shape)
        out_shape.append(jax.ShapeDtypeStruct(shp, buf.dtype))
    return pl.pallas_call(
        body,
        name=name,
        in_specs=[ANY] * nb,
        out_specs=[ANY] * nb,
        out_shape=out_shape,
        scratch_shapes=[pltpu.SemaphoreType.DMA((3 * nb,)), pltpu.SemaphoreType.DMA((3 * nb,)), pltpu.SemaphoreType.DMA((nb,))],
    )(*bufs)


def _sibling_exchange(bufs, name):
    nb = len(bufs)

    def body(*refs):
        ins = refs[:nb]
        outs = refs[nb : 2 * nb]
        send_sems, recv_sems = refs[2 * nb :]
        x, y, c = lax.axis_index("x"), lax.axis_index("y"), lax.axis_index("c")
        cps = []
        for b in range(nb):
            cp = pltpu.make_async_remote_copy(
                src_ref=ins[b], dst_ref=outs[b], send_sem=send_sems.at[b], recv_sem=recv_sems.at[b],
                device_id=(x, y, 1 - c), device_id_type=MESH)
            cp.start()
            cps.append(cp)
        for cp in cps:
            cp.wait()

    return pl.pallas_call(
        body,
        name=name,
        in_specs=[ANY] * nb,
        out_specs=[ANY] * nb,
        out_shape=[jax.ShapeDtypeStruct(b.shape, b.dtype) for b in bufs],
        scratch_shapes=[pltpu.SemaphoreType.DMA((nb,)), pltpu.SemaphoreType.DMA((nb,))],
    )(*bufs)


ROW_ALIGN = 1024


def _pack(pieces, dtype):
    flat = []
    for p in pieces:
        f = p.reshape(-1).astype(dtype)
        pad = (-f.shape[0]) % LANES
        if pad:
            f = jnp.pad(f, (0, pad))
        flat.append(f)
    tot = sum(f.shape[0] for f in flat)
    pad = (-tot) % (ROW_ALIGN * LANES)
    if pad:
        flat.append(jnp.zeros((pad,), dtype))
    return jnp.concatenate(flat).reshape(-1, LANES)


def _unpack(buf, shapes):
    lead = buf.shape[:-2]
    flat = buf.reshape(lead + (-1,))
    out = []
    off = 0
    for shp in shapes:
        n = 1
        for d in shp:
            n *= d
        out.append(flat[..., off : off + n].reshape(lead + tuple(shp)))
        off += n + ((-n) % LANES)
    return out


PARAMS = (
    ("meta_tokens", 1, "small"), ("ssd_norm", 1, "small"), ("ssd_w_in", 2, "big"), ("ssd_conv_w", 2, "small"),
    ("ssd_conv_b", 1, "small"), ("ssd_dt_bias", None, "rep"), ("ssd_a_log", None, "rep"), ("ssd_d_skip", None, "rep"),
    ("ssd_gate_norm", 1, "small"), ("ssd_w_out", 1, "big"), ("kv_norm", None, "rep"), ("w_kv", 1, "big"),
    ("sb_norm", None, "rep"), ("sb_w_q", 1, "big"), ("sb_w_o", 1, "big"), ("ffn_norm", None, "rep"),
    ("ffn_w_up", 2, "big"), ("ffn_conv_w", 2, "small"), ("ffn_conv_b", None, "rep"), ("ffn_w_down", 1, "big"),
    ("final_norm", None, "rep"),
)


def _head_cols(vec):
    return jnp.pad(vec.reshape(G, 1, E), ((0, 0), (0, 0), (0, LANES - E)))


def _head_rows(vec):
    return jnp.pad(vec.reshape(G, E, 1), ((0, 0), (0, 8 - E), (0, 0)))


def kernel(x, meta_tokens, ssd_norm, ssd_w_in, ssd_conv_w, ssd_conv_b, ssd_dt_bias, ssd_a_log, ssd_d_skip, ssd_gate_norm, ssd_w_out, kv_norm, w_kv, sb_norm, sb_w_q, sb_w_o, ffn_norm, ffn_w_up, ffn_conv_w, ffn_conv_b, ffn_w_down, final_norm, loss_target, m_meta_tokens, m_ssd_norm, m_ssd_w_in, m_ssd_conv_w, m_ssd_conv_b, m_ssd_dt_bias, m_ssd_a_log, m_ssd_d_skip, m_ssd_gate_norm, m_ssd_w_out, m_kv_norm, m_w_kv, m_sb_norm, m_sb_w_q, m_sb_w_o, m_ffn_norm, m_ffn_w_up, m_ffn_conv_w, m_ffn_conv_b, m_ffn_w_down, m_final_norm, v_meta_tokens, v_ssd_norm, v_ssd_w_in, v_ssd_conv_w, v_ssd_conv_b, v_ssd_dt_bias, v_ssd_a_log, v_ssd_d_skip, v_ssd_gate_norm, v_ssd_w_out, v_kv_norm, v_w_kv, v_sb_norm, v_sb_w_q, v_sb_w_o, v_ffn_norm, v_ffn_w_up, v_ffn_conv_w, v_ffn_conv_b, v_ffn_w_down, v_final_norm):
    local = dict(meta_tokens=meta_tokens, ssd_norm=ssd_norm, ssd_w_in=ssd_w_in, ssd_conv_w=ssd_conv_w, ssd_conv_b=ssd_conv_b, ssd_dt_bias=ssd_dt_bias, ssd_a_log=ssd_a_log, ssd_d_skip=ssd_d_skip, ssd_gate_norm=ssd_gate_norm, ssd_w_out=ssd_w_out, kv_norm=kv_norm, w_kv=w_kv, sb_norm=sb_norm, sb_w_q=sb_w_q, sb_w_o=sb_w_o, ffn_norm=ffn_norm, ffn_w_up=ffn_w_up, ffn_conv_w=ffn_conv_w, ffn_conv_b=ffn_conv_b, ffn_w_down=ffn_w_down, final_norm=final_norm)
    mom_m = dict(meta_tokens=m_meta_tokens, ssd_norm=m_ssd_norm, ssd_w_in=m_ssd_w_in, ssd_conv_w=m_ssd_conv_w, ssd_conv_b=m_ssd_conv_b, ssd_dt_bias=m_ssd_dt_bias, ssd_a_log=m_ssd_a_log, ssd_d_skip=m_ssd_d_skip, ssd_gate_norm=m_ssd_gate_norm, ssd_w_out=m_ssd_w_out, kv_norm=m_kv_norm, w_kv=m_w_kv, sb_norm=m_sb_norm, sb_w_q=m_sb_w_q, sb_w_o=m_sb_w_o, ffn_norm=m_ffn_norm, ffn_w_up=m_ffn_w_up, ffn_conv_w=m_ffn_conv_w, ffn_conv_b=m_ffn_conv_b, ffn_w_down=m_ffn_w_down, final_norm=m_final_norm)
    mom_v = dict(meta_tokens=v_meta_tokens, ssd_norm=v_ssd_norm, ssd_w_in=v_ssd_w_in, ssd_conv_w=v_ssd_conv_w, ssd_conv_b=v_ssd_conv_b, ssd_dt_bias=v_ssd_dt_bias, ssd_a_log=v_ssd_a_log, ssd_d_skip=v_ssd_d_skip, ssd_gate_norm=v_ssd_gate_norm, ssd_w_out=v_ssd_w_out, kv_norm=v_kv_norm, w_kv=v_w_kv, sb_norm=v_sb_norm, sb_w_q=v_sb_w_q, sb_w_o=v_sb_w_o, ffn_norm=v_ffn_norm, ffn_w_up=v_ffn_w_up, ffn_conv_w=v_ffn_conv_w, ffn_conv_b=v_ffn_conv_b, ffn_w_down=v_ffn_w_down, final_norm=v_final_norm)

    big_names = [n for n, _, kind in PARAMS if kind == "big"]
    small_names = [n for n, _, kind in PARAMS if kind == "small"]
    rep_names = [n for n, _, kind in PARAMS if kind == "rep"]
    axis_of = {n: ax for n, ax, _ in PARAMS}

    big_own = _pack([local[n] for n in big_names], BF16)
    small_own = _pack([local[n] for n in small_names], F32)
    big_all, small_all = _chip_exchange([big_own, small_own], True, "gather_weights")
    full = {}
    for names, buf in ((big_names, big_all), (small_names, small_all)):
        parts = _unpack(buf, [local[n].shape for n in names])
        for n, p in zip(names, parts):
            full[n] = jnp.concatenate([p[s] for s in range(NCHIP)], axis=axis_of[n])
    for n in rep_names:
        full[n] = local[n]

    w_in = full["ssd_w_in"][0]
    w_z, w_xbc = w_in[:, :DI], w_in[:, DI : DI + CD]
    w_dt = jnp.pad(w_in[:, DI + CD :], ((0, 0), (0, LANES - H)))
    w_out = full["ssd_w_out"][0]
    wkv = full["w_kv"]
    w_q = full["sb_w_q"][0]
    w_o = full["sb_w_o"][0]
    w_up_g = [full["ffn_w_up"][l][:, :DFF] for l in range(2)]
    w_up_v = [full["ffn_w_up"][l][:, DFF:] for l in range(2)]
    w_down = [full["ffn_w_down"][l] for l in range(2)]
    fcw, fcb = full["ffn_conv_w"], full["ffn_conv_b"]
    scw, scb = full["ssd_conv_w"][0], full["ssd_conv_b"]
    bias_c, bias_r = _head_cols(full["ssd_dt_bias"][0]), _head_rows(full["ssd_dt_bias"][0])
    alog_c, alog_r = _head_cols(full["ssd_a_log"][0]), _head_rows(full["ssd_a_log"][0])
    dskip_c = _head_cols(full["ssd_d_skip"][0])
    kvn = full["kv_norm"].reshape(1, D)
    fin = full["final_norm"].reshape(1, D)

    h0 = jnp.concatenate([jnp.zeros((PF, D), F32), full["meta_tokens"], x[0]], axis=0)
    (u0,) = _rms_fwd(h0, [full["ssd_norm"]], "ssd_norm_fwd")
    z = _mm(u0, w_z, name="ssd_in_z")
    xr = _mm(u0, w_xbc, name="ssd_in_xbc")
    dt_raw = _mm(u0, w_dt, name="ssd_in_dt")
    xbc = _ssd_conv_fwd(xr, scw, scb, "ssd_conv_fwd")
    dth = dt_raw[:, :H].reshape(LP, G, E)
    dtc = jnp.pad(jnp.transpose(dth, (1, 0, 2)), ((0, 0), (0, 0), (0, LANES - E)))
    dtr = jnp.pad(jnp.transpose(dth, (1, 2, 0)), ((0, 0), (0, 8 - E), (0, 0)))
    y, states = _ssd_fwd(xbc, dtc, dtr, bias_c, bias_r, alog_c, alog_r, dskip_c, "ssd_scan_fwd")
    hgn = _gate_fwd(y, z, full["ssd_gate_norm"], "ssd_gate_fwd")
    h1 = _mm(hgn, w_out, add=h0, mask_rows=True, name="ssd_out")

    def ffn_fwd(h, l, tag):
        (u,) = _rms_fwd(h, [full["ffn_norm"][l : l + 1]], f"ffn{tag}_norm_fwd")
        hg = _mm(u, w_up_g[l], name=f"ffn{tag}_up_g")
        hv = _mm(u, w_up_v[l], name=f"ffn{tag}_up_v")
        act = _ffn_act_fwd(hg, hv, fcw[l][:, :DFF], fcw[l][:, DFF:], fcb[l : l + 1, :DFF], fcb[l : l + 1, DFF:], f"ffn{tag}_act_fwd")
        hn = _mm(act, w_down[l], add=h, mask_rows=True, name=f"ffn{tag}_down")
        return hn, (u, hg, hv, act)

    h2, ffn0 = ffn_fwd(h1, 0, "0")
    ukv, uq = _rms_fwd(h2, [kvn, full["sb_norm"]], "attn_norm_fwd")
    kk = _mm(ukv, wkv[:, :D], out_dtype=BF16, name="attn_k")
    vv = _mm(ukv, wkv[:, D:], out_dtype=BF16, name="attn_v")
    qq = _mm(uq, w_q, out_dtype=BF16, scale=64.0**-0.5, name="attn_q")
    o = _attn_fwd(qq, kk, vv, "attn_fwd")
    h3 = _mm(o, w_o, add=h2, mask_rows=True, name="attn_out")
    h4, ffn1 = ffn_fwd(h3, 1, "1")
    dh, g_final, loss_rows = _loss_head(h4, fin, loss_target[0], "loss_head")
    loss = lax.psum(0.5 / D * jnp.sum(loss_rows), ("x", "y", "c"))

    grads = {"final_norm": g_final.reshape(D)}

    def ffn_bwd(dh, h, l, saved, tag):
        u, hg, hv, act = saved
        da = _mm(dh, w_down[l], tb=True, name=f"ffn{tag}_down_dx")
        gw_down = _mm(act, dh, ta=True, name=f"ffn{tag}_down_dw")
        dhg, dhv, dwg, dwv, dbg, dbv = _ffn_act_bwd(hg, hv, da, fcw[l][:, :DFF], fcw[l][:, DFF:], fcb[l : l + 1, :DFF], fcb[l : l + 1, DFF:], f"ffn{tag}_act_bwd")
        gw_up = jnp.concatenate([_mm(u, dhg, ta=True, name=f"ffn{tag}_up_g_dw"), _mm(u, dhv, ta=True, name=f"ffn{tag}_up_v_dw")], axis=1)
        du = _mm(dhg, w_up_g[l], tb=True, name=f"ffn{tag}_up_g_dx")
        du = _mm(dhv, w_up_v[l], tb=True, add=du, name=f"ffn{tag}_up_v_dx")
        dh_new, (gn,) = _rms_bwd(dh, h, [du], [full["ffn_norm"][l : l + 1]], f"ffn{tag}_norm_bwd")
        return dh_new, gw_down, gw_up, jnp.concatenate([dwg, dwv], axis=1), jnp.concatenate([dbg, dbv], axis=1), gn

    dh, gd1, gu1, gcw1, gcb1, gn1 = ffn_bwd(dh, h3, 1, ffn1, "1")
    do = _mm(dh, w_o, tb=True, name="attn_out_dx")
    grads["sb_w_o"] = _mm(o, dh, ta=True, name="attn_out_dw")[None]
    dq, dk, dv = _attn_bwd(qq, kk, vv, o, do, "attn_bwd")
    grads["sb_w_q"] = _mm(uq, dq, ta=True, scale=64.0**-0.5, name="attn_q_dw")[None]
    grads["w_kv"] = jnp.concatenate([_mm(ukv, dk, ta=True, name="attn_k_dw"), _mm(ukv, dv, ta=True, name="attn_v_dw")], axis=1)
    duq = _mm(dq, w_q, tb=True, scale=64.0**-0.5, name="attn_q_dx")
    dukv = _mm(dk, wkv[:, :D], tb=True, name="attn_k_dx")
    dukv = _mm(dv, wkv[:, D:], tb=True, add=dukv, name="attn_v_dx")
    dh, (g_kvn, g_sbn) = _rms_bwd(dh, h2, [dukv, duq], [kvn, full["sb_norm"]], "attn_norm_bwd")
    grads["kv_norm"] = g_kvn.reshape(D)
    grads["sb_norm"] = g_sbn
    dh, gd0, gu0, gcw0, gcb0, gn0 = ffn_bwd(dh, h1, 0, ffn0, "0")
    grads["ffn_w_down"] = jnp.stack([gd0, gd1])
    grads["ffn_w_up"] = jnp.stack([gu0, gu1])
    grads["ffn_conv_w"] = jnp.stack([gcw0, gcw1])
    grads["ffn_conv_b"] = jnp.concatenate([gcb0, gcb1], axis=0)
    grads["ffn_norm"] = jnp.concatenate([gn0, gn1], axis=0)
    dhgn = _mm(dh, w_out, tb=True, name="ssd_out_dx")
    grads["ssd_w_out"] = _mm(hgn, dh, ta=True, name="ssd_out_dw")[None]
    dy, dz, g_gate = _gate_bwd(dhgn, y, z, full["ssd_gate_norm"], "ssd_gate_bwd")
    grads["ssd_gate_norm"] = g_gate
    dxs, dB, dC, ddt_raw, g_bias, g_alog, g_dskip = _ssd_bwd(xbc, dy, states, dtc, dtr, bias_c, bias_r, alog_c, alog_r, dskip_c, "ssd_scan_bwd")
    grads["ssd_dt_bias"] = g_bias[:, 0, :E].reshape(1, H)
    grads["ssd_a_log"] = g_alog[:, 0, :E].reshape(1, H)
    grads["ssd_d_skip"] = g_dskip[:, 0, :E].reshape(1, H)
    dxr, g_scw, g_scb = _ssd_conv_bwd(xr, jnp.concatenate([dxs, dB, dC], axis=1), scw, scb, "ssd_conv_bwd")
    grads["ssd_conv_w"] = g_scw[None]
    grads["ssd_conv_b"] = g_scb
    ddt = jnp.pad(jnp.transpose(ddt_raw[:, :, :E], (1, 0, 2)).reshape(LP, H), ((0, 0), (0, LANES - H)))
    grads["ssd_w_in"] = jnp.concatenate(
        [_mm(u0, dz, ta=True, name="ssd_in_z_dw"), _mm(u0, dxr, ta=True, name="ssd_in_xbc_dw"), _mm(u0, ddt, ta=True, name="ssd_in_dt_dw")[:, :H]], axis=1)[None]
    du = _mm(dz, w_z, tb=True, name="ssd_in_z_dx")
    du = _mm(dxr, w_xbc, tb=True, add=du, name="ssd_in_xbc_dx")
    du = _mm(ddt, w_dt, tb=True, add=du, name="ssd_in_dt_dx")
    dh, (g_ssdn,) = _rms_bwd(dh, h0, [du], [full["ssd_norm"]], "ssd_norm_bwd")
    grads["ssd_norm"] = g_ssdn
    grads["meta_tokens"] = dh[PF : PF + N_META]
    grad_x = dh[PF + N_META :][None]

    def shard_pieces(names, s):
        out = []
        for n in names:
            ax = axis_of[n]
            out.append(grads[n] if ax is None else jnp.split(grads[n], NCHIP, axis=ax)[s])
        return out

    g_big = jnp.stack([_pack(shard_pieces(big_names, s), BF16) for s in range(NCHIP)])
    g_small = jnp.stack([_pack(shard_pieces(small_names + rep_names, s), F32) for s in range(NCHIP)])
    r_big, r_small = _chip_exchange([g_big, g_small], False, "scatter_grads")
    s_big, s_small = _sum4([r_big, r_small], "sum_chips")
    o_big, o_small = _sibling_exchange([s_big, s_small], "swap_cores")

    def rows(a):
        f = a.reshape(-1)
        pad = (-f.shape[0]) % LANES
        if pad:
            f = jnp.pad(f, (0, pad))
        return f.reshape(-1, LANES)

    order = [n for n, _, _ in PARAMS]
    res = {}
    for names, mine, other in ((big_names, s_big, o_big), (small_names + rep_names, s_small, o_small)):
        shapes = [local[n].shape for n in names]
        for n, ga, gb in zip(names, _unpack(mine, shapes), _unpack(other, shapes)):
            shp = local[n].shape
            cnt = 1
            for d in shp:
                cnt *= d
            outs = _adamw(rows(local[n]), rows(ga), rows(gb), rows(mom_m[n]), rows(mom_v[n]), f"adamw_{n}")
            res[n] = [o_.reshape(-1)[:cnt].reshape(shp) for o_ in outs]
    return (loss, grad_x, *[res[n][0] for n in order], *[res[n][1] for n in order], *[res[n][2] for n in order], *[res[n][3] for n in order])
```

```python
import functools

import jax
import jax.numpy as jnp
from jax import lax
from jax.experimental import pallas as pl
from jax.experimental.pallas import tpu as pltpu

D = 1024
SEQ = 8192
N_META = 16
EPS = 1e-6
P = 64
G = 4
N = 128
CONVW = 4
Q = 256
FC = 3
DFF = 256 * ((8 * D // 3 + 255) // 256)
DI = 2 * D
H = DI // P
E = H // G
GW = E * P
CD = DI + 2 * G * N
IN = DI + CD + H
SBH = D // 64
HP = 128
LANES = 128
PF = Q - N_META
LP = PF + N_META + SEQ
NC = LP // Q
TQ = 256
NCHIP = 4
ADAM_LR, ADAM_B1, ADAM_B2, ADAM_EPS, ADAM_WD, ADAM_STEP = 0.001, 0.9, 0.999, 1e-08, 0.01, 10

F32 = jnp.float32
BF16 = jnp.bfloat16
HI = lax.Precision.HIGHEST
MESH = pl.DeviceIdType.MESH
VMEM_LIMIT = 48 * 1024 * 1024
MM_MAX_K = 3072
T_SKIP = 110.0


def _pick(n, cands):
    for c in cands:
        if n % c == 0:
            return c
    raise ValueError((n, cands))


def _cparams(sem):
    return pltpu.CompilerParams(dimension_semantics=sem, vmem_limit_bytes=VMEM_LIMIT)


def _valid_rows(block, rows):
    r = block * rows + lax.broadcasted_iota(jnp.int32, (rows, 1), 0)
    return r >= PF


def _sigmoid(x):
    return 1.0 / (1.0 + jnp.exp(-x))


def _softplus(x):
    return jnp.maximum(x, 0.0) + jnp.log(1.0 + jnp.exp(-jnp.abs(x)))


def _sum_all(x):
    return jnp.sum(jnp.sum(x, axis=1, keepdims=True), axis=0, keepdims=True)


def _dsilu(x):
    s = _sigmoid(x)
    return s * (1.0 + x * (1.0 - s))


def _mm(a, b, *, ta=False, tb=False, out_dtype=F32, add=None, mask_rows=False, scale=None, name):
    if ta:
        K, M = a.shape
    else:
        M, K = a.shape
    if tb:
        Nn, K2 = b.shape
    else:
        K2, Nn = b.shape
    assert K == K2, (a.shape, b.shape, ta, tb)
    tn = _pick(Nn, (1408, 1024, 768, 512, 256, 128))
    if ta:
        tm = _pick(M, (1408, 1024, 768, 512, 256, 128))
        tk = _pick(K, (768, 512, 256))
    else:
        tm = _pick(M, (768, 256))
        tk = K if K <= MM_MAX_K else _pick(K, (1024, 768, 512, 256, 128))
    nk = K // tk
    dims = (((0 if ta else 1,), (1 if tb else 0,)), ((), ()))

    def body(*refs):
        a_ref, b_ref = refs[0], refs[1]
        add_ref = refs[2] if add is not None else None
        o_ref = refs[3] if add is not None else refs[2]
        acc = refs[-1] if nk > 1 else None

        def finish(r):
            if scale is not None:
                r = r * scale
            if mask_rows:
                r = jnp.where(_valid_rows(pl.program_id(0), tm), r, 0.0)
            if add_ref is not None:
                r = r + add_ref[...]
            o_ref[...] = r.astype(out_dtype)

        part = lax.dot_general(a_ref[...].astype(BF16), b_ref[...].astype(BF16), dims, preferred_element_type=F32)
        if nk == 1:
            finish(part)
        else:
            k = pl.program_id(2)

            @pl.when(k == 0)
            def _():
                acc[...] = part

            @pl.when(k > 0)
            def _():
                acc[...] += part

            @pl.when(k == nk - 1)
            def _():
                finish(acc[...])

    a_spec = pl.BlockSpec((tk, tm), lambda i, j, k: (k, i)) if ta else pl.BlockSpec((tm, tk), lambda i, j, k: (i, k))
    b_spec = pl.BlockSpec((tn, tk), lambda i, j, k: (j, k)) if tb else pl.BlockSpec((tk, tn), lambda i, j, k: (k, j))
    o_spec = pl.BlockSpec((tm, tn), lambda i, j, k: (i, j))
    in_specs = [a_spec, b_spec] + ([o_spec] if add is not None else [])
    args = (a, b) + ((add,) if add is not None else ())
    return pl.pallas_call(
        body,
        name=name,
        grid=(M // tm, Nn // tn, nk),
        in_specs=in_specs,
        out_specs=o_spec,
        out_shape=jax.ShapeDtypeStruct((M, Nn), out_dtype),
        scratch_shapes=[pltpu.VMEM((tm, tn), F32)] if nk > 1 else [],
        compiler_params=_cparams(("parallel", "parallel", "arbitrary")),
    )(*args)


def _rms_fwd(h, gains, name):
    tr = _pick(LP, (768, 256))
    ng = len(gains)

    def body(*refs):
        h_ref = refs[0]
        g_refs = refs[1 : 1 + ng]
        o_refs = refs[1 + ng :]
        x = h_ref[...]
        xh = x * lax.rsqrt(jnp.mean(x * x, axis=-1, keepdims=True) + EPS)
        for g_ref, o_ref in zip(g_refs, o_refs):
            o_ref[...] = (xh * g_ref[...]).astype(BF16)

    row = pl.BlockSpec((tr, D), lambda i: (i, 0))
    vec = pl.BlockSpec((1, D), lambda i: (0, 0))
    outs = pl.pallas_call(
        body,
        name=name,
        grid=(LP // tr,),
        in_specs=[row] + [vec] * ng,
        out_specs=[row] * ng,
        out_shape=[jax.ShapeDtypeStruct((LP, D), BF16)] * ng,
        compiler_params=_cparams(("parallel",)),
    )(h, *gains)
    return outs


def _rms_bwd(dh_in, h, dus, gains, name):
    tr = _pick(LP, (256,))
    ng = len(gains)

    def body(*refs):
        dh_ref, h_ref = refs[0], refs[1]
        du_refs = refs[2 : 2 + ng]
        g_refs = refs[2 + ng : 2 + 2 * ng]
        o_ref = refs[2 + 2 * ng]
        dg_refs = refs[3 + 2 * ng :]
        i = pl.program_id(0)
        x = h_ref[...]
        r = lax.rsqrt(jnp.mean(x * x, axis=-1, keepdims=True) + EPS)
        xh = x * r
        tot = dh_ref[...]
        for du_ref, g_ref, dg_ref in zip(du_refs, g_refs, dg_refs):
            du = du_ref[...]
            dxh = du * g_ref[...]
            tot = tot + r * (dxh - xh * jnp.mean(dxh * xh, axis=-1, keepdims=True))

            @pl.when(i == 0)
            def _():
                dg_ref[...] = jnp.zeros_like(dg_ref)

            dg_ref[...] += jnp.sum(du * xh, axis=0, keepdims=True)
        o_ref[...] = jnp.where(_valid_rows(i, tr), tot, 0.0)

    row = pl.BlockSpec((tr, D), lambda i: (i, 0))
    vec = pl.BlockSpec((1, D), lambda i: (0, 0))
    outs = pl.pallas_call(
        body,
        name=name,
        grid=(LP // tr,),
        in_specs=[row, row] + [row] * ng + [vec] * ng,
        out_specs=[row] + [vec] * ng,
        out_shape=[jax.ShapeDtypeStruct((LP, D), F32)] + [jax.ShapeDtypeStruct((1, D), F32)] * ng,
        compiler_params=_cparams(("arbitrary",)),
    )(dh_in, h, *dus, *gains)
    return outs[0], outs[1:]


def _loss_head(h, gain, target, name):
    tr = Q

    def body(h_ref, g_ref, t_ref, dh_ref, dg_ref, ls_ref):
        i = pl.program_id(0)

        @pl.when(i == 0)
        def _():
            dg_ref[...] = jnp.zeros_like(dg_ref)
            ls_ref[...] = jnp.zeros_like(ls_ref)
            dh_ref[...] = jnp.zeros_like(dh_ref)

        @pl.when(i > 0)
        def _():
            x = h_ref[...]
            g = g_ref[...]
            r = lax.rsqrt(jnp.mean(x * x, axis=-1, keepdims=True) + EPS)
            xh = x * r
            e = xh * g - t_ref[...]
            ls_ref[...] += jnp.sum(e * e, axis=0, keepdims=True)
            dy = e * (1.0 / D)
            dg_ref[...] += jnp.sum(dy * xh, axis=0, keepdims=True)
            dxh = dy * g
            dh_ref[...] = r * (dxh - xh * jnp.mean(dxh * xh, axis=-1, keepdims=True))

    row = pl.BlockSpec((tr, D), lambda i: (i, 0))
    vec = pl.BlockSpec((1, D), lambda i: (0, 0))
    return pl.pallas_call(
        body,
        name=name,
        grid=(LP // tr,),
        in_specs=[row, vec, pl.BlockSpec((tr, D), lambda i: (jnp.maximum(i - 1, 0), 0))],
        out_specs=[row, vec, vec],
        out_shape=[jax.ShapeDtypeStruct((LP, D), F32), jax.ShapeDtypeStruct((1, D), F32), jax.ShapeDtypeStruct((1, D), F32)],
        compiler_params=_cparams(("arbitrary",)),
    )(h, gain, target)


HALO = 8
CONV_COLS = (1536, 1408, 768, 512, 256)


def _conv_rows(ext, w, b, width):
    n = ext.shape[0]
    acc = b + w[width - 1 : width, :] * ext[HALO:]
    for k in range(width - 1):
        acc = acc + w[k : k + 1, :] * pltpu.roll(ext, width - 1 - k, 0)[HALO:]
    return acc


def _conv_specs(tr, tn, col):
    per = tr // HALO
    last = LP // HALO - 1
    prev = pl.BlockSpec((HALO, tn), lambda j, i: (jnp.maximum(i * per - 1, 0), col(j)))
    cur = pl.BlockSpec((tr, tn), lambda j, i: (i, col(j)))
    nxt = pl.BlockSpec((HALO, tn), lambda j, i: (jnp.minimum((i + 1) * per, last), col(j)))
    return prev, cur, nxt


def _conv_bwd_core(ext, dact_fn, w, b, width, i, nblk, tr):
    pre = _conv_rows(ext, w, b, width)
    dpre = dact_fn(pre)
    rows = i * tr + lax.broadcasted_iota(jnp.int32, (tr + HALO, 1), 0)
    dpre = jnp.where((rows >= PF) & (rows < LP), dpre, 0.0)
    n = tr + HALO
    dx = w[width - 1 : width, :] * dpre[:tr]
    for k in range(width - 1):
        sh = width - 1 - k
        dx = dx + w[k : k + 1, :] * pltpu.roll(dpre, n - sh, 0)[:tr]
    dcur = dpre[:tr]
    dws = []
    for k in range(width):
        sh = width - 1 - k
        xs = ext[HALO : HALO + tr] if sh == 0 else pltpu.roll(ext, sh, 0)[HALO : HALO + tr]
        dws.append(jnp.sum(xs * dcur, axis=0, keepdims=True))
    db = jnp.sum(dcur, axis=0, keepdims=True)
    dx = jnp.where(_valid_rows(i, tr), dx, 0.0)
    return dx, dws, db


def _ssd_conv_fwd(xr, cw, cb, name):
    tr, tn = Q, _pick(CD, CONV_COLS)

    def body(p_ref, c_ref, w_ref, b_ref, o_ref):
        i = pl.program_id(1)
        ext = jnp.concatenate([jnp.where(i > 0, p_ref[...], 0.0), c_ref[...]], axis=0)
        pre = _conv_rows(ext, w_ref[...], b_ref[...], CONVW)
        o_ref[...] = jnp.where(_valid_rows(i, tr), pre * _sigmoid(pre), 0.0)

    prev, cur, _ = _conv_specs(tr, tn, lambda j: j)
    return pl.pallas_call(
        body,
        name=name,
        grid=(CD // tn, LP // tr),
        in_specs=[prev, cur, pl.BlockSpec((CONVW, tn), lambda j, i: (0, j)), pl.BlockSpec((1, tn), lambda j, i: (0, j))],
        out_specs=cur,
        out_shape=jax.ShapeDtypeStruct((LP, CD), F32),
        compiler_params=_cparams(("parallel", "arbitrary")),
    )(xr, xr, cw, cb)


def _ssd_conv_bwd(xr, dxbc, cw, cb, name):
    tr, tn = Q, _pick(CD, CONV_COLS)
    nblk = LP // tr

    def body(p_ref, c_ref, n_ref, dc_ref, dn_ref, w_ref, b_ref, dx_ref, dw_ref, db_ref):
        i = pl.program_id(1)
        ext = jnp.concatenate([jnp.where(i > 0, p_ref[...], 0.0), c_ref[...], n_ref[...]], axis=0)
        dout = jnp.concatenate([dc_ref[...], dn_ref[...]], axis=0)
        dx, dws, db = _conv_bwd_core(ext, lambda pre: dout * _dsilu(pre), w_ref[...], b_ref[...], CONVW, i, nblk, tr)
        dx_ref[...] = dx

        @pl.when(i == 0)
        def _():
            dw_ref[...] = jnp.zeros_like(dw_ref)
            db_ref[...] = jnp.zeros_like(db_ref)

        for k in range(CONVW):
            dw_ref[k : k + 1, :] += dws[k]
        db_ref[...] += db

    prev, cur, nxt = _conv_specs(tr, tn, lambda j: j)
    wspec = pl.BlockSpec((CONVW, tn), lambda j, i: (0, j))
    bspec = pl.BlockSpec((1, tn), lambda j, i: (0, j))
    return pl.pallas_call(
        body,
        name=name,
        grid=(CD // tn, LP // tr),
        in_specs=[prev, cur, nxt, cur, nxt, wspec, bspec],
        out_specs=[cur, wspec, bspec],
        out_shape=[jax.ShapeDtypeStruct((LP, CD), F32), jax.ShapeDtypeStruct((CONVW, CD), F32), jax.ShapeDtypeStruct((1, CD), F32)],
        compiler_params=_cparams(("parallel", "arbitrary")),
    )(xr, xr, xr, dxbc, dxbc, cw, cb)


def _ffn_act_fwd(hg, hv, cwg, cwv, cbg, cbv, name):
    tr, tn = Q, _pick(DFF, CONV_COLS)

    def body(pg, cg, pv, cv, wg, wv, bg, bv, o_ref):
        i = pl.program_id(1)
        eg = jnp.concatenate([jnp.where(i > 0, pg[...], 0.0), cg[...]], axis=0)
        ev = jnp.concatenate([jnp.where(i > 0, pv[...], 0.0), cv[...]], axis=0)
        gate = _conv_rows(eg, wg[...], bg[...], FC)
        val = _conv_rows(ev, wv[...], bv[...], FC)
        o_ref[...] = (gate * _sigmoid(gate) * val).astype(BF16)

    prev, cur, _ = _conv_specs(tr, tn, lambda j: j)
    wspec = pl.BlockSpec((FC, tn), lambda j, i: (0, j))
    bspec = pl.BlockSpec((1, tn), lambda j, i: (0, j))
    return pl.pallas_call(
        body,
        name=name,
        grid=(DFF // tn, LP // tr),
        in_specs=[prev, cur, prev, cur, wspec, wspec, bspec, bspec],
        out_specs=cur,
        out_shape=jax.ShapeDtypeStruct((LP, DFF), BF16),
        compiler_params=_cparams(("parallel", "arbitrary")),
    )(hg, hg, hv, hv, cwg, cwv, cbg, cbv)


def _ffn_act_bwd(hg, hv, da, cwg, cwv, cbg, cbv, name):
    tr, tn = Q, _pick(DFF, CONV_COLS)
    nblk = LP // tr

    def body(pg, cg, ng, pv, cv, nv, dc, dn, wg, wv, bg, bv, dg_ref, dv_ref, dwg, dwv, dbg, dbv):
        i = pl.program_id(1)
        eg = jnp.concatenate([jnp.where(i > 0, pg[...], 0.0), cg[...], ng[...]], axis=0)
        ev = jnp.concatenate([jnp.where(i > 0, pv[...], 0.0), cv[...], nv[...]], axis=0)
        dout = jnp.concatenate([dc[...], dn[...]], axis=0)
        gate = _conv_rows(eg, wg[...], bg[...], FC)
        val = _conv_rows(ev, wv[...], bv[...], FC)
        dxg, dwsg, dbgv = _conv_bwd_core(eg, lambda pre: dout * val * _dsilu(pre), wg[...], bg[...], FC, i, nblk, tr)
        dxv, dwsv, dbvv = _conv_bwd_core(ev, lambda pre: dout * gate * _sigmoid(gate), wv[...], bv[...], FC, i, nblk, tr)
        dg_ref[...] = dxg
        dv_ref[...] = dxv

        @pl.when(i == 0)
        def _():
            dwg[...] = jnp.zeros_like(dwg)
            dwv[...] = jnp.zeros_like(dwv)
            dbg[...] = jnp.zeros_like(dbg)
            dbv[...] = jnp.zeros_like(dbv)

        for k in range(FC):
            dwg[k : k + 1, :] += dwsg[k]
            dwv[k : k + 1, :] += dwsv[k]
        dbg[...] += dbgv
        dbv[...] += dbvv

    prev, cur, nxt = _conv_specs(tr, tn, lambda j: j)
    wspec = pl.BlockSpec((FC, tn), lambda j, i: (0, j))
    bspec = pl.BlockSpec((1, tn), lambda j, i: (0, j))
    big = jax.ShapeDtypeStruct((LP, DFF), F32)
    wsh = jax.ShapeDtypeStruct((FC, DFF), F32)
    bsh = jax.ShapeDtypeStruct((1, DFF), F32)
    return pl.pallas_call(
        body,
        name=name,
        grid=(DFF // tn, LP // tr),
        in_specs=[prev, cur, nxt, prev, cur, nxt, cur, nxt, wspec, wspec, bspec, bspec],
        out_specs=[cur, cur, wspec, wspec, bspec, bspec],
        out_shape=[big, big, wsh, wsh, bsh, bsh],
        compiler_params=_cparams(("parallel", "arbitrary")),
    )(hg, hg, hg, hv, hv, hv, da, da, cwg, cwv, cbg, cbv)


def _gate_fwd(y, z, gg, name):
    tr = _pick(LP, (768, 256))

    def body(y_ref, z_ref, g_ref, o_ref):
        zv = z_ref[...]
        hg = y_ref[...] * zv * _sigmoid(zv)
        r = lax.rsqrt(jnp.mean(hg * hg, axis=-1, keepdims=True) + EPS)
        o_ref[...] = (hg * r * g_ref[...]).astype(BF16)

    blk = pl.BlockSpec((tr, GW), lambda i, g: (i, g))
    return pl.pallas_call(
        body,
        name=name,
        grid=(LP // tr, G),
        in_specs=[blk, blk, pl.BlockSpec((1, GW), lambda i, g: (0, g))],
        out_specs=blk,
        out_shape=jax.ShapeDtypeStruct((LP, DI), BF16),
        compiler_params=_cparams(("parallel", "parallel")),
    )(y, z, gg)


def _gate_bwd(dout, y, z, gg, name):
    tr = _pick(LP, (768, 256))

    def body(do_ref, y_ref, z_ref, g_ref, dy_ref, dz_ref, dg_ref):
        i = pl.program_id(1)
        zv = z_ref[...]
        yv = y_ref[...]
        sz = zv * _sigmoid(zv)
        hg = yv * sz
        r = lax.rsqrt(jnp.mean(hg * hg, axis=-1, keepdims=True) + EPS)
        hh = hg * r
        do = do_ref[...]
        dhh = do * g_ref[...]
        dhg = r * (dhh - hh * jnp.mean(dhh * hh, axis=-1, keepdims=True))
        dy_ref[...] = dhg * sz
        dz_ref[...] = dhg * yv * _dsilu(zv)

        @pl.when(i == 0)
        def _():
            dg_ref[...] = jnp.zeros_like(dg_ref)

        dg_ref[...] += jnp.sum(do * hh, axis=0, keepdims=True)

    blk = pl.BlockSpec((tr, GW), lambda g, i: (i, g))
    vec = pl.BlockSpec((1, GW), lambda g, i: (0, g))
    big = jax.ShapeDtypeStruct((LP, DI), F32)
    return pl.pallas_call(
        body,
        name=name,
        grid=(G, LP // tr),
        in_specs=[blk, blk, blk, vec],
        out_specs=[blk, blk, vec],
        out_shape=[big, big, jax.ShapeDtypeStruct((1, DI), F32)],
        compiler_params=_cparams(("parallel", "arbitrary")),
    )(dout, y, z, gg)


def _ssd_common(dtc_ref, dtr_ref, bc_ref, br_ref, ac_ref, ar_ref, c):
    rows = c * Q + lax.broadcasted_iota(jnp.int32, (Q, 1), 0)
    cols = c * Q + lax.broadcasted_iota(jnp.int32, (1, Q), 1)
    prec = dtc_ref[...] + bc_ref[...]
    prer = dtr_ref[...] + br_ref[...]
    dtc = jnp.where(rows >= PF, _softplus(prec), 0.0)
    dtr = jnp.where(cols >= PF, _softplus(prer), 0.0)
    a_c = -jnp.exp(ac_ref[...])
    a_r = -jnp.exp(ar_ref[...])
    li = lax.broadcasted_iota(jnp.int32, (Q, Q), 0)
    si = lax.broadcasted_iota(jnp.int32, (Q, Q), 1)
    tril = si <= li
    trif = tril.astype(F32)
    csc = jnp.dot(trif, dtc * a_c, precision=HI, preferred_element_type=F32)
    csr = lax.dot_general(dtr * a_r, trif, (((1,), (1,)), ((), ())), precision=HI, preferred_element_type=F32)
    return dict(rows=rows, prec=prec, dtc=dtc, a_c=a_c, tril=tril, trif=trif, csc=csc, csr=csr, li=li, si=si)


def _pair_expand(arr, h0, lane_lo):
    return jnp.where(lane_lo, arr[:, h0 : h0 + 1], arr[:, h0 + 1 : h0 + 2])


def _ssd_specs():
    nb = DI // N
    xs = pl.BlockSpec((Q, GW), lambda g, c: (c, g))
    bb = pl.BlockSpec((Q, N), lambda g, c: (c, nb + g))
    cc = pl.BlockSpec((Q, N), lambda g, c: (c, nb + G + g))
    dtc = pl.BlockSpec((None, Q, LANES), lambda g, c: (g, c, 0))
    dtr = pl.BlockSpec((None, 8, Q), lambda g, c: (g, 0, c))
    pc = pl.BlockSpec((None, 1, LANES), lambda g, c: (g, 0, 0))
    pr = pl.BlockSpec((None, 8, 1), lambda g, c: (g, 0, 0))
    return xs, bb, cc, dtc, dtr, pc, pr


def _ssd_fwd(xbc, dtc, dtr, bias_c, bias_r, alog_c, alog_r, dskip_c, name):
    def body(xs_ref, b_ref, c_ref, dtc_ref, dtr_ref, bc_ref, br_ref, ac_ref, ar_ref, dk_ref, y_ref, st_ref, state):
        c = pl.program_id(1)

        @pl.when(c == 0)
        def _():
            state[...] = jnp.zeros_like(state)

        st_ref[...] = state[...]
        cm = _ssd_common(dtc_ref, dtr_ref, bc_ref, br_ref, ac_ref, ar_ref, c)
        Bm = b_ref[...]
        Cm = c_ref[...]
        cb = lax.dot_general(Cm.astype(BF16), Bm.astype(BF16), (((1,), (1,)), ((), ())), preferred_element_type=F32)
        bt = Bm.T.astype(BF16)
        lane_lo = lax.broadcasted_iota(jnp.int32, (1, HP), 1) < P
        csc, csr, dtc_v = cm["csc"], cm["csr"], cm["dtc"]
        ecs = jnp.exp(csc)
        cs_end = csc[Q - 1 : Q, :]
        wdec = jnp.exp(cs_end - csc)
        eend = jnp.exp(cs_end)
        for pp in range(E // 2):
            h0 = 2 * pp
            sl = slice(pp * HP, (pp + 1) * HP)
            xp = xs_ref[:, sl]
            xdt = xp * _pair_expand(dtc_v, h0, lane_lo)
            yacc = xp * _pair_expand(dk_ref[...], h0, lane_lo)
            for e in range(2):
                h = h0 + e
                lm = jnp.where(cm["tril"], jnp.exp(jnp.minimum(csc[:, h : h + 1] - csr[h : h + 1, :], 0.0)), 0.0)
                m = (cb * lm).astype(BF16)
                xm = jnp.where(lane_lo if e == 0 else jnp.logical_not(lane_lo), xdt, 0.0).astype(BF16)
                yacc = yacc + jnp.dot(m, xm, preferred_element_type=F32)
            stp = state[:, sl]
            yoff = jnp.dot(Cm.astype(BF16), stp.astype(BF16), preferred_element_type=F32)
            y_ref[:, sl] = yacc + yoff * _pair_expand(ecs, h0, lane_lo)
            xw = (xdt * _pair_expand(wdec, h0, lane_lo)).astype(BF16)
            state[:, sl] = stp * _pair_expand(eend, h0, lane_lo) + jnp.dot(bt, xw, preferred_element_type=F32)

    xs, bb, cc, dtcs, dtrs, pc, pr = _ssd_specs()
    return pl.pallas_call(
        body,
        name=name,
        grid=(G, NC),
        in_specs=[xs, bb, cc, dtcs, dtrs, pc, pr, pc, pr, pc],
        out_specs=[xs, pl.BlockSpec((None, None, N, GW), lambda g, c: (c, g, 0, 0))],
        out_shape=[jax.ShapeDtypeStruct((LP, DI), F32), jax.ShapeDtypeStruct((NC, G, N, GW), F32)],
        scratch_shapes=[pltpu.VMEM((N, GW), F32)],
        compiler_params=_cparams(("parallel", "arbitrary")),
    )(xbc, xbc, xbc, dtc, dtr, bias_c, bias_r, alog_c, alog_r, dskip_c)


def _ssd_bwd(xbc, dy, states, dtc, dtr, bias_c, bias_r, alog_c, alog_r, dskip_c, name):
    def body(xs_ref, b_ref, c_ref, dy_ref, st_ref, dtc_ref, dtr_ref, bc_ref, br_ref, ac_ref, ar_ref, dk_ref,
             dx_ref, db_ref, dc_ref, ddt_ref, dbias_ref, dalog_ref, ddk_ref, dstate):
        ci = pl.program_id(1)
        c = NC - 1 - ci

        @pl.when(ci == 0)
        def _():
            dstate[...] = jnp.zeros_like(dstate)
            dbias_ref[...] = jnp.zeros_like(dbias_ref)
            dalog_ref[...] = jnp.zeros_like(dalog_ref)
            ddk_ref[...] = jnp.zeros_like(ddk_ref)

        cm = _ssd_common(dtc_ref, dtr_ref, bc_ref, br_ref, ac_ref, ar_ref, c)
        Bm = b_ref[...]
        Cm = c_ref[...]
        Bb = Bm.astype(BF16)
        Cb = Cm.astype(BF16)
        nt = (((1,), (1,)), ((), ()))
        cb = lax.dot_general(Cb, Bb, nt, preferred_element_type=F32)
        cbt = lax.dot_general(Bb, Cb, nt, preferred_element_type=F32)
        ct = Cm.T.astype(BF16)
        lane_lo = lax.broadcasted_iota(jnp.int32, (1, HP), 1) < P
        lane_id = lax.broadcasted_iota(jnp.int32, (1, LANES), 1)
        csc, csr, dtc_v, a_c = cm["csc"], cm["csr"], cm["dtc"], cm["a_c"]
        triu = cm["si"] >= cm["li"]
        ecs = jnp.exp(csc)
        cs_end = csc[Q - 1 : Q, :]
        wdec = jnp.exp(cs_end - csc)
        eend = jnp.exp(cs_end)
        dcb = jnp.zeros((Q, Q), F32)
        dcbt = jnp.zeros((Q, Q), F32)
        dcs = jnp.zeros((Q, LANES), F32)
        dcs_end = jnp.zeros((1, LANES), F32)
        ddt = jnp.zeros((Q, LANES), F32)
        ddk = jnp.zeros((1, LANES), F32)
        dB = jnp.zeros((Q, N), F32)
        dC = jnp.zeros((Q, N), F32)
        for pp in range(E // 2):
            h0 = 2 * pp
            sl = slice(pp * HP, (pp + 1) * HP)
            xp = xs_ref[:, sl]
            dyp = dy_ref[:, sl]
            dtx = _pair_expand(dtc_v, h0, lane_lo)
            xdt = xp * dtx
            dxdt = jnp.zeros((Q, HP), F32)
            stp = st_ref[:, sl]
            stb = stp.astype(BF16)
            dsn = dstate[:, sl]
            dsnb = dsn.astype(BF16)
            ecsx = _pair_expand(ecs, h0, lane_lo)
            wdx = _pair_expand(wdec, h0, lane_lo)
            cs_ = jnp.dot(Cb, stb, preferred_element_type=F32)
            yo = cs_ * ecsx
            dyo = dyp * ecsx
            dyob = dyo.astype(BF16)
            dC = dC + lax.dot_general(dyob, stb, nt, preferred_element_type=F32)
            ds_from_y = jnp.dot(ct, dyob, preferred_element_type=F32)
            dyyo = dyp * yo
            xw = xdt * wdx
            dB = dB + lax.dot_general(xw.astype(BF16), dsnb, nt, preferred_element_type=F32)
            dxw = jnp.dot(Bb, dsnb, preferred_element_type=F32)
            dxdt = dxdt + dxw * wdx
            w2 = dxw * xw
            rs = jnp.sum(dsn * stp, axis=0, keepdims=True) * _pair_expand(eend, h0, lane_lo)
            dstate[:, sl] = dsn * _pair_expand(eend, h0, lane_lo) + ds_from_y
            dyx = dyp * xp
            for e in range(2):
                h = h0 + e
                msk = lane_lo if e == 0 else jnp.logical_not(lane_lo)
                oh = (lane_id == h).astype(F32)
                col = csc[:, h : h + 1]
                row = csr[h : h + 1, :]
                lm = jnp.where(cm["tril"], jnp.exp(jnp.minimum(col - row, 0.0)), 0.0)
                lmt = jnp.where(triu, jnp.exp(jnp.minimum(row - col, 0.0)), 0.0)
                dye = jnp.where(msk, dyp, 0.0).astype(BF16)
                xde = jnp.where(msk, xdt, 0.0).astype(BF16)
                gm = lax.dot_general(dye, xde, nt, preferred_element_type=F32)
                gmt = lax.dot_general(xde, dye, nt, preferred_element_type=F32)
                gl = gm * lm
                glt = gmt * lmt
                dcb = dcb + gl
                dcbt = dcbt + glt
                dcs_h = (
                    jnp.sum(gl * cb, axis=1, keepdims=True)
                    - jnp.sum(glt * cbt, axis=1, keepdims=True)
                    + jnp.sum(jnp.where(msk, dyyo, 0.0), axis=1, keepdims=True)
                    - jnp.sum(jnp.where(msk, w2, 0.0), axis=1, keepdims=True)
                )
                dcs = dcs + dcs_h * oh
                dcs_end = dcs_end + (_sum_all(jnp.where(msk, w2, 0.0)) + _sum_all(jnp.where(msk, rs, 0.0))) * oh
                dxdt = dxdt + jnp.dot((cbt * lmt).astype(BF16), dye, preferred_element_type=F32)
                ddk = ddk + _sum_all(jnp.where(msk, dyx, 0.0)) * oh
            dx_ref[:, sl] = dxdt * dtx + dyp * _pair_expand(dk_ref[...], h0, lane_lo)
            dxx = dxdt * xp
            for e in range(2):
                h = h0 + e
                msk = lane_lo if e == 0 else jnp.logical_not(lane_lo)
                oh = (lane_id == h).astype(F32)
                ddt = ddt + jnp.sum(jnp.where(msk, dxx, 0.0), axis=1, keepdims=True) * oh
        dC = dC + jnp.dot(dcb.astype(BF16), Bb, preferred_element_type=F32)
        dB = dB + jnp.dot(dcbt.astype(BF16), Cb, preferred_element_type=F32)
        db_ref[...] = dB
        dc_ref[...] = dC
        last = (lax.broadcasted_iota(jnp.int32, (Q, 1), 0) == Q - 1).astype(F32)
        dcs = dcs + last * dcs_end
        dda = jnp.dot(triu.astype(F32), dcs, precision=HI, preferred_element_type=F32)
        ddt = ddt + dda * a_c
        da = jnp.sum(dda * dtc_v, axis=0, keepdims=True)
        draw = jnp.where(cm["rows"] >= PF, ddt * _sigmoid(cm["prec"]), 0.0)
        ddt_ref[...] = draw
        dbias_ref[...] += jnp.sum(draw, axis=0, keepdims=True)
        dalog_ref[...] += da * a_c
        ddk_ref[...] += ddk

    xs, bb, cc, dtcs, dtrs, pc, pr = _ssd_specs()

    def rev(spec_fn):
        return lambda g, ci: spec_fn(g, NC - 1 - ci)

    def rspec(spec):
        return pl.BlockSpec(spec.block_shape, rev(spec.index_map))

    xs_r, bb_r, cc_r, dtc_r, dtr_r = rspec(xs), rspec(bb), rspec(cc), rspec(dtcs), rspec(dtrs)
    st_r = pl.BlockSpec((None, None, N, GW), lambda g, ci: (NC - 1 - ci, g, 0, 0))
    gn = pl.BlockSpec((Q, N), lambda g, ci: (NC - 1 - ci, g))
    return pl.pallas_call(
        body,
        name=name,
        grid=(G, NC),
        in_specs=[xs_r, bb_r, cc_r, xs_r, st_r, dtc_r, dtr_r, pc, pr, pc, pr, pc],
        out_specs=[xs_r, gn, gn, dtc_r, pc, pc, pc],
        out_shape=[
            jax.ShapeDtypeStruct((LP, DI), F32),
            jax.ShapeDtypeStruct((LP, G * N), F32),
            jax.ShapeDtypeStruct((LP, G * N), F32),
            jax.ShapeDtypeStruct((G, LP, LANES), F32),
            jax.ShapeDtypeStruct((G, 1, LANES), F32),
            jax.ShapeDtypeStruct((G, 1, LANES), F32),
            jax.ShapeDtypeStruct((G, 1, LANES), F32),
        ],
        scratch_shapes=[pltpu.VMEM((N, GW), F32)],
        compiler_params=_cparams(("parallel", "arbitrary")),
    )(xbc, xbc, xbc, dy, states, dtc, dtr, bias_c, bias_r, alog_c, alog_r, dskip_c)


def _split_dot(x, u):
    hi = x.astype(BF16)
    lo = (x - hi.astype(F32)).astype(BF16)
    return jnp.dot(hi, u, preferred_element_type=F32) + jnp.dot(lo, u, preferred_element_type=F32)


def _sb_block(qe, kblk, vis, a_run, u_gt):
    l = lax.dot_general(qe, kblk, (((1,), (1,)), ((), ())), preferred_element_type=F32)
    lk = jnp.minimum(-l, 0.0) - jnp.log(1.0 + jnp.exp(-jnp.abs(l)))
    lbeta = l + lk
    if vis is not None:
        lk = jnp.where(vis, lk, 0.0)
    logw = lbeta + _split_dot(lk, u_gt) + a_run
    return lbeta, lk, logw


def _descend(i, block, carry):
    def pack(n, c):
        return (n, jnp.max(jnp.maximum(c[0], c[1]))) + tuple(c)

    st = pack(jnp.int32(1), block(i, carry, True))
    st = lax.while_loop(lambda st: (st[0] < i) & (st[1] > -T_SKIP), lambda st: pack(st[0] + 1, block(i - st[0], st[2:], False)), st)
    st = lax.while_loop(lambda st: (st[0] == i) & (st[1] > -T_SKIP), lambda st: pack(st[0] + 1, block(0, st[2:], True)), st)
    return st[2:]


def _attn_fwd(q, k, v, name):
    nq = LP // TQ

    def body(q_ref, k_ref, v_ref, o_ref):
        i = pl.program_id(1)
        qv = q_ref[...]
        lane_lo = lax.broadcasted_iota(jnp.int32, (1, HP), 1) < 64
        t_idx = i * TQ + lax.broadcasted_iota(jnp.int32, (TQ, 1), 0)
        ji = lax.broadcasted_iota(jnp.int32, (TQ, TQ), 0)
        si = lax.broadcasted_iota(jnp.int32, (TQ, TQ), 1)
        u_gt = (ji > si).astype(BF16)
        qs = [jnp.where(lane_lo, qv, jnp.zeros_like(qv)), jnp.where(lane_lo, jnp.zeros_like(qv), qv)]

        def block(kb, carry, masked):
            a0, a1, acc = carry
            off = pl.multiple_of(kb * TQ, TQ)
            kblk = k_ref[pl.ds(off, TQ), :]
            vblk = v_ref[pl.ds(off, TQ), :]
            vis = None
            if masked:
                s_idx = kb * TQ + lax.broadcasted_iota(jnp.int32, (1, TQ), 1)
                vis = (s_idx < t_idx) & (s_idx >= PF)
            new_a = []
            for e, a_run in enumerate((a0, a1)):
                _, lk, logw = _sb_block(qs[e], kblk, vis, a_run, u_gt)
                w = jnp.exp(logw)
                if masked:
                    w = jnp.where(vis, w, 0.0)
                msk = lane_lo if e == 0 else jnp.logical_not(lane_lo)
                acc = acc + jnp.dot(w.astype(BF16), jnp.where(msk, vblk, jnp.zeros_like(vblk)), preferred_element_type=F32)
                new_a.append(a_run + jnp.sum(lk, axis=1, keepdims=True))
            return new_a[0], new_a[1], acc

        z1 = jnp.zeros((TQ, 1), F32)
        _, _, acc = _descend(i, block, (z1, z1, jnp.zeros((TQ, HP), F32)))
        o_ref[...] = acc

    return pl.pallas_call(
        body,
        name=name,
        grid=(D // HP, nq),
        in_specs=[
            pl.BlockSpec((TQ, HP), lambda j, i: (i, j)),
            pl.BlockSpec((LP, HP), lambda j, i: (0, j)),
            pl.BlockSpec((LP, HP), lambda j, i: (0, j)),
        ],
        out_specs=pl.BlockSpec((TQ, HP), lambda j, i: (i, j)),
        out_shape=jax.ShapeDtypeStruct((LP, D), F32),
        compiler_params=_cparams(("parallel", "arbitrary")),
    )(q, k, v)


def _attn_bwd(q, k, v, o, do, name):
    nq = LP // TQ

    def body(q_ref, k_ref, v_ref, o_ref, do_ref, dq_ref, dk_ref, dv_ref):
        i = pl.program_id(1)

        @pl.when(i == 0)
        def _():
            dk_ref[...] = jnp.zeros_like(dk_ref)
            dv_ref[...] = jnp.zeros_like(dv_ref)

        qv = q_ref[...]
        dov = do_ref[...]
        lane_lo = lax.broadcasted_iota(jnp.int32, (1, HP), 1) < 64
        t_idx = i * TQ + lax.broadcasted_iota(jnp.int32, (TQ, 1), 0)
        ji = lax.broadcasted_iota(jnp.int32, (TQ, TQ), 0)
        si = lax.broadcasted_iota(jnp.int32, (TQ, TQ), 1)
        u_gt = (ji > si).astype(BF16)
        u_ge = (ji >= si).astype(BF16)
        msks = [lane_lo, jnp.logical_not(lane_lo)]
        qs = [jnp.where(m, qv, jnp.zeros_like(qv)) for m in msks]
        dob = [jnp.where(m, dov, 0.0).astype(BF16) for m in msks]
        ov = o_ref[...]
        deltas = [jnp.sum(d.astype(F32) * ov, axis=1, keepdims=True) for d in dob]
        nt = (((1,), (1,)), ((), ()))
        tn = (((0,), (0,)), ((), ()))

        def block(kb, carry, masked):
            a0, a1, p0, p1, dq = carry
            off = pl.multiple_of(kb * TQ, TQ)
            kblk = k_ref[pl.ds(off, TQ), :]
            vblk = v_ref[pl.ds(off, TQ), :]
            vis = None
            if masked:
                s_idx = kb * TQ + lax.broadcasted_iota(jnp.int32, (1, TQ), 1)
                vis = (s_idx < t_idx) & (s_idx >= PF)
            new_a, new_p = [], []
            dk_acc = jnp.zeros((TQ, HP), F32)
            dv_acc = jnp.zeros((TQ, HP), F32)
            for e, (a_run, p_run) in enumerate(((a0, p0), (a1, p1))):
                lbeta, lk, logw = _sb_block(qs[e], kblk, vis, a_run, u_gt)
                sig = jnp.exp(lbeta)
                w = jnp.exp(logw)
                if masked:
                    w = jnp.where(vis, w, 0.0)
                wb = w.astype(BF16)
                dw = lax.dot_general(dob[e], vblk, nt, preferred_element_type=F32)
                pm = wb.astype(F32) * dw
                cum_p = deltas[e] - (_split_dot(pm, u_ge) + p_run)
                dl = pm - (pm + cum_p) * sig
                if masked:
                    dl = jnp.where(vis, dl, 0.0)
                dl = dl.astype(BF16)
                km = jnp.where(msks[e], kblk, jnp.zeros_like(kblk))
                dq = dq + jnp.dot(dl, km, preferred_element_type=F32)
                dk_acc = dk_acc + lax.dot_general(dl, qs[e], tn, preferred_element_type=F32)
                dv_acc = dv_acc + lax.dot_general(wb, dob[e], tn, preferred_element_type=F32)
                new_a.append(a_run + jnp.sum(lk, axis=1, keepdims=True))
                new_p.append(p_run + jnp.sum(pm, axis=1, keepdims=True))
            dk_ref[pl.ds(off, TQ), :] += dk_acc
            dv_ref[pl.ds(off, TQ), :] += dv_acc
            return new_a[0], new_a[1], new_p[0], new_p[1], dq

        z1 = jnp.zeros((TQ, 1), F32)
        carry = _descend(i, block, (z1, z1, z1, z1, jnp.zeros((TQ, HP), F32)))
        dq_ref[...] = carry[4]

    blk = pl.BlockSpec((TQ, HP), lambda j, i: (i, j))
    full = pl.BlockSpec((LP, HP), lambda j, i: (0, j))
    big = jax.ShapeDtypeStruct((LP, D), F32)
    return pl.pallas_call(
        body,
        name=name,
        grid=(D // HP, nq),
        in_specs=[blk, full, full, blk, blk],
        out_specs=[blk, full, full],
        out_shape=[big, big, big],
        compiler_params=_cparams(("parallel", "arbitrary")),
    )(q, k, v, o, do)


def _adamw(w, ga, gb, m, v, name):
    R = w.shape[0]
    tr = _pick(R, (2048, 1024, 512, 256, 128, 64, 32, 16, 8)) if R % 8 == 0 else R

    def body(w_ref, ga_ref, gb_ref, m_ref, v_ref, g_out, d_out, m_out, v_out):
        g = ga_ref[...] + gb_ref[...]
        mn = ADAM_B1 * m_ref[...] + (1.0 - ADAM_B1) * g
        vn = ADAM_B2 * v_ref[...] + (1.0 - ADAM_B2) * (g * g)
        mh = mn / (1.0 - ADAM_B1**ADAM_STEP)
        vh = vn / (1.0 - ADAM_B2**ADAM_STEP)
        g_out[...] = g
        d_out[...] = -ADAM_LR * (mh / (jnp.sqrt(vh) + ADAM_EPS) + ADAM_WD * w_ref[...])
        m_out[...] = mn
        v_out[...] = vn

    blk = pl.BlockSpec((tr, LANES), lambda i: (i, 0))
    sh = jax.ShapeDtypeStruct((R, LANES), F32)
    return pl.pallas_call(
        body,
        name=name,
        grid=(R // tr,),
        in_specs=[blk] * 5,
        out_specs=[blk] * 4,
        out_shape=[sh] * 4,
        compiler_params=_cparams(("parallel",)),
    )(w, ga, gb, m, v)


def _sum4(bufs, name):
    outs = []
    for n, buf in enumerate(bufs):
        R = buf.shape[1]
        tr = _pick(R, (2048, 1024, 512, 256, 128, 64, 32, 16))

        def body(b_ref, o_ref):
            acc = b_ref[0].astype(F32)
            for s in range(1, NCHIP):
                acc = acc + b_ref[s].astype(F32)
            o_ref[...] = acc

        outs.append(
            pl.pallas_call(
                body,
                name=f"{name}_{n}",
                grid=(R // tr,),
                in_specs=[pl.BlockSpec((NCHIP, tr, LANES), lambda i: (0, i, 0))],
                out_specs=pl.BlockSpec((tr, LANES), lambda i: (i, 0)),
                out_shape=jax.ShapeDtypeStruct((R, LANES), F32),
                compiler_params=_cparams(("parallel",)),
            )(buf)
        )
    return outs


ANY = pl.BlockSpec(memory_space=pl.ANY)


def _chip_exchange(bufs, gather, name):
    nb = len(bufs)

    def body(*refs):
        ins = refs[:nb]
        outs = refs[nb : 2 * nb]
        send_sems, recv_sems, loc_sems = refs[2 * nb :]
        x, y, c = lax.axis_index("x"), lax.axis_index("y"), lax.axis_index("c")
        me = 2 * x + y
        peers = [(1 - x, y), (x, 1 - y), (1 - x, 1 - y)]
        locs = []
        for b in range(nb):
            src = ins[b] if gather else ins[b].at[me]
            cp = pltpu.make_async_copy(src, outs[b].at[me], loc_sems.at[b])
            cp.start()
            locs.append(cp)
        sends = []
        for k, (px, py) in enumerate(peers):
            for b in range(nb):
                src = ins[b] if gather else ins[b].at[2 * px + py]
                cp = pltpu.make_async_remote_copy(
                    src_ref=src, dst_ref=outs[b].at[me], send_sem=send_sems.at[k * nb + b], recv_sem=recv_sems.at[k * nb + b],
                    device_id=(px, py, c), device_id_type=MESH)
                cp.start()
                sends.append(cp)
        for k, (px, py) in enumerate(peers):
            for b in range(nb):
                src = ins[b] if gather else ins[b].at[me]
                pltpu.make_async_remote_copy(
                    src_ref=src, dst_ref=outs[b].at[2 * px + py], send_sem=send_sems.at[k * nb + b],
                    recv_sem=recv_sems.at[k * nb + b], device_id=(px, py, c), device_id_type=MESH).wait_recv()
        for cp in sends:
            cp.wait_send()
        for cp in locs:
            cp.wait()

    out_shape = []
    for buf in bufs:
        shp = (NCHIP,) + tuple(buf.shape) if gather else tuple(buf.shape)
        out_shape.append(jax.ShapeDtypeStruct(shp, buf.dtype))
    return pl.pallas_call(
        body,
        name=name,
        in_specs=[ANY] * nb,
        out_specs=[ANY] * nb,
        out_shape=out_shape,
        scratch_shapes=[pltpu.SemaphoreType.DMA((3 * nb,)), pltpu.SemaphoreType.DMA((3 * nb,)), pltpu.SemaphoreType.DMA((nb,))],
    )(*bufs)


def _sibling_exchange(bufs, name):
    nb = len(bufs)

    def body(*refs):
        ins = refs[:nb]
        outs = refs[nb : 2 * nb]
        send_sems, recv_sems = refs[2 * nb :]
        x, y, c = lax.axis_index("x"), lax.axis_index("y"), lax.axis_index("c")
        cps = []
        for b in range(nb):
            cp = pltpu.make_async_remote_copy(
                src_ref=ins[b], dst_ref=outs[b], send_sem=send_sems.at[b], recv_sem=recv_sems.at[b],
                device_id=(x, y, 1 - c), device_id_type=MESH)
            cp.start()
            cps.append(cp)
        for cp in cps:
            cp.wait()

    return pl.pallas_call(
        body,
        name=name,
        in_specs=[ANY] * nb,
        out_specs=[ANY] * nb,
        out_shape=[jax.ShapeDtypeStruct(b.shape, b.dtype) for b in bufs],
        scratch_shapes=[pltpu.SemaphoreType.DMA((nb,)), pltpu.SemaphoreType.DMA((nb,))],
    )(*bufs)


ROW_ALIGN = 1024


def _pack(pieces, dtype):
    flat = []
    for p in pieces:
        f = p.reshape(-1).astype(dtype)
        pad = (-f.shape[0]) % LANES
        if pad:
            f = jnp.pad(f, (0, pad))
        flat.append(f)
    tot = sum(f.shape[0] for f in flat)
    pad = (-tot) % (ROW_ALIGN * LANES)
    if pad:
        flat.append(jnp.zeros((pad,), dtype))
    return jnp.concatenate(flat).reshape(-1, LANES)


def _unpack(buf, shapes):
    lead = buf.shape[:-2]
    flat = buf.reshape(lead + (-1,))
    out = []
    off = 0
    for shp in shapes:
        n = 1
        for d in shp:
            n *= d
        out.append(flat[..., off : off + n].reshape(lead + tuple(shp)))
        off += n + ((-n) % LANES)
    return out


PARAMS = (
    ("meta_tokens", 1, "small"), ("ssd_norm", 1, "small"), ("ssd_w_in", 2, "big"), ("ssd_conv_w", 2, "small"),
    ("ssd_conv_b", 1, "small"), ("ssd_dt_bias", None, "rep"), ("ssd_a_log", None, "rep"), ("ssd_d_skip", None, "rep"),
    ("ssd_gate_norm", 1, "small"), ("ssd_w_out", 1, "big"), ("kv_norm", None, "rep"), ("w_kv", 1, "big"),
    ("sb_norm", None, "rep"), ("sb_w_q", 1, "big"), ("sb_w_o", 1, "big"), ("ffn_norm", None, "rep"),
    ("ffn_w_up", 2, "big"), ("ffn_conv_w", 2, "small"), ("ffn_conv_b", None, "rep"), ("ffn_w_down", 1, "big"),
    ("final_norm", None, "rep"),
)


def _head_cols(vec):
    return jnp.pad(vec.reshape(G, 1, E), ((0, 0), (0, 0), (0, LANES - E)))


def _head_rows(vec):
    return jnp.pad(vec.reshape(G, E, 1), ((0, 0), (0, 8 - E), (0, 0)))


def kernel(x, meta_tokens, ssd_norm, ssd_w_in, ssd_conv_w, ssd_conv_b, ssd_dt_bias, ssd_a_log, ssd_d_skip, ssd_gate_norm, ssd_w_out, kv_norm, w_kv, sb_norm, sb_w_q, sb_w_o, ffn_norm, ffn_w_up, ffn_conv_w, ffn_conv_b, ffn_w_down, final_norm, loss_target, m_meta_tokens, m_ssd_norm, m_ssd_w_in, m_ssd_conv_w, m_ssd_conv_b, m_ssd_dt_bias, m_ssd_a_log, m_ssd_d_skip, m_ssd_gate_norm, m_ssd_w_out, m_kv_norm, m_w_kv, m_sb_norm, m_sb_w_q, m_sb_w_o, m_ffn_norm, m_ffn_w_up, m_ffn_conv_w, m_ffn_conv_b, m_ffn_w_down, m_final_norm, v_meta_tokens, v_ssd_norm, v_ssd_w_in, v_ssd_conv_w, v_ssd_conv_b, v_ssd_dt_bias, v_ssd_a_log, v_ssd_d_skip, v_ssd_gate_norm, v_ssd_w_out, v_kv_norm, v_w_kv, v_sb_norm, v_sb_w_q, v_sb_w_o, v_ffn_norm, v_ffn_w_up, v_ffn_conv_w, v_ffn_conv_b, v_ffn_w_down, v_final_norm):
    local = dict(meta_tokens=meta_tokens, ssd_norm=ssd_norm, ssd_w_in=ssd_w_in, ssd_conv_w=ssd_conv_w, ssd_conv_b=ssd_conv_b, ssd_dt_bias=ssd_dt_bias, ssd_a_log=ssd_a_log, ssd_d_skip=ssd_d_skip, ssd_gate_norm=ssd_gate_norm, ssd_w_out=ssd_w_out, kv_norm=kv_norm, w_kv=w_kv, sb_norm=sb_norm, sb_w_q=sb_w_q, sb_w_o=sb_w_o, ffn_norm=ffn_norm, ffn_w_up=ffn_w_up, ffn_conv_w=ffn_conv_w, ffn_conv_b=ffn_conv_b, ffn_w_down=ffn_w_down, final_norm=final_norm)
    mom_m = dict(meta_tokens=m_meta_tokens, ssd_norm=m_ssd_norm, ssd_w_in=m_ssd_w_in, ssd_conv_w=m_ssd_conv_w, ssd_conv_b=m_ssd_conv_b, ssd_dt_bias=m_ssd_dt_bias, ssd_a_log=m_ssd_a_log, ssd_d_skip=m_ssd_d_skip, ssd_gate_norm=m_ssd_gate_norm, ssd_w_out=m_ssd_w_out, kv_norm=m_kv_norm, w_kv=m_w_kv, sb_norm=m_sb_norm, sb_w_q=m_sb_w_q, sb_w_o=m_sb_w_o, ffn_norm=m_ffn_norm, ffn_w_up=m_ffn_w_up, ffn_conv_w=m_ffn_conv_w, ffn_conv_b=m_ffn_conv_b, ffn_w_down=m_ffn_w_down, final_norm=m_final_norm)
    mom_v = dict(meta_tokens=v_meta_tokens, ssd_norm=v_ssd_norm, ssd_w_in=v_ssd_w_in, ssd_conv_w=v_ssd_conv_w, ssd_conv_b=v_ssd_conv_b, ssd_dt_bias=v_ssd_dt_bias, ssd_a_log=v_ssd_a_log, ssd_d_skip=v_ssd_d_skip, ssd_gate_norm=v_ssd_gate_norm, ssd_w_out=v_ssd_w_out, kv_norm=v_kv_norm, w_kv=v_w_kv, sb_norm=v_sb_norm, sb_w_q=v_sb_w_q, sb_w_o=v_sb_w_o, ffn_norm=v_ffn_norm, ffn_w_up=v_ffn_w_up, ffn_conv_w=v_ffn_conv_w, ffn_conv_b=v_ffn_conv_b, ffn_w_down=v_ffn_w_down, final_norm=v_final_norm)

    big_names = [n for n, _, kind in PARAMS if kind == "big"]
    small_names = [n for n, _, kind in PARAMS if kind == "small"]
    rep_names = [n for n, _, kind in PARAMS if kind == "rep"]
    axis_of = {n: ax for n, ax, _ in PARAMS}

    big_own = _pack([local[n] for n in big_names], BF16)
    small_own = _pack([local[n] for n in small_names], F32)
    big_all, small_all = _chip_exchange([big_own, small_own], True, "gather_weights")
    full = {}
    for names, buf in ((big_names, big_all), (small_names, small_all)):
        parts = _unpack(buf, [local[n].shape for n in names])
        for n, p in zip(names, parts):
            full[n] = jnp.concatenate([p[s] for s in range(NCHIP)], axis=axis_of[n])
    for n in rep_names:
        full[n] = local[n]

    w_in = full["ssd_w_in"][0]
    w_z, w_xbc = w_in[:, :DI], w_in[:, DI : DI + CD]
    w_dt = jnp.pad(w_in[:, DI + CD :], ((0, 0), (0, LANES - H)))
    w_out = full["ssd_w_out"][0]
    wkv = full["w_kv"]
    w_q = full["sb_w_q"][0]
    w_o = full["sb_w_o"][0]
    w_up_g = [full["ffn_w_up"][l][:, :DFF] for l in range(2)]
    w_up_v = [full["ffn_w_up"][l][:, DFF:] for l in range(2)]
    w_down = [full["ffn_w_down"][l] for l in range(2)]
    fcw, fcb = full["ffn_conv_w"], full["ffn_conv_b"]
    scw, scb = full["ssd_conv_w"][0], full["ssd_conv_b"]
    bias_c, bias_r = _head_cols(full["ssd_dt_bias"][0]), _head_rows(full["ssd_dt_bias"][0])
    alog_c, alog_r = _head_cols(full["ssd_a_log"][0]), _head_rows(full["ssd_a_log"][0])
    dskip_c = _head_cols(full["ssd_d_skip"][0])
    kvn = full["kv_norm"].reshape(1, D)
    fin = full["final_norm"].reshape(1, D)

    h0 = jnp.concatenate([jnp.zeros((PF, D), F32), full["meta_tokens"], x[0]], axis=0)
    (u0,) = _rms_fwd(h0, [full["ssd_norm"]], "ssd_norm_fwd")
    z = _mm(u0, w_z, name="ssd_in_z")
    xr = _mm(u0, w_xbc, name="ssd_in_xbc")
    dt_raw = _mm(u0, w_dt, name="ssd_in_dt")
    xbc = _ssd_conv_fwd(xr, scw, scb, "ssd_conv_fwd")
    dth = dt_raw[:, :H].reshape(LP, G, E)
    dtc = jnp.pad(jnp.transpose(dth, (1, 0, 2)), ((0, 0), (0, 0), (0, LANES - E)))
    dtr = jnp.pad(jnp.transpose(dth, (1, 2, 0)), ((0, 0), (0, 8 - E), (0, 0)))
    y, states = _ssd_fwd(xbc, dtc, dtr, bias_c, bias_r, alog_c, alog_r, dskip_c, "ssd_scan_fwd")
    hgn = _gate_fwd(y, z, full["ssd_gate_norm"], "ssd_gate_fwd")
    h1 = _mm(hgn, w_out, add=h0, mask_rows=True, name="ssd_out")

    def ffn_fwd(h, l, tag):
        (u,) = _rms_fwd(h, [full["ffn_norm"][l : l + 1]], f"ffn{tag}_norm_fwd")
        hg = _mm(u, w_up_g[l], name=f"ffn{tag}_up_g")
        hv = _mm(u, w_up_v[l], name=f"ffn{tag}_up_v")
        act = _ffn_act_fwd(hg, hv, fcw[l][:, :DFF], fcw[l][:, DFF:], fcb[l : l + 1, :DFF], fcb[l : l + 1, DFF:], f"ffn{tag}_act_fwd")
        hn = _mm(act, w_down[l], add=h, mask_rows=True, name=f"ffn{tag}_down")
        return hn, (u, hg, hv, act)

    h2, ffn0 = ffn_fwd(h1, 0, "0")
    ukv, uq = _rms_fwd(h2, [kvn, full["sb_norm"]], "attn_norm_fwd")
    kk = _mm(ukv, wkv[:, :D], out_dtype=BF16, name="attn_k")
    vv = _mm(ukv, wkv[:, D:], out_dtype=BF16, name="attn_v")
    qq = _mm(uq, w_q, out_dtype=BF16, scale=64.0**-0.5, name="attn_q")
    o = _attn_fwd(qq, kk, vv, "attn_fwd")
    h3 = _mm(o, w_o, add=h2, mask_rows=True, name="attn_out")
    h4, ffn1 = ffn_fwd(h3, 1, "1")
    dh, g_final, loss_rows = _loss_head(h4, fin, loss_target[0], "loss_head")
    loss = lax.psum(0.5 / D * jnp.sum(loss_rows), ("x", "y", "c"))

    grads = {"final_norm": g_final.reshape(D)}

    def ffn_bwd(dh, h, l, saved, tag):
        u, hg, hv, act = saved
        da = _mm(dh, w_down[l], tb=True, name=f"ffn{tag}_down_dx")
        gw_down = _mm(act, dh, ta=True, name=f"ffn{tag}_down_dw")
        dhg, dhv, dwg, dwv, dbg, dbv = _ffn_act_bwd(hg, hv, da, fcw[l][:, :DFF], fcw[l][:, DFF:], fcb[l : l + 1, :DFF], fcb[l : l + 1, DFF:], f"ffn{tag}_act_bwd")
        gw_up = jnp.concatenate([_mm(u, dhg, ta=True, name=f"ffn{tag}_up_g_dw"), _mm(u, dhv, ta=True, name=f"ffn{tag}_up_v_dw")], axis=1)
        du = _mm(dhg, w_up_g[l], tb=True, name=f"ffn{tag}_up_g_dx")
        du = _mm(dhv, w_up_v[l], tb=True, add=du, name=f"ffn{tag}_up_v_dx")
        dh_new, (gn,) = _rms_bwd(dh, h, [du], [full["ffn_norm"][l : l + 1]], f"ffn{tag}_norm_bwd")
        return dh_new, gw_down, gw_up, jnp.concatenate([dwg, dwv], axis=1), jnp.concatenate([dbg, dbv], axis=1), gn

    dh, gd1, gu1, gcw1, gcb1, gn1 = ffn_bwd(dh, h3, 1, ffn1, "1")
    do = _mm(dh, w_o, tb=True, name="attn_out_dx")
    grads["sb_w_o"] = _mm(o, dh, ta=True, name="attn_out_dw")[None]
    dq, dk, dv = _attn_bwd(qq, kk, vv, o, do, "attn_bwd")
    grads["sb_w_q"] = _mm(uq, dq, ta=True, scale=64.0**-0.5, name="attn_q_dw")[None]
    grads["w_kv"] = jnp.concatenate([_mm(ukv, dk, ta=True, name="attn_k_dw"), _mm(ukv, dv, ta=True, name="attn_v_dw")], axis=1)
    duq = _mm(dq, w_q, tb=True, scale=64.0**-0.5, name="attn_q_dx")
    dukv = _mm(dk, wkv[:, :D], tb=True, name="attn_k_dx")
    dukv = _mm(dv, wkv[:, D:], tb=True, add=dukv, name="attn_v_dx")
    dh, (g_kvn, g_sbn) = _rms_bwd(dh, h2, [dukv, duq], [kvn, full["sb_norm"]], "attn_norm_bwd")
    grads["kv_norm"] = g_kvn.reshape(D)
    grads["sb_norm"] = g_sbn
    dh, gd0, gu0, gcw0, gcb0, gn0 = ffn_bwd(dh, h1, 0, ffn0, "0")
    grads["ffn_w_down"] = jnp.stack([gd0, gd1])
    grads["ffn_w_up"] = jnp.stack([gu0, gu1])
    grads["ffn_conv_w"] = jnp.stack([gcw0, gcw1])
    grads["ffn_conv_b"] = jnp.concatenate([gcb0, gcb1], axis=0)
    grads["ffn_norm"] = jnp.concatenate([gn0, gn1], axis=0)
    dhgn = _mm(dh, w_out, tb=True, name="ssd_out_dx")
    grads["ssd_w_out"] = _mm(hgn, dh, ta=True, name="ssd_out_dw")[None]
    dy, dz, g_gate = _gate_bwd(dhgn, y, z, full["ssd_gate_norm"], "ssd_gate_bwd")
    grads["ssd_gate_norm"] = g_gate
    dxs, dB, dC, ddt_raw, g_bias, g_alog, g_dskip = _ssd_bwd(xbc, dy, states, dtc, dtr, bias_c, bias_r, alog_c, alog_r, dskip_c, "ssd_scan_bwd")
    grads["ssd_dt_bias"] = g_bias[:, 0, :E].reshape(1, H)
    grads["ssd_a_log"] = g_alog[:, 0, :E].reshape(1, H)
    grads["ssd_d_skip"] = g_dskip[:, 0, :E].reshape(1, H)
    dxr, g_scw, g_scb = _ssd_conv_bwd(xr, jnp.concatenate([dxs, dB, dC], axis=1), scw, scb, "ssd_conv_bwd")
    grads["ssd_conv_w"] = g_scw[None]
    grads["ssd_conv_b"] = g_scb
    ddt = jnp.pad(jnp.transpose(ddt_raw[:, :, :E], (1, 0, 2)).reshape(LP, H), ((0, 0), (0, LANES - H)))
    grads["ssd_w_in"] = jnp.concatenate(
        [_mm(u0, dz, ta=True, name="ssd_in_z_dw"), _mm(u0, dxr, ta=True, name="ssd_in_xbc_dw"), _mm(u0, ddt, ta=True, name="ssd_in_dt_dw")[:, :H]], axis=1)[None]
    du = _mm(dz, w_z, tb=True, name="ssd_in_z_dx")
    du = _mm(dxr, w_xbc, tb=True, add=du, name="ssd_in_xbc_dx")
    du = _mm(ddt, w_dt, tb=True, add=du, name="ssd_in_dt_dx")
    dh, (g_ssdn,) = _rms_bwd(dh, h0, [du], [full["ssd_norm"]], "ssd_norm_bwd")
    grads["ssd_norm"] = g_ssdn
    grads["meta_tokens"] = dh[PF : PF + N_META]
    grad_x = dh[PF + N_META :][None]

    def shard_pieces(names, s):
        out = []
        for n in names:
            ax = axis_of[n]
            out.append(grads[n] if ax is None else jnp.split(grads[n], NCHIP, axis=ax)[s])
        return out

    g_big = jnp.stack([_pack(shard_pieces(big_names, s), BF16) for s in range(NCHIP)])
    g_small = jnp.stack([_pack(shard_pieces(small_names + rep_names, s), F32) for s in range(NCHIP)])
    r_big, r_small = _chip_exchange([g_big, g_small], False, "scatter_grads")
    s_big, s_small = _sum4([r_big, r_small], "sum_chips")
    o_big, o_small = _sibling_exchange([s_big, s_small], "swap_cores")

    def rows(a):
        f = a.reshape(-1)
        pad = (-f.shape[0]) % LANES
        if pad:
            f = jnp.pad(f, (0, pad))
        return f.reshape(-1, LANES)

    order = [n for n, _, _ in PARAMS]
    res = {}
    for names, mine, other in ((big_names, s_big, o_big), (small_names + rep_names, s_small, o_small)):
        shapes = [local[n].shape for n in names]
        for n, ga, gb in zip(names, _unpack(mine, shapes), _unpack(other, shapes)):
            shp = local[n].shape
            cnt = 1
            for d in shp:
                cnt *= d
            outs = _adamw(rows(local[n]), rows(ga), rows(gb), rows(mom_m[n]), rows(mom_v[n]), f"adamw_{n}")
            res[n] = [o_.reshape(-1)[:cnt].reshape(shp) for o_ in outs]
    return (loss, grad_x, *[res[n][0] for n in order], *[res[n][1] for n in order], *[res[n][2] for n in order], *[res[n][3] for n in order])
```

```python
import functools

import jax
import jax.numpy as jnp
from jax import lax
from jax.experimental import pallas as pl
from jax.experimental.pallas import tpu as pltpu

D = 1024
SEQ = 8192
N_META = 16
EPS = 1e-6
P = 64
G = 4
N = 128
CONVW = 4
Q = 256
FC = 3
DFF = 256 * ((8 * D // 3 + 255) // 256)
DI = 2 * D
H = DI // P
E = H // G
GW = E * P
CD = DI + 2 * G * N
IN = DI + CD + H
SBH = D // 64
HP = 128
LANES = 128
PF = Q - N_META
LP = PF + N_META + SEQ
NC = LP // Q
TQ = 256
NCHIP = 4
ADAM_LR, ADAM_B1, ADAM_B2, ADAM_EPS, ADAM_WD, ADAM_STEP = 0.001, 0.9, 0.999, 1e-08, 0.01, 10

F32 = jnp.float32
BF16 = jnp.bfloat16
HI = lax.Precision.HIGHEST
MESH = pl.DeviceIdType.MESH
VMEM_LIMIT = 48 * 1024 * 1024
MM_MAX_K = 3072
T_SKIP = 110.0


def _pick(n, cands):
    for c in cands:
        if n % c == 0:
            return c
    raise ValueError((n, cands))


def _cparams(sem):
    return pltpu.CompilerParams(dimension_semantics=sem, vmem_limit_bytes=VMEM_LIMIT)


def _valid_rows(block, rows):
    r = block * rows + lax.broadcasted_iota(jnp.int32, (rows, 1), 0)
    return r >= PF


def _sigmoid(x):
    return 1.0 / (1.0 + jnp.exp(-x))


def _softplus(x):
    return jnp.maximum(x, 0.0) + jnp.log(1.0 + jnp.exp(-jnp.abs(x)))


def _sum_all(x):
    return jnp.sum(jnp.sum(x, axis=1, keepdims=True), axis=0, keepdims=True)


def _dsilu(x):
    s = _sigmoid(x)
    return s * (1.0 + x * (1.0 - s))


def _mm(a, b, *, ta=False, tb=False, out_dtype=F32, add=None, mask_rows=False, scale=None, name):
    if ta:
        K, M = a.shape
    else:
        M, K = a.shape
    if tb:
        Nn, K2 = b.shape
    else:
        K2, Nn = b.shape
    assert K == K2, (a.shape, b.shape, ta, tb)
    tn = _pick(Nn, (1408, 1024, 768, 512, 256, 128))
    if ta:
        tm = _pick(M, (1408, 1024, 768, 512, 256, 128))
        tk = _pick(K, (768, 512, 256))
    else:
        tm = _pick(M, (768, 256))
        tk = K if K <= MM_MAX_K else _pick(K, (1024, 768, 512, 256, 128))
    nk = K // tk
    dims = (((0 if ta else 1,), (1 if tb else 0,)), ((), ()))

    def body(*refs):
        a_ref, b_ref = refs[0], refs[1]
        add_ref = refs[2] if add is not None else None
        o_ref = refs[3] if add is not None else refs[2]
        acc = refs[-1] if nk > 1 else None

        def finish(r):
            if scale is not None:
                r = r * scale
            if mask_rows:
                r = jnp.where(_valid_rows(pl.program_id(0), tm), r, 0.0)
            if add_ref is not None:
                r = r + add_ref[...]
            o_ref[...] = r.astype(out_dtype)

        part = lax.dot_general(a_ref[...].astype(BF16), b_ref[...].astype(BF16), dims, preferred_element_type=F32)
        if nk == 1:
            finish(part)
        else:
            k = pl.program_id(2)

            @pl.when(k == 0)
            def _():
                acc[...] = part

            @pl.when(k > 0)
            def _():
                acc[...] += part

            @pl.when(k == nk - 1)
            def _():
                finish(acc[...])

    a_spec = pl.BlockSpec((tk, tm), lambda i, j, k: (k, i)) if ta else pl.BlockSpec((tm, tk), lambda i, j, k: (i, k))
    b_spec = pl.BlockSpec((tn, tk), lambda i, j, k: (j, k)) if tb else pl.BlockSpec((tk, tn), lambda i, j, k: (k, j))
    o_spec = pl.BlockSpec((tm, tn), lambda i, j, k: (i, j))
    in_specs = [a_spec, b_spec] + ([o_spec] if add is not None else [])
    args = (a, b) + ((add,) if add is not None else ())
    return pl.pallas_call(
        body,
        name=name,
        grid=(M // tm, Nn // tn, nk),
        in_specs=in_specs,
        out_specs=o_spec,
        out_shape=jax.ShapeDtypeStruct((M, Nn), out_dtype),
        scratch_shapes=[pltpu.VMEM((tm, tn), F32)] if nk > 1 else [],
        compiler_params=_cparams(("parallel", "parallel", "arbitrary")),
    )(*args)


def _rms_fwd(h, gains, name):
    tr = _pick(LP, (768, 256))
    ng = len(gains)

    def body(*refs):
        h_ref = refs[0]
        g_refs = refs[1 : 1 + ng]
        o_refs = refs[1 + ng :]
        x = h_ref[...]
        xh = x * lax.rsqrt(jnp.mean(x * x, axis=-1, keepdims=True) + EPS)
        for g_ref, o_ref in zip(g_refs, o_refs):
            o_ref[...] = (xh * g_ref[...]).astype(BF16)

    row = pl.BlockSpec((tr, D), lambda i: (i, 0))
    vec = pl.BlockSpec((1, D), lambda i: (0, 0))
    outs = pl.pallas_call(
        body,
        name=name,
        grid=(LP // tr,),
        in_specs=[row] + [vec] * ng,
        out_specs=[row] * ng,
        out_shape=[jax.ShapeDtypeStruct((LP, D), BF16)] * ng,
        compiler_params=_cparams(("parallel",)),
    )(h, *gains)
    return outs


def _rms_bwd(dh_in, h, dus, gains, name):
    tr = _pick(LP, (256,))
    ng = len(gains)

    def body(*refs):
        dh_ref, h_ref = refs[0], refs[1]
        du_refs = refs[2 : 2 + ng]
        g_refs = refs[2 + ng : 2 + 2 * ng]
        o_ref = refs[2 + 2 * ng]
        dg_refs = refs[3 + 2 * ng :]
        i = pl.program_id(0)
        x = h_ref[...]
        r = lax.rsqrt(jnp.mean(x * x, axis=-1, keepdims=True) + EPS)
        xh = x * r
        tot = dh_ref[...]
        for du_ref, g_ref, dg_ref in zip(du_refs, g_refs, dg_refs):
            du = du_ref[...]
            dxh = du * g_ref[...]
            tot = tot + r * (dxh - xh * jnp.mean(dxh * xh, axis=-1, keepdims=True))

            @pl.when(i == 0)
            def _():
                dg_ref[...] = jnp.zeros_like(dg_ref)

            dg_ref[...] += jnp.sum(du * xh, axis=0, keepdims=True)
        o_ref[...] = jnp.where(_valid_rows(i, tr), tot, 0.0)

    row = pl.BlockSpec((tr, D), lambda i: (i, 0))
    vec = pl.BlockSpec((1, D), lambda i: (0, 0))
    outs = pl.pallas_call(
        body,
        name=name,
        grid=(LP // tr,),
        in_specs=[row, row] + [row] * ng + [vec] * ng,
        out_specs=[row] + [vec] * ng,
        out_shape=[jax.ShapeDtypeStruct((LP, D), F32)] + [jax.ShapeDtypeStruct((1, D), F32)] * ng,
        compiler_params=_cparams(("arbitrary",)),
    )(dh_in, h, *dus, *gains)
    return outs[0], outs[1:]


def _loss_head(h, gain, target, name):
    tr = Q

    def body(h_ref, g_ref, t_ref, dh_ref, dg_ref, ls_ref):
        i = pl.program_id(0)

        @pl.when(i == 0)
        def _():
            dg_ref[...] = jnp.zeros_like(dg_ref)
            ls_ref[...] = jnp.zeros_like(ls_ref)
            dh_ref[...] = jnp.zeros_like(dh_ref)

        @pl.when(i > 0)
        def _():
            x = h_ref[...]
            g = g_ref[...]
            r = lax.rsqrt(jnp.mean(x * x, axis=-1, keepdims=True) + EPS)
            xh = x * r
            e = xh * g - t_ref[...]
            ls_ref[...] += jnp.sum(e * e, axis=0, keepdims=True)
            dy = e * (1.0 / D)
            dg_ref[...] += jnp.sum(dy * xh, axis=0, keepdims=True)
            dxh = dy * g
            dh_ref[...] = r * (dxh - xh * jnp.mean(dxh * xh, axis=-1, keepdims=True))

    row = pl.BlockSpec((tr, D), lambda i: (i, 0))
    vec = pl.BlockSpec((1, D), lambda i: (0, 0))
    return pl.pallas_call(
        body,
        name=name,
        grid=(LP // tr,),
        in_specs=[row, vec, pl.BlockSpec((tr, D), lambda i: (jnp.maximum(i - 1, 0), 0))],
        out_specs=[row, vec, vec],
        out_shape=[jax.ShapeDtypeStruct((LP, D), F32), jax.ShapeDtypeStruct((1, D), F32), jax.ShapeDtypeStruct((1, D), F32)],
        compiler_params=_cparams(("arbitrary",)),
    )(h, gain, target)


HALO = 8
CONV_COLS = (1536, 1408, 768, 512, 256)


def _conv_rows(ext, w, b, width):
    n = ext.shape[0]
    acc = b + w[width - 1 : width, :] * ext[HALO:]
    for k in range(width - 1):
        acc = acc + w[k : k + 1, :] * pltpu.roll(ext, width - 1 - k, 0)[HALO:]
    return acc


def _conv_specs(tr, tn, col):
    per = tr // HALO
    last = LP // HALO - 1
    prev = pl.BlockSpec((HALO, tn), lambda j, i: (jnp.maximum(i * per - 1, 0), col(j)))
    cur = pl.BlockSpec((tr, tn), lambda j, i: (i, col(j)))
    nxt = pl.BlockSpec((HALO, tn), lambda j, i: (jnp.minimum((i + 1) * per, last), col(j)))
    return prev, cur, nxt


def _conv_bwd_core(ext, dact_fn, w, b, width, i, nblk, tr):
    pre = _conv_rows(ext, w, b, width)
    dpre = dact_fn(pre)
    rows = i * tr + lax.broadcasted_iota(jnp.int32, (tr + HALO, 1), 0)
    dpre = jnp.where((rows >= PF) & (rows < LP), dpre, 0.0)
    n = tr + HALO
    dx = w[width - 1 : width, :] * dpre[:tr]
    for k in range(width - 1):
        sh = width - 1 - k
        dx = dx + w[k : k + 1, :] * pltpu.roll(dpre, n - sh, 0)[:tr]
    dcur = dpre[:tr]
    dws = []
    for k in range(width):
        sh = width - 1 - k
        xs = ext[HALO : HALO + tr] if sh == 0 else pltpu.roll(ext, sh, 0)[HALO : HALO + tr]
        dws.append(jnp.sum(xs * dcur, axis=0, keepdims=True))
    db = jnp.sum(dcur, axis=0, keepdims=True)
    dx = jnp.where(_valid_rows(i, tr), dx, 0.0)
    return dx, dws, db


def _ssd_conv_fwd(xr, cw, cb, name):
    tr, tn = Q, _pick(CD, CONV_COLS)

    def body(p_ref, c_ref, w_ref, b_ref, o_ref):
        i = pl.program_id(1)
        ext = jnp.concatenate([jnp.where(i > 0, p_ref[...], 0.0), c_ref[...]], axis=0)
        pre = _conv_rows(ext, w_ref[...], b_ref[...], CONVW)
        o_ref[...] = jnp.where(_valid_rows(i, tr), pre * _sigmoid(pre), 0.0)

    prev, cur, _ = _conv_specs(tr, tn, lambda j: j)
    return pl.pallas_call(
        body,
        name=name,
        grid=(CD // tn, LP // tr),
        in_specs=[prev, cur, pl.BlockSpec((CONVW, tn), lambda j, i: (0, j)), pl.BlockSpec((1, tn), lambda j, i: (0, j))],
        out_specs=cur,
        out_shape=jax.ShapeDtypeStruct((LP, CD), F32),
        compiler_params=_cparams(("parallel", "arbitrary")),
    )(xr, xr, cw, cb)


def _ssd_conv_bwd(xr, dxbc, cw, cb, name):
    tr, tn = Q, _pick(CD, CONV_COLS)
    nblk = LP // tr

    def body(p_ref, c_ref, n_ref, dc_ref, dn_ref, w_ref, b_ref, dx_ref, dw_ref, db_ref):
        i = pl.program_id(1)
        ext = jnp.concatenate([jnp.where(i > 0, p_ref[...], 0.0), c_ref[...], n_ref[...]], axis=0)
        dout = jnp.concatenate([dc_ref[...], dn_ref[...]], axis=0)
        dx, dws, db = _conv_bwd_core(ext, lambda pre: dout * _dsilu(pre), w_ref[...], b_ref[...], CONVW, i, nblk, tr)
        dx_ref[...] = dx

        @pl.when(i == 0)
        def _():
            dw_ref[...] = jnp.zeros_like(dw_ref)
            db_ref[...] = jnp.zeros_like(db_ref)

        for k in range(CONVW):
            dw_ref[k : k + 1, :] += dws[k]
        db_ref[...] += db

    prev, cur, nxt = _conv_specs(tr, tn, lambda j: j)
    wspec = pl.BlockSpec((CONVW, tn), lambda j, i: (0, j))
    bspec = pl.BlockSpec((1, tn), lambda j, i: (0, j))
    return pl.pallas_call(
        body,
        name=name,
        grid=(CD // tn, LP // tr),
        in_specs=[prev, cur, nxt, cur, nxt, wspec, bspec],
        out_specs=[cur, wspec, bspec],
        out_shape=[jax.ShapeDtypeStruct((LP, CD), F32), jax.ShapeDtypeStruct((CONVW, CD), F32), jax.ShapeDtypeStruct((1, CD), F32)],
        compiler_params=_cparams(("parallel", "arbitrary")),
    )(xr, xr, xr, dxbc, dxbc, cw, cb)


def _ffn_act_fwd(hg, hv, cwg, cwv, cbg, cbv, name):
    tr, tn = Q, _pick(DFF, CONV_COLS)

    def body(pg, cg, pv, cv, wg, wv, bg, bv, o_ref):
        i = pl.program_id(1)
        eg = jnp.concatenate([jnp.where(i > 0, pg[...], 0.0), cg[...]], axis=0)
        ev = jnp.concatenate([jnp.where(i > 0, pv[...], 0.0), cv[...]], axis=0)
        gate = _conv_rows(eg, wg[...], bg[...], FC)
        val = _conv_rows(ev, wv[...], bv[...], FC)
        o_ref[...] = (gate * _sigmoid(gate) * val).astype(BF16)

    prev, cur, _ = _conv_specs(tr, tn, lambda j: j)
    wspec = pl.BlockSpec((FC, tn), lambda j, i: (0, j))
    bspec = pl.BlockSpec((1, tn), lambda j, i: (0, j))
    return pl.pallas_call(
        body,
        name=name,
        grid=(DFF // tn, LP // tr),
        in_specs=[prev, cur, prev, cur, wspec, wspec, bspec, bspec],
        out_specs=cur,
        out_shape=jax.ShapeDtypeStruct((LP, DFF), BF16),
        compiler_params=_cparams(("parallel", "arbitrary")),
    )(hg, hg, hv, hv, cwg, cwv, cbg, cbv)


def _ffn_act_bwd(hg, hv, da, cwg, cwv, cbg, cbv, name):
    tr, tn = Q, _pick(DFF, CONV_COLS)
    nblk = LP // tr

    def body(pg, cg, ng, pv, cv, nv, dc, dn, wg, wv, bg, bv, dg_ref, dv_ref, dwg, dwv, dbg, dbv):
        i = pl.program_id(1)
        eg = jnp.concatenate([jnp.where(i > 0, pg[...], 0.0), cg[...], ng[...]], axis=0)
        ev = jnp.concatenate([jnp.where(i > 0, pv[...], 0.0), cv[...], nv[...]], axis=0)
        dout = jnp.concatenate([dc[...], dn[...]], axis=0)
        gate = _conv_rows(eg, wg[...], bg[...], FC)
        val = _conv_rows(ev, wv[...], bv[...], FC)
        dxg, dwsg, dbgv = _conv_bwd_core(eg, lambda pre: dout * val * _dsilu(pre), wg[...], bg[...], FC, i, nblk, tr)
        dxv, dwsv, dbvv = _conv_bwd_core(ev, lambda pre: dout * gate * _sigmoid(gate), wv[...], bv[...], FC, i, nblk, tr)
        dg_ref[...] = dxg
        dv_ref[...] = dxv

        @pl.when(i == 0)
        def _():
            dwg[...] = jnp.zeros_like(dwg)
            dwv[...] = jnp.zeros_like(dwv)
            dbg[...] = jnp.zeros_like(dbg)
            dbv[...] = jnp.zeros_like(dbv)

        for k in range(FC):
            dwg[k : k + 1, :] += dwsg[k]
            dwv[k : k + 1, :] += dwsv[k]
        dbg[...] += dbgv
        dbv[...] += dbvv

    prev, cur, nxt = _conv_specs(tr, tn, lambda j: j)
    wspec = pl.BlockSpec((FC, tn), lambda j, i: (0, j))
    bspec = pl.BlockSpec((1, tn), lambda j, i: (0, j))
    big = jax.ShapeDtypeStruct((LP, DFF), F32)
    wsh = jax.ShapeDtypeStruct((FC, DFF), F32)
    bsh = jax.ShapeDtypeStruct((1, DFF), F32)
    return pl.pallas_call(
        body,
        name=name,
        grid=(DFF // tn, LP // tr),
        in_specs=[prev, cur, nxt, prev, cur, nxt, cur, nxt, wspec, wspec, bspec, bspec],
        out_specs=[cur, cur, wspec, wspec, bspec, bspec],
        out_shape=[big, big, wsh, wsh, bsh, bsh],
        compiler_params=_cparams(("parallel", "arbitrary")),
    )(hg, hg, hg, hv, hv, hv, da, da, cwg, cwv, cbg, cbv)


def _gate_fwd(y, z, gg, name):
    tr = _pick(LP, (768, 256))

    def body(y_ref, z_ref, g_ref, o_ref):
        zv = z_ref[...]
        hg = y_ref[...] * zv * _sigmoid(zv)
        r = lax.rsqrt(jnp.mean(hg * hg, axis=-1, keepdims=True) + EPS)
        o_ref[...] = (hg * r * g_ref[...]).astype(BF16)

    blk = pl.BlockSpec((tr, GW), lambda i, g: (i, g))
    return pl.pallas_call(
        body,
        name=name,
        grid=(LP // tr, G),
        in_specs=[blk, blk, pl.BlockSpec((1, GW), lambda i, g: (0, g))],
        out_specs=blk,
        out_shape=jax.ShapeDtypeStruct((LP, DI), BF16),
        compiler_params=_cparams(("parallel", "parallel")),
    )(y, z, gg)


def _gate_bwd(dout, y, z, gg, name):
    tr = _pick(LP, (768, 256))

    def body(do_ref, y_ref, z_ref, g_ref, dy_ref, dz_ref, dg_ref):
        i = pl.program_id(1)
        zv = z_ref[...]
        yv = y_ref[...]
        sz = zv * _sigmoid(zv)
        hg = yv * sz
        r = lax.rsqrt(jnp.mean(hg * hg, axis=-1, keepdims=True) + EPS)
        hh = hg * r
        do = do_ref[...]
        dhh = do * g_ref[...]
        dhg = r * (dhh - hh * jnp.mean(dhh * hh, axis=-1, keepdims=True))
        dy_ref[...] = dhg * sz
        dz_ref[...] = dhg * yv * _dsilu(zv)

        @pl.when(i == 0)
        def _():
            dg_ref[...] = jnp.zeros_like(dg_ref)

        dg_ref[...] += jnp.sum(do * hh, axis=0, keepdims=True)

    blk = pl.BlockSpec((tr, GW), lambda g, i: (i, g))
    vec = pl.BlockSpec((1, GW), lambda g, i: (0, g))
    big = jax.ShapeDtypeStruct((LP, DI), F32)
    return pl.pallas_call(
        body,
        name=name,
        grid=(G, LP // tr),
        in_specs=[blk, blk, blk, vec],
        out_specs=[blk, blk, vec],
        out_shape=[big, big, jax.ShapeDtypeStruct((1, DI), F32)],
        compiler_params=_cparams(("parallel", "arbitrary")),
    )(dout, y, z, gg)


def _ssd_common(dtc_ref, dtr_ref, bc_ref, br_ref, ac_ref, ar_ref, c):
    rows = c * Q + lax.broadcasted_iota(jnp.int32, (Q, 1), 0)
    cols = c * Q + lax.broadcasted_iota(jnp.int32, (1, Q), 1)
    prec = dtc_ref[...] + bc_ref[...]
    prer = dtr_ref[...] + br_ref[...]
    dtc = jnp.where(rows >= PF, _softplus(prec), 0.0)
    dtr = jnp.where(cols >= PF, _softplus(prer), 0.0)
    a_c = -jnp.exp(ac_ref[...])
    a_r = -jnp.exp(ar_ref[...])
    li = lax.broadcasted_iota(jnp.int32, (Q, Q), 0)
    si = lax.broadcasted_iota(jnp.int32, (Q, Q), 1)
    tril = si <= li
    trif = tril.astype(F32)
    csc = jnp.dot(trif, dtc * a_c, precision=HI, preferred_element_type=F32)
    csr = lax.dot_general(dtr * a_r, trif, (((1,), (1,)), ((), ())), precision=HI, preferred_element_type=F32)
    return dict(rows=rows, prec=prec, dtc=dtc, a_c=a_c, tril=tril, trif=trif, csc=csc, csr=csr, li=li, si=si)


def _pair_expand(arr, h0, lane_lo):
    return jnp.where(lane_lo, arr[:, h0 : h0 + 1], arr[:, h0 + 1 : h0 + 2])


def _ssd_specs():
    nb = DI // N
    xs = pl.BlockSpec((Q, GW), lambda g, c: (c, g))
    bb = pl.BlockSpec((Q, N), lambda g, c: (c, nb + g))
    cc = pl.BlockSpec((Q, N), lambda g, c: (c, nb + G + g))
    dtc = pl.BlockSpec((None, Q, LANES), lambda g, c: (g, c, 0))
    dtr = pl.BlockSpec((None, 8, Q), lambda g, c: (g, 0, c))
    pc = pl.BlockSpec((None, 1, LANES), lambda g, c: (g, 0, 0))
    pr = pl.BlockSpec((None, 8, 1), lambda g, c: (g, 0, 0))
    return xs, bb, cc, dtc, dtr, pc, pr


def _ssd_fwd(xbc, dtc, dtr, bias_c, bias_r, alog_c, alog_r, dskip_c, name):
    def body(xs_ref, b_ref, c_ref, dtc_ref, dtr_ref, bc_ref, br_ref, ac_ref, ar_ref, dk_ref, y_ref, st_ref, state):
        c = pl.program_id(1)

        @pl.when(c == 0)
        def _():
            state[...] = jnp.zeros_like(state)

        st_ref[...] = state[...]
        cm = _ssd_common(dtc_ref, dtr_ref, bc_ref, br_ref, ac_ref, ar_ref, c)
        Bm = b_ref[...]
        Cm = c_ref[...]
        cb = lax.dot_general(Cm.astype(BF16), Bm.astype(BF16), (((1,), (1,)), ((), ())), preferred_element_type=F32)
        bt = Bm.T.astype(BF16)
        lane_lo = lax.broadcasted_iota(jnp.int32, (1, HP), 1) < P
        csc, csr, dtc_v = cm["csc"], cm["csr"], cm["dtc"]
        ecs = jnp.exp(csc)
        cs_end = csc[Q - 1 : Q, :]
        wdec = jnp.exp(cs_end - csc)
        eend = jnp.exp(cs_end)
        for pp in range(E // 2):
            h0 = 2 * pp
            sl = slice(pp * HP, (pp + 1) * HP)
            xp = xs_ref[:, sl]
            xdt = xp * _pair_expand(dtc_v, h0, lane_lo)
            yacc = xp * _pair_expand(dk_ref[...], h0, lane_lo)
            for e in range(2):
                h = h0 + e
                lm = jnp.where(cm["tril"], jnp.exp(jnp.minimum(csc[:, h : h + 1] - csr[h : h + 1, :], 0.0)), 0.0)
                m = (cb * lm).astype(BF16)
                xm = jnp.where(lane_lo if e == 0 else jnp.logical_not(lane_lo), xdt, 0.0).astype(BF16)
                yacc = yacc + jnp.dot(m, xm, preferred_element_type=F32)
            stp = state[:, sl]
            yoff = jnp.dot(Cm.astype(BF16), stp.astype(BF16), preferred_element_type=F32)
            y_ref[:, sl] = yacc + yoff * _pair_expand(ecs, h0, lane_lo)
            xw = (xdt * _pair_expand(wdec, h0, lane_lo)).astype(BF16)
            state[:, sl] = stp * _pair_expand(eend, h0, lane_lo) + jnp.dot(bt, xw, preferred_element_type=F32)

    xs, bb, cc, dtcs, dtrs, pc, pr = _ssd_specs()
    return pl.pallas_call(
        body,
        name=name,
        grid=(G, NC),
        in_specs=[xs, bb, cc, dtcs, dtrs, pc, pr, pc, pr, pc],
        out_specs=[xs, pl.BlockSpec((None, None, N, GW), lambda g, c: (c, g, 0, 0))],
        out_shape=[jax.ShapeDtypeStruct((LP, DI), F32), jax.ShapeDtypeStruct((NC, G, N, GW), F32)],
        scratch_shapes=[pltpu.VMEM((N, GW), F32)],
        compiler_params=_cparams(("parallel", "arbitrary")),
    )(xbc, xbc, xbc, dtc, dtr, bias_c, bias_r, alog_c, alog_r, dskip_c)


def _ssd_bwd(xbc, dy, states, dtc, dtr, bias_c, bias_r, alog_c, alog_r, dskip_c, name):
    def body(xs_ref, b_ref, c_ref, dy_ref, st_ref, dtc_ref, dtr_ref, bc_ref, br_ref, ac_ref, ar_ref, dk_ref,
             dx_ref, db_ref, dc_ref, ddt_ref, dbias_ref, dalog_ref, ddk_ref, dstate):
        ci = pl.program_id(1)
        c = NC - 1 - ci

        @pl.when(ci == 0)
        def _():
            dstate[...] = jnp.zeros_like(dstate)
            dbias_ref[...] = jnp.zeros_like(dbias_ref)
            dalog_ref[...] = jnp.zeros_like(dalog_ref)
            ddk_ref[...] = jnp.zeros_like(ddk_ref)

        cm = _ssd_common(dtc_ref, dtr_ref, bc_ref, br_ref, ac_ref, ar_ref, c)
        Bm = b_ref[...]
        Cm = c_ref[...]
        Bb = Bm.astype(BF16)
        Cb = Cm.astype(BF16)
        nt = (((1,), (1,)), ((), ()))
        cb = lax.dot_general(Cb, Bb, nt, preferred_element_type=F32)
        cbt = lax.dot_general(Bb, Cb, nt, preferred_element_type=F32)
        ct = Cm.T.astype(BF16)
        lane_lo = lax.broadcasted_iota(jnp.int32, (1, HP), 1) < P
        lane_id = lax.broadcasted_iota(jnp.int32, (1, LANES), 1)
        csc, csr, dtc_v, a_c = cm["csc"], cm["csr"], cm["dtc"], cm["a_c"]
        triu = cm["si"] >= cm["li"]
        ecs = jnp.exp(csc)
        cs_end = csc[Q - 1 : Q, :]
        wdec = jnp.exp(cs_end - csc)
        eend = jnp.exp(cs_end)
        dcb = jnp.zeros((Q, Q), F32)
        dcbt = jnp.zeros((Q, Q), F32)
        dcs = jnp.zeros((Q, LANES), F32)
        dcs_end = jnp.zeros((1, LANES), F32)
        ddt = jnp.zeros((Q, LANES), F32)
        ddk = jnp.zeros((1, LANES), F32)
        dB = jnp.zeros((Q, N), F32)
        dC = jnp.zeros((Q, N), F32)
        for pp in range(E // 2):
            h0 = 2 * pp
            sl = slice(pp * HP, (pp + 1) * HP)
            xp = xs_ref[:, sl]
            dyp = dy_ref[:, sl]
            dtx = _pair_expand(dtc_v, h0, lane_lo)
            xdt = xp * dtx
            dxdt = jnp.zeros((Q, HP), F32)
            stp = st_ref[:, sl]
            stb = stp.astype(BF16)
            dsn = dstate[:, sl]
            dsnb = dsn.astype(BF16)
            ecsx = _pair_expand(ecs, h0, lane_lo)
            wdx = _pair_expand(wdec, h0, lane_lo)
            cs_ = jnp.dot(Cb, stb, preferred_element_type=F32)
            yo = cs_ * ecsx
            dyo = dyp * ecsx
            dyob = dyo.astype(BF16)
            dC = dC + lax.dot_general(dyob, stb, nt, preferred_element_type=F32)
            ds_from_y = jnp.dot(ct, dyob, preferred_element_type=F32)
            dyyo = dyp * yo
            xw = xdt * wdx
            dB = dB + lax.dot_general(xw.astype(BF16), dsnb, nt, preferred_element_type=F32)
            dxw = jnp.dot(Bb, dsnb, preferred_element_type=F32)
            dxdt = dxdt + dxw * wdx
            w2 = dxw * xw
            rs = jnp.sum(dsn * stp, axis=0, keepdims=True) * _pair_expand(eend, h0, lane_lo)
            dstate[:, sl] = dsn * _pair_expand(eend, h0, lane_lo) + ds_from_y
            dyx = dyp * xp
            for e in range(2):
                h = h0 + e
                msk = lane_lo if e == 0 else jnp.logical_not(lane_lo)
                oh = (lane_id == h).astype(F32)
                col = csc[:, h : h + 1]
                row = csr[h : h + 1, :]
                lm = jnp.where(cm["tril"], jnp.exp(jnp.minimum(col - row, 0.0)), 0.0)
                lmt = jnp.where(triu, jnp.exp(jnp.minimum(row - col, 0.0)), 0.0)
                dye = jnp.where(msk, dyp, 0.0).astype(BF16)
                xde = jnp.where(msk, xdt, 0.0).astype(BF16)
                gm = lax.dot_general(dye, xde, nt, preferred_element_type=F32)
                gmt = lax.dot_general(xde, dye, nt, preferred_element_type=F32)
                gl = gm * lm
                glt = gmt * lmt
                dcb = dcb + gl
                dcbt = dcbt + glt
                dcs_h = (
                    jnp.sum(gl * cb, axis=1, keepdims=True)
                    - jnp.sum(glt * cbt, axis=1, keepdims=True)
                    + jnp.sum(jnp.where(msk, dyyo, 0.0), axis=1, keepdims=True)
                    - jnp.sum(jnp.where(msk, w2, 0.0), axis=1, keepdims=True)
                )
                dcs = dcs + dcs_h * oh
                dcs_end = dcs_end + (_sum_all(jnp.where(msk, w2, 0.0)) + _sum_all(jnp.where(msk, rs, 0.0))) * oh
                dxdt = dxdt + jnp.dot((cbt * lmt).astype(BF16), dye, preferred_element_type=F32)
                ddk = ddk + _sum_all(jnp.where(msk, dyx, 0.0)) * oh
            dx_ref[:, sl] = dxdt * dtx + dyp * _pair_expand(dk_ref[...], h0, lane_lo)
            dxx = dxdt * xp
            for e in range(2):
                h = h0 + e
                msk = lane_lo if e == 0 else jnp.logical_not(lane_lo)
                oh = (lane_id == h).astype(F32)
                ddt = ddt + jnp.sum(jnp.where(msk, dxx, 0.0), axis=1, keepdims=True) * oh
        dC = dC + jnp.dot(dcb.astype(BF16), Bb, preferred_element_type=F32)
        dB = dB + jnp.dot(dcbt.astype(BF16), Cb, preferred_element_type=F32)
        db_ref[...] = dB
        dc_ref[...] = dC
        last = (lax.broadcasted_iota(jnp.int32, (Q, 1), 0) == Q - 1).astype(F32)
        dcs = dcs + last * dcs_end
        dda = jnp.dot(triu.astype(F32), dcs, precision=HI, preferred_element_type=F32)
        ddt = ddt + dda * a_c
        da = jnp.sum(dda * dtc_v, axis=0, keepdims=True)
        draw = jnp.where(cm["rows"] >= PF, ddt * _sigmoid(cm["prec"]), 0.0)
        ddt_ref[...] = draw
        dbias_ref[...] += jnp.sum(draw, axis=0, keepdims=True)
        dalog_ref[...] += da * a_c
        ddk_ref[...] += ddk

    xs, bb, cc, dtcs, dtrs, pc, pr = _ssd_specs()

    def rev(spec_fn):
        return lambda g, ci: spec_fn(g, NC - 1 - ci)

    def rspec(spec):
        return pl.BlockSpec(spec.block_shape, rev(spec.index_map))

    xs_r, bb_r, cc_r, dtc_r, dtr_r = rspec(xs), rspec(bb), rspec(cc), rspec(dtcs), rspec(dtrs)
    st_r = pl.BlockSpec((None, None, N, GW), lambda g, ci: (NC - 1 - ci, g, 0, 0))
    gn = pl.BlockSpec((Q, N), lambda g, ci: (NC - 1 - ci, g))
    return pl.pallas_call(
        body,
        name=name,
        grid=(G, NC),
        in_specs=[xs_r, bb_r, cc_r, xs_r, st_r, dtc_r, dtr_r, pc, pr, pc, pr, pc],
        out_specs=[xs_r, gn, gn, dtc_r, pc, pc, pc],
        out_shape=[
            jax.ShapeDtypeStruct((LP, DI), F32),
            jax.ShapeDtypeStruct((LP, G * N), F32),
            jax.ShapeDtypeStruct((LP, G * N), F32),
            jax.ShapeDtypeStruct((G, LP, LANES), F32),
            jax.ShapeDtypeStruct((G, 1, LANES), F32),
            jax.ShapeDtypeStruct((G, 1, LANES), F32),
            jax.ShapeDtypeStruct((G, 1, LANES), F32),
        ],
        scratch_shapes=[pltpu.VMEM((N, GW), F32)],
        compiler_params=_cparams(("parallel", "arbitrary")),
    )(xbc, xbc, xbc, dy, states, dtc, dtr, bias_c, bias_r, alog_c, alog_r, dskip_c)


def _split_dot(x, u):
    hi = x.astype(BF16)
    lo = (x - hi.astype(F32)).astype(BF16)
    return jnp.dot(hi, u, preferred_element_type=F32) + jnp.dot(lo, u, preferred_element_type=F32)


def _sb_block(qe, kblk, vis, a_run, u_gt):
    l = lax.dot_general(qe, kblk, (((1,), (1,)), ((), ())), preferred_element_type=F32)
    lk = jnp.minimum(-l, 0.0) - jnp.log(1.0 + jnp.exp(-jnp.abs(l)))
    lbeta = l + lk
    if vis is not None:
        lk = jnp.where(vis, lk, 0.0)
    logw = lbeta + _split_dot(lk, u_gt) + a_run
    return lbeta, lk, logw


def _descend(i, block, carry):
    def pack(n, c):
        return (n, jnp.max(jnp.maximum(c[0], c[1]))) + tuple(c)

    st = pack(jnp.int32(1), block(i, carry, True))
    st = lax.while_loop(lambda st: (st[0] < i) & (st[1] > -T_SKIP), lambda st: pack(st[0] + 1, block(i - st[0], st[2:], False)), st)
    st = lax.while_loop(lambda st: (st[0] == i) & (st[1] > -T_SKIP), lambda st: pack(st[0] + 1, block(0, st[2:], True)), st)
    return st[2:]


def _attn_fwd(q, k, v, name):
    nq = LP // TQ

    def body(q_ref, k_ref, v_ref, o_ref):
        i = pl.program_id(1)
        qv = q_ref[...]
        lane_lo = lax.broadcasted_iota(jnp.int32, (1, HP), 1) < 64
        t_idx = i * TQ + lax.broadcasted_iota(jnp.int32, (TQ, 1), 0)
        ji = lax.broadcasted_iota(jnp.int32, (TQ, TQ), 0)
        si = lax.broadcasted_iota(jnp.int32, (TQ, TQ), 1)
        u_gt = (ji > si).astype(BF16)
        qs = [jnp.where(lane_lo, qv, jnp.zeros_like(qv)), jnp.where(lane_lo, jnp.zeros_like(qv), qv)]

        def block(kb, carry, masked):
            a0, a1, acc = carry
            off = pl.multiple_of(kb * TQ, TQ)
            kblk = k_ref[pl.ds(off, TQ), :]
            vblk = v_ref[pl.ds(off, TQ), :]
            vis = None
            if masked:
                s_idx = kb * TQ + lax.broadcasted_iota(jnp.int32, (1, TQ), 1)
                vis = (s_idx < t_idx) & (s_idx >= PF)
            new_a = []
            for e, a_run in enumerate((a0, a1)):
                _, lk, logw = _sb_block(qs[e], kblk, vis, a_run, u_gt)
                w = jnp.exp(logw)
                if masked:
                    w = jnp.where(vis, w, 0.0)
                msk = lane_lo if e == 0 else jnp.logical_not(lane_lo)
                acc = acc + jnp.dot(w.astype(BF16), jnp.where(msk, vblk, jnp.zeros_like(vblk)), preferred_element_type=F32)
                new_a.append(a_run + jnp.sum(lk, axis=1, keepdims=True))
            return new_a[0], new_a[1], acc

        z1 = jnp.zeros((TQ, 1), F32)
        _, _, acc = _descend(i, block, (z1, z1, jnp.zeros((TQ, HP), F32)))
        o_ref[...] = acc

    return pl.pallas_call(
        body,
        name=name,
        grid=(D // HP, nq),
        in_specs=[
            pl.BlockSpec((TQ, HP), lambda j, i: (i, j)),
            pl.BlockSpec((LP, HP), lambda j, i: (0, j)),
            pl.BlockSpec((LP, HP), lambda j, i: (0, j)),
        ],
        out_specs=pl.BlockSpec((TQ, HP), lambda j, i: (i, j)),
        out_shape=jax.ShapeDtypeStruct((LP, D), F32),
        compiler_params=_cparams(("parallel", "arbitrary")),
    )(q, k, v)


def _attn_bwd(q, k, v, o, do, name):
    nq = LP // TQ

    def body(q_ref, k_ref, v_ref, o_ref, do_ref, dq_ref, dk_ref, dv_ref):
        i = pl.program_id(1)

        @pl.when(i == 0)
        def _():
            dk_ref[...] = jnp.zeros_like(dk_ref)
            dv_ref[...] = jnp.zeros_like(dv_ref)

        qv = q_ref[...]
        dov = do_ref[...]
        lane_lo = lax.broadcasted_iota(jnp.int32, (1, HP), 1) < 64
        t_idx = i * TQ + lax.broadcasted_iota(jnp.int32, (TQ, 1), 0)
        ji = lax.broadcasted_iota(jnp.int32, (TQ, TQ), 0)
        si = lax.broadcasted_iota(jnp.int32, (TQ, TQ), 1)
        u_gt = (ji > si).astype(BF16)
        u_ge = (ji >= si).astype(BF16)
        msks = [lane_lo, jnp.logical_not(lane_lo)]
        qs = [jnp.where(m, qv, jnp.zeros_like(qv)) for m in msks]
        dob = [jnp.where(m, dov, 0.0).astype(BF16) for m in msks]
        ov = o_ref[...]
        deltas = [jnp.sum(d.astype(F32) * ov, axis=1, keepdims=True) for d in dob]
        nt = (((1,), (1,)), ((), ()))
        tn = (((0,), (0,)), ((), ()))

        def block(kb, carry, masked):
            a0, a1, p0, p1, dq = carry
            off = pl.multiple_of(kb * TQ, TQ)
            kblk = k_ref[pl.ds(off, TQ), :]
            vblk = v_ref[pl.ds(off, TQ), :]
            vis = None
            if masked:
                s_idx = kb * TQ + lax.broadcasted_iota(jnp.int32, (1, TQ), 1)
                vis = (s_idx < t_idx) & (s_idx >= PF)
            new_a, new_p = [], []
            dk_acc = jnp.zeros((TQ, HP), F32)
            dv_acc = jnp.zeros((TQ, HP), F32)
            for e, (a_run, p_run) in enumerate(((a0, p0), (a1, p1))):
                lbeta, lk, logw = _sb_block(qs[e], kblk, vis, a_run, u_gt)
                sig = jnp.exp(lbeta)
                w = jnp.exp(logw)
                if masked:
                    w = jnp.where(vis, w, 0.0)
                wb = w.astype(BF16)
                dw = lax.dot_general(dob[e], vblk, nt, preferred_element_type=F32)
                pm = wb.astype(F32) * dw
                cum_p = deltas[e] - (_split_dot(pm, u_ge) + p_run)
                dl = pm - (pm + cum_p) * sig
                if masked:
                    dl = jnp.where(vis, dl, 0.0)
                dl = dl.astype(BF16)
                km = jnp.where(msks[e], kblk, jnp.zeros_like(kblk))
                dq = dq + jnp.dot(dl, km, preferred_element_type=F32)
                dk_acc = dk_acc + lax.dot_general(dl, qs[e], tn, preferred_element_type=F32)
                dv_acc = dv_acc + lax.dot_general(wb, dob[e], tn, preferred_element_type=F32)
                new_a.append(a_run + jnp.sum(lk, axis=1, keepdims=True))
                new_p.append(p_run + jnp.sum(pm, axis=1, keepdims=True))
            dk_ref[pl.ds(off, TQ), :] += dk_acc
            dv_ref[pl.ds(off, TQ), :] += dv_acc
            return new_a[0], new_a[1], new_p[0], new_p[1], dq

        z1 = jnp.zeros((TQ, 1), F32)
        carry = _descend(i, block, (z1, z1, z1, z1, jnp.zeros((TQ, HP), F32)))
        dq_ref[...] = carry[4]

    blk = pl.BlockSpec((TQ, HP), lambda j, i: (i, j))
    full = pl.BlockSpec((LP, HP), lambda j, i: (0, j))
    big = jax.ShapeDtypeStruct((LP, D), F32)
    return pl.pallas_call(
        body,
        name=name,
        grid=(D // HP, nq),
        in_specs=[blk, full, full, blk, blk],
        out_specs=[blk, full, full],
        out_shape=[big, big, big],
        compiler_params=_cparams(("parallel", "arbitrary")),
    )(q, k, v, o, do)


ROW_TILES = (2048, 1024, 512, 256, 128, 64, 32, 16, 8)


def _row_tile(rows, cols, budget):
    if rows % 8:
        return rows
    return _pick(rows, tuple(t for t in ROW_TILES if t * cols <= budget) or (8,))


def _adamw(w, g, m, v, name):
    R, C = w.shape
    tr = _row_tile(R, C, 128 * 1024)

    def body(w_ref, g_ref, m_ref, v_ref, g_out, d_out, m_out, v_out):
        g = g_ref[...]
        mn = ADAM_B1 * m_ref[...] + (1.0 - ADAM_B1) * g
        vn = ADAM_B2 * v_ref[...] + (1.0 - ADAM_B2) * (g * g)
        mh = mn / (1.0 - ADAM_B1**ADAM_STEP)
        vh = vn / (1.0 - ADAM_B2**ADAM_STEP)
        g_out[...] = g
        d_out[...] = -ADAM_LR * (mh / (jnp.sqrt(vh) + ADAM_EPS) + ADAM_WD * w_ref[...])
        m_out[...] = mn
        v_out[...] = vn

    blk = pl.BlockSpec((tr, C), lambda i: (i, 0))
    sh = jax.ShapeDtypeStruct((R, C), F32)
    return pl.pallas_call(
        body,
        name=name,
        grid=(R // tr,),
        in_specs=[blk] * 4,
        out_specs=[blk] * 4,
        out_shape=[sh] * 4,
        compiler_params=_cparams(("parallel",)),
    )(w, g, m, v)


def _sum4(buf, name):
    _, R, C = buf.shape
    tr = _row_tile(R, C, 128 * 1024)

    def body(b_ref, o_ref):
        acc = b_ref[0].astype(F32)
        for s in range(1, NCHIP):
            acc = acc + b_ref[s].astype(F32)
        o_ref[...] = acc

    return pl.pallas_call(
        body,
        name=name,
        grid=(R // tr,),
        in_specs=[pl.BlockSpec((NCHIP, tr, C), lambda i: (0, i, 0))],
        out_specs=pl.BlockSpec((tr, C), lambda i: (i, 0)),
        out_shape=jax.ShapeDtypeStruct((R, C), F32),
        compiler_params=_cparams(("parallel",)),
    )(buf)


def _add2(a, b, name):
    S, R, C = a.shape
    tr = _row_tile(R, C, 256 * 1024)

    def body(a_ref, b_ref, o_ref):
        o_ref[...] = (a_ref[...].astype(F32) + b_ref[...].astype(F32)).astype(o_ref.dtype)

    blk = pl.BlockSpec((None, tr, C), lambda s, i: (s, i, 0))
    return pl.pallas_call(
        body,
        name=name,
        grid=(S, R // tr),
        in_specs=[blk, blk],
        out_specs=blk,
        out_shape=jax.ShapeDtypeStruct(a.shape, a.dtype),
        compiler_params=_cparams(("parallel", "parallel")),
    )(a, b)


ANY = pl.BlockSpec(memory_space=pl.ANY)


def _mesh_place():
    x, y, c = lax.axis_index("x"), lax.axis_index("y"), lax.axis_index("c")
    return x, y, c, 2 * x + y, [(1 - x, y), (x, 1 - y), (1 - x, 1 - y)]


def _gather_chips(bufs, name):
    nb = len(bufs)

    def body(*refs):
        ins = refs[:nb]
        outs = refs[nb : 2 * nb]
        send_sems, recv_sems, loc_sems = refs[2 * nb :]
        x, y, c, me, peers = _mesh_place()

        def half(ref, hc):
            hr = ref.shape[0] // 2
            return ref.at[pl.ds(hc * hr, hr)]

        def copy(k, b, src, slot, hc, to):
            return pltpu.make_async_remote_copy(
                src_ref=src, dst_ref=half(outs[b].at[slot], hc), send_sem=send_sems.at[k * nb + b],
                recv_sem=recv_sems.at[k * nb + b], device_id=to, device_id_type=MESH)

        locs = [pltpu.make_async_copy(ins[b], outs[b].at[me], loc_sems.at[b]) for b in range(nb)]
        for cp in locs:
            cp.start()
        first = [copy(k, b, half(ins[b], c), me, c, (px, py, c)) for k, (px, py) in enumerate(peers) for b in range(nb)]
        for cp in first:
            cp.start()
        passed = []
        for k, (px, py) in enumerate(peers):
            for b in range(nb):
                src = half(outs[b].at[2 * px + py], c)
                copy(k, b, src, 2 * px + py, c, (px, py, c)).wait_recv()
                cp = copy(3 + k, b, src, 2 * px + py, c, (x, y, 1 - c))
                cp.start()
                passed.append(cp)
        for k, (px, py) in enumerate(peers):
            for b in range(nb):
                copy(3 + k, b, half(outs[b].at[2 * px + py], 1 - c), 2 * px + py, 1 - c, (x, y, 1 - c)).wait_recv()
        for cp in first + passed:
            cp.wait_send()
        for cp in locs:
            cp.wait()

    return pl.pallas_call(
        body,
        name=name,
        in_specs=[ANY] * nb,
        out_specs=[ANY] * nb,
        out_shape=[jax.ShapeDtypeStruct((NCHIP,) + tuple(b.shape), b.dtype) for b in bufs],
        scratch_shapes=[pltpu.SemaphoreType.DMA((6 * nb,)), pltpu.SemaphoreType.DMA((6 * nb,)), pltpu.SemaphoreType.DMA((nb,))],
    )(*bufs)


def _scatter_chips(bufs, name):
    nb = len(bufs)

    def body(*refs):
        ins = refs[:nb]
        outs = refs[nb : 2 * nb]
        send_sems, recv_sems, loc_sems = refs[2 * nb :]
        x, y, c, me, peers = _mesh_place()

        def copy(k, b, slot_from, slot_to, to):
            return pltpu.make_async_remote_copy(
                src_ref=ins[b].at[slot_from], dst_ref=outs[b].at[slot_to], send_sem=send_sems.at[k * nb + b],
                recv_sem=recv_sems.at[k * nb + b], device_id=to, device_id_type=MESH)

        locs = [pltpu.make_async_copy(ins[b].at[me], outs[b].at[me], loc_sems.at[b]) for b in range(nb)]
        for cp in locs:
            cp.start()
        sends = [copy(k, b, 2 * px + py, me, (px, py, c)) for k, (px, py) in enumerate(peers) for b in range(nb)]
        for cp in sends:
            cp.start()
        for k, (px, py) in enumerate(peers):
            for b in range(nb):
                copy(k, b, me, 2 * px + py, (px, py, c)).wait_recv()
        for cp in sends:
            cp.wait_send()
        for cp in locs:
            cp.wait()

    return pl.pallas_call(
        body,
        name=name,
        in_specs=[ANY] * nb,
        out_specs=[ANY] * nb,
        out_shape=[jax.ShapeDtypeStruct(b.shape, b.dtype) for b in bufs],
        scratch_shapes=[pltpu.SemaphoreType.DMA((3 * nb,)), pltpu.SemaphoreType.DMA((3 * nb,)), pltpu.SemaphoreType.DMA((nb,))],
    )(*bufs)


def _split_cores(bufs, name):
    nb = len(bufs)

    def body(*refs):
        ins = refs[:nb]
        mine = refs[nb : 2 * nb]
        theirs = refs[2 * nb : 3 * nb]
        send_sems, recv_sems, loc_sems = refs[3 * nb :]
        x, y, c, _, _ = _mesh_place()
        cps, locs = [], []
        for b in range(nb):
            hr = ins[b].shape[1] // 2
            loc = pltpu.make_async_copy(ins[b].at[:, pl.ds(c * hr, hr)], mine[b], loc_sems.at[b])
            loc.start()
            locs.append(loc)
            cp = pltpu.make_async_remote_copy(
                src_ref=ins[b].at[:, pl.ds((1 - c) * hr, hr)], dst_ref=theirs[b], send_sem=send_sems.at[b],
                recv_sem=recv_sems.at[b], device_id=(x, y, 1 - c), device_id_type=MESH)
            cp.start()
            cps.append(cp)
        for cp in cps:
            cp.wait()
        for cp in locs:
            cp.wait()

    halves = [jax.ShapeDtypeStruct((b.shape[0], b.shape[1] // 2, b.shape[2]), b.dtype) for b in bufs]
    outs = pl.pallas_call(
        body,
        name=name,
        in_specs=[ANY] * nb,
        out_specs=[ANY] * (2 * nb),
        out_shape=halves + halves,
        scratch_shapes=[pltpu.SemaphoreType.DMA((nb,)), pltpu.SemaphoreType.DMA((nb,)), pltpu.SemaphoreType.DMA((nb,))],
    )(*bufs)
    return outs[:nb], outs[nb:]


def _join_cores(bufs, name):
    nb = len(bufs)

    def body(*refs):
        ins = refs[:nb]
        outs = refs[nb : 2 * nb]
        send_sems, recv_sems, loc_sems = refs[2 * nb :]
        x, y, c, _, _ = _mesh_place()
        cps, locs = [], []
        for b in range(nb):
            hr = ins[b].shape[0]
            loc = pltpu.make_async_copy(ins[b], outs[b].at[pl.ds(c * hr, hr)], loc_sems.at[b])
            loc.start()
            locs.append(loc)
            cp = pltpu.make_async_remote_copy(
                src_ref=ins[b], dst_ref=outs[b].at[pl.ds(c * hr, hr)], send_sem=send_sems.at[b], recv_sem=recv_sems.at[b],
                device_id=(x, y, 1 - c), device_id_type=MESH)
            cp.start()
            cps.append(cp)
        for b, cp in enumerate(cps):
            hr = ins[b].shape[0]
            cp.wait_send()
            pltpu.make_async_remote_copy(
                src_ref=ins[b], dst_ref=outs[b].at[pl.ds((1 - c) * hr, hr)], send_sem=send_sems.at[b],
                recv_sem=recv_sems.at[b], device_id=(x, y, 1 - c), device_id_type=MESH).wait_recv()
        for cp in locs:
            cp.wait()

    return pl.pallas_call(
        body,
        name=name,
        in_specs=[ANY] * nb,
        out_specs=[ANY] * nb,
        out_shape=[jax.ShapeDtypeStruct((2 * b.shape[0], b.shape[1]), b.dtype) for b in bufs],
        scratch_shapes=[pltpu.SemaphoreType.DMA((nb,)), pltpu.SemaphoreType.DMA((nb,)), pltpu.SemaphoreType.DMA((nb,))],
    )(*bufs)


ROW_ALIGN = 1024


def _pack(pieces, dtype):
    flat = []
    for p in pieces:
        f = p.reshape(-1).astype(dtype)
        pad = (-f.shape[0]) % LANES
        if pad:
            f = jnp.pad(f, (0, pad))
        flat.append(f)
    tot = sum(f.shape[0] for f in flat)
    pad = (-tot) % (ROW_ALIGN * LANES)
    if pad:
        flat.append(jnp.zeros((pad,), dtype))
    return jnp.concatenate(flat).reshape(-1, LANES)


def _unpack(buf, shapes):
    lead = buf.shape[:-2]
    flat = buf.reshape(lead + (-1,))
    out = []
    off = 0
    for shp in shapes:
        n = 1
        for d in shp:
            n *= d
        out.append(flat[..., off : off + n].reshape(lead + tuple(shp)))
        off += n + ((-n) % LANES)
    return out


PARAMS = (
    ("meta_tokens", 1, "small"), ("ssd_norm", 1, "small"), ("ssd_w_in", 2, "big"), ("ssd_conv_w", 2, "small"),
    ("ssd_conv_b", 1, "small"), ("ssd_dt_bias", None, "rep"), ("ssd_a_log", None, "rep"), ("ssd_d_skip", None, "rep"),
    ("ssd_gate_norm", 1, "small"), ("ssd_w_out", 1, "big"), ("kv_norm", None, "rep"), ("w_kv", 1, "big"),
    ("sb_norm", None, "rep"), ("sb_w_q", 1, "big"), ("sb_w_o", 1, "big"), ("ffn_norm", None, "rep"),
    ("ffn_w_up", 2, "big"), ("ffn_conv_w", 2, "small"), ("ffn_conv_b", None, "rep"), ("ffn_w_down", 1, "big"),
    ("final_norm", None, "rep"),
)


def _head_cols(vec):
    return jnp.pad(vec.reshape(G, 1, E), ((0, 0), (0, 0), (0, LANES - E)))


def _head_rows(vec):
    return jnp.pad(vec.reshape(G, E, 1), ((0, 0), (0, 8 - E), (0, 0)))


def kernel(x, meta_tokens, ssd_norm, ssd_w_in, ssd_conv_w, ssd_conv_b, ssd_dt_bias, ssd_a_log, ssd_d_skip, ssd_gate_norm, ssd_w_out, kv_norm, w_kv, sb_norm, sb_w_q, sb_w_o, ffn_norm, ffn_w_up, ffn_conv_w, ffn_conv_b, ffn_w_down, final_norm, loss_target, m_meta_tokens, m_ssd_norm, m_ssd_w_in, m_ssd_conv_w, m_ssd_conv_b, m_ssd_dt_bias, m_ssd_a_log, m_ssd_d_skip, m_ssd_gate_norm, m_ssd_w_out, m_kv_norm, m_w_kv, m_sb_norm, m_sb_w_q, m_sb_w_o, m_ffn_norm, m_ffn_w_up, m_ffn_conv_w, m_ffn_conv_b, m_ffn_w_down, m_final_norm, v_meta_tokens, v_ssd_norm, v_ssd_w_in, v_ssd_conv_w, v_ssd_conv_b, v_ssd_dt_bias, v_ssd_a_log, v_ssd_d_skip, v_ssd_gate_norm, v_ssd_w_out, v_kv_norm, v_w_kv, v_sb_norm, v_sb_w_q, v_sb_w_o, v_ffn_norm, v_ffn_w_up, v_ffn_conv_w, v_ffn_conv_b, v_ffn_w_down, v_final_norm):
    local = dict(meta_tokens=meta_tokens, ssd_norm=ssd_norm, ssd_w_in=ssd_w_in, ssd_conv_w=ssd_conv_w, ssd_conv_b=ssd_conv_b, ssd_dt_bias=ssd_dt_bias, ssd_a_log=ssd_a_log, ssd_d_skip=ssd_d_skip, ssd_gate_norm=ssd_gate_norm, ssd_w_out=ssd_w_out, kv_norm=kv_norm, w_kv=w_kv, sb_norm=sb_norm, sb_w_q=sb_w_q, sb_w_o=sb_w_o, ffn_norm=ffn_norm, ffn_w_up=ffn_w_up, ffn_conv_w=ffn_conv_w, ffn_conv_b=ffn_conv_b, ffn_w_down=ffn_w_down, final_norm=final_norm)
    mom_m = dict(meta_tokens=m_meta_tokens, ssd_norm=m_ssd_norm, ssd_w_in=m_ssd_w_in, ssd_conv_w=m_ssd_conv_w, ssd_conv_b=m_ssd_conv_b, ssd_dt_bias=m_ssd_dt_bias, ssd_a_log=m_ssd_a_log, ssd_d_skip=m_ssd_d_skip, ssd_gate_norm=m_ssd_gate_norm, ssd_w_out=m_ssd_w_out, kv_norm=m_kv_norm, w_kv=m_w_kv, sb_norm=m_sb_norm, sb_w_q=m_sb_w_q, sb_w_o=m_sb_w_o, ffn_norm=m_ffn_norm, ffn_w_up=m_ffn_w_up, ffn_conv_w=m_ffn_conv_w, ffn_conv_b=m_ffn_conv_b, ffn_w_down=m_ffn_w_down, final_norm=m_final_norm)
    mom_v = dict(meta_tokens=v_meta_tokens, ssd_norm=v_ssd_norm, ssd_w_in=v_ssd_w_in, ssd_conv_w=v_ssd_conv_w, ssd_conv_b=v_ssd_conv_b, ssd_dt_bias=v_ssd_dt_bias, ssd_a_log=v_ssd_a_log, ssd_d_skip=v_ssd_d_skip, ssd_gate_norm=v_ssd_gate_norm, ssd_w_out=v_ssd_w_out, kv_norm=v_kv_norm, w_kv=v_w_kv, sb_norm=v_sb_norm, sb_w_q=v_sb_w_q, sb_w_o=v_sb_w_o, ffn_norm=v_ffn_norm, ffn_w_up=v_ffn_w_up, ffn_conv_w=v_ffn_conv_w, ffn_conv_b=v_ffn_conv_b, ffn_w_down=v_ffn_w_down, final_norm=v_final_norm)

    big_names = [n for n, _, kind in PARAMS if kind == "big"]
    small_names = [n for n, _, kind in PARAMS if kind == "small"]
    rep_names = [n for n, _, kind in PARAMS if kind == "rep"]
    axis_of = {n: ax for n, ax, _ in PARAMS}

    def rows2(a):
        return a.reshape(-1, a.shape[-1])

    small_own = _pack([local[n] for n in small_names], F32)
    gathered = _gather_chips([rows2(local[n]).astype(BF16) for n in big_names] + [small_own], "gather_weights")
    full = {}
    for n, buf in zip(big_names, gathered[:-1]):
        p = buf.reshape((NCHIP,) + local[n].shape)
        full[n] = jnp.concatenate([p[s] for s in range(NCHIP)], axis=axis_of[n])
    for n, p in zip(small_names, _unpack(gathered[-1], [local[n].shape for n in small_names])):
        full[n] = jnp.concatenate([p[s] for s in range(NCHIP)], axis=axis_of[n])
    for n in rep_names:
        full[n] = local[n]

    w_in = full["ssd_w_in"][0]
    w_z, w_xbc = w_in[:, :DI], w_in[:, DI : DI + CD]
    w_dt = jnp.pad(w_in[:, DI + CD :], ((0, 0), (0, LANES - H)))
    w_out = full["ssd_w_out"][0]
    wkv = full["w_kv"]
    w_q = full["sb_w_q"][0]
    w_o = full["sb_w_o"][0]
    w_up_g = [full["ffn_w_up"][l][:, :DFF] for l in range(2)]
    w_up_v = [full["ffn_w_up"][l][:, DFF:] for l in range(2)]
    w_down = [full["ffn_w_down"][l] for l in range(2)]
    fcw, fcb = full["ffn_conv_w"], full["ffn_conv_b"]
    scw, scb = full["ssd_conv_w"][0], full["ssd_conv_b"]
    bias_c, bias_r = _head_cols(full["ssd_dt_bias"][0]), _head_rows(full["ssd_dt_bias"][0])
    alog_c, alog_r = _head_cols(full["ssd_a_log"][0]), _head_rows(full["ssd_a_log"][0])
    dskip_c = _head_cols(full["ssd_d_skip"][0])
    kvn = full["kv_norm"].reshape(1, D)
    fin = full["final_norm"].reshape(1, D)

    h0 = jnp.concatenate([jnp.zeros((PF, D), F32), full["meta_tokens"], x[0]], axis=0)
    (u0,) = _rms_fwd(h0, [full["ssd_norm"]], "ssd_norm_fwd")
    z = _mm(u0, w_z, name="ssd_in_z")
    xr = _mm(u0, w_xbc, name="ssd_in_xbc")
    dt_raw = _mm(u0, w_dt, name="ssd_in_dt")
    xbc = _ssd_conv_fwd(xr, scw, scb, "ssd_conv_fwd")
    dth = dt_raw[:, :H].reshape(LP, G, E)
    dtc = jnp.pad(jnp.transpose(dth, (1, 0, 2)), ((0, 0), (0, 0), (0, LANES - E)))
    dtr = jnp.pad(jnp.transpose(dth, (1, 2, 0)), ((0, 0), (0, 8 - E), (0, 0)))
    y, states = _ssd_fwd(xbc, dtc, dtr, bias_c, bias_r, alog_c, alog_r, dskip_c, "ssd_scan_fwd")
    hgn = _gate_fwd(y, z, full["ssd_gate_norm"], "ssd_gate_fwd")
    h1 = _mm(hgn, w_out, add=h0, mask_rows=True, name="ssd_out")

    def ffn_fwd(h, l, tag):
        (u,) = _rms_fwd(h, [full["ffn_norm"][l : l + 1]], f"ffn{tag}_norm_fwd")
        hg = _mm(u, w_up_g[l], name=f"ffn{tag}_up_g")
        hv = _mm(u, w_up_v[l], name=f"ffn{tag}_up_v")
        act = _ffn_act_fwd(hg, hv, fcw[l][:, :DFF], fcw[l][:, DFF:], fcb[l : l + 1, :DFF], fcb[l : l + 1, DFF:], f"ffn{tag}_act_fwd")
        hn = _mm(act, w_down[l], add=h, mask_rows=True, name=f"ffn{tag}_down")
        return hn, (u, hg, hv, act)

    h2, ffn0 = ffn_fwd(h1, 0, "0")
    ukv, uq = _rms_fwd(h2, [kvn, full["sb_norm"]], "attn_norm_fwd")
    kk = _mm(ukv, wkv[:, :D], out_dtype=BF16, name="attn_k")
    vv = _mm(ukv, wkv[:, D:], out_dtype=BF16, name="attn_v")
    qq = _mm(uq, w_q, out_dtype=BF16, scale=64.0**-0.5, name="attn_q")
    o = _attn_fwd(qq, kk, vv, "attn_fwd")
    h3 = _mm(o, w_o, add=h2, mask_rows=True, name="attn_out")
    h4, ffn1 = ffn_fwd(h3, 1, "1")
    dh, g_final, loss_rows = _loss_head(h4, fin, loss_target[0], "loss_head")
    loss = lax.psum(0.5 / D * jnp.sum(loss_rows), ("x", "y", "c"))

    grads = {"final_norm": g_final.reshape(D)}

    def ffn_bwd(dh, h, l, saved, tag):
        u, hg, hv, act = saved
        da = _mm(dh, w_down[l], tb=True, name=f"ffn{tag}_down_dx")
        gw_down = _mm(act, dh, ta=True, out_dtype=BF16, name=f"ffn{tag}_down_dw")
        dhg, dhv, dwg, dwv, dbg, dbv = _ffn_act_bwd(hg, hv, da, fcw[l][:, :DFF], fcw[l][:, DFF:], fcb[l : l + 1, :DFF], fcb[l : l + 1, DFF:], f"ffn{tag}_act_bwd")
        gw_up = jnp.concatenate([_mm(u, dhg, ta=True, out_dtype=BF16, name=f"ffn{tag}_up_g_dw"), _mm(u, dhv, ta=True, out_dtype=BF16, name=f"ffn{tag}_up_v_dw")], axis=1)
        du = _mm(dhg, w_up_g[l], tb=True, name=f"ffn{tag}_up_g_dx")
        du = _mm(dhv, w_up_v[l], tb=True, add=du, name=f"ffn{tag}_up_v_dx")
        dh_new, (gn,) = _rms_bwd(dh, h, [du], [full["ffn_norm"][l : l + 1]], f"ffn{tag}_norm_bwd")
        return dh_new, gw_down, gw_up, jnp.concatenate([dwg, dwv], axis=1), jnp.concatenate([dbg, dbv], axis=1), gn

    dh, gd1, gu1, gcw1, gcb1, gn1 = ffn_bwd(dh, h3, 1, ffn1, "1")
    do = _mm(dh, w_o, tb=True, name="attn_out_dx")
    grads["sb_w_o"] = _mm(o, dh, ta=True, out_dtype=BF16, name="attn_out_dw")[None]
    dq, dk, dv = _attn_bwd(qq, kk, vv, o, do, "attn_bwd")
    grads["sb_w_q"] = _mm(uq, dq, ta=True, out_dtype=BF16, scale=64.0**-0.5, name="attn_q_dw")[None]
    grads["w_kv"] = jnp.concatenate([_mm(ukv, dk, ta=True, out_dtype=BF16, name="attn_k_dw"), _mm(ukv, dv, ta=True, out_dtype=BF16, name="attn_v_dw")], axis=1)
    duq = _mm(dq, w_q, tb=True, scale=64.0**-0.5, name="attn_q_dx")
    dukv = _mm(dk, wkv[:, :D], tb=True, name="attn_k_dx")
    dukv = _mm(dv, wkv[:, D:], tb=True, add=dukv, name="attn_v_dx")
    dh, (g_kvn, g_sbn) = _rms_bwd(dh, h2, [dukv, duq], [kvn, full["sb_norm"]], "attn_norm_bwd")
    grads["kv_norm"] = g_kvn.reshape(D)
    grads["sb_norm"] = g_sbn
    dh, gd0, gu0, gcw0, gcb0, gn0 = ffn_bwd(dh, h1, 0, ffn0, "0")
    grads["ffn_w_down"] = jnp.stack([gd0, gd1])
    grads["ffn_w_up"] = jnp.stack([gu0, gu1])
    grads["ffn_conv_w"] = jnp.stack([gcw0, gcw1])
    grads["ffn_conv_b"] = jnp.concatenate([gcb0, gcb1], axis=0)
    grads["ffn_norm"] = jnp.concatenate([gn0, gn1], axis=0)
    dhgn = _mm(dh, w_out, tb=True, name="ssd_out_dx")
    grads["ssd_w_out"] = _mm(hgn, dh, ta=True, out_dtype=BF16, name="ssd_out_dw")[None]
    dy, dz, g_gate = _gate_bwd(dhgn, y, z, full["ssd_gate_norm"], "ssd_gate_bwd")
    grads["ssd_gate_norm"] = g_gate
    dxs, dB, dC, ddt_raw, g_bias, g_alog, g_dskip = _ssd_bwd(xbc, dy, states, dtc, dtr, bias_c, bias_r, alog_c, alog_r, dskip_c, "ssd_scan_bwd")
    grads["ssd_dt_bias"] = g_bias[:, 0, :E].reshape(1, H)
    grads["ssd_a_log"] = g_alog[:, 0, :E].reshape(1, H)
    grads["ssd_d_skip"] = g_dskip[:, 0, :E].reshape(1, H)
    dxr, g_scw, g_scb = _ssd_conv_bwd(xr, jnp.concatenate([dxs, dB, dC], axis=1), scw, scb, "ssd_conv_bwd")
    grads["ssd_conv_w"] = g_scw[None]
    grads["ssd_conv_b"] = g_scb
    ddt = jnp.pad(jnp.transpose(ddt_raw[:, :, :E], (1, 0, 2)).reshape(LP, H), ((0, 0), (0, LANES - H)))
    grads["ssd_w_in"] = jnp.concatenate(
        [_mm(u0, dz, ta=True, out_dtype=BF16, name="ssd_in_z_dw"), _mm(u0, dxr, ta=True, out_dtype=BF16, name="ssd_in_xbc_dw"), _mm(u0, ddt, ta=True, out_dtype=BF16, name="ssd_in_dt_dw")[:, :H]], axis=1)[None]
    du = _mm(dz, w_z, tb=True, name="ssd_in_z_dx")
    du = _mm(dxr, w_xbc, tb=True, add=du, name="ssd_in_xbc_dx")
    du = _mm(ddt, w_dt, tb=True, add=du, name="ssd_in_dt_dx")
    dh, (g_ssdn,) = _rms_bwd(dh, h0, [du], [full["ssd_norm"]], "ssd_norm_bwd")
    grads["ssd_norm"] = g_ssdn
    grads["meta_tokens"] = dh[PF : PF + N_META]
    grad_x = dh[PF + N_META :][None]

    def shard_pieces(names, s):
        out = []
        for n in names:
            ax = axis_of[n]
            out.append(grads[n] if ax is None else jnp.split(grads[n], NCHIP, axis=ax)[s])
        return out

    bufs = [jnp.stack([rows2(p) for p in jnp.split(grads[n], NCHIP, axis=axis_of[n])]) for n in big_names]
    bufs.append(jnp.stack([_pack(shard_pieces(small_names + rep_names, s), F32) for s in range(NCHIP)]))
    mine, theirs = _split_cores(bufs, "split_cores")
    pair = [_add2(a, b, f"pair_sum_{i}") for i, (a, b) in enumerate(zip(mine, theirs))]
    got = _scatter_chips(pair, "scatter_grads")
    sums = [_sum4(b, f"sum_chips_{i}") for i, b in enumerate(got)]
    gsum = _join_cores(sums, "join_cores")

    def rows(a):
        f = a.reshape(-1)
        pad = (-f.shape[0]) % LANES
        if pad:
            f = jnp.pad(f, (0, pad))
        return f.reshape(-1, LANES)

    order = [n for n, _, _ in PARAMS]
    res = {}
    for n, g2 in zip(big_names, gsum[:-1]):
        outs = _adamw(rows2(local[n]), g2, rows2(mom_m[n]), rows2(mom_v[n]), f"adamw_{n}")
        res[n] = [o_.reshape(local[n].shape) for o_ in outs]
    rest = small_names + rep_names
    for n, g1 in zip(rest, _unpack(gsum[-1], [local[n].shape for n in rest])):
        shp = local[n].shape
        cnt = 1
        for d in shp:
            cnt *= d
        outs = _adamw(rows(local[n]), rows(g1), rows(mom_m[n]), rows(mom_v[n]), f"adamw_{n}")
        res[n] = [o_.reshape(-1)[:cnt].reshape(shp) for o_ in outs]
    return (loss, grad_x, *[res[n][0] for n in order], *[res[n][1] for n in order], *[res[n][2] for n in order], *[res[n][3] for n in order])
```

```python
import functools

import jax
import jax.numpy as jnp
from jax import lax
from jax.experimental import pallas as pl
from jax.experimental.pallas import tpu as pltpu

D = 1024
SEQ = 8192
N_META = 16
EPS = 1e-6
P = 64
G = 4
N = 128
CONVW = 4
Q = 256
FC = 3
DFF = 256 * ((8 * D // 3 + 255) // 256)
DI = 2 * D
H = DI // P
E = H // G
GW = E * P
CD = DI + 2 * G * N
IN = DI + CD + H
SBH = D // 64
HP = 128
LANES = 128
PF = Q - N_META
LP = PF + N_META + SEQ
NC = LP // Q
TQ = 256
NCHIP = 4
ADAM_LR, ADAM_B1, ADAM_B2, ADAM_EPS, ADAM_WD, ADAM_STEP = 0.001, 0.9, 0.999, 1e-08, 0.01, 10

F32 = jnp.float32
BF16 = jnp.bfloat16
HI = lax.Precision.HIGHEST
MESH = pl.DeviceIdType.MESH
VMEM_LIMIT = 48 * 1024 * 1024
MM_MAX_K = 3072
T_SKIP = 110.0


def _pick(n, cands):
    for c in cands:
        if n % c == 0:
            return c
    raise ValueError((n, cands))


def _cparams(sem):
    return pltpu.CompilerParams(dimension_semantics=sem, vmem_limit_bytes=VMEM_LIMIT)


def _valid_rows(block, rows):
    r = block * rows + lax.broadcasted_iota(jnp.int32, (rows, 1), 0)
    return r >= PF


def _sigmoid(x):
    return 1.0 / (1.0 + jnp.exp(-x))


def _softplus(x):
    return jnp.maximum(x, 0.0) + jnp.log(1.0 + jnp.exp(-jnp.abs(x)))


def _sum_all(x):
    return jnp.sum(jnp.sum(x, axis=1, keepdims=True), axis=0, keepdims=True)


def _dsilu(x):
    s = _sigmoid(x)
    return s * (1.0 + x * (1.0 - s))


def _mm(a, b, *, ta=False, tb=False, out_dtype=F32, add=None, mask_rows=False, scale=None, name):
    if ta:
        K, M = a.shape
    else:
        M, K = a.shape
    if tb:
        Nn, K2 = b.shape
    else:
        K2, Nn = b.shape
    assert K == K2, (a.shape, b.shape, ta, tb)
    tn = _pick(Nn, (1408, 1024, 768, 512, 256, 128))
    if ta:
        tm = _pick(M, (1408, 1024, 768, 512, 256, 128))
        tk = _pick(K, (768, 512, 256))
    else:
        tm = _pick(M, (768, 256))
        tk = K if K <= MM_MAX_K else _pick(K, (1024, 768, 512, 256, 128))
    nk = K // tk
    dims = (((0 if ta else 1,), (1 if tb else 0,)), ((), ()))

    def body(*refs):
        a_ref, b_ref = refs[0], refs[1]
        add_ref = refs[2] if add is not None else None
        o_ref = refs[3] if add is not None else refs[2]
        acc = refs[-1] if nk > 1 else None

        def finish(r):
            if scale is not None:
                r = r * scale
            if mask_rows:
                r = jnp.where(_valid_rows(pl.program_id(0), tm), r, 0.0)
            if add_ref is not None:
                r = r + add_ref[...]
            o_ref[...] = r.astype(out_dtype)

        part = lax.dot_general(a_ref[...].astype(BF16), b_ref[...].astype(BF16), dims, preferred_element_type=F32)
        if nk == 1:
            finish(part)
        else:
            k = pl.program_id(2)

            @pl.when(k == 0)
            def _():
                acc[...] = part

            @pl.when(k > 0)
            def _():
                acc[...] += part

            @pl.when(k == nk - 1)
            def _():
                finish(acc[...])

    a_spec = pl.BlockSpec((tk, tm), lambda i, j, k: (k, i)) if ta else pl.BlockSpec((tm, tk), lambda i, j, k: (i, k))
    b_spec = pl.BlockSpec((tn, tk), lambda i, j, k: (j, k)) if tb else pl.BlockSpec((tk, tn), lambda i, j, k: (k, j))
    o_spec = pl.BlockSpec((tm, tn), lambda i, j, k: (i, j))
    in_specs = [a_spec, b_spec] + ([o_spec] if add is not None else [])
    args = (a, b) + ((add,) if add is not None else ())
    return pl.pallas_call(
        body,
        name=name,
        grid=(M // tm, Nn // tn, nk),
        in_specs=in_specs,
        out_specs=o_spec,
        out_shape=jax.ShapeDtypeStruct((M, Nn), out_dtype),
        scratch_shapes=[pltpu.VMEM((tm, tn), F32)] if nk > 1 else [],
        compiler_params=_cparams(("parallel", "parallel", "arbitrary")),
    )(*args)


def _rms_fwd(h, gains, name):
    tr = _pick(LP, (768, 256))
    ng = len(gains)

    def body(*refs):
        h_ref = refs[0]
        g_refs = refs[1 : 1 + ng]
        o_refs = refs[1 + ng :]
        x = h_ref[...]
        xh = x * lax.rsqrt(jnp.mean(x * x, axis=-1, keepdims=True) + EPS)
        for g_ref, o_ref in zip(g_refs, o_refs):
            o_ref[...] = (xh * g_ref[...]).astype(BF16)

    row = pl.BlockSpec((tr, D), lambda i: (i, 0))
    vec = pl.BlockSpec((1, D), lambda i: (0, 0))
    outs = pl.pallas_call(
        body,
        name=name,
        grid=(LP // tr,),
        in_specs=[row] + [vec] * ng,
        out_specs=[row] * ng,
        out_shape=[jax.ShapeDtypeStruct((LP, D), BF16)] * ng,
        compiler_params=_cparams(("parallel",)),
    )(h, *gains)
    return outs


def _rms_bwd(dh_in, h, dus, gains, name):
    tr = _pick(LP, (256,))
    ng = len(gains)

    def body(*refs):
        dh_ref, h_ref = refs[0], refs[1]
        du_refs = refs[2 : 2 + ng]
        g_refs = refs[2 + ng : 2 + 2 * ng]
        o_ref = refs[2 + 2 * ng]
        dg_refs = refs[3 + 2 * ng :]
        i = pl.program_id(0)
        x = h_ref[...]
        r = lax.rsqrt(jnp.mean(x * x, axis=-1, keepdims=True) + EPS)
        xh = x * r
        tot = dh_ref[...]
        for du_ref, g_ref, dg_ref in zip(du_refs, g_refs, dg_refs):
            du = du_ref[...]
            dxh = du * g_ref[...]
            tot = tot + r * (dxh - xh * jnp.mean(dxh * xh, axis=-1, keepdims=True))

            @pl.when(i == 0)
            def _():
                dg_ref[...] = jnp.zeros_like(dg_ref)

            dg_ref[...] += jnp.sum(du * xh, axis=0, keepdims=True)
        o_ref[...] = jnp.where(_valid_rows(i, tr), tot, 0.0)

    row = pl.BlockSpec((tr, D), lambda i: (i, 0))
    vec = pl.BlockSpec((1, D), lambda i: (0, 0))
    outs = pl.pallas_call(
        body,
        name=name,
        grid=(LP // tr,),
        in_specs=[row, row] + [row] * ng + [vec] * ng,
        out_specs=[row] + [vec] * ng,
        out_shape=[jax.ShapeDtypeStruct((LP, D), F32)] + [jax.ShapeDtypeStruct((1, D), F32)] * ng,
        compiler_params=_cparams(("arbitrary",)),
    )(dh_in, h, *dus, *gains)
    return outs[0], outs[1:]


def _loss_head(h, gain, target, name):
    tr = Q

    def body(h_ref, g_ref, t_ref, dh_ref, dg_ref, ls_ref):
        i = pl.program_id(0)

        @pl.when(i == 0)
        def _():
            dg_ref[...] = jnp.zeros_like(dg_ref)
            ls_ref[...] = jnp.zeros_like(ls_ref)
            dh_ref[...] = jnp.zeros_like(dh_ref)

        @pl.when(i > 0)
        def _():
            x = h_ref[...]
            g = g_ref[...]
            r = lax.rsqrt(jnp.mean(x * x, axis=-1, keepdims=True) + EPS)
            xh = x * r
            e = xh * g - t_ref[...]
            ls_ref[...] += jnp.sum(e * e, axis=0, keepdims=True)
            dy = e * (1.0 / D)
            dg_ref[...] += jnp.sum(dy * xh, axis=0, keepdims=True)
            dxh = dy * g
            dh_ref[...] = r * (dxh - xh * jnp.mean(dxh * xh, axis=-1, keepdims=True))

    row = pl.BlockSpec((tr, D), lambda i: (i, 0))
    vec = pl.BlockSpec((1, D), lambda i: (0, 0))
    return pl.pallas_call(
        body,
        name=name,
        grid=(LP // tr,),
        in_specs=[row, vec, pl.BlockSpec((tr, D), lambda i: (jnp.maximum(i - 1, 0), 0))],
        out_specs=[row, vec, vec],
        out_shape=[jax.ShapeDtypeStruct((LP, D), F32), jax.ShapeDtypeStruct((1, D), F32), jax.ShapeDtypeStruct((1, D), F32)],
        compiler_params=_cparams(("arbitrary",)),
    )(h, gain, target)


HALO = 8
CONV_COLS = (1536, 1408, 768, 512, 256)


def _conv_rows(ext, w, b, width):
    n = ext.shape[0]
    acc = b + w[width - 1 : width, :] * ext[HALO:]
    for k in range(width - 1):
        acc = acc + w[k : k + 1, :] * pltpu.roll(ext, width - 1 - k, 0)[HALO:]
    return acc


def _conv_specs(tr, tn, col):
    per = tr // HALO
    last = LP // HALO - 1
    prev = pl.BlockSpec((HALO, tn), lambda j, i: (jnp.maximum(i * per - 1, 0), col(j)))
    cur = pl.BlockSpec((tr, tn), lambda j, i: (i, col(j)))
    nxt = pl.BlockSpec((HALO, tn), lambda j, i: (jnp.minimum((i + 1) * per, last), col(j)))
    return prev, cur, nxt


def _conv_bwd_core(ext, dact_fn, w, b, width, i, nblk, tr):
    pre = _conv_rows(ext, w, b, width)
    dpre = dact_fn(pre)
    rows = i * tr + lax.broadcasted_iota(jnp.int32, (tr + HALO, 1), 0)
    dpre = jnp.where((rows >= PF) & (rows < LP), dpre, 0.0)
    n = tr + HALO
    dx = w[width - 1 : width, :] * dpre[:tr]
    for k in range(width - 1):
        sh = width - 1 - k
        dx = dx + w[k : k + 1, :] * pltpu.roll(dpre, n - sh, 0)[:tr]
    dcur = dpre[:tr]
    dws = []
    for k in range(width):
        sh = width - 1 - k
        xs = ext[HALO : HALO + tr] if sh == 0 else pltpu.roll(ext, sh, 0)[HALO : HALO + tr]
        dws.append(jnp.sum(xs * dcur, axis=0, keepdims=True))
    db = jnp.sum(dcur, axis=0, keepdims=True)
    dx = jnp.where(_valid_rows(i, tr), dx, 0.0)
    return dx, dws, db


def _ssd_conv_fwd(xr, cw, cb, name):
    tr, tn = Q, _pick(CD, CONV_COLS)

    def body(p_ref, c_ref, w_ref, b_ref, o_ref):
        i = pl.program_id(1)
        ext = jnp.concatenate([jnp.where(i > 0, p_ref[...], 0.0), c_ref[...]], axis=0)
        pre = _conv_rows(ext, w_ref[...], b_ref[...], CONVW)
        o_ref[...] = jnp.where(_valid_rows(i, tr), pre * _sigmoid(pre), 0.0)

    prev, cur, _ = _conv_specs(tr, tn, lambda j: j)
    return pl.pallas_call(
        body,
        name=name,
        grid=(CD // tn, LP // tr),
        in_specs=[prev, cur, pl.BlockSpec((CONVW, tn), lambda j, i: (0, j)), pl.BlockSpec((1, tn), lambda j, i: (0, j))],
        out_specs=cur,
        out_shape=jax.ShapeDtypeStruct((LP, CD), F32),
        compiler_params=_cparams(("parallel", "arbitrary")),
    )(xr, xr, cw, cb)


def _ssd_conv_bwd(xr, dxbc, cw, cb, name):
    tr, tn = Q, _pick(CD, CONV_COLS)
    nblk = LP // tr

    def body(p_ref, c_ref, n_ref, dc_ref, dn_ref, w_ref, b_ref, dx_ref, dw_ref, db_ref):
        i = pl.program_id(1)
        ext = jnp.concatenate([jnp.where(i > 0, p_ref[...], 0.0), c_ref[...], n_ref[...]], axis=0)
        dout = jnp.concatenate([dc_ref[...], dn_ref[...]], axis=0)
        dx, dws, db = _conv_bwd_core(ext, lambda pre: dout * _dsilu(pre), w_ref[...], b_ref[...], CONVW, i, nblk, tr)
        dx_ref[...] = dx

        @pl.when(i == 0)
        def _():
            dw_ref[...] = jnp.zeros_like(dw_ref)
            db_ref[...] = jnp.zeros_like(db_ref)

        for k in range(CONVW):
            dw_ref[k : k + 1, :] += dws[k]
        db_ref[...] += db

    prev, cur, nxt = _conv_specs(tr, tn, lambda j: j)
    wspec = pl.BlockSpec((CONVW, tn), lambda j, i: (0, j))
    bspec = pl.BlockSpec((1, tn), lambda j, i: (0, j))
    return pl.pallas_call(
        body,
        name=name,
        grid=(CD // tn, LP // tr),
        in_specs=[prev, cur, nxt, cur, nxt, wspec, bspec],
        out_specs=[cur, wspec, bspec],
        out_shape=[jax.ShapeDtypeStruct((LP, CD), F32), jax.ShapeDtypeStruct((CONVW, CD), F32), jax.ShapeDtypeStruct((1, CD), F32)],
        compiler_params=_cparams(("parallel", "arbitrary")),
    )(xr, xr, xr, dxbc, dxbc, cw, cb)


def _ffn_act_fwd(hg, hv, cwg, cwv, cbg, cbv, name):
    tr, tn = Q, _pick(DFF, CONV_COLS)

    def body(pg, cg, pv, cv, wg, wv, bg, bv, o_ref):
        i = pl.program_id(1)
        eg = jnp.concatenate([jnp.where(i > 0, pg[...], 0.0), cg[...]], axis=0)
        ev = jnp.concatenate([jnp.where(i > 0, pv[...], 0.0), cv[...]], axis=0)
        gate = _conv_rows(eg, wg[...], bg[...], FC)
        val = _conv_rows(ev, wv[...], bv[...], FC)
        o_ref[...] = (gate * _sigmoid(gate) * val).astype(BF16)

    prev, cur, _ = _conv_specs(tr, tn, lambda j: j)
    wspec = pl.BlockSpec((FC, tn), lambda j, i: (0, j))
    bspec = pl.BlockSpec((1, tn), lambda j, i: (0, j))
    return pl.pallas_call(
        body,
        name=name,
        grid=(DFF // tn, LP // tr),
        in_specs=[prev, cur, prev, cur, wspec, wspec, bspec, bspec],
        out_specs=cur,
        out_shape=jax.ShapeDtypeStruct((LP, DFF), BF16),
        compiler_params=_cparams(("parallel", "arbitrary")),
    )(hg, hg, hv, hv, cwg, cwv, cbg, cbv)


def _ffn_act_bwd(hg, hv, da, cwg, cwv, cbg, cbv, name):
    tr, tn = Q, _pick(DFF, CONV_COLS)
    nblk = LP // tr

    def body(pg, cg, ng, pv, cv, nv, dc, dn, wg, wv, bg, bv, dg_ref, dv_ref, dwg, dwv, dbg, dbv):
        i = pl.program_id(1)
        eg = jnp.concatenate([jnp.where(i > 0, pg[...], 0.0), cg[...], ng[...]], axis=0)
        ev = jnp.concatenate([jnp.where(i > 0, pv[...], 0.0), cv[...], nv[...]], axis=0)
        dout = jnp.concatenate([dc[...], dn[...]], axis=0)
        gate = _conv_rows(eg, wg[...], bg[...], FC)
        val = _conv_rows(ev, wv[...], bv[...], FC)
        dxg, dwsg, dbgv = _conv_bwd_core(eg, lambda pre: dout * val * _dsilu(pre), wg[...], bg[...], FC, i, nblk, tr)
        dxv, dwsv, dbvv = _conv_bwd_core(ev, lambda pre: dout * gate * _sigmoid(gate), wv[...], bv[...], FC, i, nblk, tr)
        dg_ref[...] = dxg
        dv_ref[...] = dxv

        @pl.when(i == 0)
        def _():
            dwg[...] = jnp.zeros_like(dwg)
            dwv[...] = jnp.zeros_like(dwv)
            dbg[...] = jnp.zeros_like(dbg)
            dbv[...] = jnp.zeros_like(dbv)

        for k in range(FC):
            dwg[k : k + 1, :] += dwsg[k]
            dwv[k : k + 1, :] += dwsv[k]
        dbg[...] += dbgv
        dbv[...] += dbvv

    prev, cur, nxt = _conv_specs(tr, tn, lambda j: j)
    wspec = pl.BlockSpec((FC, tn), lambda j, i: (0, j))
    bspec = pl.BlockSpec((1, tn), lambda j, i: (0, j))
    big = jax.ShapeDtypeStruct((LP, DFF), F32)
    wsh = jax.ShapeDtypeStruct((FC, DFF), F32)
    bsh = jax.ShapeDtypeStruct((1, DFF), F32)
    return pl.pallas_call(
        body,
        name=name,
        grid=(DFF // tn, LP // tr),
        in_specs=[prev, cur, nxt, prev, cur, nxt, cur, nxt, wspec, wspec, bspec, bspec],
        out_specs=[cur, cur, wspec, wspec, bspec, bspec],
        out_shape=[big, big, wsh, wsh, bsh, bsh],
        compiler_params=_cparams(("parallel", "arbitrary")),
    )(hg, hg, hg, hv, hv, hv, da, da, cwg, cwv, cbg, cbv)


def _gate_fwd(y, z, gg, name):
    tr = _pick(LP, (768, 256))

    def body(y_ref, z_ref, g_ref, o_ref):
        zv = z_ref[...]
        hg = y_ref[...] * zv * _sigmoid(zv)
        r = lax.rsqrt(jnp.mean(hg * hg, axis=-1, keepdims=True) + EPS)
        o_ref[...] = (hg * r * g_ref[...]).astype(BF16)

    blk = pl.BlockSpec((tr, GW), lambda i, g: (i, g))
    return pl.pallas_call(
        body,
        name=name,
        grid=(LP // tr, G),
        in_specs=[blk, blk, pl.BlockSpec((1, GW), lambda i, g: (0, g))],
        out_specs=blk,
        out_shape=jax.ShapeDtypeStruct((LP, DI), BF16),
        compiler_params=_cparams(("parallel", "parallel")),
    )(y, z, gg)


def _gate_bwd(dout, y, z, gg, name):
    tr = _pick(LP, (768, 256))

    def body(do_ref, y_ref, z_ref, g_ref, dy_ref, dz_ref, dg_ref):
        i = pl.program_id(1)
        zv = z_ref[...]
        yv = y_ref[...]
        sz = zv * _sigmoid(zv)
        hg = yv * sz
        r = lax.rsqrt(jnp.mean(hg * hg, axis=-1, keepdims=True) + EPS)
        hh = hg * r
        do = do_ref[...]
        dhh = do * g_ref[...]
        dhg = r * (dhh - hh * jnp.mean(dhh * hh, axis=-1, keepdims=True))
        dy_ref[...] = dhg * sz
        dz_ref[...] = dhg * yv * _dsilu(zv)

        @pl.when(i == 0)
        def _():
            dg_ref[...] = jnp.zeros_like(dg_ref)

        dg_ref[...] += jnp.sum(do * hh, axis=0, keepdims=True)

    blk = pl.BlockSpec((tr, GW), lambda g, i: (i, g))
    vec = pl.BlockSpec((1, GW), lambda g, i: (0, g))
    big = jax.ShapeDtypeStruct((LP, DI), F32)
    return pl.pallas_call(
        body,
        name=name,
        grid=(G, LP // tr),
        in_specs=[blk, blk, blk, vec],
        out_specs=[blk, blk, vec],
        out_shape=[big, big, jax.ShapeDtypeStruct((1, DI), F32)],
        compiler_params=_cparams(("parallel", "arbitrary")),
    )(dout, y, z, gg)


def _ssd_common(dtc_ref, dtr_ref, bc_ref, br_ref, ac_ref, ar_ref, c):
    rows = c * Q + lax.broadcasted_iota(jnp.int32, (Q, 1), 0)
    cols = c * Q + lax.broadcasted_iota(jnp.int32, (1, Q), 1)
    prec = dtc_ref[...] + bc_ref[...]
    prer = dtr_ref[...] + br_ref[...]
    dtc = jnp.where(rows >= PF, _softplus(prec), 0.0)
    dtr = jnp.where(cols >= PF, _softplus(prer), 0.0)
    a_c = -jnp.exp(ac_ref[...])
    a_r = -jnp.exp(ar_ref[...])
    li = lax.broadcasted_iota(jnp.int32, (Q, Q), 0)
    si = lax.broadcasted_iota(jnp.int32, (Q, Q), 1)
    tril = si <= li
    trif = tril.astype(F32)
    csc = jnp.dot(trif, dtc * a_c, precision=HI, preferred_element_type=F32)
    csr = lax.dot_general(dtr * a_r, trif, (((1,), (1,)), ((), ())), precision=HI, preferred_element_type=F32)
    return dict(rows=rows, prec=prec, dtc=dtc, a_c=a_c, tril=tril, trif=trif, csc=csc, csr=csr, li=li, si=si)


def _pair_expand(arr, h0, lane_lo):
    return jnp.where(lane_lo, arr[:, h0 : h0 + 1], arr[:, h0 + 1 : h0 + 2])


def _ssd_specs():
    nb = DI // N
    xs = pl.BlockSpec((Q, GW), lambda g, c: (c, g))
    bb = pl.BlockSpec((Q, N), lambda g, c: (c, nb + g))
    cc = pl.BlockSpec((Q, N), lambda g, c: (c, nb + G + g))
    dtc = pl.BlockSpec((None, Q, LANES), lambda g, c: (g, c, 0))
    dtr = pl.BlockSpec((None, 8, Q), lambda g, c: (g, 0, c))
    pc = pl.BlockSpec((None, 1, LANES), lambda g, c: (g, 0, 0))
    pr = pl.BlockSpec((None, 8, 1), lambda g, c: (g, 0, 0))
    return xs, bb, cc, dtc, dtr, pc, pr


def _ssd_fwd(xbc, dtc, dtr, bias_c, bias_r, alog_c, alog_r, dskip_c, name):
    def body(xs_ref, b_ref, c_ref, dtc_ref, dtr_ref, bc_ref, br_ref, ac_ref, ar_ref, dk_ref, y_ref, st_ref, state):
        c = pl.program_id(1)

        @pl.when(c == 0)
        def _():
            state[...] = jnp.zeros_like(state)

        st_ref[...] = state[...]
        cm = _ssd_common(dtc_ref, dtr_ref, bc_ref, br_ref, ac_ref, ar_ref, c)
        Bm = b_ref[...]
        Cm = c_ref[...]
        cb = lax.dot_general(Cm.astype(BF16), Bm.astype(BF16), (((1,), (1,)), ((), ())), preferred_element_type=F32)
        bt = Bm.T.astype(BF16)
        lane_lo = lax.broadcasted_iota(jnp.int32, (1, HP), 1) < P
        csc, csr, dtc_v = cm["csc"], cm["csr"], cm["dtc"]
        ecs = jnp.exp(csc)
        cs_end = csc[Q - 1 : Q, :]
        wdec = jnp.exp(cs_end - csc)
        eend = jnp.exp(cs_end)
        for pp in range(E // 2):
            h0 = 2 * pp
            sl = slice(pp * HP, (pp + 1) * HP)
            xp = xs_ref[:, sl]
            xdt = xp * _pair_expand(dtc_v, h0, lane_lo)
            yacc = xp * _pair_expand(dk_ref[...], h0, lane_lo)
            for e in range(2):
                h = h0 + e
                lm = jnp.where(cm["tril"], jnp.exp(jnp.minimum(csc[:, h : h + 1] - csr[h : h + 1, :], 0.0)), 0.0)
                m = (cb * lm).astype(BF16)
                xm = jnp.where(lane_lo if e == 0 else jnp.logical_not(lane_lo), xdt, 0.0).astype(BF16)
                yacc = yacc + jnp.dot(m, xm, preferred_element_type=F32)
            stp = state[:, sl]
            yoff = jnp.dot(Cm.astype(BF16), stp.astype(BF16), preferred_element_type=F32)
            y_ref[:, sl] = yacc + yoff * _pair_expand(ecs, h0, lane_lo)
            xw = (xdt * _pair_expand(wdec, h0, lane_lo)).astype(BF16)
            state[:, sl] = stp * _pair_expand(eend, h0, lane_lo) + jnp.dot(bt, xw, preferred_element_type=F32)

    xs, bb, cc, dtcs, dtrs, pc, pr = _ssd_specs()
    return pl.pallas_call(
        body,
        name=name,
        grid=(G, NC),
        in_specs=[xs, bb, cc, dtcs, dtrs, pc, pr, pc, pr, pc],
        out_specs=[xs, pl.BlockSpec((None, None, N, GW), lambda g, c: (c, g, 0, 0))],
        out_shape=[jax.ShapeDtypeStruct((LP, DI), F32), jax.ShapeDtypeStruct((NC, G, N, GW), F32)],
        scratch_shapes=[pltpu.VMEM((N, GW), F32)],
        compiler_params=_cparams(("parallel", "arbitrary")),
    )(xbc, xbc, xbc, dtc, dtr, bias_c, bias_r, alog_c, alog_r, dskip_c)


def _ssd_bwd(xbc, dy, states, dtc, dtr, bias_c, bias_r, alog_c, alog_r, dskip_c, name):
    def body(xs_ref, b_ref, c_ref, dy_ref, st_ref, dtc_ref, dtr_ref, bc_ref, br_ref, ac_ref, ar_ref, dk_ref,
             dx_ref, db_ref, dc_ref, ddt_ref, dbias_ref, dalog_ref, ddk_ref, dstate):
        ci = pl.program_id(1)
        c = NC - 1 - ci

        @pl.when(ci == 0)
        def _():
            dstate[...] = jnp.zeros_like(dstate)
            dbias_ref[...] = jnp.zeros_like(dbias_ref)
            dalog_ref[...] = jnp.zeros_like(dalog_ref)
            ddk_ref[...] = jnp.zeros_like(ddk_ref)

        cm = _ssd_common(dtc_ref, dtr_ref, bc_ref, br_ref, ac_ref, ar_ref, c)
        Bm = b_ref[...]
        Cm = c_ref[...]
        Bb = Bm.astype(BF16)
        Cb = Cm.astype(BF16)
        nt = (((1,), (1,)), ((), ()))
        cb = lax.dot_general(Cb, Bb, nt, preferred_element_type=F32)
        cbt = lax.dot_general(Bb, Cb, nt, preferred_element_type=F32)
        ct = Cm.T.astype(BF16)
        lane_lo = lax.broadcasted_iota(jnp.int32, (1, HP), 1) < P
        lane_id = lax.broadcasted_iota(jnp.int32, (1, LANES), 1)
        csc, csr, dtc_v, a_c = cm["csc"], cm["csr"], cm["dtc"], cm["a_c"]
        triu = cm["si"] >= cm["li"]
        ecs = jnp.exp(csc)
        cs_end = csc[Q - 1 : Q, :]
        wdec = jnp.exp(cs_end - csc)
        eend = jnp.exp(cs_end)
        dcb = jnp.zeros((Q, Q), F32)
        dcbt = jnp.zeros((Q, Q), F32)
        dcs = jnp.zeros((Q, LANES), F32)
        dcs_end = jnp.zeros((1, LANES), F32)
        ddt = jnp.zeros((Q, LANES), F32)
        ddk = jnp.zeros((1, LANES), F32)
        dB = jnp.zeros((Q, N), F32)
        dC = jnp.zeros((Q, N), F32)
        for pp in range(E // 2):
            h0 = 2 * pp
            sl = slice(pp * HP, (pp + 1) * HP)
            xp = xs_ref[:, sl]
            dyp = dy_ref[:, sl]
            dtx = _pair_expand(dtc_v, h0, lane_lo)
            xdt = xp * dtx
            dxdt = jnp.zeros((Q, HP), F32)
            stp = st_ref[:, sl]
            stb = stp.astype(BF16)
            dsn = dstate[:, sl]
            dsnb = dsn.astype(BF16)
            ecsx = _pair_expand(ecs, h0, lane_lo)
            wdx = _pair_expand(wdec, h0, lane_lo)
            cs_ = jnp.dot(Cb, stb, preferred_element_type=F32)
            yo = cs_ * ecsx
            dyo = dyp * ecsx
            dyob = dyo.astype(BF16)
            dC = dC + lax.dot_general(dyob, stb, nt, preferred_element_type=F32)
            ds_from_y = jnp.dot(ct, dyob, preferred_element_type=F32)
            dyyo = dyp * yo
            xw = xdt * wdx
            dB = dB + lax.dot_general(xw.astype(BF16), dsnb, nt, preferred_element_type=F32)
            dxw = jnp.dot(Bb, dsnb, preferred_element_type=F32)
            dxdt = dxdt + dxw * wdx
            w2 = dxw * xw
            rs = jnp.sum(dsn * stp, axis=0, keepdims=True) * _pair_expand(eend, h0, lane_lo)
            dstate[:, sl] = dsn * _pair_expand(eend, h0, lane_lo) + ds_from_y
            dyx = dyp * xp
            for e in range(2):
                h = h0 + e
                msk = lane_lo if e == 0 else jnp.logical_not(lane_lo)
                oh = (lane_id == h).astype(F32)
                col = csc[:, h : h + 1]
                row = csr[h : h + 1, :]
                lm = jnp.where(cm["tril"], jnp.exp(jnp.minimum(col - row, 0.0)), 0.0)
                lmt = jnp.where(triu, jnp.exp(jnp.minimum(row - col, 0.0)), 0.0)
                dye = jnp.where(msk, dyp, 0.0).astype(BF16)
                xde = jnp.where(msk, xdt, 0.0).astype(BF16)
                gm = lax.dot_general(dye, xde, nt, preferred_element_type=F32)
                gmt = lax.dot_general(xde, dye, nt, preferred_element_type=F32)
                gl = gm * lm
                glt = gmt * lmt
                dcb = dcb + gl
                dcbt = dcbt + glt
                dcs_h = (
                    jnp.sum(gl * cb, axis=1, keepdims=True)
                    - jnp.sum(glt * cbt, axis=1, keepdims=True)
                    + jnp.sum(jnp.where(msk, dyyo, 0.0), axis=1, keepdims=True)
                    - jnp.sum(jnp.where(msk, w2, 0.0), axis=1, keepdims=True)
                )
                dcs = dcs + dcs_h * oh
                dcs_end = dcs_end + (_sum_all(jnp.where(msk, w2, 0.0)) + _sum_all(jnp.where(msk, rs, 0.0))) * oh
                dxdt = dxdt + jnp.dot((cbt * lmt).astype(BF16), dye, preferred_element_type=F32)
                ddk = ddk + _sum_all(jnp.where(msk, dyx, 0.0)) * oh
            dx_ref[:, sl] = dxdt * dtx + dyp * _pair_expand(dk_ref[...], h0, lane_lo)
            dxx = dxdt * xp
            for e in range(2):
                h = h0 + e
                msk = lane_lo if e == 0 else jnp.logical_not(lane_lo)
                oh = (lane_id == h).astype(F32)
                ddt = ddt + jnp.sum(jnp.where(msk, dxx, 0.0), axis=1, keepdims=True) * oh
        dC = dC + jnp.dot(dcb.astype(BF16), Bb, preferred_element_type=F32)
        dB = dB + jnp.dot(dcbt.astype(BF16), Cb, preferred_element_type=F32)
        db_ref[...] = dB
        dc_ref[...] = dC
        last = (lax.broadcasted_iota(jnp.int32, (Q, 1), 0) == Q - 1).astype(F32)
        dcs = dcs + last * dcs_end
        dda = jnp.dot(triu.astype(F32), dcs, precision=HI, preferred_element_type=F32)
        ddt = ddt + dda * a_c
        da = jnp.sum(dda * dtc_v, axis=0, keepdims=True)
        draw = jnp.where(cm["rows"] >= PF, ddt * _sigmoid(cm["prec"]), 0.0)
        ddt_ref[...] = draw
        dbias_ref[...] += jnp.sum(draw, axis=0, keepdims=True)
        dalog_ref[...] += da * a_c
        ddk_ref[...] += ddk

    xs, bb, cc, dtcs, dtrs, pc, pr = _ssd_specs()

    def rev(spec_fn):
        return lambda g, ci: spec_fn(g, NC - 1 - ci)

    def rspec(spec):
        return pl.BlockSpec(spec.block_shape, rev(spec.index_map))

    xs_r, bb_r, cc_r, dtc_r, dtr_r = rspec(xs), rspec(bb), rspec(cc), rspec(dtcs), rspec(dtrs)
    st_r = pl.BlockSpec((None, None, N, GW), lambda g, ci: (NC - 1 - ci, g, 0, 0))
    gn = pl.BlockSpec((Q, N), lambda g, ci: (NC - 1 - ci, g))
    return pl.pallas_call(
        body,
        name=name,
        grid=(G, NC),
        in_specs=[xs_r, bb_r, cc_r, xs_r, st_r, dtc_r, dtr_r, pc, pr, pc, pr, pc],
        out_specs=[xs_r, gn, gn, dtc_r, pc, pc, pc],
        out_shape=[
            jax.ShapeDtypeStruct((LP, DI), F32),
            jax.ShapeDtypeStruct((LP, G * N), F32),
            jax.ShapeDtypeStruct((LP, G * N), F32),
            jax.ShapeDtypeStruct((G, LP, LANES), F32),
            jax.ShapeDtypeStruct((G, 1, LANES), F32),
            jax.ShapeDtypeStruct((G, 1, LANES), F32),
            jax.ShapeDtypeStruct((G, 1, LANES), F32),
        ],
        scratch_shapes=[pltpu.VMEM((N, GW), F32)],
        compiler_params=_cparams(("parallel", "arbitrary")),
    )(xbc, xbc, xbc, dy, states, dtc, dtr, bias_c, bias_r, alog_c, alog_r, dskip_c)


def _split_dot(x, u):
    hi = x.astype(BF16)
    lo = (x - hi.astype(F32)).astype(BF16)
    return jnp.dot(hi, u, preferred_element_type=F32) + jnp.dot(lo, u, preferred_element_type=F32)


def _sb_block(qe, kblk, vis, a_run, u_gt):
    l = lax.dot_general(qe, kblk, (((1,), (1,)), ((), ())), preferred_element_type=F32)
    lk = jnp.minimum(-l, 0.0) - jnp.log(1.0 + jnp.exp(-jnp.abs(l)))
    lbeta = l + lk
    if vis is not None:
        lk = jnp.where(vis, lk, 0.0)
    logw = lbeta + _split_dot(lk, u_gt) + a_run
    return lbeta, lk, logw


def _descend(i, block, carry):
    def pack(n, c):
        return (n, jnp.max(jnp.maximum(c[0], c[1]))) + tuple(c)

    st = pack(jnp.int32(1), block(i, carry, True))
    st = lax.while_loop(lambda st: (st[0] < i) & (st[1] > -T_SKIP), lambda st: pack(st[0] + 1, block(i - st[0], st[2:], False)), st)
    st = lax.while_loop(lambda st: (st[0] == i) & (st[1] > -T_SKIP), lambda st: pack(st[0] + 1, block(0, st[2:], True)), st)
    return st[2:]


def _attn_fwd(q, k, v, name):
    nq = LP // TQ

    def body(q_ref, k_ref, v_ref, o_ref):
        i = pl.program_id(1)
        qv = q_ref[...]
        lane_lo = lax.broadcasted_iota(jnp.int32, (1, HP), 1) < 64
        t_idx = i * TQ + lax.broadcasted_iota(jnp.int32, (TQ, 1), 0)
        ji = lax.broadcasted_iota(jnp.int32, (TQ, TQ), 0)
        si = lax.broadcasted_iota(jnp.int32, (TQ, TQ), 1)
        u_gt = (ji > si).astype(BF16)
        qs = [jnp.where(lane_lo, qv, jnp.zeros_like(qv)), jnp.where(lane_lo, jnp.zeros_like(qv), qv)]

        def block(kb, carry, masked):
            a0, a1, acc = carry
            off = pl.multiple_of(kb * TQ, TQ)
            kblk = k_ref[pl.ds(off, TQ), :]
            vblk = v_ref[pl.ds(off, TQ), :]
            vis = None
            if masked:
                s_idx = kb * TQ + lax.broadcasted_iota(jnp.int32, (1, TQ), 1)
                vis = (s_idx < t_idx) & (s_idx >= PF)
            new_a = []
            for e, a_run in enumerate((a0, a1)):
                _, lk, logw = _sb_block(qs[e], kblk, vis, a_run, u_gt)
                w = jnp.exp(logw)
                if masked:
                    w = jnp.where(vis, w, 0.0)
                msk = lane_lo if e == 0 else jnp.logical_not(lane_lo)
                acc = acc + jnp.dot(w.astype(BF16), jnp.where(msk, vblk, jnp.zeros_like(vblk)), preferred_element_type=F32)
                new_a.append(a_run + jnp.sum(lk, axis=1, keepdims=True))
            return new_a[0], new_a[1], acc

        z1 = jnp.zeros((TQ, 1), F32)
        _, _, acc = _descend(i, block, (z1, z1, jnp.zeros((TQ, HP), F32)))
        o_ref[...] = acc

    return pl.pallas_call(
        body,
        name=name,
        grid=(D // HP, nq),
        in_specs=[
            pl.BlockSpec((TQ, HP), lambda j, i: (i, j)),
            pl.BlockSpec((LP, HP), lambda j, i: (0, j)),
            pl.BlockSpec((LP, HP), lambda j, i: (0, j)),
        ],
        out_specs=pl.BlockSpec((TQ, HP), lambda j, i: (i, j)),
        out_shape=jax.ShapeDtypeStruct((LP, D), F32),
        compiler_params=_cparams(("parallel", "arbitrary")),
    )(q, k, v)


def _attn_bwd(q, k, v, o, do, name):
    nq = LP // TQ

    def body(q_ref, k_ref, v_ref, o_ref, do_ref, dq_ref, dk_ref, dv_ref):
        i = pl.program_id(1)

        @pl.when(i == 0)
        def _():
            dk_ref[...] = jnp.zeros_like(dk_ref)
            dv_ref[...] = jnp.zeros_like(dv_ref)

        qv = q_ref[...]
        dov = do_ref[...]
        lane_lo = lax.broadcasted_iota(jnp.int32, (1, HP), 1) < 64
        t_idx = i * TQ + lax.broadcasted_iota(jnp.int32, (TQ, 1), 0)
        ji = lax.broadcasted_iota(jnp.int32, (TQ, TQ), 0)
        si = lax.broadcasted_iota(jnp.int32, (TQ, TQ), 1)
        u_gt = (ji > si).astype(BF16)
        u_ge = (ji >= si).astype(BF16)
        msks = [lane_lo, jnp.logical_not(lane_lo)]
        qs = [jnp.where(m, qv, jnp.zeros_like(qv)) for m in msks]
        dob = [jnp.where(m, dov, 0.0).astype(BF16) for m in msks]
        ov = o_ref[...]
        deltas = [jnp.sum(d.astype(F32) * ov, axis=1, keepdims=True) for d in dob]
        nt = (((1,), (1,)), ((), ()))
        tn = (((0,), (0,)), ((), ()))

        def block(kb, carry, masked):
            a0, a1, p0, p1, dq = carry
            off = pl.multiple_of(kb * TQ, TQ)
            kblk = k_ref[pl.ds(off, TQ), :]
            vblk = v_ref[pl.ds(off, TQ), :]
            vis = None
            if masked:
                s_idx = kb * TQ + lax.broadcasted_iota(jnp.int32, (1, TQ), 1)
                vis = (s_idx < t_idx) & (s_idx >= PF)
            new_a, new_p = [], []
            dk_acc = jnp.zeros((TQ, HP), F32)
            dv_acc = jnp.zeros((TQ, HP), F32)
            for e, (a_run, p_run) in enumerate(((a0, p0), (a1, p1))):
                lbeta, lk, logw = _sb_block(qs[e], kblk, vis, a_run, u_gt)
                sig = jnp.exp(lbeta)
                w = jnp.exp(logw)
                if masked:
                    w = jnp.where(vis, w, 0.0)
                wb = w.astype(BF16)
                dw = lax.dot_general(dob[e], vblk, nt, preferred_element_type=F32)
                pm = wb.astype(F32) * dw
                cum_p = deltas[e] - (_split_dot(pm, u_ge) + p_run)
                dl = pm - (pm + cum_p) * sig
                if masked:
                    dl = jnp.where(vis, dl, 0.0)
                dl = dl.astype(BF16)
                km = jnp.where(msks[e], kblk, jnp.zeros_like(kblk))
                dq = dq + jnp.dot(dl, km, preferred_element_type=F32)
                dk_acc = dk_acc + lax.dot_general(dl, qs[e], tn, preferred_element_type=F32)
                dv_acc = dv_acc + lax.dot_general(wb, dob[e], tn, preferred_element_type=F32)
                new_a.append(a_run + jnp.sum(lk, axis=1, keepdims=True))
                new_p.append(p_run + jnp.sum(pm, axis=1, keepdims=True))
            dk_ref[pl.ds(off, TQ), :] += dk_acc
            dv_ref[pl.ds(off, TQ), :] += dv_acc
            return new_a[0], new_a[1], new_p[0], new_p[1], dq

        z1 = jnp.zeros((TQ, 1), F32)
        carry = _descend(i, block, (z1, z1, z1, z1, jnp.zeros((TQ, HP), F32)))
        dq_ref[...] = carry[4]

    blk = pl.BlockSpec((TQ, HP), lambda j, i: (i, j))
    full = pl.BlockSpec((LP, HP), lambda j, i: (0, j))
    big = jax.ShapeDtypeStruct((LP, D), F32)
    return pl.pallas_call(
        body,
        name=name,
        grid=(D // HP, nq),
        in_specs=[blk, full, full, blk, blk],
        out_specs=[blk, full, full],
        out_shape=[big, big, big],
        compiler_params=_cparams(("parallel", "arbitrary")),
    )(q, k, v, o, do)


ROW_TILES = (2048, 1024, 512, 256, 128, 64, 32, 16, 8)


def _row_tile(rows, cols, budget):
    if rows % 8:
        return rows
    return _pick(rows, tuple(t for t in ROW_TILES if t * cols <= budget) or (8,))


def _adamw(w, g, m, v, name):
    R, C = w.shape
    tr = _row_tile(R, C, 128 * 1024)

    def body(w_ref, g_ref, m_ref, v_ref, g_out, d_out, m_out, v_out):
        g = g_ref[...]
        mn = ADAM_B1 * m_ref[...] + (1.0 - ADAM_B1) * g
        vn = ADAM_B2 * v_ref[...] + (1.0 - ADAM_B2) * (g * g)
        mh = mn / (1.0 - ADAM_B1**ADAM_STEP)
        vh = vn / (1.0 - ADAM_B2**ADAM_STEP)
        g_out[...] = g
        d_out[...] = -ADAM_LR * (mh / (jnp.sqrt(vh) + ADAM_EPS) + ADAM_WD * w_ref[...])
        m_out[...] = mn
        v_out[...] = vn

    blk = pl.BlockSpec((tr, C), lambda i: (i, 0))
    sh = jax.ShapeDtypeStruct((R, C), F32)
    return pl.pallas_call(
        body,
        name=name,
        grid=(R // tr,),
        in_specs=[blk] * 4,
        out_specs=[blk] * 4,
        out_shape=[sh] * 4,
        compiler_params=_cparams(("parallel",)),
    )(w, g, m, v)


def _sum4(buf, name):
    _, R, C = buf.shape
    tr = _row_tile(R, C, 128 * 1024)

    def body(b_ref, o_ref):
        acc = b_ref[0].astype(F32)
        for s in range(1, NCHIP):
            acc = acc + b_ref[s].astype(F32)
        o_ref[...] = acc

    return pl.pallas_call(
        body,
        name=name,
        grid=(R // tr,),
        in_specs=[pl.BlockSpec((NCHIP, tr, C), lambda i: (0, i, 0))],
        out_specs=pl.BlockSpec((tr, C), lambda i: (i, 0)),
        out_shape=jax.ShapeDtypeStruct((R, C), F32),
        compiler_params=_cparams(("parallel",)),
    )(buf)


def _add2(a, b, name):
    S, R, C = a.shape
    tr = _row_tile(R, C, 256 * 1024)

    def body(a_ref, b_ref, o_ref):
        o_ref[...] = (a_ref[...].astype(F32) + b_ref[...].astype(F32)).astype(o_ref.dtype)

    blk = pl.BlockSpec((None, tr, C), lambda s, i: (s, i, 0))
    return pl.pallas_call(
        body,
        name=name,
        grid=(S, R // tr),
        in_specs=[blk, blk],
        out_specs=blk,
        out_shape=jax.ShapeDtypeStruct(a.shape, a.dtype),
        compiler_params=_cparams(("parallel", "parallel")),
    )(a, b)


ANY = pl.BlockSpec(memory_space=pl.ANY)


def _mesh_place():
    x, y, c = lax.axis_index("x"), lax.axis_index("y"), lax.axis_index("c")
    return x, y, c, 2 * x + y, [(1 - x, y), (x, 1 - y), (1 - x, 1 - y)]


def _gather_chips(bufs, name):
    nb = len(bufs)

    def body(*refs):
        ins = refs[:nb]
        outs = refs[nb : 2 * nb]
        send_sems, recv_sems = refs[2 * nb :]
        x, y, c, me, peers = _mesh_place()

        def half(ref, hc):
            hr = ref.shape[0] // 2
            return ref.at[pl.ds(hc * hr, hr)]

        def copy(k, b, src, slot, hc, to):
            return pltpu.make_async_remote_copy(
                src_ref=src, dst_ref=half(outs[b].at[slot], hc), send_sem=send_sems.at[k * nb + b],
                recv_sem=recv_sems.at[k * nb + b], device_id=to, device_id_type=MESH)

        first = [copy(k, b, half(ins[b], c), me, c, (px, py, c)) for k, (px, py) in enumerate(peers) for b in range(nb)]
        for cp in first:
            cp.start()
        passed = []
        for k, (px, py) in enumerate(peers):
            for b in range(nb):
                src = half(outs[b].at[2 * px + py], c)
                copy(k, b, src, 2 * px + py, c, (px, py, c)).wait_recv()
                cp = copy(3 + k, b, src, 2 * px + py, c, (x, y, 1 - c))
                cp.start()
                passed.append(cp)
        for k, (px, py) in enumerate(peers):
            for b in range(nb):
                copy(3 + k, b, half(outs[b].at[2 * px + py], 1 - c), 2 * px + py, 1 - c, (x, y, 1 - c)).wait_recv()
        for cp in first + passed:
            cp.wait_send()

    outs = pl.pallas_call(
        body,
        name=name,
        in_specs=[ANY] * nb,
        out_specs=[ANY] * nb,
        out_shape=[jax.ShapeDtypeStruct((NCHIP,) + tuple(b.shape), b.dtype) for b in bufs],
        scratch_shapes=[pltpu.SemaphoreType.DMA((6 * nb,)), pltpu.SemaphoreType.DMA((6 * nb,))],
    )(*bufs)
    me = 2 * lax.axis_index("x") + lax.axis_index("y")
    return [lax.dynamic_update_slice(o, b[None], (me, 0, 0)) for o, b in zip(outs, bufs)]


def _scatter_chips(bufs, name):
    nb = len(bufs)

    def body(*refs):
        ins = refs[:nb]
        outs = refs[nb : 2 * nb]
        send_sems, recv_sems = refs[2 * nb :]
        x, y, c, me, peers = _mesh_place()

        def copy(k, b, slot_from, slot_to, to):
            return pltpu.make_async_remote_copy(
                src_ref=ins[b].at[slot_from], dst_ref=outs[b].at[slot_to], send_sem=send_sems.at[k * nb + b],
                recv_sem=recv_sems.at[k * nb + b], device_id=to, device_id_type=MESH)

        sends = [copy(k, b, 2 * px + py, me, (px, py, c)) for k, (px, py) in enumerate(peers) for b in range(nb)]
        for cp in sends:
            cp.start()
        for k, (px, py) in enumerate(peers):
            for b in range(nb):
                copy(k, b, me, 2 * px + py, (px, py, c)).wait_recv()
        for cp in sends:
            cp.wait_send()

    outs = pl.pallas_call(
        body,
        name=name,
        in_specs=[ANY] * nb,
        out_specs=[ANY] * nb,
        out_shape=[jax.ShapeDtypeStruct(b.shape, b.dtype) for b in bufs],
        scratch_shapes=[pltpu.SemaphoreType.DMA((3 * nb,)), pltpu.SemaphoreType.DMA((3 * nb,))],
    )(*bufs)
    me = 2 * lax.axis_index("x") + lax.axis_index("y")
    return [lax.dynamic_update_slice(o, lax.dynamic_slice_in_dim(b, me, 1, axis=0), (me, 0, 0)) for o, b in zip(outs, bufs)]


def _split_cores(bufs, name):
    nb = len(bufs)

    def body(*refs):
        ins = refs[:nb]
        theirs = refs[nb : 2 * nb]
        send_sems, recv_sems = refs[2 * nb :]
        x, y, c, _, _ = _mesh_place()
        cps = []
        for b in range(nb):
            hr = ins[b].shape[1] // 2
            cp = pltpu.make_async_remote_copy(
                src_ref=ins[b].at[:, pl.ds((1 - c) * hr, hr)], dst_ref=theirs[b], send_sem=send_sems.at[b],
                recv_sem=recv_sems.at[b], device_id=(x, y, 1 - c), device_id_type=MESH)
            cp.start()
            cps.append(cp)
        for cp in cps:
            cp.wait()

    theirs = pl.pallas_call(
        body,
        name=name,
        in_specs=[ANY] * nb,
        out_specs=[ANY] * nb,
        out_shape=[jax.ShapeDtypeStruct((b.shape[0], b.shape[1] // 2, b.shape[2]), b.dtype) for b in bufs],
        scratch_shapes=[pltpu.SemaphoreType.DMA((nb,)), pltpu.SemaphoreType.DMA((nb,))],
    )(*bufs)
    c = lax.axis_index("c")
    mine = [lax.dynamic_slice_in_dim(b, c * (b.shape[1] // 2), b.shape[1] // 2, axis=1) for b in bufs]
    return mine, theirs


def _join_cores(bufs, name):
    nb = len(bufs)

    def body(*refs):
        ins = refs[:nb]
        outs = refs[nb : 2 * nb]
        send_sems, recv_sems = refs[2 * nb :]
        x, y, c, _, _ = _mesh_place()
        cps = []
        for b in range(nb):
            hr = ins[b].shape[0]
            cp = pltpu.make_async_remote_copy(
                src_ref=ins[b], dst_ref=outs[b].at[pl.ds(c * hr, hr)], send_sem=send_sems.at[b], recv_sem=recv_sems.at[b],
                device_id=(x, y, 1 - c), device_id_type=MESH)
            cp.start()
            cps.append(cp)
        for b, cp in enumerate(cps):
            hr = ins[b].shape[0]
            cp.wait_send()
            pltpu.make_async_remote_copy(
                src_ref=ins[b], dst_ref=outs[b].at[pl.ds((1 - c) * hr, hr)], send_sem=send_sems.at[b],
                recv_sem=recv_sems.at[b], device_id=(x, y, 1 - c), device_id_type=MESH).wait_recv()

    outs = pl.pallas_call(
        body,
        name=name,
        in_specs=[ANY] * nb,
        out_specs=[ANY] * nb,
        out_shape=[jax.ShapeDtypeStruct((2 * b.shape[0], b.shape[1]), b.dtype) for b in bufs],
        scratch_shapes=[pltpu.SemaphoreType.DMA((nb,)), pltpu.SemaphoreType.DMA((nb,))],
    )(*bufs)
    c = lax.axis_index("c")
    return [lax.dynamic_update_slice(o, b, (c * b.shape[0], 0)) for o, b in zip(outs, bufs)]


ROW_ALIGN = 1024


def _pack(pieces, dtype):
    flat = []
    for p in pieces:
        f = p.reshape(-1).astype(dtype)
        pad = (-f.shape[0]) % LANES
        if pad:
            f = jnp.pad(f, (0, pad))
        flat.append(f)
    tot = sum(f.shape[0] for f in flat)
    pad = (-tot) % (ROW_ALIGN * LANES)
    if pad:
        flat.append(jnp.zeros((pad,), dtype))
    return jnp.concatenate(flat).reshape(-1, LANES)


def _unpack(buf, shapes):
    lead = buf.shape[:-2]
    flat = buf.reshape(lead + (-1,))
    out = []
    off = 0
    for shp in shapes:
        n = 1
        for d in shp:
            n *= d
        out.append(flat[..., off : off + n].reshape(lead + tuple(shp)))
        off += n + ((-n) % LANES)
    return out


PARAMS = (
    ("meta_tokens", 1, "small"), ("ssd_norm", 1, "small"), ("ssd_w_in", 2, "big"), ("ssd_conv_w", 2, "small"),
    ("ssd_conv_b", 1, "small"), ("ssd_dt_bias", None, "rep"), ("ssd_a_log", None, "rep"), ("ssd_d_skip", None, "rep"),
    ("ssd_gate_norm", 1, "small"), ("ssd_w_out", 1, "big"), ("kv_norm", None, "rep"), ("w_kv", 1, "big"),
    ("sb_norm", None, "rep"), ("sb_w_q", 1, "big"), ("sb_w_o", 1, "big"), ("ffn_norm", None, "rep"),
    ("ffn_w_up", 2, "big"), ("ffn_conv_w", 2, "small"), ("ffn_conv_b", None, "rep"), ("ffn_w_down", 1, "big"),
    ("final_norm", None, "rep"),
)


def _head_cols(vec):
    return jnp.pad(vec.reshape(G, 1, E), ((0, 0), (0, 0), (0, LANES - E)))


def _head_rows(vec):
    return jnp.pad(vec.reshape(G, E, 1), ((0, 0), (0, 8 - E), (0, 0)))


def kernel(x, meta_tokens, ssd_norm, ssd_w_in, ssd_conv_w, ssd_conv_b, ssd_dt_bias, ssd_a_log, ssd_d_skip, ssd_gate_norm, ssd_w_out, kv_norm, w_kv, sb_norm, sb_w_q, sb_w_o, ffn_norm, ffn_w_up, ffn_conv_w, ffn_conv_b, ffn_w_down, final_norm, loss_target, m_meta_tokens, m_ssd_norm, m_ssd_w_in, m_ssd_conv_w, m_ssd_conv_b, m_ssd_dt_bias, m_ssd_a_log, m_ssd_d_skip, m_ssd_gate_norm, m_ssd_w_out, m_kv_norm, m_w_kv, m_sb_norm, m_sb_w_q, m_sb_w_o, m_ffn_norm, m_ffn_w_up, m_ffn_conv_w, m_ffn_conv_b, m_ffn_w_down, m_final_norm, v_meta_tokens, v_ssd_norm, v_ssd_w_in, v_ssd_conv_w, v_ssd_conv_b, v_ssd_dt_bias, v_ssd_a_log, v_ssd_d_skip, v_ssd_gate_norm, v_ssd_w_out, v_kv_norm, v_w_kv, v_sb_norm, v_sb_w_q, v_sb_w_o, v_ffn_norm, v_ffn_w_up, v_ffn_conv_w, v_ffn_conv_b, v_ffn_w_down, v_final_norm):
    local = dict(meta_tokens=meta_tokens, ssd_norm=ssd_norm, ssd_w_in=ssd_w_in, ssd_conv_w=ssd_conv_w, ssd_conv_b=ssd_conv_b, ssd_dt_bias=ssd_dt_bias, ssd_a_log=ssd_a_log, ssd_d_skip=ssd_d_skip, ssd_gate_norm=ssd_gate_norm, ssd_w_out=ssd_w_out, kv_norm=kv_norm, w_kv=w_kv, sb_norm=sb_norm, sb_w_q=sb_w_q, sb_w_o=sb_w_o, ffn_norm=ffn_norm, ffn_w_up=ffn_w_up, ffn_conv_w=ffn_conv_w, ffn_conv_b=ffn_conv_b, ffn_w_down=ffn_w_down, final_norm=final_norm)
    mom_m = dict(meta_tokens=m_meta_tokens, ssd_norm=m_ssd_norm, ssd_w_in=m_ssd_w_in, ssd_conv_w=m_ssd_conv_w, ssd_conv_b=m_ssd_conv_b, ssd_dt_bias=m_ssd_dt_bias, ssd_a_log=m_ssd_a_log, ssd_d_skip=m_ssd_d_skip, ssd_gate_norm=m_ssd_gate_norm, ssd_w_out=m_ssd_w_out, kv_norm=m_kv_norm, w_kv=m_w_kv, sb_norm=m_sb_norm, sb_w_q=m_sb_w_q, sb_w_o=m_sb_w_o, ffn_norm=m_ffn_norm, ffn_w_up=m_ffn_w_up, ffn_conv_w=m_ffn_conv_w, ffn_conv_b=m_ffn_conv_b, ffn_w_down=m_ffn_w_down, final_norm=m_final_norm)
    mom_v = dict(meta_tokens=v_meta_tokens, ssd_norm=v_ssd_norm, ssd_w_in=v_ssd_w_in, ssd_conv_w=v_ssd_conv_w, ssd_conv_b=v_ssd_conv_b, ssd_dt_bias=v_ssd_dt_bias, ssd_a_log=v_ssd_a_log, ssd_d_skip=v_ssd_d_skip, ssd_gate_norm=v_ssd_gate_norm, ssd_w_out=v_ssd_w_out, kv_norm=v_kv_norm, w_kv=v_w_kv, sb_norm=v_sb_norm, sb_w_q=v_sb_w_q, sb_w_o=v_sb_w_o, ffn_norm=v_ffn_norm, ffn_w_up=v_ffn_w_up, ffn_conv_w=v_ffn_conv_w, ffn_conv_b=v_ffn_conv_b, ffn_w_down=v_ffn_w_down, final_norm=v_final_norm)

    big_names = [n for n, _, kind in PARAMS if kind == "big"]
    small_names = [n for n, _, kind in PARAMS if kind == "small"]
    rep_names = [n for n, _, kind in PARAMS if kind == "rep"]
    axis_of = {n: ax for n, ax, _ in PARAMS}

    def rows2(a):
        return a.reshape(-1, a.shape[-1])

    small_own = _pack([local[n] for n in small_names], F32)
    gathered = _gather_chips([rows2(local[n]).astype(BF16) for n in big_names] + [small_own], "gather_weights")
    full = {}
    for n, buf in zip(big_names, gathered[:-1]):
        p = buf.reshape((NCHIP,) + local[n].shape)
        full[n] = jnp.concatenate([p[s] for s in range(NCHIP)], axis=axis_of[n])
    for n, p in zip(small_names, _unpack(gathered[-1], [local[n].shape for n in small_names])):
        full[n] = jnp.concatenate([p[s] for s in range(NCHIP)], axis=axis_of[n])
    for n in rep_names:
        full[n] = local[n]

    w_in = full["ssd_w_in"][0]
    w_z, w_xbc = w_in[:, :DI], w_in[:, DI : DI + CD]
    w_dt = jnp.pad(w_in[:, DI + CD :], ((0, 0), (0, LANES - H)))
    w_out = full["ssd_w_out"][0]
    wkv = full["w_kv"]
    w_q = full["sb_w_q"][0]
    w_o = full["sb_w_o"][0]
    w_up_g = [full["ffn_w_up"][l][:, :DFF] for l in range(2)]
    w_up_v = [full["ffn_w_up"][l][:, DFF:] for l in range(2)]
    w_down = [full["ffn_w_down"][l] for l in range(2)]
    fcw, fcb = full["ffn_conv_w"], full["ffn_conv_b"]
    scw, scb = full["ssd_conv_w"][0], full["ssd_conv_b"]
    bias_c, bias_r = _head_cols(full["ssd_dt_bias"][0]), _head_rows(full["ssd_dt_bias"][0])
    alog_c, alog_r = _head_cols(full["ssd_a_log"][0]), _head_rows(full["ssd_a_log"][0])
    dskip_c = _head_cols(full["ssd_d_skip"][0])
    kvn = full["kv_norm"].reshape(1, D)
    fin = full["final_norm"].reshape(1, D)

    h0 = jnp.concatenate([jnp.zeros((PF, D), F32), full["meta_tokens"], x[0]], axis=0)
    (u0,) = _rms_fwd(h0, [full["ssd_norm"]], "ssd_norm_fwd")
    z = _mm(u0, w_z, name="ssd_in_z")
    xr = _mm(u0, w_xbc, name="ssd_in_xbc")
    dt_raw = _mm(u0, w_dt, name="ssd_in_dt")
    xbc = _ssd_conv_fwd(xr, scw, scb, "ssd_conv_fwd")
    dth = dt_raw[:, :H].reshape(LP, G, E)
    dtc = jnp.pad(jnp.transpose(dth, (1, 0, 2)), ((0, 0), (0, 0), (0, LANES - E)))
    dtr = jnp.pad(jnp.transpose(dth, (1, 2, 0)), ((0, 0), (0, 8 - E), (0, 0)))
    y, states = _ssd_fwd(xbc, dtc, dtr, bias_c, bias_r, alog_c, alog_r, dskip_c, "ssd_scan_fwd")
    hgn = _gate_fwd(y, z, full["ssd_gate_norm"], "ssd_gate_fwd")
    h1 = _mm(hgn, w_out, add=h0, mask_rows=True, name="ssd_out")

    def ffn_fwd(h, l, tag):
        (u,) = _rms_fwd(h, [full["ffn_norm"][l : l + 1]], f"ffn{tag}_norm_fwd")
        hg = _mm(u, w_up_g[l], name=f"ffn{tag}_up_g")
        hv = _mm(u, w_up_v[l], name=f"ffn{tag}_up_v")
        act = _ffn_act_fwd(hg, hv, fcw[l][:, :DFF], fcw[l][:, DFF:], fcb[l : l + 1, :DFF], fcb[l : l + 1, DFF:], f"ffn{tag}_act_fwd")
        hn = _mm(act, w_down[l], add=h, mask_rows=True, name=f"ffn{tag}_down")
        return hn, (u, hg, hv, act)

    h2, ffn0 = ffn_fwd(h1, 0, "0")
    ukv, uq = _rms_fwd(h2, [kvn, full["sb_norm"]], "attn_norm_fwd")
    kk = _mm(ukv, wkv[:, :D], out_dtype=BF16, name="attn_k")
    vv = _mm(ukv, wkv[:, D:], out_dtype=BF16, name="attn_v")
    qq = _mm(uq, w_q, out_dtype=BF16, scale=64.0**-0.5, name="attn_q")
    o = _attn_fwd(qq, kk, vv, "attn_fwd")
    h3 = _mm(o, w_o, add=h2, mask_rows=True, name="attn_out")
    h4, ffn1 = ffn_fwd(h3, 1, "1")
    dh, g_final, loss_rows = _loss_head(h4, fin, loss_target[0], "loss_head")
    loss = lax.psum(0.5 / D * jnp.sum(loss_rows), ("x", "y", "c"))

    grads = {"final_norm": g_final.reshape(D)}

    def ffn_bwd(dh, h, l, saved, tag):
        u, hg, hv, act = saved
        da = _mm(dh, w_down[l], tb=True, name=f"ffn{tag}_down_dx")
        gw_down = _mm(act, dh, ta=True, out_dtype=BF16, name=f"ffn{tag}_down_dw")
        dhg, dhv, dwg, dwv, dbg, dbv = _ffn_act_bwd(hg, hv, da, fcw[l][:, :DFF], fcw[l][:, DFF:], fcb[l : l + 1, :DFF], fcb[l : l + 1, DFF:], f"ffn{tag}_act_bwd")
        gw_up = jnp.concatenate([_mm(u, dhg, ta=True, out_dtype=BF16, name=f"ffn{tag}_up_g_dw"), _mm(u, dhv, ta=True, out_dtype=BF16, name=f"ffn{tag}_up_v_dw")], axis=1)
        du = _mm(dhg, w_up_g[l], tb=True, name=f"ffn{tag}_up_g_dx")
        du = _mm(dhv, w_up_v[l], tb=True, add=du, name=f"ffn{tag}_up_v_dx")
        dh_new, (gn,) = _rms_bwd(dh, h, [du], [full["ffn_norm"][l : l + 1]], f"ffn{tag}_norm_bwd")
        return dh_new, gw_down, gw_up, jnp.concatenate([dwg, dwv], axis=1), jnp.concatenate([dbg, dbv], axis=1), gn

    dh, gd1, gu1, gcw1, gcb1, gn1 = ffn_bwd(dh, h3, 1, ffn1, "1")
    do = _mm(dh, w_o, tb=True, name="attn_out_dx")
    grads["sb_w_o"] = _mm(o, dh, ta=True, out_dtype=BF16, name="attn_out_dw")[None]
    dq, dk, dv = _attn_bwd(qq, kk, vv, o, do, "attn_bwd")
    grads["sb_w_q"] = _mm(uq, dq, ta=True, out_dtype=BF16, scale=64.0**-0.5, name="attn_q_dw")[None]
    grads["w_kv"] = jnp.concatenate([_mm(ukv, dk, ta=True, out_dtype=BF16, name="attn_k_dw"), _mm(ukv, dv, ta=True, out_dtype=BF16, name="attn_v_dw")], axis=1)
    duq = _mm(dq, w_q, tb=True, scale=64.0**-0.5, name="attn_q_dx")
    dukv = _mm(dk, wkv[:, :D], tb=True, name="attn_k_dx")
    dukv = _mm(dv, wkv[:, D:], tb=True, add=dukv, name="attn_v_dx")
    dh, (g_kvn, g_sbn) = _rms_bwd(dh, h2, [dukv, duq], [kvn, full["sb_norm"]], "attn_norm_bwd")
    grads["kv_norm"] = g_kvn.reshape(D)
    grads["sb_norm"] = g_sbn
    dh, gd0, gu0, gcw0, gcb0, gn0 = ffn_bwd(dh, h1, 0, ffn0, "0")
    grads["ffn_w_down"] = jnp.stack([gd0, gd1])
    grads["ffn_w_up"] = jnp.stack([gu0, gu1])
    grads["ffn_conv_w"] = jnp.stack([gcw0, gcw1])
    grads["ffn_conv_b"] = jnp.concatenate([gcb0, gcb1], axis=0)
    grads["ffn_norm"] = jnp.concatenate([gn0, gn1], axis=0)
    dhgn = _mm(dh, w_out, tb=True, name="ssd_out_dx")
    grads["ssd_w_out"] = _mm(hgn, dh, ta=True, out_dtype=BF16, name="ssd_out_dw")[None]
    dy, dz, g_gate = _gate_bwd(dhgn, y, z, full["ssd_gate_norm"], "ssd_gate_bwd")
    grads["ssd_gate_norm"] = g_gate
    dxs, dB, dC, ddt_raw, g_bias, g_alog, g_dskip = _ssd_bwd(xbc, dy, states, dtc, dtr, bias_c, bias_r, alog_c, alog_r, dskip_c, "ssd_scan_bwd")
    grads["ssd_dt_bias"] = g_bias[:, 0, :E].reshape(1, H)
    grads["ssd_a_log"] = g_alog[:, 0, :E].reshape(1, H)
    grads["ssd_d_skip"] = g_dskip[:, 0, :E].reshape(1, H)
    dxr, g_scw, g_scb = _ssd_conv_bwd(xr, jnp.concatenate([dxs, dB, dC], axis=1), scw, scb, "ssd_conv_bwd")
    grads["ssd_conv_w"] = g_scw[None]
    grads["ssd_conv_b"] = g_scb
    ddt = jnp.pad(jnp.transpose(ddt_raw[:, :, :E], (1, 0, 2)).reshape(LP, H), ((0, 0), (0, LANES - H)))
    grads["ssd_w_in"] = jnp.concatenate(
        [_mm(u0, dz, ta=True, out_dtype=BF16, name="ssd_in_z_dw"), _mm(u0, dxr, ta=True, out_dtype=BF16, name="ssd_in_xbc_dw"), _mm(u0, ddt, ta=True, out_dtype=BF16, name="ssd_in_dt_dw")[:, :H]], axis=1)[None]
    du = _mm(dz, w_z, tb=True, name="ssd_in_z_dx")
    du = _mm(dxr, w_xbc, tb=True, add=du, name="ssd_in_xbc_dx")
    du = _mm(ddt, w_dt, tb=True, add=du, name="ssd_in_dt_dx")
    dh, (g_ssdn,) = _rms_bwd(dh, h0, [du], [full["ssd_norm"]], "ssd_norm_bwd")
    grads["ssd_norm"] = g_ssdn
    grads["meta_tokens"] = dh[PF : PF + N_META]
    grad_x = dh[PF + N_META :][None]

    def shard_pieces(names, s):
        out = []
        for n in names:
            ax = axis_of[n]
            out.append(grads[n] if ax is None else jnp.split(grads[n], NCHIP, axis=ax)[s])
        return out

    bufs = [jnp.stack([rows2(p) for p in jnp.split(grads[n], NCHIP, axis=axis_of[n])]) for n in big_names]
    bufs.append(jnp.stack([_pack(shard_pieces(small_names + rep_names, s), F32) for s in range(NCHIP)]))
    mine, theirs = _split_cores(bufs, "split_cores")
    pair = [_add2(a, b, f"pair_sum_{i}") for i, (a, b) in enumerate(zip(mine, theirs))]
    got = _scatter_chips(pair, "scatter_grads")
    sums = [_sum4(b, f"sum_chips_{i}") for i, b in enumerate(got)]
    gsum = _join_cores(sums, "join_cores")

    def rows(a):
        f = a.reshape(-1)
        pad = (-f.shape[0]) % LANES
        if pad:
            f = jnp.pad(f, (0, pad))
        return f.reshape(-1, LANES)

    order = [n for n, _, _ in PARAMS]
    res = {}
    for n, g2 in zip(big_names, gsum[:-1]):
        outs = _adamw(rows2(local[n]), g2, rows2(mom_m[n]), rows2(mom_v[n]), f"adamw_{n}")
        res[n] = [o_.reshape(local[n].shape) for o_ in outs]
    rest = small_names + rep_names
    for n, g1 in zip(rest, _unpack(gsum[-1], [local[n].shape for n in rest])):
        shp = local[n].shape
        cnt = 1
        for d in shp:
            cnt *= d
        outs = _adamw(rows(local[n]), rows(g1), rows(mom_m[n]), rows(mom_v[n]), f"adamw_{n}")
        res[n] = [o_.reshape(-1)[:cnt].reshape(shp) for o_ in outs]
    return (loss, grad_x, *[res[n][0] for n in order], *[res[n][1] for n in order], *[res[n][2] for n in order], *[res[n][3] for n in order])
```

```python
import functools

import jax
import jax.numpy as jnp
from jax import lax
from jax.experimental import pallas as pl
from jax.experimental.pallas import tpu as pltpu

D = 1024
SEQ = 8192
N_META = 16
EPS = 1e-6
P = 64
G = 4
N = 128
CONVW = 4
Q = 256
FC = 3
DFF = 256 * ((8 * D // 3 + 255) // 256)
DI = 2 * D
H = DI // P
E = H // G
GW = E * P
CD = DI + 2 * G * N
IN = DI + CD + H
SBH = D // 64
HP = 128
LANES = 128
PF = Q - N_META
LP = PF + N_META + SEQ
NC = LP // Q
TQ = 256
NCHIP = 4
ADAM_LR, ADAM_B1, ADAM_B2, ADAM_EPS, ADAM_WD, ADAM_STEP = 0.001, 0.9, 0.999, 1e-08, 0.01, 10

F32 = jnp.float32
BF16 = jnp.bfloat16
HI = lax.Precision.HIGHEST
MESH = pl.DeviceIdType.MESH
VMEM_LIMIT = 48 * 1024 * 1024
MM_MAX_K = 3072
T_SKIP = 110.0


def _pick(n, cands):
    for c in cands:
        if n % c == 0:
            return c
    raise ValueError((n, cands))


def _cparams(sem):
    return pltpu.CompilerParams(dimension_semantics=sem, vmem_limit_bytes=VMEM_LIMIT)


def _valid_rows(block, rows):
    r = block * rows + lax.broadcasted_iota(jnp.int32, (rows, 1), 0)
    return r >= PF


def _sigmoid(x):
    return 1.0 / (1.0 + jnp.exp(-x))


def _softplus(x):
    return jnp.maximum(x, 0.0) + jnp.log(1.0 + jnp.exp(-jnp.abs(x)))


def _sum_all(x):
    return jnp.sum(jnp.sum(x, axis=1, keepdims=True), axis=0, keepdims=True)


def _dsilu(x):
    s = _sigmoid(x)
    return s * (1.0 + x * (1.0 - s))


def _mm(a, b, *, ta=False, tb=False, out_dtype=F32, add=None, mask_rows=False, scale=None, name):
    if ta:
        K, M = a.shape
    else:
        M, K = a.shape
    if tb:
        Nn, K2 = b.shape
    else:
        K2, Nn = b.shape
    assert K == K2, (a.shape, b.shape, ta, tb)
    tn = _pick(Nn, (1408, 1024, 768, 512, 256, 128))
    if ta:
        tm = _pick(M, (1408, 1024, 768, 512, 256, 128))
        tk = _pick(K, (768, 512, 256))
    else:
        tm = _pick(M, (768, 256))
        tk = K if K <= MM_MAX_K else _pick(K, (1024, 768, 512, 256, 128))
    nk = K // tk
    dims = (((0 if ta else 1,), (1 if tb else 0,)), ((), ()))

    def body(*refs):
        a_ref, b_ref = refs[0], refs[1]
        add_ref = refs[2] if add is not None else None
        o_ref = refs[3] if add is not None else refs[2]
        acc = refs[-1] if nk > 1 else None

        def finish(r):
            if scale is not None:
                r = r * scale
            if mask_rows:
                r = jnp.where(_valid_rows(pl.program_id(0), tm), r, 0.0)
            if add_ref is not None:
                r = r + add_ref[...]
            o_ref[...] = r.astype(out_dtype)

        part = lax.dot_general(a_ref[...].astype(BF16), b_ref[...].astype(BF16), dims, preferred_element_type=F32)
        if nk == 1:
            finish(part)
        else:
            k = pl.program_id(2)

            @pl.when(k == 0)
            def _():
                acc[...] = part

            @pl.when(k > 0)
            def _():
                acc[...] += part

            @pl.when(k == nk - 1)
            def _():
                finish(acc[...])

    a_spec = pl.BlockSpec((tk, tm), lambda i, j, k: (k, i)) if ta else pl.BlockSpec((tm, tk), lambda i, j, k: (i, k))
    b_spec = pl.BlockSpec((tn, tk), lambda i, j, k: (j, k)) if tb else pl.BlockSpec((tk, tn), lambda i, j, k: (k, j))
    o_spec = pl.BlockSpec((tm, tn), lambda i, j, k: (i, j))
    in_specs = [a_spec, b_spec] + ([o_spec] if add is not None else [])
    args = (a, b) + ((add,) if add is not None else ())
    return pl.pallas_call(
        body,
        name=name,
        grid=(M // tm, Nn // tn, nk),
        in_specs=in_specs,
        out_specs=o_spec,
        out_shape=jax.ShapeDtypeStruct((M, Nn), out_dtype),
        scratch_shapes=[pltpu.VMEM((tm, tn), F32)] if nk > 1 else [],
        compiler_params=_cparams(("parallel", "parallel", "arbitrary")),
    )(*args)


def _rms_fwd(h, gains, name):
    tr = _pick(LP, (768, 256))
    ng = len(gains)

    def body(*refs):
        h_ref = refs[0]
        g_refs = refs[1 : 1 + ng]
        o_refs = refs[1 + ng :]
        x = h_ref[...]
        xh = x * lax.rsqrt(jnp.mean(x * x, axis=-1, keepdims=True) + EPS)
        for g_ref, o_ref in zip(g_refs, o_refs):
            o_ref[...] = (xh * g_ref[...]).astype(BF16)

    row = pl.BlockSpec((tr, D), lambda i: (i, 0))
    vec = pl.BlockSpec((1, D), lambda i: (0, 0))
    outs = pl.pallas_call(
        body,
        name=name,
        grid=(LP // tr,),
        in_specs=[row] + [vec] * ng,
        out_specs=[row] * ng,
        out_shape=[jax.ShapeDtypeStruct((LP, D), BF16)] * ng,
        compiler_params=_cparams(("parallel",)),
    )(h, *gains)
    return outs


def _rms_bwd(dh_in, h, dus, gains, name):
    tr = _pick(LP, (256,))
    ng = len(gains)

    def body(*refs):
        dh_ref, h_ref = refs[0], refs[1]
        du_refs = refs[2 : 2 + ng]
        g_refs = refs[2 + ng : 2 + 2 * ng]
        o_ref = refs[2 + 2 * ng]
        dg_refs = refs[3 + 2 * ng :]
        i = pl.program_id(0)
        x = h_ref[...]
        r = lax.rsqrt(jnp.mean(x * x, axis=-1, keepdims=True) + EPS)
        xh = x * r
        tot = dh_ref[...]
        for du_ref, g_ref, dg_ref in zip(du_refs, g_refs, dg_refs):
            du = du_ref[...]
            dxh = du * g_ref[...]
            tot = tot + r * (dxh - xh * jnp.mean(dxh * xh, axis=-1, keepdims=True))

            @pl.when(i == 0)
            def _():
                dg_ref[...] = jnp.zeros_like(dg_ref)

            dg_ref[...] += jnp.sum(du * xh, axis=0, keepdims=True)
        o_ref[...] = jnp.where(_valid_rows(i, tr), tot, 0.0)

    row = pl.BlockSpec((tr, D), lambda i: (i, 0))
    vec = pl.BlockSpec((1, D), lambda i: (0, 0))
    outs = pl.pallas_call(
        body,
        name=name,
        grid=(LP // tr,),
        in_specs=[row, row] + [row] * ng + [vec] * ng,
        out_specs=[row] + [vec] * ng,
        out_shape=[jax.ShapeDtypeStruct((LP, D), F32)] + [jax.ShapeDtypeStruct((1, D), F32)] * ng,
        compiler_params=_cparams(("arbitrary",)),
    )(dh_in, h, *dus, *gains)
    return outs[0], outs[1:]


def _loss_head(h, gain, target, name):
    tr = Q

    def body(h_ref, g_ref, t_ref, dh_ref, dg_ref, ls_ref):
        i = pl.program_id(0)

        @pl.when(i == 0)
        def _():
            dg_ref[...] = jnp.zeros_like(dg_ref)
            ls_ref[...] = jnp.zeros_like(ls_ref)
            dh_ref[...] = jnp.zeros_like(dh_ref)

        @pl.when(i > 0)
        def _():
            x = h_ref[...]
            g = g_ref[...]
            r = lax.rsqrt(jnp.mean(x * x, axis=-1, keepdims=True) + EPS)
            xh = x * r
            e = xh * g - t_ref[...]
            ls_ref[...] += jnp.sum(e * e, axis=0, keepdims=True)
            dy = e * (1.0 / D)
            dg_ref[...] += jnp.sum(dy * xh, axis=0, keepdims=True)
            dxh = dy * g
            dh_ref[...] = r * (dxh - xh * jnp.mean(dxh * xh, axis=-1, keepdims=True))

    row = pl.BlockSpec((tr, D), lambda i: (i, 0))
    vec = pl.BlockSpec((1, D), lambda i: (0, 0))
    return pl.pallas_call(
        body,
        name=name,
        grid=(LP // tr,),
        in_specs=[row, vec, pl.BlockSpec((tr, D), lambda i: (jnp.maximum(i - 1, 0), 0))],
        out_specs=[row, vec, vec],
        out_shape=[jax.ShapeDtypeStruct((LP, D), F32), jax.ShapeDtypeStruct((1, D), F32), jax.ShapeDtypeStruct((1, D), F32)],
        compiler_params=_cparams(("arbitrary",)),
    )(h, gain, target)


HALO = 8
CONV_COLS = (1536, 1408, 768, 512, 256)


def _conv_rows(ext, w, b, width):
    n = ext.shape[0]
    acc = b + w[width - 1 : width, :] * ext[HALO:]
    for k in range(width - 1):
        acc = acc + w[k : k + 1, :] * pltpu.roll(ext, width - 1 - k, 0)[HALO:]
    return acc


def _conv_specs(tr, tn, col):
    per = tr // HALO
    last = LP // HALO - 1
    prev = pl.BlockSpec((HALO, tn), lambda j, i: (jnp.maximum(i * per - 1, 0), col(j)))
    cur = pl.BlockSpec((tr, tn), lambda j, i: (i, col(j)))
    nxt = pl.BlockSpec((HALO, tn), lambda j, i: (jnp.minimum((i + 1) * per, last), col(j)))
    return prev, cur, nxt


def _conv_bwd_core(ext, dact_fn, w, b, width, i, nblk, tr):
    pre = _conv_rows(ext, w, b, width)
    dpre = dact_fn(pre)
    rows = i * tr + lax.broadcasted_iota(jnp.int32, (tr + HALO, 1), 0)
    dpre = jnp.where((rows >= PF) & (rows < LP), dpre, 0.0)
    n = tr + HALO
    dx = w[width - 1 : width, :] * dpre[:tr]
    for k in range(width - 1):
        sh = width - 1 - k
        dx = dx + w[k : k + 1, :] * pltpu.roll(dpre, n - sh, 0)[:tr]
    dcur = dpre[:tr]
    dws = []
    for k in range(width):
        sh = width - 1 - k
        xs = ext[HALO : HALO + tr] if sh == 0 else pltpu.roll(ext, sh, 0)[HALO : HALO + tr]
        dws.append(jnp.sum(xs * dcur, axis=0, keepdims=True))
    db = jnp.sum(dcur, axis=0, keepdims=True)
    dx = jnp.where(_valid_rows(i, tr), dx, 0.0)
    return dx, dws, db


def _ssd_conv_fwd(xr, cw, cb, name):
    tr, tn = Q, _pick(CD, CONV_COLS)

    def body(p_ref, c_ref, w_ref, b_ref, o_ref):
        i = pl.program_id(1)
        ext = jnp.concatenate([jnp.where(i > 0, p_ref[...], 0.0), c_ref[...]], axis=0)
        pre = _conv_rows(ext, w_ref[...], b_ref[...], CONVW)
        o_ref[...] = jnp.where(_valid_rows(i, tr), pre * _sigmoid(pre), 0.0)

    prev, cur, _ = _conv_specs(tr, tn, lambda j: j)
    return pl.pallas_call(
        body,
        name=name,
        grid=(CD // tn, LP // tr),
        in_specs=[prev, cur, pl.BlockSpec((CONVW, tn), lambda j, i: (0, j)), pl.BlockSpec((1, tn), lambda j, i: (0, j))],
        out_specs=cur,
        out_shape=jax.ShapeDtypeStruct((LP, CD), F32),
        compiler_params=_cparams(("parallel", "arbitrary")),
    )(xr, xr, cw, cb)


def _ssd_conv_bwd(xr, dxbc, cw, cb, name):
    tr, tn = Q, _pick(CD, CONV_COLS)
    nblk = LP // tr

    def body(p_ref, c_ref, n_ref, dc_ref, dn_ref, w_ref, b_ref, dx_ref, dw_ref, db_ref):
        i = pl.program_id(1)
        ext = jnp.concatenate([jnp.where(i > 0, p_ref[...], 0.0), c_ref[...], n_ref[...]], axis=0)
        dout = jnp.concatenate([dc_ref[...], dn_ref[...]], axis=0)
        dx, dws, db = _conv_bwd_core(ext, lambda pre: dout * _dsilu(pre), w_ref[...], b_ref[...], CONVW, i, nblk, tr)
        dx_ref[...] = dx

        @pl.when(i == 0)
        def _():
            dw_ref[...] = jnp.zeros_like(dw_ref)
            db_ref[...] = jnp.zeros_like(db_ref)

        for k in range(CONVW):
            dw_ref[k : k + 1, :] += dws[k]
        db_ref[...] += db

    prev, cur, nxt = _conv_specs(tr, tn, lambda j: j)
    wspec = pl.BlockSpec((CONVW, tn), lambda j, i: (0, j))
    bspec = pl.BlockSpec((1, tn), lambda j, i: (0, j))
    return pl.pallas_call(
        body,
        name=name,
        grid=(CD // tn, LP // tr),
        in_specs=[prev, cur, nxt, cur, nxt, wspec, bspec],
        out_specs=[cur, wspec, bspec],
        out_shape=[jax.ShapeDtypeStruct((LP, CD), F32), jax.ShapeDtypeStruct((CONVW, CD), F32), jax.ShapeDtypeStruct((1, CD), F32)],
        compiler_params=_cparams(("parallel", "arbitrary")),
    )(xr, xr, xr, dxbc, dxbc, cw, cb)


def _ffn_act_fwd(hg, hv, cwg, cwv, cbg, cbv, name):
    tr, tn = Q, _pick(DFF, CONV_COLS)

    def body(pg, cg, pv, cv, wg, wv, bg, bv, o_ref):
        i = pl.program_id(1)
        eg = jnp.concatenate([jnp.where(i > 0, pg[...], 0.0), cg[...]], axis=0)
        ev = jnp.concatenate([jnp.where(i > 0, pv[...], 0.0), cv[...]], axis=0)
        gate = _conv_rows(eg, wg[...], bg[...], FC)
        val = _conv_rows(ev, wv[...], bv[...], FC)
        o_ref[...] = (gate * _sigmoid(gate) * val).astype(BF16)

    prev, cur, _ = _conv_specs(tr, tn, lambda j: j)
    wspec = pl.BlockSpec((FC, tn), lambda j, i: (0, j))
    bspec = pl.BlockSpec((1, tn), lambda j, i: (0, j))
    return pl.pallas_call(
        body,
        name=name,
        grid=(DFF // tn, LP // tr),
        in_specs=[prev, cur, prev, cur, wspec, wspec, bspec, bspec],
        out_specs=cur,
        out_shape=jax.ShapeDtypeStruct((LP, DFF), BF16),
        compiler_params=_cparams(("parallel", "arbitrary")),
    )(hg, hg, hv, hv, cwg, cwv, cbg, cbv)


def _ffn_act_bwd(hg, hv, da, cwg, cwv, cbg, cbv, name):
    tr, tn = Q, _pick(DFF, CONV_COLS)
    nblk = LP // tr

    def body(pg, cg, ng, pv, cv, nv, dc, dn, wg, wv, bg, bv, dg_ref, dv_ref, dwg, dwv, dbg, dbv):
        i = pl.program_id(1)
        eg = jnp.concatenate([jnp.where(i > 0, pg[...], 0.0), cg[...], ng[...]], axis=0)
        ev = jnp.concatenate([jnp.where(i > 0, pv[...], 0.0), cv[...], nv[...]], axis=0)
        dout = jnp.concatenate([dc[...], dn[...]], axis=0)
        gate = _conv_rows(eg, wg[...], bg[...], FC)
        val = _conv_rows(ev, wv[...], bv[...], FC)
        dxg, dwsg, dbgv = _conv_bwd_core(eg, lambda pre: dout * val * _dsilu(pre), wg[...], bg[...], FC, i, nblk, tr)
        dxv, dwsv, dbvv = _conv_bwd_core(ev, lambda pre: dout * gate * _sigmoid(gate), wv[...], bv[...], FC, i, nblk, tr)
        dg_ref[...] = dxg
        dv_ref[...] = dxv

        @pl.when(i == 0)
        def _():
            dwg[...] = jnp.zeros_like(dwg)
            dwv[...] = jnp.zeros_like(dwv)
            dbg[...] = jnp.zeros_like(dbg)
            dbv[...] = jnp.zeros_like(dbv)

        for k in range(FC):
            dwg[k : k + 1, :] += dwsg[k]
            dwv[k : k + 1, :] += dwsv[k]
        dbg[...] += dbgv
        dbv[...] += dbvv

    prev, cur, nxt = _conv_specs(tr, tn, lambda j: j)
    wspec = pl.BlockSpec((FC, tn), lambda j, i: (0, j))
    bspec = pl.BlockSpec((1, tn), lambda j, i: (0, j))
    big = jax.ShapeDtypeStruct((LP, DFF), F32)
    wsh = jax.ShapeDtypeStruct((FC, DFF), F32)
    bsh = jax.ShapeDtypeStruct((1, DFF), F32)
    return pl.pallas_call(
        body,
        name=name,
        grid=(DFF // tn, LP // tr),
        in_specs=[prev, cur, nxt, prev, cur, nxt, cur, nxt, wspec, wspec, bspec, bspec],
        out_specs=[cur, cur, wspec, wspec, bspec, bspec],
        out_shape=[big, big, wsh, wsh, bsh, bsh],
        compiler_params=_cparams(("parallel", "arbitrary")),
    )(hg, hg, hg, hv, hv, hv, da, da, cwg, cwv, cbg, cbv)


def _gate_fwd(y, z, gg, name):
    tr = _pick(LP, (768, 256))

    def body(y_ref, z_ref, g_ref, o_ref):
        zv = z_ref[...]
        hg = y_ref[...] * zv * _sigmoid(zv)
        r = lax.rsqrt(jnp.mean(hg * hg, axis=-1, keepdims=True) + EPS)
        o_ref[...] = (hg * r * g_ref[...]).astype(BF16)

    blk = pl.BlockSpec((tr, GW), lambda i, g: (i, g))
    return pl.pallas_call(
        body,
        name=name,
        grid=(LP // tr, G),
        in_specs=[blk, blk, pl.BlockSpec((1, GW), lambda i, g: (0, g))],
        out_specs=blk,
        out_shape=jax.ShapeDtypeStruct((LP, DI), BF16),
        compiler_params=_cparams(("parallel", "parallel")),
    )(y, z, gg)


def _gate_bwd(dout, y, z, gg, name):
    tr = _pick(LP, (768, 256))

    def body(do_ref, y_ref, z_ref, g_ref, dy_ref, dz_ref, dg_ref):
        i = pl.program_id(1)
        zv = z_ref[...]
        yv = y_ref[...]
        sz = zv * _sigmoid(zv)
        hg = yv * sz
        r = lax.rsqrt(jnp.mean(hg * hg, axis=-1, keepdims=True) + EPS)
        hh = hg * r
        do = do_ref[...]
        dhh = do * g_ref[...]
        dhg = r * (dhh - hh * jnp.mean(dhh * hh, axis=-1, keepdims=True))
        dy_ref[...] = dhg * sz
        dz_ref[...] = dhg * yv * _dsilu(zv)

        @pl.when(i == 0)
        def _():
            dg_ref[...] = jnp.zeros_like(dg_ref)

        dg_ref[...] += jnp.sum(do * hh, axis=0, keepdims=True)

    blk = pl.BlockSpec((tr, GW), lambda g, i: (i, g))
    vec = pl.BlockSpec((1, GW), lambda g, i: (0, g))
    big = jax.ShapeDtypeStruct((LP, DI), F32)
    return pl.pallas_call(
        body,
        name=name,
        grid=(G, LP // tr),
        in_specs=[blk, blk, blk, vec],
        out_specs=[blk, blk, vec],
        out_shape=[big, big, jax.ShapeDtypeStruct((1, DI), F32)],
        compiler_params=_cparams(("parallel", "arbitrary")),
    )(dout, y, z, gg)


def _ssd_common(dtc_ref, dtr_ref, bc_ref, br_ref, ac_ref, ar_ref, c):
    rows = c * Q + lax.broadcasted_iota(jnp.int32, (Q, 1), 0)
    cols = c * Q + lax.broadcasted_iota(jnp.int32, (1, Q), 1)
    prec = dtc_ref[...] + bc_ref[...]
    prer = dtr_ref[...] + br_ref[...]
    dtc = jnp.where(rows >= PF, _softplus(prec), 0.0)
    dtr = jnp.where(cols >= PF, _softplus(prer), 0.0)
    a_c = -jnp.exp(ac_ref[...])
    a_r = -jnp.exp(ar_ref[...])
    li = lax.broadcasted_iota(jnp.int32, (Q, Q), 0)
    si = lax.broadcasted_iota(jnp.int32, (Q, Q), 1)
    tril = si <= li
    trif = tril.astype(F32)
    csc = jnp.dot(trif, dtc * a_c, precision=HI, preferred_element_type=F32)
    csr = lax.dot_general(dtr * a_r, trif, (((1,), (1,)), ((), ())), precision=HI, preferred_element_type=F32)
    return dict(rows=rows, prec=prec, dtc=dtc, a_c=a_c, tril=tril, trif=trif, csc=csc, csr=csr, li=li, si=si)


def _pair_expand(arr, h0, lane_lo):
    return jnp.where(lane_lo, arr[:, h0 : h0 + 1], arr[:, h0 + 1 : h0 + 2])


def _ssd_specs():
    nb = DI // N
    xs = pl.BlockSpec((Q, GW), lambda g, c: (c, g))
    bb = pl.BlockSpec((Q, N), lambda g, c: (c, nb + g))
    cc = pl.BlockSpec((Q, N), lambda g, c: (c, nb + G + g))
    dtc = pl.BlockSpec((None, Q, LANES), lambda g, c: (g, c, 0))
    dtr = pl.BlockSpec((None, 8, Q), lambda g, c: (g, 0, c))
    pc = pl.BlockSpec((None, 1, LANES), lambda g, c: (g, 0, 0))
    pr = pl.BlockSpec((None, 8, 1), lambda g, c: (g, 0, 0))
    return xs, bb, cc, dtc, dtr, pc, pr


def _ssd_fwd(xbc, dtc, dtr, bias_c, bias_r, alog_c, alog_r, dskip_c, gather, name):
    nb = len(gather)

    def body(*refs):
        xs_ref, b_ref, c_ref, dtc_ref, dtr_ref, bc_ref, br_ref, ac_ref, ar_ref, dk_ref = refs[:10]
        gin = refs[10 : 10 + nb]
        y_ref, st_ref = refs[10 + nb : 12 + nb]
        gout = refs[12 + nb : 12 + 2 * nb]
        state, send_sems, recv_sems = refs[12 + 2 * nb :]
        c = pl.program_id(1)
        first_step = (pl.program_id(0) == 0) & (c == 0)
        last_step = (pl.program_id(0) == G - 1) & (c == NC - 1)

        @pl.when(first_step)
        def _():
            _gather_start(gin, gout, send_sems, recv_sems)

        @pl.when(c == 0)
        def _():
            state[...] = jnp.zeros_like(state)

        st_ref[...] = state[...]
        cm = _ssd_common(dtc_ref, dtr_ref, bc_ref, br_ref, ac_ref, ar_ref, c)
        Bm = b_ref[...]
        Cm = c_ref[...]
        cb = lax.dot_general(Cm.astype(BF16), Bm.astype(BF16), (((1,), (1,)), ((), ())), preferred_element_type=F32)
        bt = Bm.T.astype(BF16)
        lane_lo = lax.broadcasted_iota(jnp.int32, (1, HP), 1) < P
        csc, csr, dtc_v = cm["csc"], cm["csr"], cm["dtc"]
        ecs = jnp.exp(csc)
        cs_end = csc[Q - 1 : Q, :]
        wdec = jnp.exp(cs_end - csc)
        eend = jnp.exp(cs_end)
        for pp in range(E // 2):
            h0 = 2 * pp
            sl = slice(pp * HP, (pp + 1) * HP)
            xp = xs_ref[:, sl]
            xdt = xp * _pair_expand(dtc_v, h0, lane_lo)
            yacc = xp * _pair_expand(dk_ref[...], h0, lane_lo)
            for e in range(2):
                h = h0 + e
                lm = jnp.where(cm["tril"], jnp.exp(jnp.minimum(csc[:, h : h + 1] - csr[h : h + 1, :], 0.0)), 0.0)
                m = (cb * lm).astype(BF16)
                xm = jnp.where(lane_lo if e == 0 else jnp.logical_not(lane_lo), xdt, 0.0).astype(BF16)
                yacc = yacc + jnp.dot(m, xm, preferred_element_type=F32)
            stp = state[:, sl]
            yoff = jnp.dot(Cm.astype(BF16), stp.astype(BF16), preferred_element_type=F32)
            y_ref[:, sl] = yacc + yoff * _pair_expand(ecs, h0, lane_lo)
            xw = (xdt * _pair_expand(wdec, h0, lane_lo)).astype(BF16)
            state[:, sl] = stp * _pair_expand(eend, h0, lane_lo) + jnp.dot(bt, xw, preferred_element_type=F32)

        @pl.when(last_step)
        def _():
            _gather_finish(gin, gout, send_sems, recv_sems)

    xs, bb, cc, dtcs, dtrs, pc, pr = _ssd_specs()
    outs = pl.pallas_call(
        body,
        name=name,
        grid=(G, NC),
        in_specs=[xs, bb, cc, dtcs, dtrs, pc, pr, pc, pr, pc] + [ANY] * nb,
        out_specs=[xs, pl.BlockSpec((None, None, N, GW), lambda g, c: (c, g, 0, 0))] + [ANY] * nb,
        out_shape=[jax.ShapeDtypeStruct((LP, DI), F32), jax.ShapeDtypeStruct((NC, G, N, GW), F32)] + _gather_shapes(gather),
        scratch_shapes=[pltpu.VMEM((N, GW), F32)] + _gather_sems(nb),
        compiler_params=_cparams(("arbitrary", "arbitrary")),
    )(xbc, xbc, xbc, dtc, dtr, bias_c, bias_r, alog_c, alog_r, dskip_c, *gather)
    return outs[0], outs[1], _gather_own(outs[2:], gather)


def _ssd_bwd(xbc, dy, states, dtc, dtr, bias_c, bias_r, alog_c, alog_r, dskip_c, scatter, name):
    nb = len(scatter)

    def body(*refs):
        xs_ref, b_ref, c_ref, dy_ref, st_ref, dtc_ref, dtr_ref, bc_ref, br_ref, ac_ref, ar_ref, dk_ref = refs[:12]
        sin = refs[12 : 12 + nb]
        dx_ref, db_ref, dc_ref, ddt_ref, dbias_ref, dalog_ref, ddk_ref = refs[12 + nb : 19 + nb]
        sout = refs[19 + nb : 19 + 2 * nb]
        dstate, send_sems, recv_sems = refs[19 + 2 * nb :]
        ci = pl.program_id(1)
        c = NC - 1 - ci

        @pl.when((pl.program_id(0) == 0) & (ci == 0))
        def _():
            _scatter_start(sin, sout, send_sems, recv_sems)

        @pl.when(ci == 0)
        def _():
            dstate[...] = jnp.zeros_like(dstate)
            dbias_ref[...] = jnp.zeros_like(dbias_ref)
            dalog_ref[...] = jnp.zeros_like(dalog_ref)
            ddk_ref[...] = jnp.zeros_like(ddk_ref)

        cm = _ssd_common(dtc_ref, dtr_ref, bc_ref, br_ref, ac_ref, ar_ref, c)
        Bm = b_ref[...]
        Cm = c_ref[...]
        Bb = Bm.astype(BF16)
        Cb = Cm.astype(BF16)
        nt = (((1,), (1,)), ((), ()))
        cb = lax.dot_general(Cb, Bb, nt, preferred_element_type=F32)
        cbt = lax.dot_general(Bb, Cb, nt, preferred_element_type=F32)
        ct = Cm.T.astype(BF16)
        lane_lo = lax.broadcasted_iota(jnp.int32, (1, HP), 1) < P
        lane_id = lax.broadcasted_iota(jnp.int32, (1, LANES), 1)
        csc, csr, dtc_v, a_c = cm["csc"], cm["csr"], cm["dtc"], cm["a_c"]
        triu = cm["si"] >= cm["li"]
        ecs = jnp.exp(csc)
        cs_end = csc[Q - 1 : Q, :]
        wdec = jnp.exp(cs_end - csc)
        eend = jnp.exp(cs_end)
        dcb = jnp.zeros((Q, Q), F32)
        dcbt = jnp.zeros((Q, Q), F32)
        dcs = jnp.zeros((Q, LANES), F32)
        dcs_end = jnp.zeros((1, LANES), F32)
        ddt = jnp.zeros((Q, LANES), F32)
        ddk = jnp.zeros((1, LANES), F32)
        dB = jnp.zeros((Q, N), F32)
        dC = jnp.zeros((Q, N), F32)
        for pp in range(E // 2):
            h0 = 2 * pp
            sl = slice(pp * HP, (pp + 1) * HP)
            xp = xs_ref[:, sl]
            dyp = dy_ref[:, sl]
            dtx = _pair_expand(dtc_v, h0, lane_lo)
            xdt = xp * dtx
            dxdt = jnp.zeros((Q, HP), F32)
            stp = st_ref[:, sl]
            stb = stp.astype(BF16)
            dsn = dstate[:, sl]
            dsnb = dsn.astype(BF16)
            ecsx = _pair_expand(ecs, h0, lane_lo)
            wdx = _pair_expand(wdec, h0, lane_lo)
            cs_ = jnp.dot(Cb, stb, preferred_element_type=F32)
            yo = cs_ * ecsx
            dyo = dyp * ecsx
            dyob = dyo.astype(BF16)
            dC = dC + lax.dot_general(dyob, stb, nt, preferred_element_type=F32)
            ds_from_y = jnp.dot(ct, dyob, preferred_element_type=F32)
            xw = xdt * wdx
            dB = dB + lax.dot_general(xw.astype(BF16), dsnb, nt, preferred_element_type=F32)
            dxw = jnp.dot(Bb, dsnb, preferred_element_type=F32)
            dxdt = dxdt + dxw * wdx
            w2 = dxw * xw
            rs = jnp.sum(dsn * stp, axis=0, keepdims=True) * _pair_expand(eend, h0, lane_lo)
            dstate[:, sl] = dsn * _pair_expand(eend, h0, lane_lo) + ds_from_y
            dyx = dyp * xp
            yd = jnp.zeros((Q, HP), F32)
            dxd = jnp.zeros((Q, HP), F32)
            for e in range(2):
                h = h0 + e
                msk = lane_lo if e == 0 else jnp.logical_not(lane_lo)
                col = csc[:, h : h + 1]
                row = csr[h : h + 1, :]
                lm = jnp.where(cm["tril"], jnp.exp(jnp.minimum(col - row, 0.0)), 0.0)
                lmt = jnp.where(triu, jnp.exp(jnp.minimum(row - col, 0.0)), 0.0)
                dye = jnp.where(msk, dyp, 0.0).astype(BF16)
                xde = jnp.where(msk, xdt, 0.0).astype(BF16)
                gm = lax.dot_general(dye, xde, nt, preferred_element_type=F32)
                gmt = lax.dot_general(xde, dye, nt, preferred_element_type=F32)
                dcb = dcb + gm * lm
                dcbt = dcbt + gmt * lmt
                yd = yd + jnp.dot((cb * lm).astype(BF16), xde, preferred_element_type=F32)
                dxd = dxd + jnp.dot((cbt * lmt).astype(BF16), dye, preferred_element_type=F32)
            dxdt = dxdt + dxd
            tt = dyp * yo - w2 + dyp.astype(BF16).astype(F32) * yd - xdt.astype(BF16).astype(F32) * dxd
            dx_ref[:, sl] = dxdt * dtx + dyp * _pair_expand(dk_ref[...], h0, lane_lo)
            dxx = dxdt * xp
            for e in range(2):
                h = h0 + e
                msk = lane_lo if e == 0 else jnp.logical_not(lane_lo)
                oh = (lane_id == h).astype(F32)
                dcs = dcs + jnp.sum(jnp.where(msk, tt, 0.0), axis=1, keepdims=True) * oh
                dcs_end = dcs_end + (_sum_all(jnp.where(msk, w2, 0.0)) + _sum_all(jnp.where(msk, rs, 0.0))) * oh
                ddk = ddk + _sum_all(jnp.where(msk, dyx, 0.0)) * oh
                ddt = ddt + jnp.sum(jnp.where(msk, dxx, 0.0), axis=1, keepdims=True) * oh
        dC = dC + jnp.dot(dcb.astype(BF16), Bb, preferred_element_type=F32)
        dB = dB + jnp.dot(dcbt.astype(BF16), Cb, preferred_element_type=F32)
        db_ref[...] = dB
        dc_ref[...] = dC
        last = (lax.broadcasted_iota(jnp.int32, (Q, 1), 0) == Q - 1).astype(F32)
        dcs = dcs + last * dcs_end
        dda = jnp.dot(triu.astype(F32), dcs, precision=HI, preferred_element_type=F32)
        ddt = ddt + dda * a_c
        da = jnp.sum(dda * dtc_v, axis=0, keepdims=True)
        draw = jnp.where(cm["rows"] >= PF, ddt * _sigmoid(cm["prec"]), 0.0)
        ddt_ref[...] = draw
        dbias_ref[...] += jnp.sum(draw, axis=0, keepdims=True)
        dalog_ref[...] += da * a_c
        ddk_ref[...] += ddk

        @pl.when((pl.program_id(0) == G - 1) & (ci == NC - 1))
        def _():
            _scatter_finish(sin, sout, send_sems, recv_sems)

    xs, bb, cc, dtcs, dtrs, pc, pr = _ssd_specs()

    def rev(spec_fn):
        return lambda g, ci: spec_fn(g, NC - 1 - ci)

    def rspec(spec):
        return pl.BlockSpec(spec.block_shape, rev(spec.index_map))

    xs_r, bb_r, cc_r, dtc_r, dtr_r = rspec(xs), rspec(bb), rspec(cc), rspec(dtcs), rspec(dtrs)
    st_r = pl.BlockSpec((None, None, N, GW), lambda g, ci: (NC - 1 - ci, g, 0, 0))
    gn = pl.BlockSpec((Q, N), lambda g, ci: (NC - 1 - ci, g))
    outs = pl.pallas_call(
        body,
        name=name,
        grid=(G, NC),
        in_specs=[xs_r, bb_r, cc_r, xs_r, st_r, dtc_r, dtr_r, pc, pr, pc, pr, pc] + [ANY] * nb,
        out_specs=[xs_r, gn, gn, dtc_r, pc, pc, pc] + [ANY] * nb,
        out_shape=[
            jax.ShapeDtypeStruct((LP, DI), F32),
            jax.ShapeDtypeStruct((LP, G * N), F32),
            jax.ShapeDtypeStruct((LP, G * N), F32),
            jax.ShapeDtypeStruct((G, LP, LANES), F32),
            jax.ShapeDtypeStruct((G, 1, LANES), F32),
            jax.ShapeDtypeStruct((G, 1, LANES), F32),
            jax.ShapeDtypeStruct((G, 1, LANES), F32),
        ]
        + [jax.ShapeDtypeStruct(b.shape, b.dtype) for b in scatter],
        scratch_shapes=[pltpu.VMEM((N, GW), F32)] + _scatter_sems(nb),
        compiler_params=_cparams(("arbitrary", "arbitrary")),
    )(xbc, xbc, xbc, dy, states, dtc, dtr, bias_c, bias_r, alog_c, alog_r, dskip_c, *scatter)
    return tuple(outs[:7]) + (_scatter_own(outs[7:], scatter),)


def _split_dot(x, u):
    hi = x.astype(BF16)
    lo = (x - hi.astype(F32)).astype(BF16)
    return jnp.dot(hi, u, preferred_element_type=F32) + jnp.dot(lo, u, preferred_element_type=F32)


def _sb_block(qe, kblk, vis, a_run, u_gt):
    l = lax.dot_general(qe, kblk, (((1,), (1,)), ((), ())), preferred_element_type=F32)
    lk = jnp.minimum(-l, 0.0) - jnp.log(1.0 + jnp.exp(-jnp.abs(l)))
    lbeta = l + lk
    if vis is not None:
        lk = jnp.where(vis, lk, 0.0)
    logw = lbeta + _split_dot(lk, u_gt) + a_run
    return lbeta, lk, logw


def _descend(i, block, carry):
    def pack(n, c):
        return (n, jnp.max(jnp.maximum(c[0], c[1]))) + tuple(c)

    st = pack(jnp.int32(1), block(i, carry, True))
    st = lax.while_loop(lambda st: (st[0] < i) & (st[1] > -T_SKIP), lambda st: pack(st[0] + 1, block(i - st[0], st[2:], False)), st)
    st = lax.while_loop(lambda st: (st[0] == i) & (st[1] > -T_SKIP), lambda st: pack(st[0] + 1, block(0, st[2:], True)), st)
    return st[2:]


def _attn_fwd(q, k, v, name):
    nq = LP // TQ

    def body(q_ref, k_ref, v_ref, o_ref):
        i = pl.program_id(1)
        qv = q_ref[...]
        lane_lo = lax.broadcasted_iota(jnp.int32, (1, HP), 1) < 64
        t_idx = i * TQ + lax.broadcasted_iota(jnp.int32, (TQ, 1), 0)
        ji = lax.broadcasted_iota(jnp.int32, (TQ, TQ), 0)
        si = lax.broadcasted_iota(jnp.int32, (TQ, TQ), 1)
        u_gt = (ji > si).astype(BF16)
        qs = [jnp.where(lane_lo, qv, jnp.zeros_like(qv)), jnp.where(lane_lo, jnp.zeros_like(qv), qv)]

        def block(kb, carry, masked):
            a0, a1, acc = carry
            off = pl.multiple_of(kb * TQ, TQ)
            kblk = k_ref[pl.ds(off, TQ), :]
            vblk = v_ref[pl.ds(off, TQ), :]
            vis = None
            if masked:
                s_idx = kb * TQ + lax.broadcasted_iota(jnp.int32, (1, TQ), 1)
                vis = (s_idx < t_idx) & (s_idx >= PF)
            new_a = []
            for e, a_run in enumerate((a0, a1)):
                _, lk, logw = _sb_block(qs[e], kblk, vis, a_run, u_gt)
                w = jnp.exp(logw)
                if masked:
                    w = jnp.where(vis, w, 0.0)
                msk = lane_lo if e == 0 else jnp.logical_not(lane_lo)
                acc = acc + jnp.dot(w.astype(BF16), jnp.where(msk, vblk, jnp.zeros_like(vblk)), preferred_element_type=F32)
                new_a.append(a_run + jnp.sum(lk, axis=1, keepdims=True))
            return new_a[0], new_a[1], acc

        z1 = jnp.zeros((TQ, 1), F32)
        _, _, acc = _descend(i, block, (z1, z1, jnp.zeros((TQ, HP), F32)))
        o_ref[...] = acc

    return pl.pallas_call(
        body,
        name=name,
        grid=(D // HP, nq),
        in_specs=[
            pl.BlockSpec((TQ, HP), lambda j, i: (i, j)),
            pl.BlockSpec((LP, HP), lambda j, i: (0, j)),
            pl.BlockSpec((LP, HP), lambda j, i: (0, j)),
        ],
        out_specs=pl.BlockSpec((TQ, HP), lambda j, i: (i, j)),
        out_shape=jax.ShapeDtypeStruct((LP, D), F32),
        compiler_params=_cparams(("parallel", "arbitrary")),
    )(q, k, v)


def _attn_bwd(q, k, v, o, do, name):
    nq = LP // TQ

    def body(q_ref, k_ref, v_ref, o_ref, do_ref, dq_ref, dk_ref, dv_ref):
        i = pl.program_id(1)

        @pl.when(i == 0)
        def _():
            dk_ref[...] = jnp.zeros_like(dk_ref)
            dv_ref[...] = jnp.zeros_like(dv_ref)

        qv = q_ref[...]
        dov = do_ref[...]
        lane_lo = lax.broadcasted_iota(jnp.int32, (1, HP), 1) < 64
        t_idx = i * TQ + lax.broadcasted_iota(jnp.int32, (TQ, 1), 0)
        ji = lax.broadcasted_iota(jnp.int32, (TQ, TQ), 0)
        si = lax.broadcasted_iota(jnp.int32, (TQ, TQ), 1)
        u_gt = (ji > si).astype(BF16)
        u_ge = (ji >= si).astype(BF16)
        msks = [lane_lo, jnp.logical_not(lane_lo)]
        qs = [jnp.where(m, qv, jnp.zeros_like(qv)) for m in msks]
        dob = [jnp.where(m, dov, 0.0).astype(BF16) for m in msks]
        ov = o_ref[...]
        deltas = [jnp.sum(d.astype(F32) * ov, axis=1, keepdims=True) for d in dob]
        nt = (((1,), (1,)), ((), ()))
        tn = (((0,), (0,)), ((), ()))

        def block(kb, carry, masked):
            a0, a1, p0, p1, dq = carry
            off = pl.multiple_of(kb * TQ, TQ)
            kblk = k_ref[pl.ds(off, TQ), :]
            vblk = v_ref[pl.ds(off, TQ), :]
            vis = None
            if masked:
                s_idx = kb * TQ + lax.broadcasted_iota(jnp.int32, (1, TQ), 1)
                vis = (s_idx < t_idx) & (s_idx >= PF)
            new_a, new_p = [], []
            dk_acc = jnp.zeros((TQ, HP), F32)
            dv_acc = jnp.zeros((TQ, HP), F32)
            for e, (a_run, p_run) in enumerate(((a0, p0), (a1, p1))):
                lbeta, lk, logw = _sb_block(qs[e], kblk, vis, a_run, u_gt)
                sig = jnp.exp(lbeta)
                w = jnp.exp(logw)
                if masked:
                    w = jnp.where(vis, w, 0.0)
                wb = w.astype(BF16)
                dw = lax.dot_general(dob[e], vblk, nt, preferred_element_type=F32)
                pm = wb.astype(F32) * dw
                cum_p = deltas[e] - (_split_dot(pm, u_ge) + p_run)
                dl = pm - (pm + cum_p) * sig
                if masked:
                    dl = jnp.where(vis, dl, 0.0)
                dl = dl.astype(BF16)
                km = jnp.where(msks[e], kblk, jnp.zeros_like(kblk))
                dq = dq + jnp.dot(dl, km, preferred_element_type=F32)
                dk_acc = dk_acc + lax.dot_general(dl, qs[e], tn, preferred_element_type=F32)
                dv_acc = dv_acc + lax.dot_general(wb, dob[e], tn, preferred_element_type=F32)
                new_a.append(a_run + jnp.sum(lk, axis=1, keepdims=True))
                new_p.append(p_run + jnp.sum(pm, axis=1, keepdims=True))
            dk_ref[pl.ds(off, TQ), :] += dk_acc
            dv_ref[pl.ds(off, TQ), :] += dv_acc
            return new_a[0], new_a[1], new_p[0], new_p[1], dq

        z1 = jnp.zeros((TQ, 1), F32)
        carry = _descend(i, block, (z1, z1, z1, z1, jnp.zeros((TQ, HP), F32)))
        dq_ref[...] = carry[4]

    blk = pl.BlockSpec((TQ, HP), lambda j, i: (i, j))
    full = pl.BlockSpec((LP, HP), lambda j, i: (0, j))
    big = jax.ShapeDtypeStruct((LP, D), F32)
    return pl.pallas_call(
        body,
        name=name,
        grid=(D // HP, nq),
        in_specs=[blk, full, full, blk, blk],
        out_specs=[blk, full, full],
        out_shape=[big, big, big],
        compiler_params=_cparams(("parallel", "arbitrary")),
    )(q, k, v, o, do)


ROW_TILES = (2048, 1024, 512, 256, 128, 64, 32, 16, 8)


def _row_tile(rows, cols, budget):
    if rows % 8:
        return rows
    return _pick(rows, tuple(t for t in ROW_TILES if t * cols <= budget) or (8,))


def _adamw(w, g, m, v, name):
    R, C = w.shape
    tr = _row_tile(R, C, 128 * 1024)

    def body(w_ref, g_ref, m_ref, v_ref, g_out, d_out, m_out, v_out):
        g = g_ref[...]
        mn = ADAM_B1 * m_ref[...] + (1.0 - ADAM_B1) * g
        vn = ADAM_B2 * v_ref[...] + (1.0 - ADAM_B2) * (g * g)
        mh = mn / (1.0 - ADAM_B1**ADAM_STEP)
        vh = vn / (1.0 - ADAM_B2**ADAM_STEP)
        g_out[...] = g
        d_out[...] = -ADAM_LR * (mh / (jnp.sqrt(vh) + ADAM_EPS) + ADAM_WD * w_ref[...])
        m_out[...] = mn
        v_out[...] = vn

    blk = pl.BlockSpec((tr, C), lambda i: (i, 0))
    sh = jax.ShapeDtypeStruct((R, C), F32)
    return pl.pallas_call(
        body,
        name=name,
        grid=(R // tr,),
        in_specs=[blk] * 4,
        out_specs=[blk] * 4,
        out_shape=[sh] * 4,
        compiler_params=_cparams(("parallel",)),
    )(w, g, m, v)


def _sum4(buf, name):
    _, R, C = buf.shape
    tr = _row_tile(R, C, 128 * 1024)

    def body(b_ref, o_ref):
        acc = b_ref[0].astype(F32)
        for s in range(1, NCHIP):
            acc = acc + b_ref[s].astype(F32)
        o_ref[...] = acc

    return pl.pallas_call(
        body,
        name=name,
        grid=(R // tr,),
        in_specs=[pl.BlockSpec((NCHIP, tr, C), lambda i: (0, i, 0))],
        out_specs=pl.BlockSpec((tr, C), lambda i: (i, 0)),
        out_shape=jax.ShapeDtypeStruct((R, C), F32),
        compiler_params=_cparams(("parallel",)),
    )(buf)


def _add2(a, b, name):
    S, R, C = a.shape
    tr = _row_tile(R, C, 256 * 1024)

    def body(a_ref, b_ref, o_ref):
        o_ref[...] = (a_ref[...].astype(F32) + b_ref[...].astype(F32)).astype(o_ref.dtype)

    blk = pl.BlockSpec((None, tr, C), lambda s, i: (s, i, 0))
    return pl.pallas_call(
        body,
        name=name,
        grid=(S, R // tr),
        in_specs=[blk, blk],
        out_specs=blk,
        out_shape=jax.ShapeDtypeStruct(a.shape, a.dtype),
        compiler_params=_cparams(("parallel", "parallel")),
    )(a, b)


ANY = pl.BlockSpec(memory_space=pl.ANY)


def _mesh_place():
    x, y, c = lax.axis_index("x"), lax.axis_index("y"), lax.axis_index("c")
    return x, y, c, 2 * x + y, [(1 - x, y), (x, 1 - y), (1 - x, 1 - y)]


def _gather_chips(bufs, name):
    nb = len(bufs)

    def body(*refs):
        ins = refs[:nb]
        outs = refs[nb : 2 * nb]
        send_sems, recv_sems = refs[2 * nb :]
        _gather_start(ins, outs, send_sems, recv_sems)
        _gather_finish(ins, outs, send_sems, recv_sems)

    outs = pl.pallas_call(
        body,
        name=name,
        in_specs=[ANY] * nb,
        out_specs=[ANY] * nb,
        out_shape=_gather_shapes(bufs),
        scratch_shapes=_gather_sems(nb),
    )(*bufs)
    return _gather_own(outs, bufs)


def _gather_shapes(bufs):
    return [jax.ShapeDtypeStruct((NCHIP,) + tuple(b.shape), b.dtype) for b in bufs]


def _gather_sems(nb):
    return [pltpu.SemaphoreType.DMA((6 * nb,)), pltpu.SemaphoreType.DMA((6 * nb,))]


def _gather_own(outs, bufs):
    me = 2 * lax.axis_index("x") + lax.axis_index("y")
    return [lax.dynamic_update_slice(o, b[None], (me, 0, 0)) for o, b in zip(outs, bufs)]


def _gather_copy(outs, send_sems, recv_sems, k, b, src, slot, hc, to):
    nb = len(outs)
    hr = outs[b].shape[1] // 2
    return pltpu.make_async_remote_copy(
        src_ref=src, dst_ref=outs[b].at[slot, pl.ds(hc * hr, hr)], send_sem=send_sems.at[k * nb + b],
        recv_sem=recv_sems.at[k * nb + b], device_id=to, device_id_type=MESH)


def _gather_start(ins, outs, send_sems, recv_sems):
    x, y, c, me, peers = _mesh_place()
    for k, (px, py) in enumerate(peers):
        for b in range(len(ins)):
            hr = ins[b].shape[0] // 2
            _gather_copy(outs, send_sems, recv_sems, k, b, ins[b].at[pl.ds(c * hr, hr)], me, c, (px, py, c)).start()


def _gather_finish(ins, outs, send_sems, recv_sems):
    x, y, c, me, peers = _mesh_place()
    nb = len(ins)
    sent = []
    for k, (px, py) in enumerate(peers):
        for b in range(nb):
            hr = ins[b].shape[0] // 2
            slot = 2 * px + py
            landed = outs[b].at[slot, pl.ds(c * hr, hr)]
            _gather_copy(outs, send_sems, recv_sems, k, b, landed, slot, c, (px, py, c)).wait_recv()
            cp = _gather_copy(outs, send_sems, recv_sems, 3 + k, b, landed, slot, c, (x, y, 1 - c))
            cp.start()
            sent.append(cp)
            sent.append(_gather_copy(outs, send_sems, recv_sems, k, b, ins[b].at[pl.ds(c * hr, hr)], me, c, (px, py, c)))
    for k, (px, py) in enumerate(peers):
        for b in range(nb):
            hr = ins[b].shape[0] // 2
            slot = 2 * px + py
            theirs = outs[b].at[slot, pl.ds((1 - c) * hr, hr)]
            _gather_copy(outs, send_sems, recv_sems, 3 + k, b, theirs, slot, 1 - c, (x, y, 1 - c)).wait_recv()
    for cp in sent:
        cp.wait_send()


def _scatter_chips(bufs, name):
    nb = len(bufs)

    def body(*refs):
        ins = refs[:nb]
        outs = refs[nb : 2 * nb]
        send_sems, recv_sems = refs[2 * nb :]
        _scatter_start(ins, outs, send_sems, recv_sems)
        _scatter_finish(ins, outs, send_sems, recv_sems)

    outs = pl.pallas_call(
        body,
        name=name,
        in_specs=[ANY] * nb,
        out_specs=[ANY] * nb,
        out_shape=[jax.ShapeDtypeStruct(b.shape, b.dtype) for b in bufs],
        scratch_shapes=_scatter_sems(nb),
    )(*bufs)
    return _scatter_own(outs, bufs)


def _scatter_sems(nb):
    return [pltpu.SemaphoreType.DMA((3 * nb,)), pltpu.SemaphoreType.DMA((3 * nb,))]


def _scatter_own(outs, bufs):
    me = 2 * lax.axis_index("x") + lax.axis_index("y")
    return [lax.dynamic_update_slice(o, lax.dynamic_slice_in_dim(b, me, 1, axis=0), (me, 0, 0)) for o, b in zip(outs, bufs)]


def _scatter_copy(ins, outs, send_sems, recv_sems, k, b, slot_from, slot_to, to):
    nb = len(ins)
    return pltpu.make_async_remote_copy(
        src_ref=ins[b].at[slot_from], dst_ref=outs[b].at[slot_to], send_sem=send_sems.at[k * nb + b],
        recv_sem=recv_sems.at[k * nb + b], device_id=to, device_id_type=MESH)


def _scatter_start(ins, outs, send_sems, recv_sems):
    x, y, c, me, peers = _mesh_place()
    for k, (px, py) in enumerate(peers):
        for b in range(len(ins)):
            _scatter_copy(ins, outs, send_sems, recv_sems, k, b, 2 * px + py, me, (px, py, c)).start()


def _scatter_finish(ins, outs, send_sems, recv_sems):
    x, y, c, me, peers = _mesh_place()
    for k, (px, py) in enumerate(peers):
        for b in range(len(ins)):
            _scatter_copy(ins, outs, send_sems, recv_sems, k, b, me, 2 * px + py, (px, py, c)).wait_recv()
    for k, (px, py) in enumerate(peers):
        for b in range(len(ins)):
            _scatter_copy(ins, outs, send_sems, recv_sems, k, b, 2 * px + py, me, (px, py, c)).wait_send()


def _split_cores(bufs, name):
    nb = len(bufs)

    def body(*refs):
        ins = refs[:nb]
        theirs = refs[nb : 2 * nb]
        send_sems, recv_sems = refs[2 * nb :]
        x, y, c, _, _ = _mesh_place()
        cps = []
        for b in range(nb):
            hr = ins[b].shape[1] // 2
            cp = pltpu.make_async_remote_copy(
                src_ref=ins[b].at[:, pl.ds((1 - c) * hr, hr)], dst_ref=theirs[b], send_sem=send_sems.at[b],
                recv_sem=recv_sems.at[b], device_id=(x, y, 1 - c), device_id_type=MESH)
            cp.start()
            cps.append(cp)
        for cp in cps:
            cp.wait()

    theirs = pl.pallas_call(
        body,
        name=name,
        in_specs=[ANY] * nb,
        out_specs=[ANY] * nb,
        out_shape=[jax.ShapeDtypeStruct((b.shape[0], b.shape[1] // 2, b.shape[2]), b.dtype) for b in bufs],
        scratch_shapes=[pltpu.SemaphoreType.DMA((nb,)), pltpu.SemaphoreType.DMA((nb,))],
    )(*bufs)
    c = lax.axis_index("c")
    mine = [lax.dynamic_slice_in_dim(b, c * (b.shape[1] // 2), b.shape[1] // 2, axis=1) for b in bufs]
    return mine, theirs


def _join_cores(bufs, name):
    nb = len(bufs)

    def body(*refs):
        ins = refs[:nb]
        outs = refs[nb : 2 * nb]
        send_sems, recv_sems = refs[2 * nb :]
        x, y, c, _, _ = _mesh_place()
        cps = []
        for b in range(nb):
            hr = ins[b].shape[0]
            cp = pltpu.make_async_remote_copy(
                src_ref=ins[b], dst_ref=outs[b].at[pl.ds(c * hr, hr)], send_sem=send_sems.at[b], recv_sem=recv_sems.at[b],
                device_id=(x, y, 1 - c), device_id_type=MESH)
            cp.start()
            cps.append(cp)
        for b, cp in enumerate(cps):
            hr = ins[b].shape[0]
            cp.wait_send()
            pltpu.make_async_remote_copy(
                src_ref=ins[b], dst_ref=outs[b].at[pl.ds((1 - c) * hr, hr)], send_sem=send_sems.at[b],
                recv_sem=recv_sems.at[b], device_id=(x, y, 1 - c), device_id_type=MESH).wait_recv()

    outs = pl.pallas_call(
        body,
        name=name,
        in_specs=[ANY] * nb,
        out_specs=[ANY] * nb,
        out_shape=[jax.ShapeDtypeStruct((2 * b.shape[0], b.shape[1]), b.dtype) for b in bufs],
        scratch_shapes=[pltpu.SemaphoreType.DMA((nb,)), pltpu.SemaphoreType.DMA((nb,))],
    )(*bufs)
    c = lax.axis_index("c")
    return [lax.dynamic_update_slice(o, b, (c * b.shape[0], 0)) for o, b in zip(outs, bufs)]


ROW_ALIGN = 1024


def _pack(pieces, dtype):
    flat = []
    for p in pieces:
        f = p.reshape(-1).astype(dtype)
        pad = (-f.shape[0]) % LANES
        if pad:
            f = jnp.pad(f, (0, pad))
        flat.append(f)
    tot = sum(f.shape[0] for f in flat)
    pad = (-tot) % (ROW_ALIGN * LANES)
    if pad:
        flat.append(jnp.zeros((pad,), dtype))
    return jnp.concatenate(flat).reshape(-1, LANES)


def _unpack(buf, shapes):
    lead = buf.shape[:-2]
    flat = buf.reshape(lead + (-1,))
    out = []
    off = 0
    for shp in shapes:
        n = 1
        for d in shp:
            n *= d
        out.append(flat[..., off : off + n].reshape(lead + tuple(shp)))
        off += n + ((-n) % LANES)
    return out


PARAMS = (
    ("meta_tokens", 1, "small"), ("ssd_norm", 1, "small"), ("ssd_w_in", 2, "big"), ("ssd_conv_w", 2, "small"),
    ("ssd_conv_b", 1, "small"), ("ssd_dt_bias", None, "rep"), ("ssd_a_log", None, "rep"), ("ssd_d_skip", None, "rep"),
    ("ssd_gate_norm", 1, "small"), ("ssd_w_out", 1, "big"), ("kv_norm", None, "rep"), ("w_kv", 1, "big"),
    ("sb_norm", None, "rep"), ("sb_w_q", 1, "big"), ("sb_w_o", 1, "big"), ("ffn_norm", None, "rep"),
    ("ffn_w_up", 2, "big"), ("ffn_conv_w", 2, "small"), ("ffn_conv_b", None, "rep"), ("ffn_w_down", 1, "big"),
    ("final_norm", None, "rep"),
)


def _head_cols(vec):
    return jnp.pad(vec.reshape(G, 1, E), ((0, 0), (0, 0), (0, LANES - E)))


def _head_rows(vec):
    return jnp.pad(vec.reshape(G, E, 1), ((0, 0), (0, 8 - E), (0, 0)))


def kernel(x, meta_tokens, ssd_norm, ssd_w_in, ssd_conv_w, ssd_conv_b, ssd_dt_bias, ssd_a_log, ssd_d_skip, ssd_gate_norm, ssd_w_out, kv_norm, w_kv, sb_norm, sb_w_q, sb_w_o, ffn_norm, ffn_w_up, ffn_conv_w, ffn_conv_b, ffn_w_down, final_norm, loss_target, m_meta_tokens, m_ssd_norm, m_ssd_w_in, m_ssd_conv_w, m_ssd_conv_b, m_ssd_dt_bias, m_ssd_a_log, m_ssd_d_skip, m_ssd_gate_norm, m_ssd_w_out, m_kv_norm, m_w_kv, m_sb_norm, m_sb_w_q, m_sb_w_o, m_ffn_norm, m_ffn_w_up, m_ffn_conv_w, m_ffn_conv_b, m_ffn_w_down, m_final_norm, v_meta_tokens, v_ssd_norm, v_ssd_w_in, v_ssd_conv_w, v_ssd_conv_b, v_ssd_dt_bias, v_ssd_a_log, v_ssd_d_skip, v_ssd_gate_norm, v_ssd_w_out, v_kv_norm, v_w_kv, v_sb_norm, v_sb_w_q, v_sb_w_o, v_ffn_norm, v_ffn_w_up, v_ffn_conv_w, v_ffn_conv_b, v_ffn_w_down, v_final_norm):
    local = dict(meta_tokens=meta_tokens, ssd_norm=ssd_norm, ssd_w_in=ssd_w_in, ssd_conv_w=ssd_conv_w, ssd_conv_b=ssd_conv_b, ssd_dt_bias=ssd_dt_bias, ssd_a_log=ssd_a_log, ssd_d_skip=ssd_d_skip, ssd_gate_norm=ssd_gate_norm, ssd_w_out=ssd_w_out, kv_norm=kv_norm, w_kv=w_kv, sb_norm=sb_norm, sb_w_q=sb_w_q, sb_w_o=sb_w_o, ffn_norm=ffn_norm, ffn_w_up=ffn_w_up, ffn_conv_w=ffn_conv_w, ffn_conv_b=ffn_conv_b, ffn_w_down=ffn_w_down, final_norm=final_norm)
    mom_m = dict(meta_tokens=m_meta_tokens, ssd_norm=m_ssd_norm, ssd_w_in=m_ssd_w_in, ssd_conv_w=m_ssd_conv_w, ssd_conv_b=m_ssd_conv_b, ssd_dt_bias=m_ssd_dt_bias, ssd_a_log=m_ssd_a_log, ssd_d_skip=m_ssd_d_skip, ssd_gate_norm=m_ssd_gate_norm, ssd_w_out=m_ssd_w_out, kv_norm=m_kv_norm, w_kv=m_w_kv, sb_norm=m_sb_norm, sb_w_q=m_sb_w_q, sb_w_o=m_sb_w_o, ffn_norm=m_ffn_norm, ffn_w_up=m_ffn_w_up, ffn_conv_w=m_ffn_conv_w, ffn_conv_b=m_ffn_conv_b, ffn_w_down=m_ffn_w_down, final_norm=m_final_norm)
    mom_v = dict(meta_tokens=v_meta_tokens, ssd_norm=v_ssd_norm, ssd_w_in=v_ssd_w_in, ssd_conv_w=v_ssd_conv_w, ssd_conv_b=v_ssd_conv_b, ssd_dt_bias=v_ssd_dt_bias, ssd_a_log=v_ssd_a_log, ssd_d_skip=v_ssd_d_skip, ssd_gate_norm=v_ssd_gate_norm, ssd_w_out=v_ssd_w_out, kv_norm=v_kv_norm, w_kv=v_w_kv, sb_norm=v_sb_norm, sb_w_q=v_sb_w_q, sb_w_o=v_sb_w_o, ffn_norm=v_ffn_norm, ffn_w_up=v_ffn_w_up, ffn_conv_w=v_ffn_conv_w, ffn_conv_b=v_ffn_conv_b, ffn_w_down=v_ffn_w_down, final_norm=v_final_norm)

    big_names = [n for n, _, kind in PARAMS if kind == "big"]
    small_names = [n for n, _, kind in PARAMS if kind == "small"]
    rep_names = [n for n, _, kind in PARAMS if kind == "rep"]
    axis_of = {n: ax for n, ax, _ in PARAMS}

    def rows2(a):
        return a.reshape(-1, a.shape[-1])

    first_big, later_big = big_names[:1], big_names[1:]
    full = {}

    def assemble(names, bufs):
        for n, buf in zip(names, bufs):
            p = buf.reshape((NCHIP,) + local[n].shape)
            full[n] = jnp.concatenate([p[s] for s in range(NCHIP)], axis=axis_of[n])

    small_own = _pack([local[n] for n in small_names], F32)
    gathered = _gather_chips([rows2(local[n]).astype(BF16) for n in first_big] + [small_own], "gather_first")
    assemble(first_big, gathered[:-1])
    for n, p in zip(small_names, _unpack(gathered[-1], [local[n].shape for n in small_names])):
        full[n] = jnp.concatenate([p[s] for s in range(NCHIP)], axis=axis_of[n])
    for n in rep_names:
        full[n] = local[n]

    w_in = full["ssd_w_in"][0]
    w_z, w_xbc = w_in[:, :DI], w_in[:, DI : DI + CD]
    w_dt = jnp.pad(w_in[:, DI + CD :], ((0, 0), (0, LANES - H)))
    fcw, fcb = full["ffn_conv_w"], full["ffn_conv_b"]
    scw, scb = full["ssd_conv_w"][0], full["ssd_conv_b"]
    bias_c, bias_r = _head_cols(full["ssd_dt_bias"][0]), _head_rows(full["ssd_dt_bias"][0])
    alog_c, alog_r = _head_cols(full["ssd_a_log"][0]), _head_rows(full["ssd_a_log"][0])
    dskip_c = _head_cols(full["ssd_d_skip"][0])
    kvn = full["kv_norm"].reshape(1, D)
    fin = full["final_norm"].reshape(1, D)

    h0 = jnp.concatenate([jnp.zeros((PF, D), F32), full["meta_tokens"], x[0]], axis=0)
    (u0,) = _rms_fwd(h0, [full["ssd_norm"]], "ssd_norm_fwd")
    z = _mm(u0, w_z, name="ssd_in_z")
    xr = _mm(u0, w_xbc, name="ssd_in_xbc")
    dt_raw = _mm(u0, w_dt, name="ssd_in_dt")
    xbc = _ssd_conv_fwd(xr, scw, scb, "ssd_conv_fwd")
    dth = dt_raw[:, :H].reshape(LP, G, E)
    dtc = jnp.pad(jnp.transpose(dth, (1, 0, 2)), ((0, 0), (0, 0), (0, LANES - E)))
    dtr = jnp.pad(jnp.transpose(dth, (1, 2, 0)), ((0, 0), (0, 8 - E), (0, 0)))
    later_own = [rows2(local[n]).astype(BF16) for n in later_big]
    y, states, later_all = _ssd_fwd(xbc, dtc, dtr, bias_c, bias_r, alog_c, alog_r, dskip_c, later_own, "ssd_scan_fwd")
    assemble(later_big, later_all)
    w_out = full["ssd_w_out"][0]
    wkv = full["w_kv"]
    w_q = full["sb_w_q"][0]
    w_o = full["sb_w_o"][0]
    w_up_g = [full["ffn_w_up"][l][:, :DFF] for l in range(2)]
    w_up_v = [full["ffn_w_up"][l][:, DFF:] for l in range(2)]
    w_down = [full["ffn_w_down"][l] for l in range(2)]
    hgn = _gate_fwd(y, z, full["ssd_gate_norm"], "ssd_gate_fwd")
    h1 = _mm(hgn, w_out, add=h0, mask_rows=True, name="ssd_out")

    def ffn_fwd(h, l, tag):
        (u,) = _rms_fwd(h, [full["ffn_norm"][l : l + 1]], f"ffn{tag}_norm_fwd")
        hg = _mm(u, w_up_g[l], name=f"ffn{tag}_up_g")
        hv = _mm(u, w_up_v[l], name=f"ffn{tag}_up_v")
        act = _ffn_act_fwd(hg, hv, fcw[l][:, :DFF], fcw[l][:, DFF:], fcb[l : l + 1, :DFF], fcb[l : l + 1, DFF:], f"ffn{tag}_act_fwd")
        hn = _mm(act, w_down[l], add=h, mask_rows=True, name=f"ffn{tag}_down")
        return hn, (u, hg, hv, act)

    h2, ffn0 = ffn_fwd(h1, 0, "0")
    ukv, uq = _rms_fwd(h2, [kvn, full["sb_norm"]], "attn_norm_fwd")
    kk = _mm(ukv, wkv[:, :D], out_dtype=BF16, name="attn_k")
    vv = _mm(ukv, wkv[:, D:], out_dtype=BF16, name="attn_v")
    qq = _mm(uq, w_q, out_dtype=BF16, scale=64.0**-0.5, name="attn_q")
    o = _attn_fwd(qq, kk, vv, "attn_fwd")
    h3 = _mm(o, w_o, add=h2, mask_rows=True, name="attn_out")
    h4, ffn1 = ffn_fwd(h3, 1, "1")
    dh, g_final, loss_rows = _loss_head(h4, fin, loss_target[0], "loss_head")
    loss = lax.psum(0.5 / D * jnp.sum(loss_rows), ("x", "y", "c"))

    grads = {"final_norm": g_final.reshape(D)}

    def ffn_bwd(dh, h, l, saved, tag):
        u, hg, hv, act = saved
        da = _mm(dh, w_down[l], tb=True, name=f"ffn{tag}_down_dx")
        gw_down = _mm(act, dh, ta=True, out_dtype=BF16, name=f"ffn{tag}_down_dw")
        dhg, dhv, dwg, dwv, dbg, dbv = _ffn_act_bwd(hg, hv, da, fcw[l][:, :DFF], fcw[l][:, DFF:], fcb[l : l + 1, :DFF], fcb[l : l + 1, DFF:], f"ffn{tag}_act_bwd")
        gw_up = jnp.concatenate([_mm(u, dhg, ta=True, out_dtype=BF16, name=f"ffn{tag}_up_g_dw"), _mm(u, dhv, ta=True, out_dtype=BF16, name=f"ffn{tag}_up_v_dw")], axis=1)
        du = _mm(dhg, w_up_g[l], tb=True, name=f"ffn{tag}_up_g_dx")
        du = _mm(dhv, w_up_v[l], tb=True, add=du, name=f"ffn{tag}_up_v_dx")
        dh_new, (gn,) = _rms_bwd(dh, h, [du], [full["ffn_norm"][l : l + 1]], f"ffn{tag}_norm_bwd")
        return dh_new, gw_down, gw_up, jnp.concatenate([dwg, dwv], axis=1), jnp.concatenate([dbg, dbv], axis=1), gn

    dh, gd1, gu1, gcw1, gcb1, gn1 = ffn_bwd(dh, h3, 1, ffn1, "1")
    do = _mm(dh, w_o, tb=True, name="attn_out_dx")
    grads["sb_w_o"] = _mm(o, dh, ta=True, out_dtype=BF16, name="attn_out_dw")[None]
    dq, dk, dv = _attn_bwd(qq, kk, vv, o, do, "attn_bwd")
    grads["sb_w_q"] = _mm(uq, dq, ta=True, out_dtype=BF16, scale=64.0**-0.5, name="attn_q_dw")[None]
    grads["w_kv"] = jnp.concatenate([_mm(ukv, dk, ta=True, out_dtype=BF16, name="attn_k_dw"), _mm(ukv, dv, ta=True, out_dtype=BF16, name="attn_v_dw")], axis=1)
    duq = _mm(dq, w_q, tb=True, scale=64.0**-0.5, name="attn_q_dx")
    dukv = _mm(dk, wkv[:, :D], tb=True, name="attn_k_dx")
    dukv = _mm(dv, wkv[:, D:], tb=True, add=dukv, name="attn_v_dx")
    dh, (g_kvn, g_sbn) = _rms_bwd(dh, h2, [dukv, duq], [kvn, full["sb_norm"]], "attn_norm_bwd")
    grads["kv_norm"] = g_kvn.reshape(D)
    grads["sb_norm"] = g_sbn
    dh, gd0, gu0, gcw0, gcb0, gn0 = ffn_bwd(dh, h1, 0, ffn0, "0")
    grads["ffn_w_down"] = jnp.stack([gd0, gd1])
    grads["ffn_w_up"] = jnp.stack([gu0, gu1])
    grads["ffn_conv_w"] = jnp.stack([gcw0, gcw1])
    grads["ffn_conv_b"] = jnp.concatenate([gcb0, gcb1], axis=0)
    grads["ffn_norm"] = jnp.concatenate([gn0, gn1], axis=0)
    dhgn = _mm(dh, w_out, tb=True, name="ssd_out_dx")
    grads["ssd_w_out"] = _mm(hgn, dh, ta=True, out_dtype=BF16, name="ssd_out_dw")[None]
    dy, dz, g_gate = _gate_bwd(dhgn, y, z, full["ssd_gate_norm"], "ssd_gate_bwd")
    grads["ssd_gate_norm"] = g_gate
    def slots(n):
        return jnp.stack([rows2(p) for p in jnp.split(grads[n], NCHIP, axis=axis_of[n])])

    mine, theirs = _split_cores([slots(n) for n in later_big], "split_cores_a")
    pair = [_add2(a, b, f"pair_sum_a{i}") for i, (a, b) in enumerate(zip(mine, theirs))]
    dxs, dB, dC, ddt_raw, g_bias, g_alog, g_dskip, got_a = _ssd_bwd(
        xbc, dy, states, dtc, dtr, bias_c, bias_r, alog_c, alog_r, dskip_c, pair, "ssd_scan_bwd")
    grads["ssd_dt_bias"] = g_bias[:, 0, :E].reshape(1, H)
    grads["ssd_a_log"] = g_alog[:, 0, :E].reshape(1, H)
    grads["ssd_d_skip"] = g_dskip[:, 0, :E].reshape(1, H)
    dxr, g_scw, g_scb = _ssd_conv_bwd(xr, jnp.concatenate([dxs, dB, dC], axis=1), scw, scb, "ssd_conv_bwd")
    grads["ssd_conv_w"] = g_scw[None]
    grads["ssd_conv_b"] = g_scb
    ddt = jnp.pad(jnp.transpose(ddt_raw[:, :, :E], (1, 0, 2)).reshape(LP, H), ((0, 0), (0, LANES - H)))
    grads["ssd_w_in"] = jnp.concatenate(
        [_mm(u0, dz, ta=True, out_dtype=BF16, name="ssd_in_z_dw"), _mm(u0, dxr, ta=True, out_dtype=BF16, name="ssd_in_xbc_dw"), _mm(u0, ddt, ta=True, out_dtype=BF16, name="ssd_in_dt_dw")[:, :H]], axis=1)[None]
    du = _mm(dz, w_z, tb=True, name="ssd_in_z_dx")
    du = _mm(dxr, w_xbc, tb=True, add=du, name="ssd_in_xbc_dx")
    du = _mm(ddt, w_dt, tb=True, add=du, name="ssd_in_dt_dx")
    dh, (g_ssdn,) = _rms_bwd(dh, h0, [du], [full["ssd_norm"]], "ssd_norm_bwd")
    grads["ssd_norm"] = g_ssdn
    grads["meta_tokens"] = dh[PF : PF + N_META]
    grad_x = dh[PF + N_META :][None]

    def shard_pieces(names, s):
        out = []
        for n in names:
            ax = axis_of[n]
            out.append(grads[n] if ax is None else jnp.split(grads[n], NCHIP, axis=ax)[s])
        return out

    bufs = [slots(n) for n in first_big]
    bufs.append(jnp.stack([_pack(shard_pieces(small_names + rep_names, s), F32) for s in range(NCHIP)]))
    mine, theirs = _split_cores(bufs, "split_cores_b")
    pair = [_add2(a, b, f"pair_sum_b{i}") for i, (a, b) in enumerate(zip(mine, theirs))]
    got_b = _scatter_chips(pair, "scatter_grads")
    got = got_b[:-1] + got_a + got_b[-1:]
    sums = [_sum4(b, f"sum_chips_{i}") for i, b in enumerate(got)]
    gsum = _join_cores(sums, "join_cores")

    def rows(a):
        f = a.reshape(-1)
        pad = (-f.shape[0]) % LANES
        if pad:
            f = jnp.pad(f, (0, pad))
        return f.reshape(-1, LANES)

    order = [n for n, _, _ in PARAMS]
    res = {}
    for n, g2 in zip(big_names, gsum[:-1]):
        outs = _adamw(rows2(local[n]), g2, rows2(mom_m[n]), rows2(mom_v[n]), f"adamw_{n}")
        res[n] = [o_.reshape(local[n].shape) for o_ in outs]
    rest = small_names + rep_names
    for n, g1 in zip(rest, _unpack(gsum[-1], [local[n].shape for n in rest])):
        shp = local[n].shape
        cnt = 1
        for d in shp:
            cnt *= d
        outs = _adamw(rows(local[n]), rows(g1), rows(mom_m[n]), rows(mom_v[n]), f"adamw_{n}")
        res[n] = [o_.reshape(-1)[:cnt].reshape(shp) for o_ in outs]
    return (loss, grad_x, *[res[n][0] for n in order], *[res[n][1] for n in order], *[res[n][2] for n in order], *[res[n][3] for n in order])
```

```python
import functools

import jax
import jax.numpy as jnp
from jax import lax
from jax.experimental import pallas as pl
from jax.experimental.pallas import tpu as pltpu

D = 1024
SEQ = 8192
N_META = 16
EPS = 1e-6
P = 64
G = 4
N = 128
CONVW = 4
Q = 256
FC = 3
DFF = 256 * ((8 * D // 3 + 255) // 256)
DI = 2 * D
H = DI // P
E = H // G
GW = E * P
CD = DI + 2 * G * N
IN = DI + CD + H
SBH = D // 64
HP = 128
LANES = 128
PF = Q - N_META
LP = PF + N_META + SEQ
NC = LP // Q
TQ = 256
NCHIP = 4
ADAM_LR, ADAM_B1, ADAM_B2, ADAM_EPS, ADAM_WD, ADAM_STEP = 0.001, 0.9, 0.999, 1e-08, 0.01, 10

F32 = jnp.float32
BF16 = jnp.bfloat16
HI = lax.Precision.HIGHEST
MESH = pl.DeviceIdType.MESH
VMEM_LIMIT = 48 * 1024 * 1024
MM_MAX_K = 3072
T_SKIP = 110.0


def _pick(n, cands):
    for c in cands:
        if n % c == 0:
            return c
    raise ValueError((n, cands))


def _cparams(sem):
    return pltpu.CompilerParams(dimension_semantics=sem, vmem_limit_bytes=VMEM_LIMIT)


def _valid_rows(block, rows):
    r = block * rows + lax.broadcasted_iota(jnp.int32, (rows, 1), 0)
    return r >= PF


def _sigmoid(x):
    return 1.0 / (1.0 + jnp.exp(-x))


def _softplus(x):
    return jnp.maximum(x, 0.0) + jnp.log(1.0 + jnp.exp(-jnp.abs(x)))


def _sum_all(x):
    return jnp.sum(jnp.sum(x, axis=1, keepdims=True), axis=0, keepdims=True)


def _dsilu(x):
    s = _sigmoid(x)
    return s * (1.0 + x * (1.0 - s))


def _mm(a, b, *, ta=False, tb=False, out_dtype=F32, add=None, mask_rows=False, scale=None, name):
    if ta:
        K, M = a.shape
    else:
        M, K = a.shape
    if tb:
        Nn, K2 = b.shape
    else:
        K2, Nn = b.shape
    assert K == K2, (a.shape, b.shape, ta, tb)
    tn = _pick(Nn, (1408, 1024, 768, 512, 256, 128))
    if ta:
        tm = _pick(M, (1408, 1024, 768, 512, 256, 128))
        tk = _pick(K, (768, 512, 256))
    else:
        tm = _pick(M, (768, 256))
        tk = K if K <= MM_MAX_K else _pick(K, (1024, 768, 512, 256, 128))
    nk = K // tk
    dims = (((0 if ta else 1,), (1 if tb else 0,)), ((), ()))

    def body(*refs):
        a_ref, b_ref = refs[0], refs[1]
        add_ref = refs[2] if add is not None else None
        o_ref = refs[3] if add is not None else refs[2]
        acc = refs[-1] if nk > 1 else None

        def finish(r):
            if scale is not None:
                r = r * scale
            if mask_rows:
                r = jnp.where(_valid_rows(pl.program_id(0), tm), r, 0.0)
            if add_ref is not None:
                r = r + add_ref[...]
            o_ref[...] = r.astype(out_dtype)

        part = lax.dot_general(a_ref[...].astype(BF16), b_ref[...].astype(BF16), dims, preferred_element_type=F32)
        if nk == 1:
            finish(part)
        else:
            k = pl.program_id(2)

            @pl.when(k == 0)
            def _():
                acc[...] = part

            @pl.when(k > 0)
            def _():
                acc[...] += part

            @pl.when(k == nk - 1)
            def _():
                finish(acc[...])

    a_spec = pl.BlockSpec((tk, tm), lambda i, j, k: (k, i)) if ta else pl.BlockSpec((tm, tk), lambda i, j, k: (i, k))
    b_spec = pl.BlockSpec((tn, tk), lambda i, j, k: (j, k)) if tb else pl.BlockSpec((tk, tn), lambda i, j, k: (k, j))
    o_spec = pl.BlockSpec((tm, tn), lambda i, j, k: (i, j))
    in_specs = [a_spec, b_spec] + ([o_spec] if add is not None else [])
    args = (a, b) + ((add,) if add is not None else ())
    return pl.pallas_call(
        body,
        name=name,
        grid=(M // tm, Nn // tn, nk),
        in_specs=in_specs,
        out_specs=o_spec,
        out_shape=jax.ShapeDtypeStruct((M, Nn), out_dtype),
        scratch_shapes=[pltpu.VMEM((tm, tn), F32)] if nk > 1 else [],
        compiler_params=_cparams(("parallel", "parallel", "arbitrary")),
    )(*args)


def _rms_fwd(h, gains, name):
    tr = _pick(LP, (768, 256))
    ng = len(gains)

    def body(*refs):
        h_ref = refs[0]
        g_refs = refs[1 : 1 + ng]
        o_refs = refs[1 + ng :]
        x = h_ref[...]
        xh = x * lax.rsqrt(jnp.mean(x * x, axis=-1, keepdims=True) + EPS)
        for g_ref, o_ref in zip(g_refs, o_refs):
            o_ref[...] = (xh * g_ref[...]).astype(BF16)

    row = pl.BlockSpec((tr, D), lambda i: (i, 0))
    vec = pl.BlockSpec((1, D), lambda i: (0, 0))
    outs = pl.pallas_call(
        body,
        name=name,
        grid=(LP // tr,),
        in_specs=[row] + [vec] * ng,
        out_specs=[row] * ng,
        out_shape=[jax.ShapeDtypeStruct((LP, D), BF16)] * ng,
        compiler_params=_cparams(("parallel",)),
    )(h, *gains)
    return outs


def _rms_bwd(dh_in, h, dus, gains, name):
    tr = _pick(LP, (256,))
    ng = len(gains)

    def body(*refs):
        dh_ref, h_ref = refs[0], refs[1]
        du_refs = refs[2 : 2 + ng]
        g_refs = refs[2 + ng : 2 + 2 * ng]
        o_ref = refs[2 + 2 * ng]
        dg_refs = refs[3 + 2 * ng :]
        i = pl.program_id(0)
        x = h_ref[...]
        r = lax.rsqrt(jnp.mean(x * x, axis=-1, keepdims=True) + EPS)
        xh = x * r
        tot = dh_ref[...]
        for du_ref, g_ref, dg_ref in zip(du_refs, g_refs, dg_refs):
            du = du_ref[...]
            dxh = du * g_ref[...]
            tot = tot + r * (dxh - xh * jnp.mean(dxh * xh, axis=-1, keepdims=True))

            @pl.when(i == 0)
            def _():
                dg_ref[...] = jnp.zeros_like(dg_ref)

            dg_ref[...] += jnp.sum(du * xh, axis=0, keepdims=True)
        o_ref[...] = jnp.where(_valid_rows(i, tr), tot, 0.0)

    row = pl.BlockSpec((tr, D), lambda i: (i, 0))
    vec = pl.BlockSpec((1, D), lambda i: (0, 0))
    outs = pl.pallas_call(
        body,
        name=name,
        grid=(LP // tr,),
        in_specs=[row, row] + [row] * ng + [vec] * ng,
        out_specs=[row] + [vec] * ng,
        out_shape=[jax.ShapeDtypeStruct((LP, D), F32)] + [jax.ShapeDtypeStruct((1, D), F32)] * ng,
        compiler_params=_cparams(("arbitrary",)),
    )(dh_in, h, *dus, *gains)
    return outs[0], outs[1:]


def _loss_head(h, gain, target, name):
    tr = Q

    def body(h_ref, g_ref, t_ref, dh_ref, dg_ref, ls_ref):
        i = pl.program_id(0)

        @pl.when(i == 0)
        def _():
            dg_ref[...] = jnp.zeros_like(dg_ref)
            ls_ref[...] = jnp.zeros_like(ls_ref)
            dh_ref[...] = jnp.zeros_like(dh_ref)

        @pl.when(i > 0)
        def _():
            x = h_ref[...]
            g = g_ref[...]
            r = lax.rsqrt(jnp.mean(x * x, axis=-1, keepdims=True) + EPS)
            xh = x * r
            e = xh * g - t_ref[...]
            ls_ref[...] += jnp.sum(e * e, axis=0, keepdims=True)
            dy = e * (1.0 / D)
            dg_ref[...] += jnp.sum(dy * xh, axis=0, keepdims=True)
            dxh = dy * g
            dh_ref[...] = r * (dxh - xh * jnp.mean(dxh * xh, axis=-1, keepdims=True))

    row = pl.BlockSpec((tr, D), lambda i: (i, 0))
    vec = pl.BlockSpec((1, D), lambda i: (0, 0))
    return pl.pallas_call(
        body,
        name=name,
        grid=(LP // tr,),
        in_specs=[row, vec, pl.BlockSpec((tr, D), lambda i: (jnp.maximum(i - 1, 0), 0))],
        out_specs=[row, vec, vec],
        out_shape=[jax.ShapeDtypeStruct((LP, D), F32), jax.ShapeDtypeStruct((1, D), F32), jax.ShapeDtypeStruct((1, D), F32)],
        compiler_params=_cparams(("arbitrary",)),
    )(h, gain, target)


HALO = 8
CONV_COLS = (1536, 1408, 768, 512, 256)


def _conv_rows(ext, w, b, width):
    n = ext.shape[0]
    acc = b + w[width - 1 : width, :] * ext[HALO:]
    for k in range(width - 1):
        acc = acc + w[k : k + 1, :] * pltpu.roll(ext, width - 1 - k, 0)[HALO:]
    return acc


def _conv_specs(tr, tn, col):
    per = tr // HALO
    last = LP // HALO - 1
    prev = pl.BlockSpec((HALO, tn), lambda j, i: (jnp.maximum(i * per - 1, 0), col(j)))
    cur = pl.BlockSpec((tr, tn), lambda j, i: (i, col(j)))
    nxt = pl.BlockSpec((HALO, tn), lambda j, i: (jnp.minimum((i + 1) * per, last), col(j)))
    return prev, cur, nxt


def _conv_bwd_core(ext, dact_fn, w, b, width, i, nblk, tr):
    pre = _conv_rows(ext, w, b, width)
    dpre = dact_fn(pre)
    rows = i * tr + lax.broadcasted_iota(jnp.int32, (tr + HALO, 1), 0)
    dpre = jnp.where((rows >= PF) & (rows < LP), dpre, 0.0)
    n = tr + HALO
    dx = w[width - 1 : width, :] * dpre[:tr]
    for k in range(width - 1):
        sh = width - 1 - k
        dx = dx + w[k : k + 1, :] * pltpu.roll(dpre, n - sh, 0)[:tr]
    dcur = dpre[:tr]
    dws = []
    for k in range(width):
        sh = width - 1 - k
        xs = ext[HALO : HALO + tr] if sh == 0 else pltpu.roll(ext, sh, 0)[HALO : HALO + tr]
        dws.append(jnp.sum(xs * dcur, axis=0, keepdims=True))
    db = jnp.sum(dcur, axis=0, keepdims=True)
    dx = jnp.where(_valid_rows(i, tr), dx, 0.0)
    return dx, dws, db


def _ssd_conv_fwd(xr, cw, cb, name):
    tr, tn = Q, _pick(CD, CONV_COLS)

    def body(p_ref, c_ref, w_ref, b_ref, o_ref):
        i = pl.program_id(1)
        ext = jnp.concatenate([jnp.where(i > 0, p_ref[...], 0.0), c_ref[...]], axis=0)
        pre = _conv_rows(ext, w_ref[...], b_ref[...], CONVW)
        o_ref[...] = jnp.where(_valid_rows(i, tr), pre * _sigmoid(pre), 0.0)

    prev, cur, _ = _conv_specs(tr, tn, lambda j: j)
    return pl.pallas_call(
        body,
        name=name,
        grid=(CD // tn, LP // tr),
        in_specs=[prev, cur, pl.BlockSpec((CONVW, tn), lambda j, i: (0, j)), pl.BlockSpec((1, tn), lambda j, i: (0, j))],
        out_specs=cur,
        out_shape=jax.ShapeDtypeStruct((LP, CD), F32),
        compiler_params=_cparams(("parallel", "arbitrary")),
    )(xr, xr, cw, cb)


def _ssd_conv_bwd(xr, dxbc, cw, cb, name):
    tr, tn = Q, _pick(CD, CONV_COLS)
    nblk = LP // tr

    def body(p_ref, c_ref, n_ref, dc_ref, dn_ref, w_ref, b_ref, dx_ref, dw_ref, db_ref):
        i = pl.program_id(1)
        ext = jnp.concatenate([jnp.where(i > 0, p_ref[...], 0.0), c_ref[...], n_ref[...]], axis=0)
        dout = jnp.concatenate([dc_ref[...], dn_ref[...]], axis=0)
        dx, dws, db = _conv_bwd_core(ext, lambda pre: dout * _dsilu(pre), w_ref[...], b_ref[...], CONVW, i, nblk, tr)
        dx_ref[...] = dx

        @pl.when(i == 0)
        def _():
            dw_ref[...] = jnp.zeros_like(dw_ref)
            db_ref[...] = jnp.zeros_like(db_ref)

        for k in range(CONVW):
            dw_ref[k : k + 1, :] += dws[k]
        db_ref[...] += db

    prev, cur, nxt = _conv_specs(tr, tn, lambda j: j)
    wspec = pl.BlockSpec((CONVW, tn), lambda j, i: (0, j))
    bspec = pl.BlockSpec((1, tn), lambda j, i: (0, j))
    return pl.pallas_call(
        body,
        name=name,
        grid=(CD // tn, LP // tr),
        in_specs=[prev, cur, nxt, cur, nxt, wspec, bspec],
        out_specs=[cur, wspec, bspec],
        out_shape=[jax.ShapeDtypeStruct((LP, CD), F32), jax.ShapeDtypeStruct((CONVW, CD), F32), jax.ShapeDtypeStruct((1, CD), F32)],
        compiler_params=_cparams(("parallel", "arbitrary")),
    )(xr, xr, xr, dxbc, dxbc, cw, cb)


def _ffn_act_fwd(hg, hv, cwg, cwv, cbg, cbv, name):
    tr, tn = Q, _pick(DFF, CONV_COLS)

    def body(pg, cg, pv, cv, wg, wv, bg, bv, o_ref):
        i = pl.program_id(1)
        eg = jnp.concatenate([jnp.where(i > 0, pg[...], 0.0), cg[...]], axis=0)
        ev = jnp.concatenate([jnp.where(i > 0, pv[...], 0.0), cv[...]], axis=0)
        gate = _conv_rows(eg, wg[...], bg[...], FC)
        val = _conv_rows(ev, wv[...], bv[...], FC)
        o_ref[...] = (gate * _sigmoid(gate) * val).astype(BF16)

    prev, cur, _ = _conv_specs(tr, tn, lambda j: j)
    wspec = pl.BlockSpec((FC, tn), lambda j, i: (0, j))
    bspec = pl.BlockSpec((1, tn), lambda j, i: (0, j))
    return pl.pallas_call(
        body,
        name=name,
        grid=(DFF // tn, LP // tr),
        in_specs=[prev, cur, prev, cur, wspec, wspec, bspec, bspec],
        out_specs=cur,
        out_shape=jax.ShapeDtypeStruct((LP, DFF), BF16),
        compiler_params=_cparams(("parallel", "arbitrary")),
    )(hg, hg, hv, hv, cwg, cwv, cbg, cbv)


def _ffn_act_bwd(hg, hv, da, cwg, cwv, cbg, cbv, name):
    tr, tn = Q, _pick(DFF, CONV_COLS)
    nblk = LP // tr

    def body(pg, cg, ng, pv, cv, nv, dc, dn, wg, wv, bg, bv, dg_ref, dv_ref, dwg, dwv, dbg, dbv):
        i = pl.program_id(1)
        eg = jnp.concatenate([jnp.where(i > 0, pg[...], 0.0), cg[...], ng[...]], axis=0)
        ev = jnp.concatenate([jnp.where(i > 0, pv[...], 0.0), cv[...], nv[...]], axis=0)
        dout = jnp.concatenate([dc[...], dn[...]], axis=0)
        gate = _conv_rows(eg, wg[...], bg[...], FC)
        val = _conv_rows(ev, wv[...], bv[...], FC)
        dxg, dwsg, dbgv = _conv_bwd_core(eg, lambda pre: dout * val * _dsilu(pre), wg[...], bg[...], FC, i, nblk, tr)
        dxv, dwsv, dbvv = _conv_bwd_core(ev, lambda pre: dout * gate * _sigmoid(gate), wv[...], bv[...], FC, i, nblk, tr)
        dg_ref[...] = dxg
        dv_ref[...] = dxv

        @pl.when(i == 0)
        def _():
            dwg[...] = jnp.zeros_like(dwg)
            dwv[...] = jnp.zeros_like(dwv)
            dbg[...] = jnp.zeros_like(dbg)
            dbv[...] = jnp.zeros_like(dbv)

        for k in range(FC):
            dwg[k : k + 1, :] += dwsg[k]
            dwv[k : k + 1, :] += dwsv[k]
        dbg[...] += dbgv
        dbv[...] += dbvv

    prev, cur, nxt = _conv_specs(tr, tn, lambda j: j)
    wspec = pl.BlockSpec((FC, tn), lambda j, i: (0, j))
    bspec = pl.BlockSpec((1, tn), lambda j, i: (0, j))
    big = jax.ShapeDtypeStruct((LP, DFF), F32)
    wsh = jax.ShapeDtypeStruct((FC, DFF), F32)
    bsh = jax.ShapeDtypeStruct((1, DFF), F32)
    return pl.pallas_call(
        body,
        name=name,
        grid=(DFF // tn, LP // tr),
        in_specs=[prev, cur, nxt, prev, cur, nxt, cur, nxt, wspec, wspec, bspec, bspec],
        out_specs=[cur, cur, wspec, wspec, bspec, bspec],
        out_shape=[big, big, wsh, wsh, bsh, bsh],
        compiler_params=_cparams(("parallel", "arbitrary")),
    )(hg, hg, hg, hv, hv, hv, da, da, cwg, cwv, cbg, cbv)


def _gate_fwd(y, z, gg, name):
    tr = _pick(LP, (768, 256))

    def body(y_ref, z_ref, g_ref, o_ref):
        zv = z_ref[...]
        hg = y_ref[...] * zv * _sigmoid(zv)
        r = lax.rsqrt(jnp.mean(hg * hg, axis=-1, keepdims=True) + EPS)
        o_ref[...] = (hg * r * g_ref[...]).astype(BF16)

    blk = pl.BlockSpec((tr, GW), lambda i, g: (i, g))
    return pl.pallas_call(
        body,
        name=name,
        grid=(LP // tr, G),
        in_specs=[blk, blk, pl.BlockSpec((1, GW), lambda i, g: (0, g))],
        out_specs=blk,
        out_shape=jax.ShapeDtypeStruct((LP, DI), BF16),
        compiler_params=_cparams(("parallel", "parallel")),
    )(y, z, gg)


def _gate_bwd(dout, y, z, gg, name):
    tr = _pick(LP, (768, 256))

    def body(do_ref, y_ref, z_ref, g_ref, dy_ref, dz_ref, dg_ref):
        i = pl.program_id(1)
        zv = z_ref[...]
        yv = y_ref[...]
        sz = zv * _sigmoid(zv)
        hg = yv * sz
        r = lax.rsqrt(jnp.mean(hg * hg, axis=-1, keepdims=True) + EPS)
        hh = hg * r
        do = do_ref[...]
        dhh = do * g_ref[...]
        dhg = r * (dhh - hh * jnp.mean(dhh * hh, axis=-1, keepdims=True))
        dy_ref[...] = dhg * sz
        dz_ref[...] = dhg * yv * _dsilu(zv)

        @pl.when(i == 0)
        def _():
            dg_ref[...] = jnp.zeros_like(dg_ref)

        dg_ref[...] += jnp.sum(do * hh, axis=0, keepdims=True)

    blk = pl.BlockSpec((tr, GW), lambda g, i: (i, g))
    vec = pl.BlockSpec((1, GW), lambda g, i: (0, g))
    big = jax.ShapeDtypeStruct((LP, DI), F32)
    return pl.pallas_call(
        body,
        name=name,
        grid=(G, LP // tr),
        in_specs=[blk, blk, blk, vec],
        out_specs=[blk, blk, vec],
        out_shape=[big, big, jax.ShapeDtypeStruct((1, DI), F32)],
        compiler_params=_cparams(("parallel", "arbitrary")),
    )(dout, y, z, gg)


def _ssd_common(dtc_ref, dtr_ref, bc_ref, br_ref, ac_ref, ar_ref, c):
    rows = c * Q + lax.broadcasted_iota(jnp.int32, (Q, 1), 0)
    cols = c * Q + lax.broadcasted_iota(jnp.int32, (1, Q), 1)
    prec = dtc_ref[...] + bc_ref[...]
    prer = dtr_ref[...] + br_ref[...]
    dtc = jnp.where(rows >= PF, _softplus(prec), 0.0)
    dtr = jnp.where(cols >= PF, _softplus(prer), 0.0)
    a_c = -jnp.exp(ac_ref[...])
    a_r = -jnp.exp(ar_ref[...])
    li = lax.broadcasted_iota(jnp.int32, (Q, Q), 0)
    si = lax.broadcasted_iota(jnp.int32, (Q, Q), 1)
    tril = si <= li
    trif = tril.astype(F32)
    csc = jnp.dot(trif, dtc * a_c, precision=HI, preferred_element_type=F32)
    csr = lax.dot_general(dtr * a_r, trif, (((1,), (1,)), ((), ())), precision=HI, preferred_element_type=F32)
    return dict(rows=rows, prec=prec, dtc=dtc, a_c=a_c, tril=tril, trif=trif, csc=csc, csr=csr, li=li, si=si)


def _pair_expand(arr, h0, lane_lo):
    return jnp.where(lane_lo, arr[:, h0 : h0 + 1], arr[:, h0 + 1 : h0 + 2])


def _ssd_specs():
    nb = DI // N
    xs = pl.BlockSpec((Q, GW), lambda g, c: (c, g))
    bb = pl.BlockSpec((Q, N), lambda g, c: (c, nb + g))
    cc = pl.BlockSpec((Q, N), lambda g, c: (c, nb + G + g))
    dtc = pl.BlockSpec((None, Q, LANES), lambda g, c: (g, c, 0))
    dtr = pl.BlockSpec((None, 8, Q), lambda g, c: (g, 0, c))
    pc = pl.BlockSpec((None, 1, LANES), lambda g, c: (g, 0, 0))
    pr = pl.BlockSpec((None, 8, 1), lambda g, c: (g, 0, 0))
    return xs, bb, cc, dtc, dtr, pc, pr


def _ssd_fwd(xbc, dtc, dtr, bias_c, bias_r, alog_c, alog_r, dskip_c, gather, name):
    nb = len(gather)

    def body(*refs):
        xs_ref, b_ref, c_ref, dtc_ref, dtr_ref, bc_ref, br_ref, ac_ref, ar_ref, dk_ref = refs[:10]
        gin = refs[10 : 10 + nb]
        y_ref, st_ref = refs[10 + nb : 12 + nb]
        gout = refs[12 + nb : 12 + 2 * nb]
        state, send_sems, recv_sems = refs[12 + 2 * nb :]
        c = pl.program_id(1)
        first_step = (pl.program_id(0) == 0) & (c == 0)
        last_step = (pl.program_id(0) == G - 1) & (c == NC - 1)

        @pl.when(first_step)
        def _():
            _gather_start(gin, gout, send_sems, recv_sems)

        @pl.when(c == 0)
        def _():
            state[...] = jnp.zeros_like(state)

        st_ref[...] = state[...]
        cm = _ssd_common(dtc_ref, dtr_ref, bc_ref, br_ref, ac_ref, ar_ref, c)
        Bm = b_ref[...]
        Cm = c_ref[...]
        cb = lax.dot_general(Cm.astype(BF16), Bm.astype(BF16), (((1,), (1,)), ((), ())), preferred_element_type=F32)
        bt = Bm.T.astype(BF16)
        lane_lo = lax.broadcasted_iota(jnp.int32, (1, HP), 1) < P
        csc, csr, dtc_v = cm["csc"], cm["csr"], cm["dtc"]
        ecs = jnp.exp(csc)
        cs_end = csc[Q - 1 : Q, :]
        wdec = jnp.exp(cs_end - csc)
        eend = jnp.exp(cs_end)
        for pp in range(E // 2):
            h0 = 2 * pp
            sl = slice(pp * HP, (pp + 1) * HP)
            xp = xs_ref[:, sl]
            xdt = xp * _pair_expand(dtc_v, h0, lane_lo)
            yacc = xp * _pair_expand(dk_ref[...], h0, lane_lo)
            for e in range(2):
                h = h0 + e
                lm = jnp.where(cm["tril"], jnp.exp(jnp.minimum(csc[:, h : h + 1] - csr[h : h + 1, :], 0.0)), 0.0)
                m = (cb * lm).astype(BF16)
                xm = jnp.where(lane_lo if e == 0 else jnp.logical_not(lane_lo), xdt, 0.0).astype(BF16)
                yacc = yacc + jnp.dot(m, xm, preferred_element_type=F32)
            stp = state[:, sl]
            yoff = jnp.dot(Cm.astype(BF16), stp.astype(BF16), preferred_element_type=F32)
            y_ref[:, sl] = yacc + yoff * _pair_expand(ecs, h0, lane_lo)
            xw = (xdt * _pair_expand(wdec, h0, lane_lo)).astype(BF16)
            state[:, sl] = stp * _pair_expand(eend, h0, lane_lo) + jnp.dot(bt, xw, preferred_element_type=F32)

        @pl.when(last_step)
        def _():
            _gather_finish(gin, gout, send_sems, recv_sems)

    xs, bb, cc, dtcs, dtrs, pc, pr = _ssd_specs()
    outs = pl.pallas_call(
        body,
        name=name,
        grid=(G, NC),
        in_specs=[xs, bb, cc, dtcs, dtrs, pc, pr, pc, pr, pc] + [ANY] * nb,
        out_specs=[xs, pl.BlockSpec((None, None, N, GW), lambda g, c: (c, g, 0, 0))] + [ANY] * nb,
        out_shape=[jax.ShapeDtypeStruct((LP, DI), F32), jax.ShapeDtypeStruct((NC, G, N, GW), F32)] + _gather_shapes(gather),
        scratch_shapes=[pltpu.VMEM((N, GW), F32)] + _gather_sems(nb),
        compiler_params=_cparams(("arbitrary", "arbitrary")),
    )(xbc, xbc, xbc, dtc, dtr, bias_c, bias_r, alog_c, alog_r, dskip_c, *gather)
    return outs[0], outs[1], _gather_own(outs[2:], gather)


def _ssd_bwd(xbc, dy, states, dtc, dtr, bias_c, bias_r, alog_c, alog_r, dskip_c, scatter, name):
    nb = len(scatter)

    def body(*refs):
        xs_ref, b_ref, c_ref, dy_ref, st_ref, dtc_ref, dtr_ref, bc_ref, br_ref, ac_ref, ar_ref, dk_ref = refs[:12]
        sin = refs[12 : 12 + nb]
        dx_ref, db_ref, dc_ref, ddt_ref, dbias_ref, dalog_ref, ddk_ref = refs[12 + nb : 19 + nb]
        sout = refs[19 + nb : 19 + 2 * nb]
        dstate, send_sems, recv_sems = refs[19 + 2 * nb :]
        ci = pl.program_id(1)
        c = NC - 1 - ci

        @pl.when((pl.program_id(0) == 0) & (ci == 0))
        def _():
            _scatter_start(sin, sout, send_sems, recv_sems)

        @pl.when(ci == 0)
        def _():
            dstate[...] = jnp.zeros_like(dstate)
            dbias_ref[...] = jnp.zeros_like(dbias_ref)
            dalog_ref[...] = jnp.zeros_like(dalog_ref)
            ddk_ref[...] = jnp.zeros_like(ddk_ref)

        cm = _ssd_common(dtc_ref, dtr_ref, bc_ref, br_ref, ac_ref, ar_ref, c)
        Bm = b_ref[...]
        Cm = c_ref[...]
        Bb = Bm.astype(BF16)
        Cb = Cm.astype(BF16)
        nt = (((1,), (1,)), ((), ()))
        cb = lax.dot_general(Cb, Bb, nt, preferred_element_type=F32)
        ct = Cm.T.astype(BF16)
        lane_lo = lax.broadcasted_iota(jnp.int32, (1, HP), 1) < P
        lane_id = lax.broadcasted_iota(jnp.int32, (1, LANES), 1)
        csc, csr, dtc_v, a_c = cm["csc"], cm["csr"], cm["dtc"], cm["a_c"]
        triu = cm["si"] >= cm["li"]
        ecs = jnp.exp(csc)
        cs_end = csc[Q - 1 : Q, :]
        wdec = jnp.exp(cs_end - csc)
        eend = jnp.exp(cs_end)
        dcb = jnp.zeros((Q, Q), F32)
        dcs = jnp.zeros((Q, LANES), F32)
        dcs_end = jnp.zeros((1, LANES), F32)
        ddt = jnp.zeros((Q, LANES), F32)
        ddk = jnp.zeros((1, LANES), F32)
        dB = jnp.zeros((Q, N), F32)
        dC = jnp.zeros((Q, N), F32)
        for pp in range(E // 2):
            h0 = 2 * pp
            sl = slice(pp * HP, (pp + 1) * HP)
            xp = xs_ref[:, sl]
            dyp = dy_ref[:, sl]
            dtx = _pair_expand(dtc_v, h0, lane_lo)
            xdt = xp * dtx
            dxdt = jnp.zeros((Q, HP), F32)
            stp = st_ref[:, sl]
            stb = stp.astype(BF16)
            dsn = dstate[:, sl]
            dsnb = dsn.astype(BF16)
            ecsx = _pair_expand(ecs, h0, lane_lo)
            wdx = _pair_expand(wdec, h0, lane_lo)
            cs_ = jnp.dot(Cb, stb, preferred_element_type=F32)
            yo = cs_ * ecsx
            dyo = dyp * ecsx
            dyob = dyo.astype(BF16)
            dC = dC + lax.dot_general(dyob, stb, nt, preferred_element_type=F32)
            ds_from_y = jnp.dot(ct, dyob, preferred_element_type=F32)
            xw = xdt * wdx
            dB = dB + lax.dot_general(xw.astype(BF16), dsnb, nt, preferred_element_type=F32)
            dxw = jnp.dot(Bb, dsnb, preferred_element_type=F32)
            dxdt = dxdt + dxw * wdx
            w2 = dxw * xw
            rs = jnp.sum(dsn * stp, axis=0, keepdims=True) * _pair_expand(eend, h0, lane_lo)
            dstate[:, sl] = dsn * _pair_expand(eend, h0, lane_lo) + ds_from_y
            dyx = dyp * xp
            yd = jnp.zeros((Q, HP), F32)
            dxd = jnp.zeros((Q, HP), F32)
            for e in range(2):
                h = h0 + e
                msk = lane_lo if e == 0 else jnp.logical_not(lane_lo)
                col = csc[:, h : h + 1]
                row = csr[h : h + 1, :]
                lm = jnp.where(cm["tril"], jnp.exp(jnp.minimum(col - row, 0.0)), 0.0)
                dye = jnp.where(msk, dyp, 0.0).astype(BF16)
                xde = jnp.where(msk, xdt, 0.0).astype(BF16)
                gm = lax.dot_general(dye, xde, nt, preferred_element_type=F32)
                dcb = dcb + gm * lm
                mb = (cb * lm).astype(BF16)
                yd = yd + jnp.dot(mb, xde, preferred_element_type=F32)
                dxd = dxd + lax.dot_general(mb, dye, (((0,), (0,)), ((), ())), preferred_element_type=F32)
            dxdt = dxdt + dxd
            tt = dyp * yo - w2 + dyp.astype(BF16).astype(F32) * yd - xdt.astype(BF16).astype(F32) * dxd
            dx_ref[:, sl] = dxdt * dtx + dyp * _pair_expand(dk_ref[...], h0, lane_lo)
            dxx = dxdt * xp
            for e in range(2):
                h = h0 + e
                msk = lane_lo if e == 0 else jnp.logical_not(lane_lo)
                oh = (lane_id == h).astype(F32)
                dcs = dcs + jnp.sum(jnp.where(msk, tt, 0.0), axis=1, keepdims=True) * oh
                dcs_end = dcs_end + (_sum_all(jnp.where(msk, w2, 0.0)) + _sum_all(jnp.where(msk, rs, 0.0))) * oh
                ddk = ddk + _sum_all(jnp.where(msk, dyx, 0.0)) * oh
                ddt = ddt + jnp.sum(jnp.where(msk, dxx, 0.0), axis=1, keepdims=True) * oh
        dC = dC + jnp.dot(dcb.astype(BF16), Bb, preferred_element_type=F32)
        dB = dB + lax.dot_general(dcb.astype(BF16), Cb, (((0,), (0,)), ((), ())), preferred_element_type=F32)
        db_ref[...] = dB
        dc_ref[...] = dC
        last = (lax.broadcasted_iota(jnp.int32, (Q, 1), 0) == Q - 1).astype(F32)
        dcs = dcs + last * dcs_end
        dda = jnp.dot(triu.astype(F32), dcs, precision=HI, preferred_element_type=F32)
        ddt = ddt + dda * a_c
        da = jnp.sum(dda * dtc_v, axis=0, keepdims=True)
        draw = jnp.where(cm["rows"] >= PF, ddt * _sigmoid(cm["prec"]), 0.0)
        ddt_ref[...] = draw
        dbias_ref[...] += jnp.sum(draw, axis=0, keepdims=True)
        dalog_ref[...] += da * a_c
        ddk_ref[...] += ddk

        @pl.when((pl.program_id(0) == G - 1) & (ci == NC - 1))
        def _():
            _scatter_finish(sin, sout, send_sems, recv_sems)

    xs, bb, cc, dtcs, dtrs, pc, pr = _ssd_specs()

    def rev(spec_fn):
        return lambda g, ci: spec_fn(g, NC - 1 - ci)

    def rspec(spec):
        return pl.BlockSpec(spec.block_shape, rev(spec.index_map))

    xs_r, bb_r, cc_r, dtc_r, dtr_r = rspec(xs), rspec(bb), rspec(cc), rspec(dtcs), rspec(dtrs)
    st_r = pl.BlockSpec((None, None, N, GW), lambda g, ci: (NC - 1 - ci, g, 0, 0))
    gn = pl.BlockSpec((Q, N), lambda g, ci: (NC - 1 - ci, g))
    outs = pl.pallas_call(
        body,
        name=name,
        grid=(G, NC),
        in_specs=[xs_r, bb_r, cc_r, xs_r, st_r, dtc_r, dtr_r, pc, pr, pc, pr, pc] + [ANY] * nb,
        out_specs=[xs_r, gn, gn, dtc_r, pc, pc, pc] + [ANY] * nb,
        out_shape=[
            jax.ShapeDtypeStruct((LP, DI), F32),
            jax.ShapeDtypeStruct((LP, G * N), F32),
            jax.ShapeDtypeStruct((LP, G * N), F32),
            jax.ShapeDtypeStruct((G, LP, LANES), F32),
            jax.ShapeDtypeStruct((G, 1, LANES), F32),
            jax.ShapeDtypeStruct((G, 1, LANES), F32),
            jax.ShapeDtypeStruct((G, 1, LANES), F32),
        ]
        + [jax.ShapeDtypeStruct(b.shape, b.dtype) for b in scatter],
        scratch_shapes=[pltpu.VMEM((N, GW), F32)] + _scatter_sems(nb),
        compiler_params=_cparams(("arbitrary", "arbitrary")),
    )(xbc, xbc, xbc, dy, states, dtc, dtr, bias_c, bias_r, alog_c, alog_r, dskip_c, *scatter)
    return tuple(outs[:7]) + (_scatter_own(outs[7:], scatter),)


def _split_dot(x, u):
    hi = x.astype(BF16)
    lo = (x - hi.astype(F32)).astype(BF16)
    return jnp.dot(hi, u, preferred_element_type=F32) + jnp.dot(lo, u, preferred_element_type=F32)


def _sb_block(qe, kblk, vis, a_run, u_gt):
    l = lax.dot_general(qe, kblk, (((1,), (1,)), ((), ())), preferred_element_type=F32)
    lk = jnp.minimum(-l, 0.0) - jnp.log(1.0 + jnp.exp(-jnp.abs(l)))
    lbeta = l + lk
    if vis is not None:
        lk = jnp.where(vis, lk, 0.0)
    logw = lbeta + _split_dot(lk, u_gt) + a_run
    return lbeta, lk, logw


def _descend(i, block, carry):
    def pack(n, c):
        return (n, jnp.max(jnp.maximum(c[0], c[1]))) + tuple(c)

    st = pack(jnp.int32(1), block(i, carry, True))
    st = lax.while_loop(lambda st: (st[0] < i) & (st[1] > -T_SKIP), lambda st: pack(st[0] + 1, block(i - st[0], st[2:], False)), st)
    st = lax.while_loop(lambda st: (st[0] == i) & (st[1] > -T_SKIP), lambda st: pack(st[0] + 1, block(0, st[2:], True)), st)
    return st[2:]


def _attn_fwd(q, k, v, name):
    nq = LP // TQ

    def body(q_ref, k_ref, v_ref, o_ref):
        i = pl.program_id(1)
        qv = q_ref[...]
        lane_lo = lax.broadcasted_iota(jnp.int32, (1, HP), 1) < 64
        t_idx = i * TQ + lax.broadcasted_iota(jnp.int32, (TQ, 1), 0)
        ji = lax.broadcasted_iota(jnp.int32, (TQ, TQ), 0)
        si = lax.broadcasted_iota(jnp.int32, (TQ, TQ), 1)
        u_gt = (ji > si).astype(BF16)
        qs = [jnp.where(lane_lo, qv, jnp.zeros_like(qv)), jnp.where(lane_lo, jnp.zeros_like(qv), qv)]

        def block(kb, carry, masked):
            a0, a1, acc = carry
            off = pl.multiple_of(kb * TQ, TQ)
            kblk = k_ref[pl.ds(off, TQ), :]
            vblk = v_ref[pl.ds(off, TQ), :]
            vis = None
            if masked:
                s_idx = kb * TQ + lax.broadcasted_iota(jnp.int32, (1, TQ), 1)
                vis = (s_idx < t_idx) & (s_idx >= PF)
            new_a = []
            for e, a_run in enumerate((a0, a1)):
                _, lk, logw = _sb_block(qs[e], kblk, vis, a_run, u_gt)
                w = jnp.exp(logw)
                if masked:
                    w = jnp.where(vis, w, 0.0)
                msk = lane_lo if e == 0 else jnp.logical_not(lane_lo)
                acc = acc + jnp.dot(w.astype(BF16), jnp.where(msk, vblk, jnp.zeros_like(vblk)), preferred_element_type=F32)
                new_a.append(a_run + jnp.sum(lk, axis=1, keepdims=True))
            return new_a[0], new_a[1], acc

        z1 = jnp.zeros((TQ, 1), F32)
        _, _, acc = _descend(i, block, (z1, z1, jnp.zeros((TQ, HP), F32)))
        o_ref[...] = acc

    return pl.pallas_call(
        body,
        name=name,
        grid=(D // HP, nq),
        in_specs=[
            pl.BlockSpec((TQ, HP), lambda j, i: (i, j)),
            pl.BlockSpec((LP, HP), lambda j, i: (0, j)),
            pl.BlockSpec((LP, HP), lambda j, i: (0, j)),
        ],
        out_specs=pl.BlockSpec((TQ, HP), lambda j, i: (i, j)),
        out_shape=jax.ShapeDtypeStruct((LP, D), F32),
        compiler_params=_cparams(("parallel", "arbitrary")),
    )(q, k, v)


def _attn_bwd(q, k, v, o, do, name):
    nq = LP // TQ

    def body(q_ref, k_ref, v_ref, o_ref, do_ref, dq_ref, dk_ref, dv_ref):
        i = pl.program_id(1)

        @pl.when(i == 0)
        def _():
            dk_ref[...] = jnp.zeros_like(dk_ref)
            dv_ref[...] = jnp.zeros_like(dv_ref)

        qv = q_ref[...]
        dov = do_ref[...]
        lane_lo = lax.broadcasted_iota(jnp.int32, (1, HP), 1) < 64
        t_idx = i * TQ + lax.broadcasted_iota(jnp.int32, (TQ, 1), 0)
        ji = lax.broadcasted_iota(jnp.int32, (TQ, TQ), 0)
        si = lax.broadcasted_iota(jnp.int32, (TQ, TQ), 1)
        u_gt = (ji > si).astype(BF16)
        u_ge = (ji >= si).astype(BF16)
        msks = [lane_lo, jnp.logical_not(lane_lo)]
        qs = [jnp.where(m, qv, jnp.zeros_like(qv)) for m in msks]
        dob = [jnp.where(m, dov, 0.0).astype(BF16) for m in msks]
        ov = o_ref[...]
        deltas = [jnp.sum(d.astype(F32) * ov, axis=1, keepdims=True) for d in dob]
        nt = (((1,), (1,)), ((), ()))
        tn = (((0,), (0,)), ((), ()))

        def block(kb, carry, masked):
            a0, a1, p0, p1, dq = carry
            off = pl.multiple_of(kb * TQ, TQ)
            kblk = k_ref[pl.ds(off, TQ), :]
            vblk = v_ref[pl.ds(off, TQ), :]
            vis = None
            if masked:
                s_idx = kb * TQ + lax.broadcasted_iota(jnp.int32, (1, TQ), 1)
                vis = (s_idx < t_idx) & (s_idx >= PF)
            new_a, new_p = [], []
            dk_acc = jnp.zeros((TQ, HP), F32)
            dv_acc = jnp.zeros((TQ, HP), F32)
            for e, (a_run, p_run) in enumerate(((a0, p0), (a1, p1))):
                lbeta, lk, logw = _sb_block(qs[e], kblk, vis, a_run, u_gt)
                sig = jnp.exp(lbeta)
                w = jnp.exp(logw)
                if masked:
                    w = jnp.where(vis, w, 0.0)
                wb = w.astype(BF16)
                dw = lax.dot_general(dob[e], vblk, nt, preferred_element_type=F32)
                pm = wb.astype(F32) * dw
                cum_p = deltas[e] - (_split_dot(pm, u_ge) + p_run)
                dl = pm - (pm + cum_p) * sig
                if masked:
                    dl = jnp.where(vis, dl, 0.0)
                dl = dl.astype(BF16)
                km = jnp.where(msks[e], kblk, jnp.zeros_like(kblk))
                dq = dq + jnp.dot(dl, km, preferred_element_type=F32)
                dk_acc = dk_acc + lax.dot_general(dl, qs[e], tn, preferred_element_type=F32)
                dv_acc = dv_acc + lax.dot_general(wb, dob[e], tn, preferred_element_type=F32)
                new_a.append(a_run + jnp.sum(lk, axis=1, keepdims=True))
                new_p.append(p_run + jnp.sum(pm, axis=1, keepdims=True))
            dk_ref[pl.ds(off, TQ), :] += dk_acc
            dv_ref[pl.ds(off, TQ), :] += dv_acc
            return new_a[0], new_a[1], new_p[0], new_p[1], dq

        z1 = jnp.zeros((TQ, 1), F32)
        carry = _descend(i, block, (z1, z1, z1, z1, jnp.zeros((TQ, HP), F32)))
        dq_ref[...] = carry[4]

    blk = pl.BlockSpec((TQ, HP), lambda j, i: (i, j))
    full = pl.BlockSpec((LP, HP), lambda j, i: (0, j))
    big = jax.ShapeDtypeStruct((LP, D), F32)
    return pl.pallas_call(
        body,
        name=name,
        grid=(D // HP, nq),
        in_specs=[blk, full, full, blk, blk],
        out_specs=[blk, full, full],
        out_shape=[big, big, big],
        compiler_params=_cparams(("parallel", "arbitrary")),
    )(q, k, v, o, do)


ROW_TILES = (2048, 1024, 512, 256, 128, 64, 32, 16, 8)


def _row_tile(rows, cols, budget):
    if rows % 8:
        return rows
    return _pick(rows, tuple(t for t in ROW_TILES if t * cols <= budget) or (8,))


def _adamw(w, g, m, v, name):
    R, C = w.shape
    tr = _row_tile(R, C, 128 * 1024)

    def body(w_ref, g_ref, m_ref, v_ref, g_out, d_out, m_out, v_out):
        g = g_ref[...]
        mn = ADAM_B1 * m_ref[...] + (1.0 - ADAM_B1) * g
        vn = ADAM_B2 * v_ref[...] + (1.0 - ADAM_B2) * (g * g)
        mh = mn / (1.0 - ADAM_B1**ADAM_STEP)
        vh = vn / (1.0 - ADAM_B2**ADAM_STEP)
        g_out[...] = g
        d_out[...] = -ADAM_LR * (mh / (jnp.sqrt(vh) + ADAM_EPS) + ADAM_WD * w_ref[...])
        m_out[...] = mn
        v_out[...] = vn

    blk = pl.BlockSpec((tr, C), lambda i: (i, 0))
    sh = jax.ShapeDtypeStruct((R, C), F32)
    return pl.pallas_call(
        body,
        name=name,
        grid=(R // tr,),
        in_specs=[blk] * 4,
        out_specs=[blk] * 4,
        out_shape=[sh] * 4,
        compiler_params=_cparams(("parallel",)),
    )(w, g, m, v)


def _sum4(buf, name):
    _, R, C = buf.shape
    tr = _row_tile(R, C, 128 * 1024)

    def body(b_ref, o_ref):
        acc = b_ref[0].astype(F32)
        for s in range(1, NCHIP):
            acc = acc + b_ref[s].astype(F32)
        o_ref[...] = acc

    return pl.pallas_call(
        body,
        name=name,
        grid=(R // tr,),
        in_specs=[pl.BlockSpec((NCHIP, tr, C), lambda i: (0, i, 0))],
        out_specs=pl.BlockSpec((tr, C), lambda i: (i, 0)),
        out_shape=jax.ShapeDtypeStruct((R, C), F32),
        compiler_params=_cparams(("parallel",)),
    )(buf)


def _add2(a, b, name):
    S, R, C = a.shape
    tr = _row_tile(R, C, 256 * 1024)

    def body(a_ref, b_ref, o_ref):
        o_ref[...] = (a_ref[...].astype(F32) + b_ref[...].astype(F32)).astype(o_ref.dtype)

    blk = pl.BlockSpec((None, tr, C), lambda s, i: (s, i, 0))
    return pl.pallas_call(
        body,
        name=name,
        grid=(S, R // tr),
        in_specs=[blk, blk],
        out_specs=blk,
        out_shape=jax.ShapeDtypeStruct(a.shape, a.dtype),
        compiler_params=_cparams(("parallel", "parallel")),
    )(a, b)


ANY = pl.BlockSpec(memory_space=pl.ANY)


def _mesh_place():
    x, y, c = lax.axis_index("x"), lax.axis_index("y"), lax.axis_index("c")
    return x, y, c, 2 * x + y, [(1 - x, y), (x, 1 - y), (1 - x, 1 - y)]


def _gather_chips(bufs, name):
    nb = len(bufs)

    def body(*refs):
        ins = refs[:nb]
        outs = refs[nb : 2 * nb]
        send_sems, recv_sems = refs[2 * nb :]
        _gather_start(ins, outs, send_sems, recv_sems)
        _gather_finish(ins, outs, send_sems, recv_sems)

    outs = pl.pallas_call(
        body,
        name=name,
        in_specs=[ANY] * nb,
        out_specs=[ANY] * nb,
        out_shape=_gather_shapes(bufs),
        scratch_shapes=_gather_sems(nb),
    )(*bufs)
    return _gather_own(outs, bufs)


def _gather_shapes(bufs):
    return [jax.ShapeDtypeStruct((NCHIP,) + tuple(b.shape), b.dtype) for b in bufs]


def _gather_sems(nb):
    return [pltpu.SemaphoreType.DMA((6 * nb,)), pltpu.SemaphoreType.DMA((6 * nb,))]


def _gather_own(outs, bufs):
    me = 2 * lax.axis_index("x") + lax.axis_index("y")
    return [lax.dynamic_update_slice(o, b[None], (me, 0, 0)) for o, b in zip(outs, bufs)]


def _gather_copy(outs, send_sems, recv_sems, k, b, src, slot, hc, to):
    nb = len(outs)
    hr = outs[b].shape[1] // 2
    return pltpu.make_async_remote_copy(
        src_ref=src, dst_ref=outs[b].at[slot, pl.ds(hc * hr, hr)], send_sem=send_sems.at[k * nb + b],
        recv_sem=recv_sems.at[k * nb + b], device_id=to, device_id_type=MESH)


def _gather_start(ins, outs, send_sems, recv_sems):
    x, y, c, me, peers = _mesh_place()
    for k, (px, py) in enumerate(peers):
        for b in range(len(ins)):
            hr = ins[b].shape[0] // 2
            _gather_copy(outs, send_sems, recv_sems, k, b, ins[b].at[pl.ds(c * hr, hr)], me, c, (px, py, c)).start()


def _gather_finish(ins, outs, send_sems, recv_sems):
    x, y, c, me, peers = _mesh_place()
    nb = len(ins)
    sent = []
    for k, (px, py) in enumerate(peers):
        for b in range(nb):
            hr = ins[b].shape[0] // 2
            slot = 2 * px + py
            landed = outs[b].at[slot, pl.ds(c * hr, hr)]
            _gather_copy(outs, send_sems, recv_sems, k, b, landed, slot, c, (px, py, c)).wait_recv()
            cp = _gather_copy(outs, send_sems, recv_sems, 3 + k, b, landed, slot, c, (x, y, 1 - c))
            cp.start()
            sent.append(cp)
            sent.append(_gather_copy(outs, send_sems, recv_sems, k, b, ins[b].at[pl.ds(c * hr, hr)], me, c, (px, py, c)))
    for k, (px, py) in enumerate(peers):
        for b in range(nb):
            hr = ins[b].shape[0] // 2
            slot = 2 * px + py
            theirs = outs[b].at[slot, pl.ds((1 - c) * hr, hr)]
            _gather_copy(outs, send_sems, recv_sems, 3 + k, b, theirs, slot, 1 - c, (x, y, 1 - c)).wait_recv()
    for cp in sent:
        cp.wait_send()


def _scatter_chips(bufs, name):
    nb = len(bufs)

    def body(*refs):
        ins = refs[:nb]
        outs = refs[nb : 2 * nb]
        send_sems, recv_sems = refs[2 * nb :]
        _scatter_start(ins, outs, send_sems, recv_sems)
        _scatter_finish(ins, outs, send_sems, recv_sems)

    outs = pl.pallas_call(
        body,
        name=name,
        in_specs=[ANY] * nb,
        out_specs=[ANY] * nb,
        out_shape=[jax.ShapeDtypeStruct(b.shape, b.dtype) for b in bufs],
        scratch_shapes=_scatter_sems(nb),
    )(*bufs)
    return _scatter_own(outs, bufs)


def _scatter_sems(nb):
    return [pltpu.SemaphoreType.DMA((3 * nb,)), pltpu.SemaphoreType.DMA((3 * nb,))]


def _scatter_own(outs, bufs):
    me = 2 * lax.axis_index("x") + lax.axis_index("y")
    return [lax.dynamic_update_slice(o, lax.dynamic_slice_in_dim(b, me, 1, axis=0), (me, 0, 0)) for o, b in zip(outs, bufs)]


def _scatter_copy(ins, outs, send_sems, recv_sems, k, b, slot_from, slot_to, to):
    nb = len(ins)
    return pltpu.make_async_remote_copy(
        src_ref=ins[b].at[slot_from], dst_ref=outs[b].at[slot_to], send_sem=send_sems.at[k * nb + b],
        recv_sem=recv_sems.at[k * nb + b], device_id=to, device_id_type=MESH)


def _scatter_start(ins, outs, send_sems, recv_sems):
    x, y, c, me, peers = _mesh_place()
    for k, (px, py) in enumerate(peers):
        for b in range(len(ins)):
            _scatter_copy(ins, outs, send_sems, recv_sems, k, b, 2 * px + py, me, (px, py, c)).start()


def _scatter_finish(ins, outs, send_sems, recv_sems):
    x, y, c, me, peers = _mesh_place()
    for k, (px, py) in enumerate(peers):
        for b in range(len(ins)):
            _scatter_copy(ins, outs, send_sems, recv_sems, k, b, me, 2 * px + py, (px, py, c)).wait_recv()
    for k, (px, py) in enumerate(peers):
        for b in range(len(ins)):
            _scatter_copy(ins, outs, send_sems, recv_sems, k, b, 2 * px + py, me, (px, py, c)).wait_send()


def _split_cores(bufs, name):
    nb = len(bufs)

    def body(*refs):
        ins = refs[:nb]
        theirs = refs[nb : 2 * nb]
        send_sems, recv_sems = refs[2 * nb :]
        x, y, c, _, _ = _mesh_place()
        cps = []
        for b in range(nb):
            hr = ins[b].shape[1] // 2
            cp = pltpu.make_async_remote_copy(
                src_ref=ins[b].at[:, pl.ds((1 - c) * hr, hr)], dst_ref=theirs[b], send_sem=send_sems.at[b],
                recv_sem=recv_sems.at[b], device_id=(x, y, 1 - c), device_id_type=MESH)
            cp.start()
            cps.append(cp)
        for cp in cps:
            cp.wait()

    theirs = pl.pallas_call(
        body,
        name=name,
        in_specs=[ANY] * nb,
        out_specs=[ANY] * nb,
        out_shape=[jax.ShapeDtypeStruct((b.shape[0], b.shape[1] // 2, b.shape[2]), b.dtype) for b in bufs],
        scratch_shapes=[pltpu.SemaphoreType.DMA((nb,)), pltpu.SemaphoreType.DMA((nb,))],
    )(*bufs)
    c = lax.axis_index("c")
    mine = [lax.dynamic_slice_in_dim(b, c * (b.shape[1] // 2), b.shape[1] // 2, axis=1) for b in bufs]
    return mine, theirs


def _join_cores(bufs, name):
    nb = len(bufs)

    def body(*refs):
        ins = refs[:nb]
        outs = refs[nb : 2 * nb]
        send_sems, recv_sems = refs[2 * nb :]
        x, y, c, _, _ = _mesh_place()
        cps = []
        for b in range(nb):
            hr = ins[b].shape[0]
            cp = pltpu.make_async_remote_copy(
                src_ref=ins[b], dst_ref=outs[b].at[pl.ds(c * hr, hr)], send_sem=send_sems.at[b], recv_sem=recv_sems.at[b],
                device_id=(x, y, 1 - c), device_id_type=MESH)
            cp.start()
            cps.append(cp)
        for b, cp in enumerate(cps):
            hr = ins[b].shape[0]
            cp.wait_send()
            pltpu.make_async_remote_copy(
                src_ref=ins[b], dst_ref=outs[b].at[pl.ds((1 - c) * hr, hr)], send_sem=send_sems.at[b],
                recv_sem=recv_sems.at[b], device_id=(x, y, 1 - c), device_id_type=MESH).wait_recv()

    outs = pl.pallas_call(
        body,
        name=name,
        in_specs=[ANY] * nb,
        out_specs=[ANY] * nb,
        out_shape=[jax.ShapeDtypeStruct((2 * b.shape[0], b.shape[1]), b.dtype) for b in bufs],
        scratch_shapes=[pltpu.SemaphoreType.DMA((nb,)), pltpu.SemaphoreType.DMA((nb,))],
    )(*bufs)
    c = lax.axis_index("c")
    return [lax.dynamic_update_slice(o, b, (c * b.shape[0], 0)) for o, b in zip(outs, bufs)]


ROW_ALIGN = 1024


def _pack(pieces, dtype):
    flat = []
    for p in pieces:
        f = p.reshape(-1).astype(dtype)
        pad = (-f.shape[0]) % LANES
        if pad:
            f = jnp.pad(f, (0, pad))
        flat.append(f)
    tot = sum(f.shape[0] for f in flat)
    pad = (-tot) % (ROW_ALIGN * LANES)
    if pad:
        flat.append(jnp.zeros((pad,), dtype))
    return jnp.concatenate(flat).reshape(-1, LANES)


def _unpack(buf, shapes):
    lead = buf.shape[:-2]
    flat = buf.reshape(lead + (-1,))
    out = []
    off = 0
    for shp in shapes:
        n = 1
        for d in shp:
            n *= d
        out.append(flat[..., off : off + n].reshape(lead + tuple(shp)))
        off += n + ((-n) % LANES)
    return out


PARAMS = (
    ("meta_tokens", 1, "small"), ("ssd_norm", 1, "small"), ("ssd_w_in", 2, "big"), ("ssd_conv_w", 2, "small"),
    ("ssd_conv_b", 1, "small"), ("ssd_dt_bias", None, "rep"), ("ssd_a_log", None, "rep"), ("ssd_d_skip", None, "rep"),
    ("ssd_gate_norm", 1, "small"), ("ssd_w_out", 1, "big"), ("kv_norm", None, "rep"), ("w_kv", 1, "big"),
    ("sb_norm", None, "rep"), ("sb_w_q", 1, "big"), ("sb_w_o", 1, "big"), ("ffn_norm", None, "rep"),
    ("ffn_w_up", 2, "big"), ("ffn_conv_w", 2, "small"), ("ffn_conv_b", None, "rep"), ("ffn_w_down", 1, "big"),
    ("final_norm", None, "rep"),
)


def _head_cols(vec):
    return jnp.pad(vec.reshape(G, 1, E), ((0, 0), (0, 0), (0, LANES - E)))


def _head_rows(vec):
    return jnp.pad(vec.reshape(G, E, 1), ((0, 0), (0, 8 - E), (0, 0)))


def kernel(x, meta_tokens, ssd_norm, ssd_w_in, ssd_conv_w, ssd_conv_b, ssd_dt_bias, ssd_a_log, ssd_d_skip, ssd_gate_norm, ssd_w_out, kv_norm, w_kv, sb_norm, sb_w_q, sb_w_o, ffn_norm, ffn_w_up, ffn_conv_w, ffn_conv_b, ffn_w_down, final_norm, loss_target, m_meta_tokens, m_ssd_norm, m_ssd_w_in, m_ssd_conv_w, m_ssd_conv_b, m_ssd_dt_bias, m_ssd_a_log, m_ssd_d_skip, m_ssd_gate_norm, m_ssd_w_out, m_kv_norm, m_w_kv, m_sb_norm, m_sb_w_q, m_sb_w_o, m_ffn_norm, m_ffn_w_up, m_ffn_conv_w, m_ffn_conv_b, m_ffn_w_down, m_final_norm, v_meta_tokens, v_ssd_norm, v_ssd_w_in, v_ssd_conv_w, v_ssd_conv_b, v_ssd_dt_bias, v_ssd_a_log, v_ssd_d_skip, v_ssd_gate_norm, v_ssd_w_out, v_kv_norm, v_w_kv, v_sb_norm, v_sb_w_q, v_sb_w_o, v_ffn_norm, v_ffn_w_up, v_ffn_conv_w, v_ffn_conv_b, v_ffn_w_down, v_final_norm):
    local = dict(meta_tokens=meta_tokens, ssd_norm=ssd_norm, ssd_w_in=ssd_w_in, ssd_conv_w=ssd_conv_w, ssd_conv_b=ssd_conv_b, ssd_dt_bias=ssd_dt_bias, ssd_a_log=ssd_a_log, ssd_d_skip=ssd_d_skip, ssd_gate_norm=ssd_gate_norm, ssd_w_out=ssd_w_out, kv_norm=kv_norm, w_kv=w_kv, sb_norm=sb_norm, sb_w_q=sb_w_q, sb_w_o=sb_w_o, ffn_norm=ffn_norm, ffn_w_up=ffn_w_up, ffn_conv_w=ffn_conv_w, ffn_conv_b=ffn_conv_b, ffn_w_down=ffn_w_down, final_norm=final_norm)
    mom_m = dict(meta_tokens=m_meta_tokens, ssd_norm=m_ssd_norm, ssd_w_in=m_ssd_w_in, ssd_conv_w=m_ssd_conv_w, ssd_conv_b=m_ssd_conv_b, ssd_dt_bias=m_ssd_dt_bias, ssd_a_log=m_ssd_a_log, ssd_d_skip=m_ssd_d_skip, ssd_gate_norm=m_ssd_gate_norm, ssd_w_out=m_ssd_w_out, kv_norm=m_kv_norm, w_kv=m_w_kv, sb_norm=m_sb_norm, sb_w_q=m_sb_w_q, sb_w_o=m_sb_w_o, ffn_norm=m_ffn_norm, ffn_w_up=m_ffn_w_up, ffn_conv_w=m_ffn_conv_w, ffn_conv_b=m_ffn_conv_b, ffn_w_down=m_ffn_w_down, final_norm=m_final_norm)
    mom_v = dict(meta_tokens=v_meta_tokens, ssd_norm=v_ssd_norm, ssd_w_in=v_ssd_w_in, ssd_conv_w=v_ssd_conv_w, ssd_conv_b=v_ssd_conv_b, ssd_dt_bias=v_ssd_dt_bias, ssd_a_log=v_ssd_a_log, ssd_d_skip=v_ssd_d_skip, ssd_gate_norm=v_ssd_gate_norm, ssd_w_out=v_ssd_w_out, kv_norm=v_kv_norm, w_kv=v_w_kv, sb_norm=v_sb_norm, sb_w_q=v_sb_w_q, sb_w_o=v_sb_w_o, ffn_norm=v_ffn_norm, ffn_w_up=v_ffn_w_up, ffn_conv_w=v_ffn_conv_w, ffn_conv_b=v_ffn_conv_b, ffn_w_down=v_ffn_w_down, final_norm=v_final_norm)

    big_names = [n for n, _, kind in PARAMS if kind == "big"]
    small_names = [n for n, _, kind in PARAMS if kind == "small"]
    rep_names = [n for n, _, kind in PARAMS if kind == "rep"]
    axis_of = {n: ax for n, ax, _ in PARAMS}

    def rows2(a):
        return a.reshape(-1, a.shape[-1])

    first_big, later_big = big_names[:1], big_names[1:]
    full = {}

    def assemble(names, bufs):
        for n, buf in zip(names, bufs):
            p = buf.reshape((NCHIP,) + local[n].shape)
            full[n] = jnp.concatenate([p[s] for s in range(NCHIP)], axis=axis_of[n])

    small_own = _pack([local[n] for n in small_names], F32)
    gathered = _gather_chips([rows2(local[n]).astype(BF16) for n in first_big] + [small_own], "gather_first")
    assemble(first_big, gathered[:-1])
    for n, p in zip(small_names, _unpack(gathered[-1], [local[n].shape for n in small_names])):
        full[n] = jnp.concatenate([p[s] for s in range(NCHIP)], axis=axis_of[n])
    for n in rep_names:
        full[n] = local[n]

    w_in = full["ssd_w_in"][0]
    w_z, w_xbc = w_in[:, :DI], w_in[:, DI : DI + CD]
    w_dt = jnp.pad(w_in[:, DI + CD :], ((0, 0), (0, LANES - H)))
    fcw, fcb = full["ffn_conv_w"], full["ffn_conv_b"]
    scw, scb = full["ssd_conv_w"][0], full["ssd_conv_b"]
    bias_c, bias_r = _head_cols(full["ssd_dt_bias"][0]), _head_rows(full["ssd_dt_bias"][0])
    alog_c, alog_r = _head_cols(full["ssd_a_log"][0]), _head_rows(full["ssd_a_log"][0])
    dskip_c = _head_cols(full["ssd_d_skip"][0])
    kvn = full["kv_norm"].reshape(1, D)
    fin = full["final_norm"].reshape(1, D)

    h0 = jnp.concatenate([jnp.zeros((PF, D), F32), full["meta_tokens"], x[0]], axis=0)
    (u0,) = _rms_fwd(h0, [full["ssd_norm"]], "ssd_norm_fwd")
    z = _mm(u0, w_z, name="ssd_in_z")
    xr = _mm(u0, w_xbc, name="ssd_in_xbc")
    dt_raw = _mm(u0, w_dt, name="ssd_in_dt")
    xbc = _ssd_conv_fwd(xr, scw, scb, "ssd_conv_fwd")
    dth = dt_raw[:, :H].reshape(LP, G, E)
    dtc = jnp.pad(jnp.transpose(dth, (1, 0, 2)), ((0, 0), (0, 0), (0, LANES - E)))
    dtr = jnp.pad(jnp.transpose(dth, (1, 2, 0)), ((0, 0), (0, 8 - E), (0, 0)))
    later_own = [rows2(local[n]).astype(BF16) for n in later_big]
    y, states, later_all = _ssd_fwd(xbc, dtc, dtr, bias_c, bias_r, alog_c, alog_r, dskip_c, later_own, "ssd_scan_fwd")
    assemble(later_big, later_all)
    w_out = full["ssd_w_out"][0]
    wkv = full["w_kv"]
    w_q = full["sb_w_q"][0]
    w_o = full["sb_w_o"][0]
    w_up_g = [full["ffn_w_up"][l][:, :DFF] for l in range(2)]
    w_up_v = [full["ffn_w_up"][l][:, DFF:] for l in range(2)]
    w_down = [full["ffn_w_down"][l] for l in range(2)]
    hgn = _gate_fwd(y, z, full["ssd_gate_norm"], "ssd_gate_fwd")
    h1 = _mm(hgn, w_out, add=h0, mask_rows=True, name="ssd_out")

    def ffn_fwd(h, l, tag):
        (u,) = _rms_fwd(h, [full["ffn_norm"][l : l + 1]], f"ffn{tag}_norm_fwd")
        hg = _mm(u, w_up_g[l], name=f"ffn{tag}_up_g")
        hv = _mm(u, w_up_v[l], name=f"ffn{tag}_up_v")
        act = _ffn_act_fwd(hg, hv, fcw[l][:, :DFF], fcw[l][:, DFF:], fcb[l : l + 1, :DFF], fcb[l : l + 1, DFF:], f"ffn{tag}_act_fwd")
        hn = _mm(act, w_down[l], add=h, mask_rows=True, name=f"ffn{tag}_down")
        return hn, (u, hg, hv, act)

    h2, ffn0 = ffn_fwd(h1, 0, "0")
    ukv, uq = _rms_fwd(h2, [kvn, full["sb_norm"]], "attn_norm_fwd")
    kk = _mm(ukv, wkv[:, :D], out_dtype=BF16, name="attn_k")
    vv = _mm(ukv, wkv[:, D:], out_dtype=BF16, name="attn_v")
    qq = _mm(uq, w_q, out_dtype=BF16, scale=64.0**-0.5, name="attn_q")
    o = _attn_fwd(qq, kk, vv, "attn_fwd")
    h3 = _mm(o, w_o, add=h2, mask_rows=True, name="attn_out")
    h4, ffn1 = ffn_fwd(h3, 1, "1")
    dh, g_final, loss_rows = _loss_head(h4, fin, loss_target[0], "loss_head")
    loss = lax.psum(0.5 / D * jnp.sum(loss_rows), ("x", "y", "c"))

    grads = {"final_norm": g_final.reshape(D)}

    def ffn_bwd(dh, h, l, saved, tag):
        u, hg, hv, act = saved
        da = _mm(dh, w_down[l], tb=True, name=f"ffn{tag}_down_dx")
        gw_down = _mm(act, dh, ta=True, out_dtype=BF16, name=f"ffn{tag}_down_dw")
        dhg, dhv, dwg, dwv, dbg, dbv = _ffn_act_bwd(hg, hv, da, fcw[l][:, :DFF], fcw[l][:, DFF:], fcb[l : l + 1, :DFF], fcb[l : l + 1, DFF:], f"ffn{tag}_act_bwd")
        gw_up = jnp.concatenate([_mm(u, dhg, ta=True, out_dtype=BF16, name=f"ffn{tag}_up_g_dw"), _mm(u, dhv, ta=True, out_dtype=BF16, name=f"ffn{tag}_up_v_dw")], axis=1)
        du = _mm(dhg, w_up_g[l], tb=True, name=f"ffn{tag}_up_g_dx")
        du = _mm(dhv, w_up_v[l], tb=True, add=du, name=f"ffn{tag}_up_v_dx")
        dh_new, (gn,) = _rms_bwd(dh, h, [du], [full["ffn_norm"][l : l + 1]], f"ffn{tag}_norm_bwd")
        return dh_new, gw_down, gw_up, jnp.concatenate([dwg, dwv], axis=1), jnp.concatenate([dbg, dbv], axis=1), gn

    dh, gd1, gu1, gcw1, gcb1, gn1 = ffn_bwd(dh, h3, 1, ffn1, "1")
    do = _mm(dh, w_o, tb=True, name="attn_out_dx")
    grads["sb_w_o"] = _mm(o, dh, ta=True, out_dtype=BF16, name="attn_out_dw")[None]
    dq, dk, dv = _attn_bwd(qq, kk, vv, o, do, "attn_bwd")
    grads["sb_w_q"] = _mm(uq, dq, ta=True, out_dtype=BF16, scale=64.0**-0.5, name="attn_q_dw")[None]
    grads["w_kv"] = jnp.concatenate([_mm(ukv, dk, ta=True, out_dtype=BF16, name="attn_k_dw"), _mm(ukv, dv, ta=True, out_dtype=BF16, name="attn_v_dw")], axis=1)
    duq = _mm(dq, w_q, tb=True, scale=64.0**-0.5, name="attn_q_dx")
    dukv = _mm(dk, wkv[:, :D], tb=True, name="attn_k_dx")
    dukv = _mm(dv, wkv[:, D:], tb=True, add=dukv, name="attn_v_dx")
    dh, (g_kvn, g_sbn) = _rms_bwd(dh, h2, [dukv, duq], [kvn, full["sb_norm"]], "attn_norm_bwd")
    grads["kv_norm"] = g_kvn.reshape(D)
    grads["sb_norm"] = g_sbn
    dh, gd0, gu0, gcw0, gcb0, gn0 = ffn_bwd(dh, h1, 0, ffn0, "0")
    grads["ffn_w_down"] = jnp.stack([gd0, gd1])
    grads["ffn_w_up"] = jnp.stack([gu0, gu1])
    grads["ffn_conv_w"] = jnp.stack([gcw0, gcw1])
    grads["ffn_conv_b"] = jnp.concatenate([gcb0, gcb1], axis=0)
    grads["ffn_norm"] = jnp.concatenate([gn0, gn1], axis=0)
    dhgn = _mm(dh, w_out, tb=True, name="ssd_out_dx")
    grads["ssd_w_out"] = _mm(hgn, dh, ta=True, out_dtype=BF16, name="ssd_out_dw")[None]
    dy, dz, g_gate = _gate_bwd(dhgn, y, z, full["ssd_gate_norm"], "ssd_gate_bwd")
    grads["ssd_gate_norm"] = g_gate
    def slots(n):
        return jnp.stack([rows2(p) for p in jnp.split(grads[n], NCHIP, axis=axis_of[n])])

    mine, theirs = _split_cores([slots(n) for n in later_big], "split_cores_a")
    pair = [_add2(a, b, f"pair_sum_a{i}") for i, (a, b) in enumerate(zip(mine, theirs))]
    dxs, dB, dC, ddt_raw, g_bias, g_alog, g_dskip, got_a = _ssd_bwd(
        xbc, dy, states, dtc, dtr, bias_c, bias_r, alog_c, alog_r, dskip_c, pair, "ssd_scan_bwd")
    grads["ssd_dt_bias"] = g_bias[:, 0, :E].reshape(1, H)
    grads["ssd_a_log"] = g_alog[:, 0, :E].reshape(1, H)
    grads["ssd_d_skip"] = g_dskip[:, 0, :E].reshape(1, H)
    dxr, g_scw, g_scb = _ssd_conv_bwd(xr, jnp.concatenate([dxs, dB, dC], axis=1), scw, scb, "ssd_conv_bwd")
    grads["ssd_conv_w"] = g_scw[None]
    grads["ssd_conv_b"] = g_scb
    ddt = jnp.pad(jnp.transpose(ddt_raw[:, :, :E], (1, 0, 2)).reshape(LP, H), ((0, 0), (0, LANES - H)))
    grads["ssd_w_in"] = jnp.concatenate(
        [_mm(u0, dz, ta=True, out_dtype=BF16, name="ssd_in_z_dw"), _mm(u0, dxr, ta=True, out_dtype=BF16, name="ssd_in_xbc_dw"), _mm(u0, ddt, ta=True, out_dtype=BF16, name="ssd_in_dt_dw")[:, :H]], axis=1)[None]
    du = _mm(dz, w_z, tb=True, name="ssd_in_z_dx")
    du = _mm(dxr, w_xbc, tb=True, add=du, name="ssd_in_xbc_dx")
    du = _mm(ddt, w_dt, tb=True, add=du, name="ssd_in_dt_dx")
    dh, (g_ssdn,) = _rms_bwd(dh, h0, [du], [full["ssd_norm"]], "ssd_norm_bwd")
    grads["ssd_norm"] = g_ssdn
    grads["meta_tokens"] = dh[PF : PF + N_META]
    grad_x = dh[PF + N_META :][None]

    def shard_pieces(names, s):
        out = []
        for n in names:
            ax = axis_of[n]
            out.append(grads[n] if ax is None else jnp.split(grads[n], NCHIP, axis=ax)[s])
        return out

    bufs = [slots(n) for n in first_big]
    bufs.append(jnp.stack([_pack(shard_pieces(small_names + rep_names, s), F32) for s in range(NCHIP)]))
    mine, theirs = _split_cores(bufs, "split_cores_b")
    pair = [_add2(a, b, f"pair_sum_b{i}") for i, (a, b) in enumerate(zip(mine, theirs))]
    got_b = _scatter_chips(pair, "scatter_grads")
    got = got_b[:-1] + got_a + got_b[-1:]
    sums = [_sum4(b, f"sum_chips_{i}") for i, b in enumerate(got)]
    gsum = _join_cores(sums, "join_cores")

    def rows(a):
        f = a.reshape(-1)
        pad = (-f.shape[0]) % LANES
        if pad:
            f = jnp.pad(f, (0, pad))
        return f.reshape(-1, LANES)

    order = [n for n, _, _ in PARAMS]
    res = {}
    for n, g2 in zip(big_names, gsum[:-1]):
        outs = _adamw(rows2(local[n]), g2, rows2(mom_m[n]), rows2(mom_v[n]), f"adamw_{n}")
        res[n] = [o_.reshape(local[n].shape) for o_ in outs]
    rest = small_names + rep_names
    for n, g1 in zip(rest, _unpack(gsum[-1], [local[n].shape for n in rest])):
        shp = local[n].shape
        cnt = 1
        for d in shp:
            cnt *= d
        outs = _adamw(rows(local[n]), rows(g1), rows(mom_m[n]), rows(mom_v[n]), f"adamw_{n}")
        res[n] = [o_.reshape(-1)[:cnt].reshape(shp) for o_ in outs]
    return (loss, grad_x, *[res[n][0] for n in order], *[res[n][1] for n in order], *[res[n][2] for n in order], *[res[n][3] for n in order])
```

```python
import functools

import jax
import jax.numpy as jnp
from jax import lax
from jax.experimental import pallas as pl
from jax.experimental.pallas import tpu as pltpu

D = 1024
SEQ = 8192
N_META = 16
EPS = 1e-6
P = 64
G = 4
N = 128
CONVW = 4
Q = 256
FC = 3
DFF = 256 * ((8 * D // 3 + 255) // 256)
DI = 2 * D
H = DI // P
E = H // G
GW = E * P
CD = DI + 2 * G * N
IN = DI + CD + H
SBH = D // 64
HP = 128
LANES = 128
PF = Q - N_META
LP = PF + N_META + SEQ
NC = LP // Q
TQ = 256
NCHIP = 4
ADAM_LR, ADAM_B1, ADAM_B2, ADAM_EPS, ADAM_WD, ADAM_STEP = 0.001, 0.9, 0.999, 1e-08, 0.01, 10

F32 = jnp.float32
BF16 = jnp.bfloat16
HI = lax.Precision.HIGHEST
MESH = pl.DeviceIdType.MESH
VMEM_LIMIT = 48 * 1024 * 1024
MM_MAX_K = 3072
T_SKIP = 110.0


def _pick(n, cands):
    for c in cands:
        if n % c == 0:
            return c
    raise ValueError((n, cands))


def _cparams(sem):
    return pltpu.CompilerParams(dimension_semantics=sem, vmem_limit_bytes=VMEM_LIMIT)


def _valid_rows(block, rows):
    r = block * rows + lax.broadcasted_iota(jnp.int32, (rows, 1), 0)
    return r >= PF


def _sigmoid(x):
    return 1.0 / (1.0 + jnp.exp(-x))


def _softplus(x):
    return jnp.maximum(x, 0.0) + jnp.log(1.0 + jnp.exp(-jnp.abs(x)))


def _sum_all(x):
    return jnp.sum(jnp.sum(x, axis=1, keepdims=True), axis=0, keepdims=True)


def _dsilu(x):
    s = _sigmoid(x)
    return s * (1.0 + x * (1.0 - s))


def _mm(a, b, *, ta=False, tb=False, out_dtype=F32, add=None, mask_rows=False, scale=None, name):
    if ta:
        K, M = a.shape
    else:
        M, K = a.shape
    if tb:
        Nn, K2 = b.shape
    else:
        K2, Nn = b.shape
    assert K == K2, (a.shape, b.shape, ta, tb)
    tn = _pick(Nn, (1408, 1024, 768, 512, 256, 128))
    if ta:
        tm = _pick(M, (1408, 1024, 768, 512, 256, 128))
        tk = _pick(K, (768, 512, 256))
    else:
        tm = _pick(M, (768, 256))
        tk = K if K <= MM_MAX_K else _pick(K, (1024, 768, 512, 256, 128))
    nk = K // tk
    dims = (((0 if ta else 1,), (1 if tb else 0,)), ((), ()))

    def body(*refs):
        a_ref, b_ref = refs[0], refs[1]
        add_ref = refs[2] if add is not None else None
        o_ref = refs[3] if add is not None else refs[2]
        acc = refs[-1] if nk > 1 else None

        def finish(r):
            if scale is not None:
                r = r * scale
            if mask_rows:
                r = jnp.where(_valid_rows(pl.program_id(0), tm), r, 0.0)
            if add_ref is not None:
                r = r + add_ref[...]
            o_ref[...] = r.astype(out_dtype)

        part = lax.dot_general(a_ref[...].astype(BF16), b_ref[...].astype(BF16), dims, preferred_element_type=F32)
        if nk == 1:
            finish(part)
        else:
            k = pl.program_id(2)

            @pl.when(k == 0)
            def _():
                acc[...] = part

            @pl.when(k > 0)
            def _():
                acc[...] += part

            @pl.when(k == nk - 1)
            def _():
                finish(acc[...])

    a_spec = pl.BlockSpec((tk, tm), lambda i, j, k: (k, i)) if ta else pl.BlockSpec((tm, tk), lambda i, j, k: (i, k))
    b_spec = pl.BlockSpec((tn, tk), lambda i, j, k: (j, k)) if tb else pl.BlockSpec((tk, tn), lambda i, j, k: (k, j))
    o_spec = pl.BlockSpec((tm, tn), lambda i, j, k: (i, j))
    in_specs = [a_spec, b_spec] + ([o_spec] if add is not None else [])
    args = (a, b) + ((add,) if add is not None else ())
    return pl.pallas_call(
        body,
        name=name,
        grid=(M // tm, Nn // tn, nk),
        in_specs=in_specs,
        out_specs=o_spec,
        out_shape=jax.ShapeDtypeStruct((M, Nn), out_dtype),
        scratch_shapes=[pltpu.VMEM((tm, tn), F32)] if nk > 1 else [],
        compiler_params=_cparams(("parallel", "parallel", "arbitrary")),
    )(*args)


def _rms_fwd(h, gains, name):
    tr = _pick(LP, (768, 256))
    ng = len(gains)

    def body(*refs):
        h_ref = refs[0]
        g_refs = refs[1 : 1 + ng]
        o_refs = refs[1 + ng :]
        x = h_ref[...]
        xh = x * lax.rsqrt(jnp.mean(x * x, axis=-1, keepdims=True) + EPS)
        for g_ref, o_ref in zip(g_refs, o_refs):
            o_ref[...] = (xh * g_ref[...]).astype(BF16)

    row = pl.BlockSpec((tr, D), lambda i: (i, 0))
    vec = pl.BlockSpec((1, D), lambda i: (0, 0))
    outs = pl.pallas_call(
        body,
        name=name,
        grid=(LP // tr,),
        in_specs=[row] + [vec] * ng,
        out_specs=[row] * ng,
        out_shape=[jax.ShapeDtypeStruct((LP, D), BF16)] * ng,
        compiler_params=_cparams(("parallel",)),
    )(h, *gains)
    return outs


def _rms_bwd(dh_in, h, dus, gains, name):
    tr = _pick(LP, (256,))
    ng = len(gains)

    def body(*refs):
        dh_ref, h_ref = refs[0], refs[1]
        du_refs = refs[2 : 2 + ng]
        g_refs = refs[2 + ng : 2 + 2 * ng]
        o_ref = refs[2 + 2 * ng]
        dg_refs = refs[3 + 2 * ng :]
        i = pl.program_id(0)
        x = h_ref[...]
        r = lax.rsqrt(jnp.mean(x * x, axis=-1, keepdims=True) + EPS)
        xh = x * r
        tot = dh_ref[...]
        for du_ref, g_ref, dg_ref in zip(du_refs, g_refs, dg_refs):
            du = du_ref[...]
            dxh = du * g_ref[...]
            tot = tot + r * (dxh - xh * jnp.mean(dxh * xh, axis=-1, keepdims=True))

            @pl.when(i == 0)
            def _():
                dg_ref[...] = jnp.zeros_like(dg_ref)

            dg_ref[...] += jnp.sum(du * xh, axis=0, keepdims=True)
        o_ref[...] = jnp.where(_valid_rows(i, tr), tot, 0.0)

    row = pl.BlockSpec((tr, D), lambda i: (i, 0))
    vec = pl.BlockSpec((1, D), lambda i: (0, 0))
    outs = pl.pallas_call(
        body,
        name=name,
        grid=(LP // tr,),
        in_specs=[row, row] + [row] * ng + [vec] * ng,
        out_specs=[row] + [vec] * ng,
        out_shape=[jax.ShapeDtypeStruct((LP, D), F32)] + [jax.ShapeDtypeStruct((1, D), F32)] * ng,
        compiler_params=_cparams(("arbitrary",)),
    )(dh_in, h, *dus, *gains)
    return outs[0], outs[1:]


def _loss_head(h, gain, target, name):
    tr = Q

    def body(h_ref, g_ref, t_ref, dh_ref, dg_ref, ls_ref):
        i = pl.program_id(0)

        @pl.when(i == 0)
        def _():
            dg_ref[...] = jnp.zeros_like(dg_ref)
            ls_ref[...] = jnp.zeros_like(ls_ref)
            dh_ref[...] = jnp.zeros_like(dh_ref)

        @pl.when(i > 0)
        def _():
            x = h_ref[...]
            g = g_ref[...]
            r = lax.rsqrt(jnp.mean(x * x, axis=-1, keepdims=True) + EPS)
            xh = x * r
            e = xh * g - t_ref[...]
            ls_ref[...] += jnp.sum(e * e, axis=0, keepdims=True)
            dy = e * (1.0 / D)
            dg_ref[...] += jnp.sum(dy * xh, axis=0, keepdims=True)
            dxh = dy * g
            dh_ref[...] = r * (dxh - xh * jnp.mean(dxh * xh, axis=-1, keepdims=True))

    row = pl.BlockSpec((tr, D), lambda i: (i, 0))
    vec = pl.BlockSpec((1, D), lambda i: (0, 0))
    return pl.pallas_call(
        body,
        name=name,
        grid=(LP // tr,),
        in_specs=[row, vec, pl.BlockSpec((tr, D), lambda i: (jnp.maximum(i - 1, 0), 0))],
        out_specs=[row, vec, vec],
        out_shape=[jax.ShapeDtypeStruct((LP, D), F32), jax.ShapeDtypeStruct((1, D), F32), jax.ShapeDtypeStruct((1, D), F32)],
        compiler_params=_cparams(("arbitrary",)),
    )(h, gain, target)


HALO = 8
CONV_COLS = (1536, 1408, 768, 512, 256)


def _conv_rows(ext, w, b, width):
    n = ext.shape[0]
    acc = b + w[width - 1 : width, :] * ext[HALO:]
    for k in range(width - 1):
        acc = acc + w[k : k + 1, :] * pltpu.roll(ext, width - 1 - k, 0)[HALO:]
    return acc


def _conv_specs(tr, tn, col):
    per = tr // HALO
    last = LP // HALO - 1
    prev = pl.BlockSpec((HALO, tn), lambda j, i: (jnp.maximum(i * per - 1, 0), col(j)))
    cur = pl.BlockSpec((tr, tn), lambda j, i: (i, col(j)))
    nxt = pl.BlockSpec((HALO, tn), lambda j, i: (jnp.minimum((i + 1) * per, last), col(j)))
    return prev, cur, nxt


def _conv_bwd_core(ext, dpre, w, width, i, tr):
    rows = i * tr + lax.broadcasted_iota(jnp.int32, (tr + HALO, 1), 0)
    dpre = jnp.where((rows >= PF) & (rows < LP), dpre, 0.0)
    n = tr + HALO
    dx = w[width - 1 : width, :] * dpre[:tr]
    for k in range(width - 1):
        sh = width - 1 - k
        dx = dx + w[k : k + 1, :] * pltpu.roll(dpre, n - sh, 0)[:tr]
    dcur = dpre[:tr]
    dws = []
    for k in range(width):
        sh = width - 1 - k
        xs = ext[HALO:] if sh == 0 else pltpu.roll(ext, sh, 0)[HALO:]
        dws.append(jnp.sum(xs * dcur, axis=0, keepdims=True))
    db = jnp.sum(dcur, axis=0, keepdims=True)
    dx = jnp.where(_valid_rows(i, tr), dx, 0.0)
    return dx, dws, db


def _ssd_conv_fwd(xr, cw, cb, name):
    tr, tn = Q, _pick(CD, CONV_COLS)

    def body(p_ref, c_ref, w_ref, b_ref, o_ref, pre_ref):
        i = pl.program_id(1)
        ext = jnp.concatenate([jnp.where(i > 0, p_ref[...], 0.0), c_ref[...]], axis=0)
        pre = _conv_rows(ext, w_ref[...], b_ref[...], CONVW)
        pre_ref[...] = pre
        o_ref[...] = jnp.where(_valid_rows(i, tr), pre * _sigmoid(pre), 0.0)

    prev, cur, _ = _conv_specs(tr, tn, lambda j: j)
    return pl.pallas_call(
        body,
        name=name,
        grid=(CD // tn, LP // tr),
        in_specs=[prev, cur, pl.BlockSpec((CONVW, tn), lambda j, i: (0, j)), pl.BlockSpec((1, tn), lambda j, i: (0, j))],
        out_specs=[cur, cur],
        out_shape=[jax.ShapeDtypeStruct((LP, CD), F32)] * 2,
        compiler_params=_cparams(("parallel", "arbitrary")),
    )(xr, xr, cw, cb)


def _ssd_conv_bwd(xr, pre, dxbc, cw, name):
    tr, tn = Q, _pick(CD, CONV_COLS)

    def body(p_ref, c_ref, prc_ref, prn_ref, dc_ref, dn_ref, w_ref, dx_ref, dw_ref, db_ref):
        i = pl.program_id(1)
        ext = jnp.concatenate([jnp.where(i > 0, p_ref[...], 0.0), c_ref[...]], axis=0)
        dout = jnp.concatenate([dc_ref[...], dn_ref[...]], axis=0)
        prev_ = jnp.concatenate([prc_ref[...], prn_ref[...]], axis=0)
        dx, dws, db = _conv_bwd_core(ext, dout * _dsilu(prev_), w_ref[...], CONVW, i, tr)
        dx_ref[...] = dx

        @pl.when(i == 0)
        def _():
            dw_ref[...] = jnp.zeros_like(dw_ref)
            db_ref[...] = jnp.zeros_like(db_ref)

        for k in range(CONVW):
            dw_ref[k : k + 1, :] += dws[k]
        db_ref[...] += db

    prev, cur, nxt = _conv_specs(tr, tn, lambda j: j)
    wspec = pl.BlockSpec((CONVW, tn), lambda j, i: (0, j))
    bspec = pl.BlockSpec((1, tn), lambda j, i: (0, j))
    return pl.pallas_call(
        body,
        name=name,
        grid=(CD // tn, LP // tr),
        in_specs=[prev, cur, cur, nxt, cur, nxt, wspec],
        out_specs=[cur, wspec, bspec],
        out_shape=[jax.ShapeDtypeStruct((LP, CD), F32), jax.ShapeDtypeStruct((CONVW, CD), F32), jax.ShapeDtypeStruct((1, CD), F32)],
        compiler_params=_cparams(("parallel", "arbitrary")),
    )(xr, xr, pre, pre, dxbc, dxbc, cw)


def _ffn_act_fwd(hg, hv, cwg, cwv, cbg, cbv, name):
    tr, tn = Q, _pick(DFF, CONV_COLS)

    def body(pg, cg, pv, cv, wg, wv, bg, bv, o_ref, gate_ref, val_ref):
        i = pl.program_id(1)
        eg = jnp.concatenate([jnp.where(i > 0, pg[...], 0.0), cg[...]], axis=0)
        ev = jnp.concatenate([jnp.where(i > 0, pv[...], 0.0), cv[...]], axis=0)
        gate = _conv_rows(eg, wg[...], bg[...], FC)
        val = _conv_rows(ev, wv[...], bv[...], FC)
        gate_ref[...] = gate
        val_ref[...] = val
        o_ref[...] = (gate * _sigmoid(gate) * val).astype(BF16)

    prev, cur, _ = _conv_specs(tr, tn, lambda j: j)
    wspec = pl.BlockSpec((FC, tn), lambda j, i: (0, j))
    bspec = pl.BlockSpec((1, tn), lambda j, i: (0, j))
    return pl.pallas_call(
        body,
        name=name,
        grid=(DFF // tn, LP // tr),
        in_specs=[prev, cur, prev, cur, wspec, wspec, bspec, bspec],
        out_specs=[cur, cur, cur],
        out_shape=[jax.ShapeDtypeStruct((LP, DFF), BF16), jax.ShapeDtypeStruct((LP, DFF), F32), jax.ShapeDtypeStruct((LP, DFF), F32)],
        compiler_params=_cparams(("parallel", "arbitrary")),
    )(hg, hg, hv, hv, cwg, cwv, cbg, cbv)


def _ffn_act_bwd(hg, hv, gate_pre, val_pre, da, cwg, cwv, name):
    tr, tn = Q, _pick(DFF, CONV_COLS)

    def body(pg, cg, pv, cv, gc, gn, vc, vn, dc, dn, wg, wv, dg_ref, dv_ref, dwg, dwv, dbg, dbv):
        i = pl.program_id(1)
        eg = jnp.concatenate([jnp.where(i > 0, pg[...], 0.0), cg[...]], axis=0)
        ev = jnp.concatenate([jnp.where(i > 0, pv[...], 0.0), cv[...]], axis=0)
        dout = jnp.concatenate([dc[...], dn[...]], axis=0)
        gate = jnp.concatenate([gc[...], gn[...]], axis=0)
        val = jnp.concatenate([vc[...], vn[...]], axis=0)
        s = _sigmoid(gate)
        dxg, dwsg, dbgv = _conv_bwd_core(eg, dout * val * (s * (1.0 + gate * (1.0 - s))), wg[...], FC, i, tr)
        dxv, dwsv, dbvv = _conv_bwd_core(ev, dout * (gate * s), wv[...], FC, i, tr)
        dg_ref[...] = dxg
        dv_ref[...] = dxv

        @pl.when(i == 0)
        def _():
            dwg[...] = jnp.zeros_like(dwg)
            dwv[...] = jnp.zeros_like(dwv)
            dbg[...] = jnp.zeros_like(dbg)
            dbv[...] = jnp.zeros_like(dbv)

        for k in range(FC):
            dwg[k : k + 1, :] += dwsg[k]
            dwv[k : k + 1, :] += dwsv[k]
        dbg[...] += dbgv
        dbv[...] += dbvv

    prev, cur, nxt = _conv_specs(tr, tn, lambda j: j)
    wspec = pl.BlockSpec((FC, tn), lambda j, i: (0, j))
    bspec = pl.BlockSpec((1, tn), lambda j, i: (0, j))
    big = jax.ShapeDtypeStruct((LP, DFF), F32)
    wsh = jax.ShapeDtypeStruct((FC, DFF), F32)
    bsh = jax.ShapeDtypeStruct((1, DFF), F32)
    return pl.pallas_call(
        body,
        name=name,
        grid=(DFF // tn, LP // tr),
        in_specs=[prev, cur, prev, cur, cur, nxt, cur, nxt, cur, nxt, wspec, wspec],
        out_specs=[cur, cur, wspec, wspec, bspec, bspec],
        out_shape=[big, big, wsh, wsh, bsh, bsh],
        compiler_params=_cparams(("parallel", "arbitrary")),
    )(hg, hg, hv, hv, gate_pre, gate_pre, val_pre, val_pre, da, da, cwg, cwv)


def _gate_fwd(y, z, gg, name):
    tr = _pick(LP, (768, 256))

    def body(y_ref, z_ref, g_ref, o_ref):
        zv = z_ref[...]
        hg = y_ref[...] * zv * _sigmoid(zv)
        r = lax.rsqrt(jnp.mean(hg * hg, axis=-1, keepdims=True) + EPS)
        o_ref[...] = (hg * r * g_ref[...]).astype(BF16)

    blk = pl.BlockSpec((tr, GW), lambda i, g: (i, g))
    return pl.pallas_call(
        body,
        name=name,
        grid=(LP // tr, G),
        in_specs=[blk, blk, pl.BlockSpec((1, GW), lambda i, g: (0, g))],
        out_specs=blk,
        out_shape=jax.ShapeDtypeStruct((LP, DI), BF16),
        compiler_params=_cparams(("parallel", "parallel")),
    )(y, z, gg)


def _gate_bwd(dout, y, z, gg, name):
    tr = _pick(LP, (768, 256))

    def body(do_ref, y_ref, z_ref, g_ref, dy_ref, dz_ref, dg_ref):
        i = pl.program_id(1)
        zv = z_ref[...]
        yv = y_ref[...]
        sz = zv * _sigmoid(zv)
        hg = yv * sz
        r = lax.rsqrt(jnp.mean(hg * hg, axis=-1, keepdims=True) + EPS)
        hh = hg * r
        do = do_ref[...]
        dhh = do * g_ref[...]
        dhg = r * (dhh - hh * jnp.mean(dhh * hh, axis=-1, keepdims=True))
        dy_ref[...] = dhg * sz
        dz_ref[...] = dhg * yv * _dsilu(zv)

        @pl.when(i == 0)
        def _():
            dg_ref[...] = jnp.zeros_like(dg_ref)

        dg_ref[...] += jnp.sum(do * hh, axis=0, keepdims=True)

    blk = pl.BlockSpec((tr, GW), lambda g, i: (i, g))
    vec = pl.BlockSpec((1, GW), lambda g, i: (0, g))
    big = jax.ShapeDtypeStruct((LP, DI), F32)
    return pl.pallas_call(
        body,
        name=name,
        grid=(G, LP // tr),
        in_specs=[blk, blk, blk, vec],
        out_specs=[blk, blk, vec],
        out_shape=[big, big, jax.ShapeDtypeStruct((1, DI), F32)],
        compiler_params=_cparams(("parallel", "arbitrary")),
    )(dout, y, z, gg)


def _ssd_common(dtc_ref, dtr_ref, bc_ref, br_ref, ac_ref, ar_ref, c):
    rows = c * Q + lax.broadcasted_iota(jnp.int32, (Q, 1), 0)
    cols = c * Q + lax.broadcasted_iota(jnp.int32, (1, Q), 1)
    prec = dtc_ref[...] + bc_ref[...]
    prer = dtr_ref[...] + br_ref[...]
    dtc = jnp.where(rows >= PF, _softplus(prec), 0.0)
    dtr = jnp.where(cols >= PF, _softplus(prer), 0.0)
    a_c = -jnp.exp(ac_ref[...])
    a_r = -jnp.exp(ar_ref[...])
    li = lax.broadcasted_iota(jnp.int32, (Q, Q), 0)
    si = lax.broadcasted_iota(jnp.int32, (Q, Q), 1)
    tril = si <= li
    trif = tril.astype(F32)
    csc = jnp.dot(trif, dtc * a_c, precision=HI, preferred_element_type=F32)
    csr = lax.dot_general(dtr * a_r, trif, (((1,), (1,)), ((), ())), precision=HI, preferred_element_type=F32)
    return dict(rows=rows, prec=prec, dtc=dtc, a_c=a_c, tril=tril, trif=trif, csc=csc, csr=csr, li=li, si=si)


def _pair_expand(arr, h0, lane_lo):
    return jnp.where(lane_lo, arr[:, h0 : h0 + 1], arr[:, h0 + 1 : h0 + 2])


def _ssd_specs():
    nb = DI // N
    xs = pl.BlockSpec((Q, GW), lambda g, c: (c, g))
    bb = pl.BlockSpec((Q, N), lambda g, c: (c, nb + g))
    cc = pl.BlockSpec((Q, N), lambda g, c: (c, nb + G + g))
    dtc = pl.BlockSpec((None, Q, LANES), lambda g, c: (g, c, 0))
    dtr = pl.BlockSpec((None, 8, Q), lambda g, c: (g, 0, c))
    pc = pl.BlockSpec((None, 1, LANES), lambda g, c: (g, 0, 0))
    pr = pl.BlockSpec((None, 8, 1), lambda g, c: (g, 0, 0))
    return xs, bb, cc, dtc, dtr, pc, pr


def _ssd_fwd(xbc, dtc, dtr, bias_c, bias_r, alog_c, alog_r, dskip_c, gather, name):
    nb = len(gather)

    def body(*refs):
        xs_ref, b_ref, c_ref, dtc_ref, dtr_ref, bc_ref, br_ref, ac_ref, ar_ref, dk_ref = refs[:10]
        gin = refs[10 : 10 + nb]
        y_ref, st_ref = refs[10 + nb : 12 + nb]
        gout = refs[12 + nb : 12 + 2 * nb]
        state, send_sems, recv_sems = refs[12 + 2 * nb :]
        c = pl.program_id(1)
        first_step = (pl.program_id(0) == 0) & (c == 0)
        last_step = (pl.program_id(0) == G - 1) & (c == NC - 1)

        @pl.when(first_step)
        def _():
            _gather_start(gin, gout, send_sems, recv_sems)

        @pl.when(c == 0)
        def _():
            state[...] = jnp.zeros_like(state)

        st_ref[...] = state[...]
        cm = _ssd_common(dtc_ref, dtr_ref, bc_ref, br_ref, ac_ref, ar_ref, c)
        Bm = b_ref[...]
        Cm = c_ref[...]
        cb = lax.dot_general(Cm.astype(BF16), Bm.astype(BF16), (((1,), (1,)), ((), ())), preferred_element_type=F32)
        bt = Bm.T.astype(BF16)
        lane_lo = lax.broadcasted_iota(jnp.int32, (1, HP), 1) < P
        csc, csr, dtc_v = cm["csc"], cm["csr"], cm["dtc"]
        ecs = jnp.exp(csc)
        cs_end = csc[Q - 1 : Q, :]
        wdec = jnp.exp(cs_end - csc)
        eend = jnp.exp(cs_end)
        for pp in range(E // 2):
            h0 = 2 * pp
            sl = slice(pp * HP, (pp + 1) * HP)
            xp = xs_ref[:, sl]
            xdt = xp * _pair_expand(dtc_v, h0, lane_lo)
            yacc = xp * _pair_expand(dk_ref[...], h0, lane_lo)
            for e in range(2):
                h = h0 + e
                lm = jnp.where(cm["tril"], jnp.exp(jnp.minimum(csc[:, h : h + 1] - csr[h : h + 1, :], 0.0)), 0.0)
                m = (cb * lm).astype(BF16)
                xm = jnp.where(lane_lo if e == 0 else jnp.logical_not(lane_lo), xdt, 0.0).astype(BF16)
                yacc = yacc + jnp.dot(m, xm, preferred_element_type=F32)
            stp = state[:, sl]
            yoff = jnp.dot(Cm.astype(BF16), stp.astype(BF16), preferred_element_type=F32)
            y_ref[:, sl] = yacc + yoff * _pair_expand(ecs, h0, lane_lo)
            xw = (xdt * _pair_expand(wdec, h0, lane_lo)).astype(BF16)
            state[:, sl] = stp * _pair_expand(eend, h0, lane_lo) + jnp.dot(bt, xw, preferred_element_type=F32)

        @pl.when(last_step)
        def _():
            _gather_finish(gin, gout, send_sems, recv_sems)

    xs, bb, cc, dtcs, dtrs, pc, pr = _ssd_specs()
    outs = pl.pallas_call(
        body,
        name=name,
        grid=(G, NC),
        in_specs=[xs, bb, cc, dtcs, dtrs, pc, pr, pc, pr, pc] + [ANY] * nb,
        out_specs=[xs, pl.BlockSpec((None, None, N, GW), lambda g, c: (c, g, 0, 0))] + [ANY] * nb,
        out_shape=[jax.ShapeDtypeStruct((LP, DI), F32), jax.ShapeDtypeStruct((NC, G, N, GW), F32)] + _gather_shapes(gather),
        scratch_shapes=[pltpu.VMEM((N, GW), F32)] + _gather_sems(nb),
        compiler_params=_cparams(("arbitrary", "arbitrary")),
    )(xbc, xbc, xbc, dtc, dtr, bias_c, bias_r, alog_c, alog_r, dskip_c, *gather)
    return outs[0], outs[1], _gather_own(outs[2:], gather)


def _ssd_bwd(xbc, dy, states, dtc, dtr, bias_c, bias_r, alog_c, alog_r, dskip_c, scatter, name):
    nb = len(scatter)

    def body(*refs):
        xs_ref, b_ref, c_ref, dy_ref, st_ref, dtc_ref, dtr_ref, bc_ref, br_ref, ac_ref, ar_ref, dk_ref = refs[:12]
        sin = refs[12 : 12 + nb]
        dx_ref, db_ref, dc_ref, ddt_ref, dbias_ref, dalog_ref, ddk_ref = refs[12 + nb : 19 + nb]
        sout = refs[19 + nb : 19 + 2 * nb]
        dstate, send_sems, recv_sems = refs[19 + 2 * nb :]
        ci = pl.program_id(1)
        c = NC - 1 - ci

        @pl.when((pl.program_id(0) == 0) & (ci == 0))
        def _():
            _scatter_start(sin, sout, send_sems, recv_sems)

        @pl.when(ci == 0)
        def _():
            dstate[...] = jnp.zeros_like(dstate)
            dbias_ref[...] = jnp.zeros_like(dbias_ref)
            dalog_ref[...] = jnp.zeros_like(dalog_ref)
            ddk_ref[...] = jnp.zeros_like(ddk_ref)

        cm = _ssd_common(dtc_ref, dtr_ref, bc_ref, br_ref, ac_ref, ar_ref, c)
        Bm = b_ref[...]
        Cm = c_ref[...]
        Bb = Bm.astype(BF16)
        Cb = Cm.astype(BF16)
        nt = (((1,), (1,)), ((), ()))
        cb = lax.dot_general(Cb, Bb, nt, preferred_element_type=F32)
        ct = Cm.T.astype(BF16)
        lane_lo = lax.broadcasted_iota(jnp.int32, (1, HP), 1) < P
        lane_id = lax.broadcasted_iota(jnp.int32, (1, LANES), 1)
        csc, csr, dtc_v, a_c = cm["csc"], cm["csr"], cm["dtc"], cm["a_c"]
        triu = cm["si"] >= cm["li"]
        ecs = jnp.exp(csc)
        cs_end = csc[Q - 1 : Q, :]
        wdec = jnp.exp(cs_end - csc)
        eend = jnp.exp(cs_end)
        dcb = jnp.zeros((Q, Q), F32)
        dcs = jnp.zeros((Q, LANES), F32)
        dcs_end = jnp.zeros((1, LANES), F32)
        ddt = jnp.zeros((Q, LANES), F32)
        ddk = jnp.zeros((1, LANES), F32)
        dB = jnp.zeros((Q, N), F32)
        dC = jnp.zeros((Q, N), F32)
        for pp in range(E // 2):
            h0 = 2 * pp
            sl = slice(pp * HP, (pp + 1) * HP)
            xp = xs_ref[:, sl]
            dyp = dy_ref[:, sl]
            dtx = _pair_expand(dtc_v, h0, lane_lo)
            xdt = xp * dtx
            dxdt = jnp.zeros((Q, HP), F32)
            stp = st_ref[:, sl]
            stb = stp.astype(BF16)
            dsn = dstate[:, sl]
            dsnb = dsn.astype(BF16)
            ecsx = _pair_expand(ecs, h0, lane_lo)
            wdx = _pair_expand(wdec, h0, lane_lo)
            cs_ = jnp.dot(Cb, stb, preferred_element_type=F32)
            yo = cs_ * ecsx
            dyo = dyp * ecsx
            dyob = dyo.astype(BF16)
            dC = dC + lax.dot_general(dyob, stb, nt, preferred_element_type=F32)
            ds_from_y = jnp.dot(ct, dyob, preferred_element_type=F32)
            xw = xdt * wdx
            dB = dB + lax.dot_general(xw.astype(BF16), dsnb, nt, preferred_element_type=F32)
            dxw = jnp.dot(Bb, dsnb, preferred_element_type=F32)
            dxdt = dxdt + dxw * wdx
            w2 = dxw * xw
            rs = jnp.sum(dsn * stp, axis=0, keepdims=True) * _pair_expand(eend, h0, lane_lo)
            dstate[:, sl] = dsn * _pair_expand(eend, h0, lane_lo) + ds_from_y
            dyx = dyp * xp
            yd = jnp.zeros((Q, HP), F32)
            dxd = jnp.zeros((Q, HP), F32)
            for e in range(2):
                h = h0 + e
                msk = lane_lo if e == 0 else jnp.logical_not(lane_lo)
                col = csc[:, h : h + 1]
                row = csr[h : h + 1, :]
                lm = jnp.where(cm["tril"], jnp.exp(jnp.minimum(col - row, 0.0)), 0.0)
                dye = jnp.where(msk, dyp, 0.0).astype(BF16)
                xde = jnp.where(msk, xdt, 0.0).astype(BF16)
                gm = lax.dot_general(dye, xde, nt, preferred_element_type=F32)
                dcb = dcb + gm * lm
                mb = (cb * lm).astype(BF16)
                yd = yd + jnp.dot(mb, xde, preferred_element_type=F32)
                dxd = dxd + lax.dot_general(mb, dye, (((0,), (0,)), ((), ())), preferred_element_type=F32)
            dxdt = dxdt + dxd
            tt = dyp * yo - w2 + dyp.astype(BF16).astype(F32) * yd - xdt.astype(BF16).astype(F32) * dxd
            dx_ref[:, sl] = dxdt * dtx + dyp * _pair_expand(dk_ref[...], h0, lane_lo)
            dxx = dxdt * xp
            for e in range(2):
                h = h0 + e
                msk = lane_lo if e == 0 else jnp.logical_not(lane_lo)
                oh = (lane_id == h).astype(F32)
                dcs = dcs + jnp.sum(jnp.where(msk, tt, 0.0), axis=1, keepdims=True) * oh
                dcs_end = dcs_end + (_sum_all(jnp.where(msk, w2, 0.0)) + _sum_all(jnp.where(msk, rs, 0.0))) * oh
                ddk = ddk + _sum_all(jnp.where(msk, dyx, 0.0)) * oh
                ddt = ddt + jnp.sum(jnp.where(msk, dxx, 0.0), axis=1, keepdims=True) * oh
        dC = dC + jnp.dot(dcb.astype(BF16), Bb, preferred_element_type=F32)
        dB = dB + lax.dot_general(dcb.astype(BF16), Cb, (((0,), (0,)), ((), ())), preferred_element_type=F32)
        db_ref[...] = dB
        dc_ref[...] = dC
        last = (lax.broadcasted_iota(jnp.int32, (Q, 1), 0) == Q - 1).astype(F32)
        dcs = dcs + last * dcs_end
        dda = jnp.dot(triu.astype(F32), dcs, precision=HI, preferred_element_type=F32)
        ddt = ddt + dda * a_c
        da = jnp.sum(dda * dtc_v, axis=0, keepdims=True)
        draw = jnp.where(cm["rows"] >= PF, ddt * _sigmoid(cm["prec"]), 0.0)
        ddt_ref[...] = draw
        dbias_ref[...] += jnp.sum(draw, axis=0, keepdims=True)
        dalog_ref[...] += da * a_c
        ddk_ref[...] += ddk

        @pl.when((pl.program_id(0) == G - 1) & (ci == NC - 1))
        def _():
            _scatter_finish(sin, sout, send_sems, recv_sems)

    xs, bb, cc, dtcs, dtrs, pc, pr = _ssd_specs()

    def rev(spec_fn):
        return lambda g, ci: spec_fn(g, NC - 1 - ci)

    def rspec(spec):
        return pl.BlockSpec(spec.block_shape, rev(spec.index_map))

    xs_r, bb_r, cc_r, dtc_r, dtr_r = rspec(xs), rspec(bb), rspec(cc), rspec(dtcs), rspec(dtrs)
    st_r = pl.BlockSpec((None, None, N, GW), lambda g, ci: (NC - 1 - ci, g, 0, 0))
    gn = pl.BlockSpec((Q, N), lambda g, ci: (NC - 1 - ci, g))
    outs = pl.pallas_call(
        body,
        name=name,
        grid=(G, NC),
        in_specs=[xs_r, bb_r, cc_r, xs_r, st_r, dtc_r, dtr_r, pc, pr, pc, pr, pc] + [ANY] * nb,
        out_specs=[xs_r, gn, gn, dtc_r, pc, pc, pc] + [ANY] * nb,
        out_shape=[
            jax.ShapeDtypeStruct((LP, DI), F32),
            jax.ShapeDtypeStruct((LP, G * N), F32),
            jax.ShapeDtypeStruct((LP, G * N), F32),
            jax.ShapeDtypeStruct((G, LP, LANES), F32),
            jax.ShapeDtypeStruct((G, 1, LANES), F32),
            jax.ShapeDtypeStruct((G, 1, LANES), F32),
            jax.ShapeDtypeStruct((G, 1, LANES), F32),
        ]
        + [jax.ShapeDtypeStruct(b.shape, b.dtype) for b in scatter],
        scratch_shapes=[pltpu.VMEM((N, GW), F32)] + _scatter_sems(nb),
        compiler_params=_cparams(("arbitrary", "arbitrary")),
    )(xbc, xbc, xbc, dy, states, dtc, dtr, bias_c, bias_r, alog_c, alog_r, dskip_c, *scatter)
    return tuple(outs[:7]) + (_scatter_own(outs[7:], scatter),)


def _split_dot(x, u):
    hi = x.astype(BF16)
    lo = (x - hi.astype(F32)).astype(BF16)
    return jnp.dot(hi, u, preferred_element_type=F32) + jnp.dot(lo, u, preferred_element_type=F32)


def _sb_block(qe, kblk, vis, a_run, u_gt):
    l = lax.dot_general(qe, kblk, (((1,), (1,)), ((), ())), preferred_element_type=F32)
    lk = jnp.minimum(-l, 0.0) - jnp.log(1.0 + jnp.exp(-jnp.abs(l)))
    lbeta = l + lk
    if vis is not None:
        lk = jnp.where(vis, lk, 0.0)
    logw = lbeta + _split_dot(lk, u_gt) + a_run
    return lbeta, lk, logw


def _descend(i, block, carry):
    def pack(n, c):
        return (n, jnp.max(jnp.maximum(c[0], c[1]))) + tuple(c)

    st = pack(jnp.int32(1), block(i, carry, True))
    st = lax.while_loop(lambda st: (st[0] < i) & (st[1] > -T_SKIP), lambda st: pack(st[0] + 1, block(i - st[0], st[2:], False)), st)
    st = lax.while_loop(lambda st: (st[0] == i) & (st[1] > -T_SKIP), lambda st: pack(st[0] + 1, block(0, st[2:], True)), st)
    return st[2:]


def _attn_fwd(q, k, v, name):
    nq = LP // TQ

    def body(q_ref, k_ref, v_ref, o_ref):
        i = pl.program_id(1)
        qv = q_ref[...]
        lane_lo = lax.broadcasted_iota(jnp.int32, (1, HP), 1) < 64
        t_idx = i * TQ + lax.broadcasted_iota(jnp.int32, (TQ, 1), 0)
        ji = lax.broadcasted_iota(jnp.int32, (TQ, TQ), 0)
        si = lax.broadcasted_iota(jnp.int32, (TQ, TQ), 1)
        u_gt = (ji > si).astype(BF16)
        qs = [jnp.where(lane_lo, qv, jnp.zeros_like(qv)), jnp.where(lane_lo, jnp.zeros_like(qv), qv)]

        def block(kb, carry, masked):
            a0, a1, acc = carry
            off = pl.multiple_of(kb * TQ, TQ)
            kblk = k_ref[pl.ds(off, TQ), :]
            vblk = v_ref[pl.ds(off, TQ), :]
            vis = None
            if masked:
                s_idx = kb * TQ + lax.broadcasted_iota(jnp.int32, (1, TQ), 1)
                vis = (s_idx < t_idx) & (s_idx >= PF)
            new_a = []
            for e, a_run in enumerate((a0, a1)):
                _, lk, logw = _sb_block(qs[e], kblk, vis, a_run, u_gt)
                w = jnp.exp(logw)
                if masked:
                    w = jnp.where(vis, w, 0.0)
                msk = lane_lo if e == 0 else jnp.logical_not(lane_lo)
                acc = acc + jnp.dot(w.astype(BF16), jnp.where(msk, vblk, jnp.zeros_like(vblk)), preferred_element_type=F32)
                new_a.append(a_run + jnp.sum(lk, axis=1, keepdims=True))
            return new_a[0], new_a[1], acc

        z1 = jnp.zeros((TQ, 1), F32)
        _, _, acc = _descend(i, block, (z1, z1, jnp.zeros((TQ, HP), F32)))
        o_ref[...] = acc

    return pl.pallas_call(
        body,
        name=name,
        grid=(D // HP, nq),
        in_specs=[
            pl.BlockSpec((TQ, HP), lambda j, i: (i, j)),
            pl.BlockSpec((LP, HP), lambda j, i: (0, j)),
            pl.BlockSpec((LP, HP), lambda j, i: (0, j)),
        ],
        out_specs=pl.BlockSpec((TQ, HP), lambda j, i: (i, j)),
        out_shape=jax.ShapeDtypeStruct((LP, D), F32),
        compiler_params=_cparams(("parallel", "arbitrary")),
    )(q, k, v)


def _attn_bwd(q, k, v, o, do, name):
    nq = LP // TQ

    def body(q_ref, k_ref, v_ref, o_ref, do_ref, dq_ref, dk_ref, dv_ref):
        i = pl.program_id(1)

        @pl.when(i == 0)
        def _():
            dk_ref[...] = jnp.zeros_like(dk_ref)
            dv_ref[...] = jnp.zeros_like(dv_ref)

        qv = q_ref[...]
        dov = do_ref[...]
        lane_lo = lax.broadcasted_iota(jnp.int32, (1, HP), 1) < 64
        t_idx = i * TQ + lax.broadcasted_iota(jnp.int32, (TQ, 1), 0)
        ji = lax.broadcasted_iota(jnp.int32, (TQ, TQ), 0)
        si = lax.broadcasted_iota(jnp.int32, (TQ, TQ), 1)
        u_gt = (ji > si).astype(BF16)
        u_ge = (ji >= si).astype(BF16)
        msks = [lane_lo, jnp.logical_not(lane_lo)]
        qs = [jnp.where(m, qv, jnp.zeros_like(qv)) for m in msks]
        dob = [jnp.where(m, dov, 0.0).astype(BF16) for m in msks]
        ov = o_ref[...]
        deltas = [jnp.sum(d.astype(F32) * ov, axis=1, keepdims=True) for d in dob]
        nt = (((1,), (1,)), ((), ()))
        tn = (((0,), (0,)), ((), ()))

        def block(kb, carry, masked):
            a0, a1, p0, p1, dq = carry
            off = pl.multiple_of(kb * TQ, TQ)
            kblk = k_ref[pl.ds(off, TQ), :]
            vblk = v_ref[pl.ds(off, TQ), :]
            vis = None
            if masked:
                s_idx = kb * TQ + lax.broadcasted_iota(jnp.int32, (1, TQ), 1)
                vis = (s_idx < t_idx) & (s_idx >= PF)
            new_a, new_p = [], []
            dk_acc = jnp.zeros((TQ, HP), F32)
            dv_acc = jnp.zeros((TQ, HP), F32)
            for e, (a_run, p_run) in enumerate(((a0, p0), (a1, p1))):
                lbeta, lk, logw = _sb_block(qs[e], kblk, vis, a_run, u_gt)
                sig = jnp.exp(lbeta)
                w = jnp.exp(logw)
                if masked:
                    w = jnp.where(vis, w, 0.0)
                wb = w.astype(BF16)
                dw = lax.dot_general(dob[e], vblk, nt, preferred_element_type=F32)
                pm = wb.astype(F32) * dw
                cum_p = deltas[e] - (_split_dot(pm, u_ge) + p_run)
                dl = pm - (pm + cum_p) * sig
                if masked:
                    dl = jnp.where(vis, dl, 0.0)
                dl = dl.astype(BF16)
                km = jnp.where(msks[e], kblk, jnp.zeros_like(kblk))
                dq = dq + jnp.dot(dl, km, preferred_element_type=F32)
                dk_acc = dk_acc + lax.dot_general(dl, qs[e], tn, preferred_element_type=F32)
                dv_acc = dv_acc + lax.dot_general(wb, dob[e], tn, preferred_element_type=F32)
                new_a.append(a_run + jnp.sum(lk, axis=1, keepdims=True))
                new_p.append(p_run + jnp.sum(pm, axis=1, keepdims=True))
            dk_ref[pl.ds(off, TQ), :] += dk_acc
            dv_ref[pl.ds(off, TQ), :] += dv_acc
            return new_a[0], new_a[1], new_p[0], new_p[1], dq

        z1 = jnp.zeros((TQ, 1), F32)
        carry = _descend(i, block, (z1, z1, z1, z1, jnp.zeros((TQ, HP), F32)))
        dq_ref[...] = carry[4]

    blk = pl.BlockSpec((TQ, HP), lambda j, i: (i, j))
    full = pl.BlockSpec((LP, HP), lambda j, i: (0, j))
    big = jax.ShapeDtypeStruct((LP, D), F32)
    return pl.pallas_call(
        body,
        name=name,
        grid=(D // HP, nq),
        in_specs=[blk, full, full, blk, blk],
        out_specs=[blk, full, full],
        out_shape=[big, big, big],
        compiler_params=_cparams(("parallel", "arbitrary")),
    )(q, k, v, o, do)


ROW_TILES = (2048, 1024, 512, 256, 128, 64, 32, 16, 8)


def _row_tile(rows, cols, budget):
    if rows % 8:
        return rows
    return _pick(rows, tuple(t for t in ROW_TILES if t * cols <= budget) or (8,))


def _adamw(w, g, m, v, name):
    R, C = w.shape
    tr = _row_tile(R, C, 128 * 1024)

    def body(w_ref, g_ref, m_ref, v_ref, g_out, d_out, m_out, v_out):
        g = g_ref[...]
        mn = ADAM_B1 * m_ref[...] + (1.0 - ADAM_B1) * g
        vn = ADAM_B2 * v_ref[...] + (1.0 - ADAM_B2) * (g * g)
        mh = mn / (1.0 - ADAM_B1**ADAM_STEP)
        vh = vn / (1.0 - ADAM_B2**ADAM_STEP)
        g_out[...] = g
        d_out[...] = -ADAM_LR * (mh / (jnp.sqrt(vh) + ADAM_EPS) + ADAM_WD * w_ref[...])
        m_out[...] = mn
        v_out[...] = vn

    blk = pl.BlockSpec((tr, C), lambda i: (i, 0))
    sh = jax.ShapeDtypeStruct((R, C), F32)
    return pl.pallas_call(
        body,
        name=name,
        grid=(R // tr,),
        in_specs=[blk] * 4,
        out_specs=[blk] * 4,
        out_shape=[sh] * 4,
        compiler_params=_cparams(("parallel",)),
    )(w, g, m, v)


def _sum4(buf, name):
    _, R, C = buf.shape
    tr = _row_tile(R, C, 128 * 1024)

    def body(b_ref, o_ref):
        acc = b_ref[0].astype(F32)
        for s in range(1, NCHIP):
            acc = acc + b_ref[s].astype(F32)
        o_ref[...] = acc

    return pl.pallas_call(
        body,
        name=name,
        grid=(R // tr,),
        in_specs=[pl.BlockSpec((NCHIP, tr, C), lambda i: (0, i, 0))],
        out_specs=pl.BlockSpec((tr, C), lambda i: (i, 0)),
        out_shape=jax.ShapeDtypeStruct((R, C), F32),
        compiler_params=_cparams(("parallel",)),
    )(buf)


def _add2(a, b, name):
    S, R, C = a.shape
    tr = _row_tile(R, C, 256 * 1024)

    def body(a_ref, b_ref, o_ref):
        o_ref[...] = (a_ref[...].astype(F32) + b_ref[...].astype(F32)).astype(o_ref.dtype)

    blk = pl.BlockSpec((None, tr, C), lambda s, i: (s, i, 0))
    return pl.pallas_call(
        body,
        name=name,
        grid=(S, R // tr),
        in_specs=[blk, blk],
        out_specs=blk,
        out_shape=jax.ShapeDtypeStruct(a.shape, a.dtype),
        compiler_params=_cparams(("parallel", "parallel")),
    )(a, b)


ANY = pl.BlockSpec(memory_space=pl.ANY)


def _mesh_place():
    x, y, c = lax.axis_index("x"), lax.axis_index("y"), lax.axis_index("c")
    return x, y, c, 2 * x + y, [(1 - x, y), (x, 1 - y), (1 - x, 1 - y)]


def _gather_chips(bufs, name):
    nb = len(bufs)

    def body(*refs):
        ins = refs[:nb]
        outs = refs[nb : 2 * nb]
        send_sems, recv_sems = refs[2 * nb :]
        _gather_start(ins, outs, send_sems, recv_sems)
        _gather_finish(ins, outs, send_sems, recv_sems)

    outs = pl.pallas_call(
        body,
        name=name,
        in_specs=[ANY] * nb,
        out_specs=[ANY] * nb,
        out_shape=_gather_shapes(bufs),
        scratch_shapes=_gather_sems(nb),
    )(*bufs)
    return _gather_own(outs, bufs)


def _gather_shapes(bufs):
    return [jax.ShapeDtypeStruct((NCHIP,) + tuple(b.shape), b.dtype) for b in bufs]


def _gather_sems(nb):
    return [pltpu.SemaphoreType.DMA((6 * nb,)), pltpu.SemaphoreType.DMA((6 * nb,))]


def _gather_own(outs, bufs):
    me = 2 * lax.axis_index("x") + lax.axis_index("y")
    return [lax.dynamic_update_slice(o, b[None], (me, 0, 0)) for o, b in zip(outs, bufs)]


def _gather_copy(outs, send_sems, recv_sems, k, b, src, slot, hc, to):
    nb = len(outs)
    hr = outs[b].shape[1] // 2
    return pltpu.make_async_remote_copy(
        src_ref=src, dst_ref=outs[b].at[slot, pl.ds(hc * hr, hr)], send_sem=send_sems.at[k * nb + b],
        recv_sem=recv_sems.at[k * nb + b], device_id=to, device_id_type=MESH)


def _gather_start(ins, outs, send_sems, recv_sems):
    x, y, c, me, peers = _mesh_place()
    for k, (px, py) in enumerate(peers):
        for b in range(len(ins)):
            hr = ins[b].shape[0] // 2
            _gather_copy(outs, send_sems, recv_sems, k, b, ins[b].at[pl.ds(c * hr, hr)], me, c, (px, py, c)).start()


def _gather_finish(ins, outs, send_sems, recv_sems):
    x, y, c, me, peers = _mesh_place()
    nb = len(ins)
    sent = []
    for k, (px, py) in enumerate(peers):
        for b in range(nb):
            hr = ins[b].shape[0] // 2
            slot = 2 * px + py
            landed = outs[b].at[slot, pl.ds(c * hr, hr)]
            _gather_copy(outs, send_sems, recv_sems, k, b, landed, slot, c, (px, py, c)).wait_recv()
            cp = _gather_copy(outs, send_sems, recv_sems, 3 + k, b, landed, slot, c, (x, y, 1 - c))
            cp.start()
            sent.append(cp)
            sent.append(_gather_copy(outs, send_sems, recv_sems, k, b, ins[b].at[pl.ds(c * hr, hr)], me, c, (px, py, c)))
    for k, (px, py) in enumerate(peers):
        for b in range(nb):
            hr = ins[b].shape[0] // 2
            slot = 2 * px + py
            theirs = outs[b].at[slot, pl.ds((1 - c) * hr, hr)]
            _gather_copy(outs, send_sems, recv_sems, 3 + k, b, theirs, slot, 1 - c, (x, y, 1 - c)).wait_recv()
    for cp in sent:
        cp.wait_send()


def _scatter_chips(bufs, name):
    nb = len(bufs)

    def body(*refs):
        ins = refs[:nb]
        outs = refs[nb : 2 * nb]
        send_sems, recv_sems = refs[2 * nb :]
        _scatter_start(ins, outs, send_sems, recv_sems)
        _scatter_finish(ins, outs, send_sems, recv_sems)

    outs = pl.pallas_call(
        body,
        name=name,
        in_specs=[ANY] * nb,
        out_specs=[ANY] * nb,
        out_shape=[jax.ShapeDtypeStruct(b.shape, b.dtype) for b in bufs],
        scratch_shapes=_scatter_sems(nb),
    )(*bufs)
    return _scatter_own(outs, bufs)


def _scatter_sems(nb):
    return [pltpu.SemaphoreType.DMA((3 * nb,)), pltpu.SemaphoreType.DMA((3 * nb,))]


def _scatter_own(outs, bufs):
    me = 2 * lax.axis_index("x") + lax.axis_index("y")
    return [lax.dynamic_update_slice(o, lax.dynamic_slice_in_dim(b, me, 1, axis=0), (me, 0, 0)) for o, b in zip(outs, bufs)]


def _scatter_copy(ins, outs, send_sems, recv_sems, k, b, slot_from, slot_to, to):
    nb = len(ins)
    return pltpu.make_async_remote_copy(
        src_ref=ins[b].at[slot_from], dst_ref=outs[b].at[slot_to], send_sem=send_sems.at[k * nb + b],
        recv_sem=recv_sems.at[k * nb + b], device_id=to, device_id_type=MESH)


def _scatter_start(ins, outs, send_sems, recv_sems):
    x, y, c, me, peers = _mesh_place()
    for k, (px, py) in enumerate(peers):
        for b in range(len(ins)):
            _scatter_copy(ins, outs, send_sems, recv_sems, k, b, 2 * px + py, me, (px, py, c)).start()


def _scatter_finish(ins, outs, send_sems, recv_sems):
    x, y, c, me, peers = _mesh_place()
    for k, (px, py) in enumerate(peers):
        for b in range(len(ins)):
            _scatter_copy(ins, outs, send_sems, recv_sems, k, b, me, 2 * px + py, (px, py, c)).wait_recv()
    for k, (px, py) in enumerate(peers):
        for b in range(len(ins)):
            _scatter_copy(ins, outs, send_sems, recv_sems, k, b, 2 * px + py, me, (px, py, c)).wait_send()


def _split_cores(bufs, name):
    nb = len(bufs)

    def body(*refs):
        ins = refs[:nb]
        theirs = refs[nb : 2 * nb]
        send_sems, recv_sems = refs[2 * nb :]
        x, y, c, _, _ = _mesh_place()
        cps = []
        for b in range(nb):
            hr = ins[b].shape[1] // 2
            cp = pltpu.make_async_remote_copy(
                src_ref=ins[b].at[:, pl.ds((1 - c) * hr, hr)], dst_ref=theirs[b], send_sem=send_sems.at[b],
                recv_sem=recv_sems.at[b], device_id=(x, y, 1 - c), device_id_type=MESH)
            cp.start()
            cps.append(cp)
        for cp in cps:
            cp.wait()

    theirs = pl.pallas_call(
        body,
        name=name,
        in_specs=[ANY] * nb,
        out_specs=[ANY] * nb,
        out_shape=[jax.ShapeDtypeStruct((b.shape[0], b.shape[1] // 2, b.shape[2]), b.dtype) for b in bufs],
        scratch_shapes=[pltpu.SemaphoreType.DMA((nb,)), pltpu.SemaphoreType.DMA((nb,))],
    )(*bufs)
    c = lax.axis_index("c")
    mine = [lax.dynamic_slice_in_dim(b, c * (b.shape[1] // 2), b.shape[1] // 2, axis=1) for b in bufs]
    return mine, theirs


def _join_cores(bufs, name):
    nb = len(bufs)

    def body(*refs):
        ins = refs[:nb]
        outs = refs[nb : 2 * nb]
        send_sems, recv_sems = refs[2 * nb :]
        x, y, c, _, _ = _mesh_place()
        cps = []
        for b in range(nb):
            hr = ins[b].shape[0]
            cp = pltpu.make_async_remote_copy(
                src_ref=ins[b], dst_ref=outs[b].at[pl.ds(c * hr, hr)], send_sem=send_sems.at[b], recv_sem=recv_sems.at[b],
                device_id=(x, y, 1 - c), device_id_type=MESH)
            cp.start()
            cps.append(cp)
        for b, cp in enumerate(cps):
            hr = ins[b].shape[0]
            cp.wait_send()
            pltpu.make_async_remote_copy(
                src_ref=ins[b], dst_ref=outs[b].at[pl.ds((1 - c) * hr, hr)], send_sem=send_sems.at[b],
                recv_sem=recv_sems.at[b], device_id=(x, y, 1 - c), device_id_type=MESH).wait_recv()

    outs = pl.pallas_call(
        body,
        name=name,
        in_specs=[ANY] * nb,
        out_specs=[ANY] * nb,
        out_shape=[jax.ShapeDtypeStruct((2 * b.shape[0], b.shape[1]), b.dtype) for b in bufs],
        scratch_shapes=[pltpu.SemaphoreType.DMA((nb,)), pltpu.SemaphoreType.DMA((nb,))],
    )(*bufs)
    c = lax.axis_index("c")
    return [lax.dynamic_update_slice(o, b, (c * b.shape[0], 0)) for o, b in zip(outs, bufs)]


ROW_ALIGN = 1024


def _pack(pieces, dtype):
    flat = []
    for p in pieces:
        f = p.reshape(-1).astype(dtype)
        pad = (-f.shape[0]) % LANES
        if pad:
            f = jnp.pad(f, (0, pad))
        flat.append(f)
    tot = sum(f.shape[0] for f in flat)
    pad = (-tot) % (ROW_ALIGN * LANES)
    if pad:
        flat.append(jnp.zeros((pad,), dtype))
    return jnp.concatenate(flat).reshape(-1, LANES)


def _unpack(buf, shapes):
    lead = buf.shape[:-2]
    flat = buf.reshape(lead + (-1,))
    out = []
    off = 0
    for shp in shapes:
        n = 1
        for d in shp:
            n *= d
        out.append(flat[..., off : off + n].reshape(lead + tuple(shp)))
        off += n + ((-n) % LANES)
    return out


PARAMS = (
    ("meta_tokens", 1, "small"), ("ssd_norm", 1, "small"), ("ssd_w_in", 2, "big"), ("ssd_conv_w", 2, "small"),
    ("ssd_conv_b", 1, "small"), ("ssd_dt_bias", None, "rep"), ("ssd_a_log", None, "rep"), ("ssd_d_skip", None, "rep"),
    ("ssd_gate_norm", 1, "small"), ("ssd_w_out", 1, "big"), ("kv_norm", None, "rep"), ("w_kv", 1, "big"),
    ("sb_norm", None, "rep"), ("sb_w_q", 1, "big"), ("sb_w_o", 1, "big"), ("ffn_norm", None, "rep"),
    ("ffn_w_up", 2, "big"), ("ffn_conv_w", 2, "small"), ("ffn_conv_b", None, "rep"), ("ffn_w_down", 1, "big"),
    ("final_norm", None, "rep"),
)


def _head_cols(vec):
    return jnp.pad(vec.reshape(G, 1, E), ((0, 0), (0, 0), (0, LANES - E)))


def _head_rows(vec):
    return jnp.pad(vec.reshape(G, E, 1), ((0, 0), (0, 8 - E), (0, 0)))


def kernel(x, meta_tokens, ssd_norm, ssd_w_in, ssd_conv_w, ssd_conv_b, ssd_dt_bias, ssd_a_log, ssd_d_skip, ssd_gate_norm, ssd_w_out, kv_norm, w_kv, sb_norm, sb_w_q, sb_w_o, ffn_norm, ffn_w_up, ffn_conv_w, ffn_conv_b, ffn_w_down, final_norm, loss_target, m_meta_tokens, m_ssd_norm, m_ssd_w_in, m_ssd_conv_w, m_ssd_conv_b, m_ssd_dt_bias, m_ssd_a_log, m_ssd_d_skip, m_ssd_gate_norm, m_ssd_w_out, m_kv_norm, m_w_kv, m_sb_norm, m_sb_w_q, m_sb_w_o, m_ffn_norm, m_ffn_w_up, m_ffn_conv_w, m_ffn_conv_b, m_ffn_w_down, m_final_norm, v_meta_tokens, v_ssd_norm, v_ssd_w_in, v_ssd_conv_w, v_ssd_conv_b, v_ssd_dt_bias, v_ssd_a_log, v_ssd_d_skip, v_ssd_gate_norm, v_ssd_w_out, v_kv_norm, v_w_kv, v_sb_norm, v_sb_w_q, v_sb_w_o, v_ffn_norm, v_ffn_w_up, v_ffn_conv_w, v_ffn_conv_b, v_ffn_w_down, v_final_norm):
    local = dict(meta_tokens=meta_tokens, ssd_norm=ssd_norm, ssd_w_in=ssd_w_in, ssd_conv_w=ssd_conv_w, ssd_conv_b=ssd_conv_b, ssd_dt_bias=ssd_dt_bias, ssd_a_log=ssd_a_log, ssd_d_skip=ssd_d_skip, ssd_gate_norm=ssd_gate_norm, ssd_w_out=ssd_w_out, kv_norm=kv_norm, w_kv=w_kv, sb_norm=sb_norm, sb_w_q=sb_w_q, sb_w_o=sb_w_o, ffn_norm=ffn_norm, ffn_w_up=ffn_w_up, ffn_conv_w=ffn_conv_w, ffn_conv_b=ffn_conv_b, ffn_w_down=ffn_w_down, final_norm=final_norm)
    mom_m = dict(meta_tokens=m_meta_tokens, ssd_norm=m_ssd_norm, ssd_w_in=m_ssd_w_in, ssd_conv_w=m_ssd_conv_w, ssd_conv_b=m_ssd_conv_b, ssd_dt_bias=m_ssd_dt_bias, ssd_a_log=m_ssd_a_log, ssd_d_skip=m_ssd_d_skip, ssd_gate_norm=m_ssd_gate_norm, ssd_w_out=m_ssd_w_out, kv_norm=m_kv_norm, w_kv=m_w_kv, sb_norm=m_sb_norm, sb_w_q=m_sb_w_q, sb_w_o=m_sb_w_o, ffn_norm=m_ffn_norm, ffn_w_up=m_ffn_w_up, ffn_conv_w=m_ffn_conv_w, ffn_conv_b=m_ffn_conv_b, ffn_w_down=m_ffn_w_down, final_norm=m_final_norm)
    mom_v = dict(meta_tokens=v_meta_tokens, ssd_norm=v_ssd_norm, ssd_w_in=v_ssd_w_in, ssd_conv_w=v_ssd_conv_w, ssd_conv_b=v_ssd_conv_b, ssd_dt_bias=v_ssd_dt_bias, ssd_a_log=v_ssd_a_log, ssd_d_skip=v_ssd_d_skip, ssd_gate_norm=v_ssd_gate_norm, ssd_w_out=v_ssd_w_out, kv_norm=v_kv_norm, w_kv=v_w_kv, sb_norm=v_sb_norm, sb_w_q=v_sb_w_q, sb_w_o=v_sb_w_o, ffn_norm=v_ffn_norm, ffn_w_up=v_ffn_w_up, ffn_conv_w=v_ffn_conv_w, ffn_conv_b=v_ffn_conv_b, ffn_w_down=v_ffn_w_down, final_norm=v_final_norm)

    big_names = [n for n, _, kind in PARAMS if kind == "big"]
    small_names = [n for n, _, kind in PARAMS if kind == "small"]
    rep_names = [n for n, _, kind in PARAMS if kind == "rep"]
    axis_of = {n: ax for n, ax, _ in PARAMS}

    def rows2(a):
        return a.reshape(-1, a.shape[-1])

    first_big, later_big = big_names[:1], big_names[1:]
    full = {}

    def assemble(names, bufs):
        for n, buf in zip(names, bufs):
            p = buf.reshape((NCHIP,) + local[n].shape)
            full[n] = jnp.concatenate([p[s] for s in range(NCHIP)], axis=axis_of[n])

    small_own = _pack([local[n] for n in small_names], F32)
    gathered = _gather_chips([rows2(local[n]).astype(BF16) for n in first_big] + [small_own], "gather_first")
    assemble(first_big, gathered[:-1])
    for n, p in zip(small_names, _unpack(gathered[-1], [local[n].shape for n in small_names])):
        full[n] = jnp.concatenate([p[s] for s in range(NCHIP)], axis=axis_of[n])
    for n in rep_names:
        full[n] = local[n]

    w_in = full["ssd_w_in"][0]
    w_z, w_xbc = w_in[:, :DI], w_in[:, DI : DI + CD]
    w_dt = jnp.pad(w_in[:, DI + CD :], ((0, 0), (0, LANES - H)))
    fcw, fcb = full["ffn_conv_w"], full["ffn_conv_b"]
    scw, scb = full["ssd_conv_w"][0], full["ssd_conv_b"]
    bias_c, bias_r = _head_cols(full["ssd_dt_bias"][0]), _head_rows(full["ssd_dt_bias"][0])
    alog_c, alog_r = _head_cols(full["ssd_a_log"][0]), _head_rows(full["ssd_a_log"][0])
    dskip_c = _head_cols(full["ssd_d_skip"][0])
    kvn = full["kv_norm"].reshape(1, D)
    fin = full["final_norm"].reshape(1, D)

    h0 = jnp.concatenate([jnp.zeros((PF, D), F32), full["meta_tokens"], x[0]], axis=0)
    (u0,) = _rms_fwd(h0, [full["ssd_norm"]], "ssd_norm_fwd")
    z = _mm(u0, w_z, name="ssd_in_z")
    xr = _mm(u0, w_xbc, name="ssd_in_xbc")
    dt_raw = _mm(u0, w_dt, name="ssd_in_dt")
    xbc, xpre = _ssd_conv_fwd(xr, scw, scb, "ssd_conv_fwd")
    dth = dt_raw[:, :H].reshape(LP, G, E)
    dtc = jnp.pad(jnp.transpose(dth, (1, 0, 2)), ((0, 0), (0, 0), (0, LANES - E)))
    dtr = jnp.pad(jnp.transpose(dth, (1, 2, 0)), ((0, 0), (0, 8 - E), (0, 0)))
    later_own = [rows2(local[n]).astype(BF16) for n in later_big]
    y, states, later_all = _ssd_fwd(xbc, dtc, dtr, bias_c, bias_r, alog_c, alog_r, dskip_c, later_own, "ssd_scan_fwd")
    assemble(later_big, later_all)
    w_out = full["ssd_w_out"][0]
    wkv = full["w_kv"]
    w_q = full["sb_w_q"][0]
    w_o = full["sb_w_o"][0]
    w_up_g = [full["ffn_w_up"][l][:, :DFF] for l in range(2)]
    w_up_v = [full["ffn_w_up"][l][:, DFF:] for l in range(2)]
    w_down = [full["ffn_w_down"][l] for l in range(2)]
    hgn = _gate_fwd(y, z, full["ssd_gate_norm"], "ssd_gate_fwd")
    h1 = _mm(hgn, w_out, add=h0, mask_rows=True, name="ssd_out")

    def ffn_fwd(h, l, tag):
        (u,) = _rms_fwd(h, [full["ffn_norm"][l : l + 1]], f"ffn{tag}_norm_fwd")
        hg = _mm(u, w_up_g[l], name=f"ffn{tag}_up_g")
        hv = _mm(u, w_up_v[l], name=f"ffn{tag}_up_v")
        act, gpre, vpre = _ffn_act_fwd(hg, hv, fcw[l][:, :DFF], fcw[l][:, DFF:], fcb[l : l + 1, :DFF], fcb[l : l + 1, DFF:], f"ffn{tag}_act_fwd")
        hn = _mm(act, w_down[l], add=h, mask_rows=True, name=f"ffn{tag}_down")
        return hn, (u, hg, hv, act, gpre, vpre)

    h2, ffn0 = ffn_fwd(h1, 0, "0")
    ukv, uq = _rms_fwd(h2, [kvn, full["sb_norm"]], "attn_norm_fwd")
    kk = _mm(ukv, wkv[:, :D], out_dtype=BF16, name="attn_k")
    vv = _mm(ukv, wkv[:, D:], out_dtype=BF16, name="attn_v")
    qq = _mm(uq, w_q, out_dtype=BF16, scale=64.0**-0.5, name="attn_q")
    o = _attn_fwd(qq, kk, vv, "attn_fwd")
    h3 = _mm(o, w_o, add=h2, mask_rows=True, name="attn_out")
    h4, ffn1 = ffn_fwd(h3, 1, "1")
    dh, g_final, loss_rows = _loss_head(h4, fin, loss_target[0], "loss_head")
    loss = lax.psum(0.5 / D * jnp.sum(loss_rows), ("x", "y", "c"))

    grads = {"final_norm": g_final.reshape(D)}

    def ffn_bwd(dh, h, l, saved, tag):
        u, hg, hv, act, gpre, vpre = saved
        da = _mm(dh, w_down[l], tb=True, name=f"ffn{tag}_down_dx")
        gw_down = _mm(act, dh, ta=True, out_dtype=BF16, name=f"ffn{tag}_down_dw")
        dhg, dhv, dwg, dwv, dbg, dbv = _ffn_act_bwd(hg, hv, gpre, vpre, da, fcw[l][:, :DFF], fcw[l][:, DFF:], f"ffn{tag}_act_bwd")
        gw_up = jnp.concatenate([_mm(u, dhg, ta=True, out_dtype=BF16, name=f"ffn{tag}_up_g_dw"), _mm(u, dhv, ta=True, out_dtype=BF16, name=f"ffn{tag}_up_v_dw")], axis=1)
        du = _mm(dhg, w_up_g[l], tb=True, name=f"ffn{tag}_up_g_dx")
        du = _mm(dhv, w_up_v[l], tb=True, add=du, name=f"ffn{tag}_up_v_dx")
        dh_new, (gn,) = _rms_bwd(dh, h, [du], [full["ffn_norm"][l : l + 1]], f"ffn{tag}_norm_bwd")
        return dh_new, gw_down, gw_up, jnp.concatenate([dwg, dwv], axis=1), jnp.concatenate([dbg, dbv], axis=1), gn

    dh, gd1, gu1, gcw1, gcb1, gn1 = ffn_bwd(dh, h3, 1, ffn1, "1")
    do = _mm(dh, w_o, tb=True, name="attn_out_dx")
    grads["sb_w_o"] = _mm(o, dh, ta=True, out_dtype=BF16, name="attn_out_dw")[None]
    dq, dk, dv = _attn_bwd(qq, kk, vv, o, do, "attn_bwd")
    grads["sb_w_q"] = _mm(uq, dq, ta=True, out_dtype=BF16, scale=64.0**-0.5, name="attn_q_dw")[None]
    grads["w_kv"] = jnp.concatenate([_mm(ukv, dk, ta=True, out_dtype=BF16, name="attn_k_dw"), _mm(ukv, dv, ta=True, out_dtype=BF16, name="attn_v_dw")], axis=1)
    duq = _mm(dq, w_q, tb=True, scale=64.0**-0.5, name="attn_q_dx")
    dukv = _mm(dk, wkv[:, :D], tb=True, name="attn_k_dx")
    dukv = _mm(dv, wkv[:, D:], tb=True, add=dukv, name="attn_v_dx")
    dh, (g_kvn, g_sbn) = _rms_bwd(dh, h2, [dukv, duq], [kvn, full["sb_norm"]], "attn_norm_bwd")
    grads["kv_norm"] = g_kvn.reshape(D)
    grads["sb_norm"] = g_sbn
    dh, gd0, gu0, gcw0, gcb0, gn0 = ffn_bwd(dh, h1, 0, ffn0, "0")
    grads["ffn_w_down"] = jnp.stack([gd0, gd1])
    grads["ffn_w_up"] = jnp.stack([gu0, gu1])
    grads["ffn_conv_w"] = jnp.stack([gcw0, gcw1])
    grads["ffn_conv_b"] = jnp.concatenate([gcb0, gcb1], axis=0)
    grads["ffn_norm"] = jnp.concatenate([gn0, gn1], axis=0)
    dhgn = _mm(dh, w_out, tb=True, name="ssd_out_dx")
    grads["ssd_w_out"] = _mm(hgn, dh, ta=True, out_dtype=BF16, name="ssd_out_dw")[None]
    dy, dz, g_gate = _gate_bwd(dhgn, y, z, full["ssd_gate_norm"], "ssd_gate_bwd")
    grads["ssd_gate_norm"] = g_gate
    def slots(n):
        return jnp.stack([rows2(p) for p in jnp.split(grads[n], NCHIP, axis=axis_of[n])])

    mine, theirs = _split_cores([slots(n) for n in later_big], "split_cores_a")
    pair = [_add2(a, b, f"pair_sum_a{i}") for i, (a, b) in enumerate(zip(mine, theirs))]
    dxs, dB, dC, ddt_raw, g_bias, g_alog, g_dskip, got_a = _ssd_bwd(
        xbc, dy, states, dtc, dtr, bias_c, bias_r, alog_c, alog_r, dskip_c, pair, "ssd_scan_bwd")
    grads["ssd_dt_bias"] = g_bias[:, 0, :E].reshape(1, H)
    grads["ssd_a_log"] = g_alog[:, 0, :E].reshape(1, H)
    grads["ssd_d_skip"] = g_dskip[:, 0, :E].reshape(1, H)
    dxr, g_scw, g_scb = _ssd_conv_bwd(xr, xpre, jnp.concatenate([dxs, dB, dC], axis=1), scw, "ssd_conv_bwd")
    grads["ssd_conv_w"] = g_scw[None]
    grads["ssd_conv_b"] = g_scb
    ddt = jnp.pad(jnp.transpose(ddt_raw[:, :, :E], (1, 0, 2)).reshape(LP, H), ((0, 0), (0, LANES - H)))
    grads["ssd_w_in"] = jnp.concatenate(
        [_mm(u0, dz, ta=True, out_dtype=BF16, name="ssd_in_z_dw"), _mm(u0, dxr, ta=True, out_dtype=BF16, name="ssd_in_xbc_dw"), _mm(u0, ddt, ta=True, out_dtype=BF16, name="ssd_in_dt_dw")[:, :H]], axis=1)[None]
    du = _mm(dz, w_z, tb=True, name="ssd_in_z_dx")
    du = _mm(dxr, w_xbc, tb=True, add=du, name="ssd_in_xbc_dx")
    du = _mm(ddt, w_dt, tb=True, add=du, name="ssd_in_dt_dx")
    dh, (g_ssdn,) = _rms_bwd(dh, h0, [du], [full["ssd_norm"]], "ssd_norm_bwd")
    grads["ssd_norm"] = g_ssdn
    grads["meta_tokens"] = dh[PF : PF + N_META]
    grad_x = dh[PF + N_META :][None]

    def shard_pieces(names, s):
        out = []
        for n in names:
            ax = axis_of[n]
            out.append(grads[n] if ax is None else jnp.split(grads[n], NCHIP, axis=ax)[s])
        return out

    bufs = [slots(n) for n in first_big]
    bufs.append(jnp.stack([_pack(shard_pieces(small_names + rep_names, s), F32) for s in range(NCHIP)]))
    mine, theirs = _split_cores(bufs, "split_cores_b")
    pair = [_add2(a, b, f"pair_sum_b{i}") for i, (a, b) in enumerate(zip(mine, theirs))]
    got_b = _scatter_chips(pair, "scatter_grads")
    got = got_b[:-1] + got_a + got_b[-1:]
    sums = [_sum4(b, f"sum_chips_{i}") for i, b in enumerate(got)]
    gsum = _join_cores(sums, "join_cores")

    def rows(a):
        f = a.reshape(-1)
        pad = (-f.shape[0]) % LANES
        if pad:
            f = jnp.pad(f, (0, pad))
        return f.reshape(-1, LANES)

    order = [n for n, _, _ in PARAMS]
    res = {}
    for n, g2 in zip(big_names, gsum[:-1]):
        outs = _adamw(rows2(local[n]), g2, rows2(mom_m[n]), rows2(mom_v[n]), f"adamw_{n}")
        res[n] = [o_.reshape(local[n].shape) for o_ in outs]
    rest = small_names + rep_names
    for n, g1 in zip(rest, _unpack(gsum[-1], [local[n].shape for n in rest])):
        shp = local[n].shape
        cnt = 1
        for d in shp:
            cnt *= d
        outs = _adamw(rows(local[n]), rows(g1), rows(mom_m[n]), rows(mom_v[n]), f"adamw_{n}")
        res[n] = [o_.reshape(-1)[:cnt].reshape(shp) for o_ in outs]
    return (loss, grad_x, *[res[n][0] for n in order], *[res[n][1] for n in order], *[res[n][2] for n in order], *[res[n][3] for n in order])
```

```python
import functools

import jax
import jax.numpy as jnp
from jax import lax
from jax.experimental import pallas as pl
from jax.experimental.pallas import tpu as pltpu

D = 1024
SEQ = 8192
N_META = 16
EPS = 1e-6
P = 64
G = 4
N = 128
CONVW = 4
Q = 256
FC = 3
DFF = 256 * ((8 * D // 3 + 255) // 256)
DI = 2 * D
H = DI // P
E = H // G
GW = E * P
CD = DI + 2 * G * N
IN = DI + CD + H
SBH = D // 64
HP = 128
LANES = 128
PF = Q - N_META
LP = PF + N_META + SEQ
NC = LP // Q
TQ = 256
TK = 2 * TQ
NCHIP = 4
ADAM_LR, ADAM_B1, ADAM_B2, ADAM_EPS, ADAM_WD, ADAM_STEP = 0.001, 0.9, 0.999, 1e-08, 0.01, 10

F32 = jnp.float32
BF16 = jnp.bfloat16
HI = lax.Precision.HIGHEST
MESH = pl.DeviceIdType.MESH
VMEM_LIMIT = 48 * 1024 * 1024
MM_MAX_K = 3072
T_SKIP = 110.0


def _pick(n, cands):
    for c in cands:
        if n % c == 0:
            return c
    raise ValueError((n, cands))


def _cparams(sem):
    return pltpu.CompilerParams(dimension_semantics=sem, vmem_limit_bytes=VMEM_LIMIT)


def _valid_rows(block, rows):
    r = block * rows + lax.broadcasted_iota(jnp.int32, (rows, 1), 0)
    return r >= PF


def _sigmoid(x):
    return 1.0 / (1.0 + jnp.exp(-x))


def _softplus(x):
    return jnp.maximum(x, 0.0) + jnp.log(1.0 + jnp.exp(-jnp.abs(x)))


def _sum_all(x):
    return jnp.sum(jnp.sum(x, axis=1, keepdims=True), axis=0, keepdims=True)


def _dsilu(x):
    s = _sigmoid(x)
    return s * (1.0 + x * (1.0 - s))


def _mm(a, b, *, ta=False, tb=False, out_dtype=F32, add=None, mask_rows=False, scale=None, name):
    if ta:
        K, M = a.shape
    else:
        M, K = a.shape
    if tb:
        Nn, K2 = b.shape
    else:
        K2, Nn = b.shape
    assert K == K2, (a.shape, b.shape, ta, tb)
    tn = _pick(Nn, (1408, 1024, 768, 512, 256, 128))
    if ta:
        tm = _pick(M, (1408, 1024, 768, 512, 256, 128))
        tk = _pick(K, (768, 512, 256))
    else:
        tm = _pick(M, (768, 256))
        tk = K if K <= MM_MAX_K else _pick(K, (1024, 768, 512, 256, 128))
    nk = K // tk
    dims = (((0 if ta else 1,), (1 if tb else 0,)), ((), ()))

    def body(*refs):
        a_ref, b_ref = refs[0], refs[1]
        add_ref = refs[2] if add is not None else None
        o_ref = refs[3] if add is not None else refs[2]
        acc = refs[-1] if nk > 1 else None

        def finish(r):
            if scale is not None:
                r = r * scale
            if mask_rows:
                r = jnp.where(_valid_rows(pl.program_id(0), tm), r, 0.0)
            if add_ref is not None:
                r = r + add_ref[...]
            o_ref[...] = r.astype(out_dtype)

        part = lax.dot_general(a_ref[...].astype(BF16), b_ref[...].astype(BF16), dims, preferred_element_type=F32)
        if nk == 1:
            finish(part)
        else:
            k = pl.program_id(2)

            @pl.when(k == 0)
            def _():
                acc[...] = part

            @pl.when(k > 0)
            def _():
                acc[...] += part

            @pl.when(k == nk - 1)
            def _():
                finish(acc[...])

    a_spec = pl.BlockSpec((tk, tm), lambda i, j, k: (k, i)) if ta else pl.BlockSpec((tm, tk), lambda i, j, k: (i, k))
    b_spec = pl.BlockSpec((tn, tk), lambda i, j, k: (j, k)) if tb else pl.BlockSpec((tk, tn), lambda i, j, k: (k, j))
    o_spec = pl.BlockSpec((tm, tn), lambda i, j, k: (i, j))
    in_specs = [a_spec, b_spec] + ([o_spec] if add is not None else [])
    args = (a, b) + ((add,) if add is not None else ())
    return pl.pallas_call(
        body,
        name=name,
        grid=(M // tm, Nn // tn, nk),
        in_specs=in_specs,
        out_specs=o_spec,
        out_shape=jax.ShapeDtypeStruct((M, Nn), out_dtype),
        scratch_shapes=[pltpu.VMEM((tm, tn), F32)] if nk > 1 else [],
        compiler_params=_cparams(("parallel", "parallel", "arbitrary")),
    )(*args)


def _rms_fwd(h, gains, name):
    tr = _pick(LP, (768, 256))
    ng = len(gains)

    def body(*refs):
        h_ref = refs[0]
        g_refs = refs[1 : 1 + ng]
        o_refs = refs[1 + ng :]
        x = h_ref[...]
        xh = x * lax.rsqrt(jnp.mean(x * x, axis=-1, keepdims=True) + EPS)
        for g_ref, o_ref in zip(g_refs, o_refs):
            o_ref[...] = (xh * g_ref[...]).astype(BF16)

    row = pl.BlockSpec((tr, D), lambda i: (i, 0))
    vec = pl.BlockSpec((1, D), lambda i: (0, 0))
    outs = pl.pallas_call(
        body,
        name=name,
        grid=(LP // tr,),
        in_specs=[row] + [vec] * ng,
        out_specs=[row] * ng,
        out_shape=[jax.ShapeDtypeStruct((LP, D), BF16)] * ng,
        compiler_params=_cparams(("parallel",)),
    )(h, *gains)
    return outs


def _rms_bwd(dh_in, h, dus, gains, name):
    tr = _pick(LP, (256,))
    ng = len(gains)

    def body(*refs):
        dh_ref, h_ref = refs[0], refs[1]
        du_refs = refs[2 : 2 + ng]
        g_refs = refs[2 + ng : 2 + 2 * ng]
        o_ref = refs[2 + 2 * ng]
        dg_refs = refs[3 + 2 * ng :]
        i = pl.program_id(0)
        x = h_ref[...]
        r = lax.rsqrt(jnp.mean(x * x, axis=-1, keepdims=True) + EPS)
        xh = x * r
        tot = dh_ref[...]
        for du_ref, g_ref, dg_ref in zip(du_refs, g_refs, dg_refs):
            du = du_ref[...]
            dxh = du * g_ref[...]
            tot = tot + r * (dxh - xh * jnp.mean(dxh * xh, axis=-1, keepdims=True))

            @pl.when(i == 0)
            def _():
                dg_ref[...] = jnp.zeros_like(dg_ref)

            dg_ref[...] += jnp.sum(du * xh, axis=0, keepdims=True)
        o_ref[...] = jnp.where(_valid_rows(i, tr), tot, 0.0)

    row = pl.BlockSpec((tr, D), lambda i: (i, 0))
    vec = pl.BlockSpec((1, D), lambda i: (0, 0))
    outs = pl.pallas_call(
        body,
        name=name,
        grid=(LP // tr,),
        in_specs=[row, row] + [row] * ng + [vec] * ng,
        out_specs=[row] + [vec] * ng,
        out_shape=[jax.ShapeDtypeStruct((LP, D), F32)] + [jax.ShapeDtypeStruct((1, D), F32)] * ng,
        compiler_params=_cparams(("arbitrary",)),
    )(dh_in, h, *dus, *gains)
    return outs[0], outs[1:]


def _loss_head(h, gain, target, name):
    tr = Q

    def body(h_ref, g_ref, t_ref, dh_ref, dg_ref, ls_ref):
        i = pl.program_id(0)

        @pl.when(i == 0)
        def _():
            dg_ref[...] = jnp.zeros_like(dg_ref)
            ls_ref[...] = jnp.zeros_like(ls_ref)
            dh_ref[...] = jnp.zeros_like(dh_ref)

        @pl.when(i > 0)
        def _():
            x = h_ref[...]
            g = g_ref[...]
            r = lax.rsqrt(jnp.mean(x * x, axis=-1, keepdims=True) + EPS)
            xh = x * r
            e = xh * g - t_ref[...]
            ls_ref[...] += jnp.sum(e * e, axis=0, keepdims=True)
            dy = e * (1.0 / D)
            dg_ref[...] += jnp.sum(dy * xh, axis=0, keepdims=True)
            dxh = dy * g
            dh_ref[...] = r * (dxh - xh * jnp.mean(dxh * xh, axis=-1, keepdims=True))

    row = pl.BlockSpec((tr, D), lambda i: (i, 0))
    vec = pl.BlockSpec((1, D), lambda i: (0, 0))
    return pl.pallas_call(
        body,
        name=name,
        grid=(LP // tr,),
        in_specs=[row, vec, pl.BlockSpec((tr, D), lambda i: (jnp.maximum(i - 1, 0), 0))],
        out_specs=[row, vec, vec],
        out_shape=[jax.ShapeDtypeStruct((LP, D), F32), jax.ShapeDtypeStruct((1, D), F32), jax.ShapeDtypeStruct((1, D), F32)],
        compiler_params=_cparams(("arbitrary",)),
    )(h, gain, target)


HALO = 8
CONV_COLS = (1536, 1408, 768, 512, 256)


def _conv_rows(ext, w, b, width):
    n = ext.shape[0]
    acc = b + w[width - 1 : width, :] * ext[HALO:]
    for k in range(width - 1):
        acc = acc + w[k : k + 1, :] * pltpu.roll(ext, width - 1 - k, 0)[HALO:]
    return acc


def _conv_specs(tr, tn, col):
    per = tr // HALO
    last = LP // HALO - 1
    prev = pl.BlockSpec((HALO, tn), lambda j, i: (jnp.maximum(i * per - 1, 0), col(j)))
    cur = pl.BlockSpec((tr, tn), lambda j, i: (i, col(j)))
    nxt = pl.BlockSpec((HALO, tn), lambda j, i: (jnp.minimum((i + 1) * per, last), col(j)))
    return prev, cur, nxt


def _conv_bwd_core(ext, dpre, w, width, i, tr):
    rows = i * tr + lax.broadcasted_iota(jnp.int32, (tr + HALO, 1), 0)
    dpre = jnp.where((rows >= PF) & (rows < LP), dpre, 0.0)
    n = tr + HALO
    dx = w[width - 1 : width, :] * dpre[:tr]
    for k in range(width - 1):
        sh = width - 1 - k
        dx = dx + w[k : k + 1, :] * pltpu.roll(dpre, n - sh, 0)[:tr]
    dcur = dpre[:tr]
    dws = []
    for k in range(width):
        sh = width - 1 - k
        xs = ext[HALO:] if sh == 0 else pltpu.roll(ext, sh, 0)[HALO:]
        dws.append(jnp.sum(xs * dcur, axis=0, keepdims=True))
    db = jnp.sum(dcur, axis=0, keepdims=True)
    dx = jnp.where(_valid_rows(i, tr), dx, 0.0)
    return dx, dws, db


def _ssd_conv_fwd(xr, cw, cb, name):
    tr, tn = Q, _pick(CD, CONV_COLS)

    def body(p_ref, c_ref, w_ref, b_ref, o_ref, pre_ref):
        i = pl.program_id(1)
        ext = jnp.concatenate([jnp.where(i > 0, p_ref[...], 0.0), c_ref[...]], axis=0)
        pre = _conv_rows(ext, w_ref[...], b_ref[...], CONVW)
        pre_ref[...] = pre
        o_ref[...] = jnp.where(_valid_rows(i, tr), pre * _sigmoid(pre), 0.0)

    prev, cur, _ = _conv_specs(tr, tn, lambda j: j)
    return pl.pallas_call(
        body,
        name=name,
        grid=(CD // tn, LP // tr),
        in_specs=[prev, cur, pl.BlockSpec((CONVW, tn), lambda j, i: (0, j)), pl.BlockSpec((1, tn), lambda j, i: (0, j))],
        out_specs=[cur, cur],
        out_shape=[jax.ShapeDtypeStruct((LP, CD), F32)] * 2,
        compiler_params=_cparams(("parallel", "arbitrary")),
    )(xr, xr, cw, cb)


def _ssd_conv_bwd(xr, pre, dxbc, cw, name):
    tr, tn = Q, _pick(CD, CONV_COLS)

    def body(p_ref, c_ref, prc_ref, prn_ref, dc_ref, dn_ref, w_ref, dx_ref, dw_ref, db_ref):
        i = pl.program_id(1)
        ext = jnp.concatenate([jnp.where(i > 0, p_ref[...], 0.0), c_ref[...]], axis=0)
        dout = jnp.concatenate([dc_ref[...], dn_ref[...]], axis=0)
        prev_ = jnp.concatenate([prc_ref[...], prn_ref[...]], axis=0)
        dx, dws, db = _conv_bwd_core(ext, dout * _dsilu(prev_), w_ref[...], CONVW, i, tr)
        dx_ref[...] = dx

        @pl.when(i == 0)
        def _():
            dw_ref[...] = jnp.zeros_like(dw_ref)
            db_ref[...] = jnp.zeros_like(db_ref)

        for k in range(CONVW):
            dw_ref[k : k + 1, :] += dws[k]
        db_ref[...] += db

    prev, cur, nxt = _conv_specs(tr, tn, lambda j: j)
    wspec = pl.BlockSpec((CONVW, tn), lambda j, i: (0, j))
    bspec = pl.BlockSpec((1, tn), lambda j, i: (0, j))
    return pl.pallas_call(
        body,
        name=name,
        grid=(CD // tn, LP // tr),
        in_specs=[prev, cur, cur, nxt, cur, nxt, wspec],
        out_specs=[cur, wspec, bspec],
        out_shape=[jax.ShapeDtypeStruct((LP, CD), F32), jax.ShapeDtypeStruct((CONVW, CD), F32), jax.ShapeDtypeStruct((1, CD), F32)],
        compiler_params=_cparams(("parallel", "arbitrary")),
    )(xr, xr, pre, pre, dxbc, dxbc, cw)


def _ffn_act_fwd(hg, hv, cwg, cwv, cbg, cbv, name):
    tr, tn = Q, _pick(DFF, CONV_COLS)

    def body(pg, cg, pv, cv, wg, wv, bg, bv, o_ref, gate_ref, val_ref):
        i = pl.program_id(1)
        eg = jnp.concatenate([jnp.where(i > 0, pg[...], 0.0), cg[...]], axis=0)
        ev = jnp.concatenate([jnp.where(i > 0, pv[...], 0.0), cv[...]], axis=0)
        gate = _conv_rows(eg, wg[...], bg[...], FC)
        val = _conv_rows(ev, wv[...], bv[...], FC)
        gate_ref[...] = gate
        val_ref[...] = val
        o_ref[...] = (gate * _sigmoid(gate) * val).astype(BF16)

    prev, cur, _ = _conv_specs(tr, tn, lambda j: j)
    wspec = pl.BlockSpec((FC, tn), lambda j, i: (0, j))
    bspec = pl.BlockSpec((1, tn), lambda j, i: (0, j))
    return pl.pallas_call(
        body,
        name=name,
        grid=(DFF // tn, LP // tr),
        in_specs=[prev, cur, prev, cur, wspec, wspec, bspec, bspec],
        out_specs=[cur, cur, cur],
        out_shape=[jax.ShapeDtypeStruct((LP, DFF), BF16), jax.ShapeDtypeStruct((LP, DFF), F32), jax.ShapeDtypeStruct((LP, DFF), F32)],
        compiler_params=_cparams(("parallel", "arbitrary")),
    )(hg, hg, hv, hv, cwg, cwv, cbg, cbv)


def _ffn_act_bwd(hg, hv, gate_pre, val_pre, da, cwg, cwv, name):
    tr, tn = Q, _pick(DFF, CONV_COLS)

    def body(pg, cg, pv, cv, gc, gn, vc, vn, dc, dn, wg, wv, dg_ref, dv_ref, dwg, dwv, dbg, dbv):
        i = pl.program_id(1)
        eg = jnp.concatenate([jnp.where(i > 0, pg[...], 0.0), cg[...]], axis=0)
        ev = jnp.concatenate([jnp.where(i > 0, pv[...], 0.0), cv[...]], axis=0)
        dout = jnp.concatenate([dc[...], dn[...]], axis=0)
        gate = jnp.concatenate([gc[...], gn[...]], axis=0)
        val = jnp.concatenate([vc[...], vn[...]], axis=0)
        s = _sigmoid(gate)
        dxg, dwsg, dbgv = _conv_bwd_core(eg, dout * val * (s * (1.0 + gate * (1.0 - s))), wg[...], FC, i, tr)
        dxv, dwsv, dbvv = _conv_bwd_core(ev, dout * (gate * s), wv[...], FC, i, tr)
        dg_ref[...] = dxg
        dv_ref[...] = dxv

        @pl.when(i == 0)
        def _():
            dwg[...] = jnp.zeros_like(dwg)
            dwv[...] = jnp.zeros_like(dwv)
            dbg[...] = jnp.zeros_like(dbg)
            dbv[...] = jnp.zeros_like(dbv)

        for k in range(FC):
            dwg[k : k + 1, :] += dwsg[k]
            dwv[k : k + 1, :] += dwsv[k]
        dbg[...] += dbgv
        dbv[...] += dbvv

    prev, cur, nxt = _conv_specs(tr, tn, lambda j: j)
    wspec = pl.BlockSpec((FC, tn), lambda j, i: (0, j))
    bspec = pl.BlockSpec((1, tn), lambda j, i: (0, j))
    big = jax.ShapeDtypeStruct((LP, DFF), F32)
    wsh = jax.ShapeDtypeStruct((FC, DFF), F32)
    bsh = jax.ShapeDtypeStruct((1, DFF), F32)
    return pl.pallas_call(
        body,
        name=name,
        grid=(DFF // tn, LP // tr),
        in_specs=[prev, cur, prev, cur, cur, nxt, cur, nxt, cur, nxt, wspec, wspec],
        out_specs=[cur, cur, wspec, wspec, bspec, bspec],
        out_shape=[big, big, wsh, wsh, bsh, bsh],
        compiler_params=_cparams(("parallel", "arbitrary")),
    )(hg, hg, hv, hv, gate_pre, gate_pre, val_pre, val_pre, da, da, cwg, cwv)


def _gate_fwd(y, z, gg, name):
    tr = _pick(LP, (768, 256))

    def body(y_ref, z_ref, g_ref, o_ref):
        zv = z_ref[...]
        hg = y_ref[...] * zv * _sigmoid(zv)
        r = lax.rsqrt(jnp.mean(hg * hg, axis=-1, keepdims=True) + EPS)
        o_ref[...] = (hg * r * g_ref[...]).astype(BF16)

    blk = pl.BlockSpec((tr, GW), lambda i, g: (i, g))
    return pl.pallas_call(
        body,
        name=name,
        grid=(LP // tr, G),
        in_specs=[blk, blk, pl.BlockSpec((1, GW), lambda i, g: (0, g))],
        out_specs=blk,
        out_shape=jax.ShapeDtypeStruct((LP, DI), BF16),
        compiler_params=_cparams(("parallel", "parallel")),
    )(y, z, gg)


def _gate_bwd(dout, y, z, gg, name):
    tr = _pick(LP, (768, 256))

    def body(do_ref, y_ref, z_ref, g_ref, dy_ref, dz_ref, dg_ref):
        i = pl.program_id(1)
        zv = z_ref[...]
        yv = y_ref[...]
        sz = zv * _sigmoid(zv)
        hg = yv * sz
        r = lax.rsqrt(jnp.mean(hg * hg, axis=-1, keepdims=True) + EPS)
        hh = hg * r
        do = do_ref[...]
        dhh = do * g_ref[...]
        dhg = r * (dhh - hh * jnp.mean(dhh * hh, axis=-1, keepdims=True))
        dy_ref[...] = dhg * sz
        dz_ref[...] = dhg * yv * _dsilu(zv)

        @pl.when(i == 0)
        def _():
            dg_ref[...] = jnp.zeros_like(dg_ref)

        dg_ref[...] += jnp.sum(do * hh, axis=0, keepdims=True)

    blk = pl.BlockSpec((tr, GW), lambda g, i: (i, g))
    vec = pl.BlockSpec((1, GW), lambda g, i: (0, g))
    big = jax.ShapeDtypeStruct((LP, DI), F32)
    return pl.pallas_call(
        body,
        name=name,
        grid=(G, LP // tr),
        in_specs=[blk, blk, blk, vec],
        out_specs=[blk, blk, vec],
        out_shape=[big, big, jax.ShapeDtypeStruct((1, DI), F32)],
        compiler_params=_cparams(("parallel", "arbitrary")),
    )(dout, y, z, gg)


def _ssd_common(dtc_ref, dtr_ref, bc_ref, br_ref, ac_ref, ar_ref, c):
    rows = c * Q + lax.broadcasted_iota(jnp.int32, (Q, 1), 0)
    cols = c * Q + lax.broadcasted_iota(jnp.int32, (1, Q), 1)
    prec = dtc_ref[...] + bc_ref[...]
    prer = dtr_ref[...] + br_ref[...]
    dtc = jnp.where(rows >= PF, _softplus(prec), 0.0)
    dtr = jnp.where(cols >= PF, _softplus(prer), 0.0)
    a_c = -jnp.exp(ac_ref[...])
    a_r = -jnp.exp(ar_ref[...])
    li = lax.broadcasted_iota(jnp.int32, (Q, Q), 0)
    si = lax.broadcasted_iota(jnp.int32, (Q, Q), 1)
    tril = si <= li
    trif = tril.astype(F32)
    csc = jnp.dot(trif, dtc * a_c, precision=HI, preferred_element_type=F32)
    csr = lax.dot_general(dtr * a_r, trif, (((1,), (1,)), ((), ())), precision=HI, preferred_element_type=F32)
    return dict(rows=rows, prec=prec, dtc=dtc, a_c=a_c, tril=tril, trif=trif, csc=csc, csr=csr, li=li, si=si)


def _pair_expand(arr, h0, lane_lo):
    return jnp.where(lane_lo, arr[:, h0 : h0 + 1], arr[:, h0 + 1 : h0 + 2])


def _ssd_specs():
    nb = DI // N
    xs = pl.BlockSpec((Q, GW), lambda g, c: (c, g))
    bb = pl.BlockSpec((Q, N), lambda g, c: (c, nb + g))
    cc = pl.BlockSpec((Q, N), lambda g, c: (c, nb + G + g))
    dtc = pl.BlockSpec((None, Q, LANES), lambda g, c: (g, c, 0))
    dtr = pl.BlockSpec((None, 8, Q), lambda g, c: (g, 0, c))
    pc = pl.BlockSpec((None, 1, LANES), lambda g, c: (g, 0, 0))
    pr = pl.BlockSpec((None, 8, 1), lambda g, c: (g, 0, 0))
    return xs, bb, cc, dtc, dtr, pc, pr


def _ssd_fwd(xbc, dtc, dtr, bias_c, bias_r, alog_c, alog_r, dskip_c, gather, name):
    nb = len(gather)

    def body(*refs):
        xs_ref, b_ref, c_ref, dtc_ref, dtr_ref, bc_ref, br_ref, ac_ref, ar_ref, dk_ref = refs[:10]
        gin = refs[10 : 10 + nb]
        y_ref, st_ref = refs[10 + nb : 12 + nb]
        gout = refs[12 + nb : 12 + 2 * nb]
        state, send_sems, recv_sems = refs[12 + 2 * nb :]
        c = pl.program_id(1)
        first_step = (pl.program_id(0) == 0) & (c == 0)
        last_step = (pl.program_id(0) == G - 1) & (c == NC - 1)

        @pl.when(first_step)
        def _():
            _gather_start(gin, gout, send_sems, recv_sems)

        @pl.when(c == 0)
        def _():
            state[...] = jnp.zeros_like(state)

        st_ref[...] = state[...]
        cm = _ssd_common(dtc_ref, dtr_ref, bc_ref, br_ref, ac_ref, ar_ref, c)
        Bm = b_ref[...]
        Cm = c_ref[...]
        cb = lax.dot_general(Cm.astype(BF16), Bm.astype(BF16), (((1,), (1,)), ((), ())), preferred_element_type=F32)
        bt = Bm.T.astype(BF16)
        lane_lo = lax.broadcasted_iota(jnp.int32, (1, HP), 1) < P
        csc, csr, dtc_v = cm["csc"], cm["csr"], cm["dtc"]
        ecs = jnp.exp(csc)
        cs_end = csc[Q - 1 : Q, :]
        wdec = jnp.exp(cs_end - csc)
        eend = jnp.exp(cs_end)
        for pp in range(E // 2):
            h0 = 2 * pp
            sl = slice(pp * HP, (pp + 1) * HP)
            xp = xs_ref[:, sl]
            xdt = xp * _pair_expand(dtc_v, h0, lane_lo)
            yacc = xp * _pair_expand(dk_ref[...], h0, lane_lo)
            for e in range(2):
                h = h0 + e
                lm = jnp.where(cm["tril"], jnp.exp(jnp.minimum(csc[:, h : h + 1] - csr[h : h + 1, :], 0.0)), 0.0)
                m = (cb * lm).astype(BF16)
                xm = jnp.where(lane_lo if e == 0 else jnp.logical_not(lane_lo), xdt, 0.0).astype(BF16)
                yacc = yacc + jnp.dot(m, xm, preferred_element_type=F32)
            stp = state[:, sl]
            yoff = jnp.dot(Cm.astype(BF16), stp.astype(BF16), preferred_element_type=F32)
            y_ref[:, sl] = yacc + yoff * _pair_expand(ecs, h0, lane_lo)
            xw = (xdt * _pair_expand(wdec, h0, lane_lo)).astype(BF16)
            state[:, sl] = stp * _pair_expand(eend, h0, lane_lo) + jnp.dot(bt, xw, preferred_element_type=F32)

        @pl.when(last_step)
        def _():
            _gather_finish(gin, gout, send_sems, recv_sems)

    xs, bb, cc, dtcs, dtrs, pc, pr = _ssd_specs()
    outs = pl.pallas_call(
        body,
        name=name,
        grid=(G, NC),
        in_specs=[xs, bb, cc, dtcs, dtrs, pc, pr, pc, pr, pc] + [ANY] * nb,
        out_specs=[xs, pl.BlockSpec((None, None, N, GW), lambda g, c: (c, g, 0, 0))] + [ANY] * nb,
        out_shape=[jax.ShapeDtypeStruct((LP, DI), F32), jax.ShapeDtypeStruct((NC, G, N, GW), F32)] + _gather_shapes(gather),
        scratch_shapes=[pltpu.VMEM((N, GW), F32)] + _gather_sems(nb),
        compiler_params=_cparams(("arbitrary", "arbitrary")),
    )(xbc, xbc, xbc, dtc, dtr, bias_c, bias_r, alog_c, alog_r, dskip_c, *gather)
    return outs[0], outs[1], _gather_own(outs[2:], gather)


def _ssd_bwd(xbc, dy, states, dtc, dtr, bias_c, bias_r, alog_c, alog_r, dskip_c, scatter, name):
    nb = len(scatter)

    def body(*refs):
        xs_ref, b_ref, c_ref, dy_ref, st_ref, dtc_ref, dtr_ref, bc_ref, br_ref, ac_ref, ar_ref, dk_ref = refs[:12]
        sin = refs[12 : 12 + nb]
        dx_ref, db_ref, dc_ref, ddt_ref, dbias_ref, dalog_ref, ddk_ref = refs[12 + nb : 19 + nb]
        sout = refs[19 + nb : 19 + 2 * nb]
        dstate, send_sems, recv_sems = refs[19 + 2 * nb :]
        ci = pl.program_id(1)
        c = NC - 1 - ci

        @pl.when((pl.program_id(0) == 0) & (ci == 0))
        def _():
            _scatter_start(sin, sout, send_sems, recv_sems)

        @pl.when(ci == 0)
        def _():
            dstate[...] = jnp.zeros_like(dstate)
            dbias_ref[...] = jnp.zeros_like(dbias_ref)
            dalog_ref[...] = jnp.zeros_like(dalog_ref)
            ddk_ref[...] = jnp.zeros_like(ddk_ref)

        cm = _ssd_common(dtc_ref, dtr_ref, bc_ref, br_ref, ac_ref, ar_ref, c)
        Bm = b_ref[...]
        Cm = c_ref[...]
        Bb = Bm.astype(BF16)
        Cb = Cm.astype(BF16)
        nt = (((1,), (1,)), ((), ()))
        cb = lax.dot_general(Cb, Bb, nt, preferred_element_type=F32)
        ct = Cm.T.astype(BF16)
        lane_lo = lax.broadcasted_iota(jnp.int32, (1, HP), 1) < P
        lane_id = lax.broadcasted_iota(jnp.int32, (1, LANES), 1)
        csc, csr, dtc_v, a_c = cm["csc"], cm["csr"], cm["dtc"], cm["a_c"]
        triu = cm["si"] >= cm["li"]
        ecs = jnp.exp(csc)
        cs_end = csc[Q - 1 : Q, :]
        wdec = jnp.exp(cs_end - csc)
        eend = jnp.exp(cs_end)
        dcb = jnp.zeros((Q, Q), F32)
        dcs = jnp.zeros((Q, LANES), F32)
        dcs_end = jnp.zeros((1, LANES), F32)
        ddt = jnp.zeros((Q, LANES), F32)
        ddk = jnp.zeros((1, LANES), F32)
        dB = jnp.zeros((Q, N), F32)
        dC = jnp.zeros((Q, N), F32)
        for pp in range(E // 2):
            h0 = 2 * pp
            sl = slice(pp * HP, (pp + 1) * HP)
            xp = xs_ref[:, sl]
            dyp = dy_ref[:, sl]
            dtx = _pair_expand(dtc_v, h0, lane_lo)
            xdt = xp * dtx
            dxdt = jnp.zeros((Q, HP), F32)
            stp = st_ref[:, sl]
            stb = stp.astype(BF16)
            dsn = dstate[:, sl]
            dsnb = dsn.astype(BF16)
            ecsx = _pair_expand(ecs, h0, lane_lo)
            wdx = _pair_expand(wdec, h0, lane_lo)
            cs_ = jnp.dot(Cb, stb, preferred_element_type=F32)
            yo = cs_ * ecsx
            dyo = dyp * ecsx
            dyob = dyo.astype(BF16)
            dC = dC + lax.dot_general(dyob, stb, nt, preferred_element_type=F32)
            ds_from_y = jnp.dot(ct, dyob, preferred_element_type=F32)
            xw = xdt * wdx
            dB = dB + lax.dot_general(xw.astype(BF16), dsnb, nt, preferred_element_type=F32)
            dxw = jnp.dot(Bb, dsnb, preferred_element_type=F32)
            dxdt = dxdt + dxw * wdx
            w2 = dxw * xw
            rs = jnp.sum(dsn * stp, axis=0, keepdims=True) * _pair_expand(eend, h0, lane_lo)
            dstate[:, sl] = dsn * _pair_expand(eend, h0, lane_lo) + ds_from_y
            dyx = dyp * xp
            yd = jnp.zeros((Q, HP), F32)
            dxd = jnp.zeros((Q, HP), F32)
            for e in range(2):
                h = h0 + e
                msk = lane_lo if e == 0 else jnp.logical_not(lane_lo)
                col = csc[:, h : h + 1]
                row = csr[h : h + 1, :]
                lm = jnp.where(cm["tril"], jnp.exp(jnp.minimum(col - row, 0.0)), 0.0)
                dye = jnp.where(msk, dyp, 0.0).astype(BF16)
                xde = jnp.where(msk, xdt, 0.0).astype(BF16)
                gm = lax.dot_general(dye, xde, nt, preferred_element_type=F32)
                dcb = dcb + gm * lm
                mb = (cb * lm).astype(BF16)
                yd = yd + jnp.dot(mb, xde, preferred_element_type=F32)
                dxd = dxd + lax.dot_general(mb, dye, (((0,), (0,)), ((), ())), preferred_element_type=F32)
            dxdt = dxdt + dxd
            tt = dyp * yo - w2 + dyp.astype(BF16).astype(F32) * yd - xdt.astype(BF16).astype(F32) * dxd
            dx_ref[:, sl] = dxdt * dtx + dyp * _pair_expand(dk_ref[...], h0, lane_lo)
            dxx = dxdt * xp
            for e in range(2):
                h = h0 + e
                msk = lane_lo if e == 0 else jnp.logical_not(lane_lo)
                oh = (lane_id == h).astype(F32)
                dcs = dcs + jnp.sum(jnp.where(msk, tt, 0.0), axis=1, keepdims=True) * oh
                dcs_end = dcs_end + (_sum_all(jnp.where(msk, w2, 0.0)) + _sum_all(jnp.where(msk, rs, 0.0))) * oh
                ddk = ddk + _sum_all(jnp.where(msk, dyx, 0.0)) * oh
                ddt = ddt + jnp.sum(jnp.where(msk, dxx, 0.0), axis=1, keepdims=True) * oh
        dC = dC + jnp.dot(dcb.astype(BF16), Bb, preferred_element_type=F32)
        dB = dB + lax.dot_general(dcb.astype(BF16), Cb, (((0,), (0,)), ((), ())), preferred_element_type=F32)
        db_ref[...] = dB
        dc_ref[...] = dC
        last = (lax.broadcasted_iota(jnp.int32, (Q, 1), 0) == Q - 1).astype(F32)
        dcs = dcs + last * dcs_end
        dda = jnp.dot(triu.astype(F32), dcs, precision=HI, preferred_element_type=F32)
        ddt = ddt + dda * a_c
        da = jnp.sum(dda * dtc_v, axis=0, keepdims=True)
        draw = jnp.where(cm["rows"] >= PF, ddt * _sigmoid(cm["prec"]), 0.0)
        ddt_ref[...] = draw
        dbias_ref[...] += jnp.sum(draw, axis=0, keepdims=True)
        dalog_ref[...] += da * a_c
        ddk_ref[...] += ddk

        @pl.when((pl.program_id(0) == G - 1) & (ci == NC - 1))
        def _():
            _scatter_finish(sin, sout, send_sems, recv_sems)

    xs, bb, cc, dtcs, dtrs, pc, pr = _ssd_specs()

    def rev(spec_fn):
        return lambda g, ci: spec_fn(g, NC - 1 - ci)

    def rspec(spec):
        return pl.BlockSpec(spec.block_shape, rev(spec.index_map))

    xs_r, bb_r, cc_r, dtc_r, dtr_r = rspec(xs), rspec(bb), rspec(cc), rspec(dtcs), rspec(dtrs)
    st_r = pl.BlockSpec((None, None, N, GW), lambda g, ci: (NC - 1 - ci, g, 0, 0))
    gn = pl.BlockSpec((Q, N), lambda g, ci: (NC - 1 - ci, g))
    outs = pl.pallas_call(
        body,
        name=name,
        grid=(G, NC),
        in_specs=[xs_r, bb_r, cc_r, xs_r, st_r, dtc_r, dtr_r, pc, pr, pc, pr, pc] + [ANY] * nb,
        out_specs=[xs_r, gn, gn, dtc_r, pc, pc, pc] + [ANY] * nb,
        out_shape=[
            jax.ShapeDtypeStruct((LP, DI), F32),
            jax.ShapeDtypeStruct((LP, G * N), F32),
            jax.ShapeDtypeStruct((LP, G * N), F32),
            jax.ShapeDtypeStruct((G, LP, LANES), F32),
            jax.ShapeDtypeStruct((G, 1, LANES), F32),
            jax.ShapeDtypeStruct((G, 1, LANES), F32),
            jax.ShapeDtypeStruct((G, 1, LANES), F32),
        ]
        + [jax.ShapeDtypeStruct(b.shape, b.dtype) for b in scatter],
        scratch_shapes=[pltpu.VMEM((N, GW), F32)] + _scatter_sems(nb),
        compiler_params=_cparams(("arbitrary", "arbitrary")),
    )(xbc, xbc, xbc, dy, states, dtc, dtr, bias_c, bias_r, alog_c, alog_r, dskip_c, *scatter)
    return tuple(outs[:7]) + (_scatter_own(outs[7:], scatter),)


def _split_dot(x, u):
    hi = x.astype(BF16)
    lo = (x - hi.astype(F32)).astype(BF16)
    return jnp.dot(hi, u, preferred_element_type=F32) + jnp.dot(lo, u, preferred_element_type=F32)


def _suffix_sums(x, u):
    right = x[:, TQ:]
    return jnp.concatenate([_split_dot(x[:, :TQ], u) + jnp.sum(right, axis=1, keepdims=True), _split_dot(right, u)], axis=1)


def _sb_block(qe, kblk, vis, a_run, u_gt):
    l = lax.dot_general(qe, kblk, (((1,), (1,)), ((), ())), preferred_element_type=F32)
    lk = jnp.minimum(-l, 0.0) - jnp.log(1.0 + jnp.exp(-jnp.abs(l)))
    lbeta = l + lk
    if vis is not None:
        lk = jnp.where(vis, lk, 0.0)
    logw = lbeta + _suffix_sums(lk, u_gt) + a_run
    return lbeta, lk, logw


def _window(i, n):
    start = TQ * (i - 2 * n - 1)
    return pl.multiple_of(jnp.maximum(start, 0), TQ), start + TK


def _window_mask(i, n, t_idx):
    off, hi = _window(i, n)
    s_idx = off + lax.broadcasted_iota(jnp.int32, (1, TK), 1)
    return (s_idx < t_idx) & (s_idx >= PF) & (s_idx < hi)


def _descend(i, block, carry):
    def pack(n, c):
        return (n, jnp.max(jnp.maximum(c[0], c[1]))) + tuple(c)

    last = i // 2
    st = pack(jnp.int32(1), block(jnp.int32(0), carry, True))
    st = lax.while_loop(lambda st: (st[0] < last) & (st[1] > -T_SKIP), lambda st: pack(st[0] + 1, block(st[0], st[2:], False)), st)
    st = lax.while_loop(lambda st: (st[0] == last) & (st[1] > -T_SKIP), lambda st: pack(st[0] + 1, block(last, st[2:], True)), st)
    return st[2:]


def _attn_fwd(q, k, v, name):
    nq = LP // TQ

    def body(q_ref, k_ref, v_ref, o_ref):
        i = pl.program_id(1)
        qv = q_ref[...]
        lane_lo = lax.broadcasted_iota(jnp.int32, (1, HP), 1) < 64
        t_idx = i * TQ + lax.broadcasted_iota(jnp.int32, (TQ, 1), 0)
        ji = lax.broadcasted_iota(jnp.int32, (TQ, TQ), 0)
        si = lax.broadcasted_iota(jnp.int32, (TQ, TQ), 1)
        u_gt = (ji > si).astype(BF16)
        qs = [jnp.where(lane_lo, qv, jnp.zeros_like(qv)), jnp.where(lane_lo, jnp.zeros_like(qv), qv)]

        def block(n, carry, masked):
            a0, a1, acc = carry
            off, _ = _window(i, n)
            kblk = k_ref[pl.ds(off, TK), :]
            vblk = v_ref[pl.ds(off, TK), :]
            vis = _window_mask(i, n, t_idx) if masked else None
            new_a = []
            for e, a_run in enumerate((a0, a1)):
                _, lk, logw = _sb_block(qs[e], kblk, vis, a_run, u_gt)
                w = jnp.exp(logw)
                if masked:
                    w = jnp.where(vis, w, 0.0)
                msk = lane_lo if e == 0 else jnp.logical_not(lane_lo)
                acc = acc + jnp.dot(w.astype(BF16), jnp.where(msk, vblk, jnp.zeros_like(vblk)), preferred_element_type=F32)
                new_a.append(a_run + jnp.sum(lk, axis=1, keepdims=True))
            return new_a[0], new_a[1], acc

        z1 = jnp.zeros((TQ, 1), F32)
        _, _, acc = _descend(i, block, (z1, z1, jnp.zeros((TQ, HP), F32)))
        o_ref[...] = acc

    return pl.pallas_call(
        body,
        name=name,
        grid=(D // HP, nq),
        in_specs=[
            pl.BlockSpec((TQ, HP), lambda j, i: (i, j)),
            pl.BlockSpec((LP, HP), lambda j, i: (0, j)),
            pl.BlockSpec((LP, HP), lambda j, i: (0, j)),
        ],
        out_specs=pl.BlockSpec((TQ, HP), lambda j, i: (i, j)),
        out_shape=jax.ShapeDtypeStruct((LP, D), F32),
        compiler_params=_cparams(("parallel", "arbitrary")),
    )(q, k, v)


def _attn_bwd(q, k, v, o, do, name):
    nq = LP // TQ

    def body(q_ref, k_ref, v_ref, o_ref, do_ref, dq_ref, dk_ref, dv_ref):
        i = pl.program_id(1)

        @pl.when(i == 0)
        def _():
            dk_ref[...] = jnp.zeros_like(dk_ref)
            dv_ref[...] = jnp.zeros_like(dv_ref)

        qv = q_ref[...]
        dov = do_ref[...]
        lane_lo = lax.broadcasted_iota(jnp.int32, (1, HP), 1) < 64
        t_idx = i * TQ + lax.broadcasted_iota(jnp.int32, (TQ, 1), 0)
        ji = lax.broadcasted_iota(jnp.int32, (TQ, TQ), 0)
        si = lax.broadcasted_iota(jnp.int32, (TQ, TQ), 1)
        u_gt = (ji > si).astype(BF16)
        u_ge = (ji >= si).astype(BF16)
        msks = [lane_lo, jnp.logical_not(lane_lo)]
        qs = [jnp.where(m, qv, jnp.zeros_like(qv)) for m in msks]
        dob = [jnp.where(m, dov, 0.0).astype(BF16) for m in msks]
        ov = o_ref[...]
        deltas = [jnp.sum(d.astype(F32) * ov, axis=1, keepdims=True) for d in dob]
        nt = (((1,), (1,)), ((), ()))
        tn = (((0,), (0,)), ((), ()))

        def block(n, carry, masked):
            a0, a1, p0, p1, dq = carry
            off, _ = _window(i, n)
            kblk = k_ref[pl.ds(off, TK), :]
            vblk = v_ref[pl.ds(off, TK), :]
            vis = _window_mask(i, n, t_idx) if masked else None
            new_a, new_p = [], []
            dk_acc = jnp.zeros((TK, HP), F32)
            dv_acc = jnp.zeros((TK, HP), F32)
            for e, (a_run, p_run) in enumerate(((a0, p0), (a1, p1))):
                lbeta, lk, logw = _sb_block(qs[e], kblk, vis, a_run, u_gt)
                sig = jnp.exp(lbeta)
                w = jnp.exp(logw)
                if masked:
                    w = jnp.where(vis, w, 0.0)
                wb = w.astype(BF16)
                dw = lax.dot_general(dob[e], vblk, nt, preferred_element_type=F32)
                pm = wb.astype(F32) * dw
                cum_p = deltas[e] - (_suffix_sums(pm, u_ge) + p_run)
                dl = pm - (pm + cum_p) * sig
                if masked:
                    dl = jnp.where(vis, dl, 0.0)
                dl = dl.astype(BF16)
                km = jnp.where(msks[e], kblk, jnp.zeros_like(kblk))
                dq = dq + jnp.dot(dl, km, preferred_element_type=F32)
                dk_acc = dk_acc + lax.dot_general(dl, qs[e], tn, preferred_element_type=F32)
                dv_acc = dv_acc + lax.dot_general(wb, dob[e], tn, preferred_element_type=F32)
                new_a.append(a_run + jnp.sum(lk, axis=1, keepdims=True))
                new_p.append(p_run + jnp.sum(pm, axis=1, keepdims=True))
            dk_ref[pl.ds(off, TK), :] += dk_acc
            dv_ref[pl.ds(off, TK), :] += dv_acc
            return new_a[0], new_a[1], new_p[0], new_p[1], dq

        z1 = jnp.zeros((TQ, 1), F32)
        carry = _descend(i, block, (z1, z1, z1, z1, jnp.zeros((TQ, HP), F32)))
        dq_ref[...] = carry[4]

    blk = pl.BlockSpec((TQ, HP), lambda j, i: (i, j))
    full = pl.BlockSpec((LP, HP), lambda j, i: (0, j))
    big = jax.ShapeDtypeStruct((LP, D), F32)
    return pl.pallas_call(
        body,
        name=name,
        grid=(D // HP, nq),
        in_specs=[blk, full, full, blk, blk],
        out_specs=[blk, full, full],
        out_shape=[big, big, big],
        compiler_params=_cparams(("parallel", "arbitrary")),
    )(q, k, v, o, do)


ROW_TILES = (2048, 1024, 512, 256, 128, 64, 32, 16, 8)


def _row_tile(rows, cols, budget):
    if rows % 8:
        return rows
    return _pick(rows, tuple(t for t in ROW_TILES if t * cols <= budget) or (8,))


def _adamw(w, g, m, v, name):
    R, C = w.shape
    tr = _row_tile(R, C, 128 * 1024)

    def body(w_ref, g_ref, m_ref, v_ref, g_out, d_out, m_out, v_out):
        g = g_ref[...]
        mn = ADAM_B1 * m_ref[...] + (1.0 - ADAM_B1) * g
        vn = ADAM_B2 * v_ref[...] + (1.0 - ADAM_B2) * (g * g)
        mh = mn / (1.0 - ADAM_B1**ADAM_STEP)
        vh = vn / (1.0 - ADAM_B2**ADAM_STEP)
        g_out[...] = g
        d_out[...] = -ADAM_LR * (mh / (jnp.sqrt(vh) + ADAM_EPS) + ADAM_WD * w_ref[...])
        m_out[...] = mn
        v_out[...] = vn

    blk = pl.BlockSpec((tr, C), lambda i: (i, 0))
    sh = jax.ShapeDtypeStruct((R, C), F32)
    return pl.pallas_call(
        body,
        name=name,
        grid=(R // tr,),
        in_specs=[blk] * 4,
        out_specs=[blk] * 4,
        out_shape=[sh] * 4,
        compiler_params=_cparams(("parallel",)),
    )(w, g, m, v)


def _sum4(buf, name):
    _, R, C = buf.shape
    tr = _row_tile(R, C, 128 * 1024)

    def body(b_ref, o_ref):
        acc = b_ref[0].astype(F32)
        for s in range(1, NCHIP):
            acc = acc + b_ref[s].astype(F32)
        o_ref[...] = acc

    return pl.pallas_call(
        body,
        name=name,
        grid=(R // tr,),
        in_specs=[pl.BlockSpec((NCHIP, tr, C), lambda i: (0, i, 0))],
        out_specs=pl.BlockSpec((tr, C), lambda i: (i, 0)),
        out_shape=jax.ShapeDtypeStruct((R, C), F32),
        compiler_params=_cparams(("parallel",)),
    )(buf)


def _add2(a, b, name):
    S, R, C = a.shape
    tr = _row_tile(R, C, 256 * 1024)

    def body(a_ref, b_ref, o_ref):
        o_ref[...] = (a_ref[...].astype(F32) + b_ref[...].astype(F32)).astype(o_ref.dtype)

    blk = pl.BlockSpec((None, tr, C), lambda s, i: (s, i, 0))
    return pl.pallas_call(
        body,
        name=name,
        grid=(S, R // tr),
        in_specs=[blk, blk],
        out_specs=blk,
        out_shape=jax.ShapeDtypeStruct(a.shape, a.dtype),
        compiler_params=_cparams(("parallel", "parallel")),
    )(a, b)


ANY = pl.BlockSpec(memory_space=pl.ANY)


def _mesh_place():
    x, y, c = lax.axis_index("x"), lax.axis_index("y"), lax.axis_index("c")
    return x, y, c, 2 * x + y, [(1 - x, y), (x, 1 - y), (1 - x, 1 - y)]


def _gather_chips(bufs, name):
    nb = len(bufs)

    def body(*refs):
        ins = refs[:nb]
        outs = refs[nb : 2 * nb]
        send_sems, recv_sems = refs[2 * nb :]
        _gather_start(ins, outs, send_sems, recv_sems)
        _gather_finish(ins, outs, send_sems, recv_sems)

    outs = pl.pallas_call(
        body,
        name=name,
        in_specs=[ANY] * nb,
        out_specs=[ANY] * nb,
        out_shape=_gather_shapes(bufs),
        scratch_shapes=_gather_sems(nb),
    )(*bufs)
    return _gather_own(outs, bufs)


def _gather_shapes(bufs):
    return [jax.ShapeDtypeStruct((NCHIP,) + tuple(b.shape), b.dtype) for b in bufs]


def _gather_sems(nb):
    return [pltpu.SemaphoreType.DMA((6 * nb,)), pltpu.SemaphoreType.DMA((6 * nb,))]


def _gather_own(outs, bufs):
    me = 2 * lax.axis_index("x") + lax.axis_index("y")
    return [lax.dynamic_update_slice(o, b[None], (me, 0, 0)) for o, b in zip(outs, bufs)]


def _gather_copy(outs, send_sems, recv_sems, k, b, src, slot, hc, to):
    nb = len(outs)
    hr = outs[b].shape[1] // 2
    return pltpu.make_async_remote_copy(
        src_ref=src, dst_ref=outs[b].at[slot, pl.ds(hc * hr, hr)], send_sem=send_sems.at[k * nb + b],
        recv_sem=recv_sems.at[k * nb + b], device_id=to, device_id_type=MESH)


def _gather_start(ins, outs, send_sems, recv_sems):
    x, y, c, me, peers = _mesh_place()
    for k, (px, py) in enumerate(peers):
        for b in range(len(ins)):
            hr = ins[b].shape[0] // 2
            _gather_copy(outs, send_sems, recv_sems, k, b, ins[b].at[pl.ds(c * hr, hr)], me, c, (px, py, c)).start()


def _gather_finish(ins, outs, send_sems, recv_sems):
    x, y, c, me, peers = _mesh_place()
    nb = len(ins)
    sent = []
    for k, (px, py) in enumerate(peers):
        for b in range(nb):
            hr = ins[b].shape[0] // 2
            slot = 2 * px + py
            landed = outs[b].at[slot, pl.ds(c * hr, hr)]
            _gather_copy(outs, send_sems, recv_sems, k, b, landed, slot, c, (px, py, c)).wait_recv()
            cp = _gather_copy(outs, send_sems, recv_sems, 3 + k, b, landed, slot, c, (x, y, 1 - c))
            cp.start()
            sent.append(cp)
            sent.append(_gather_copy(outs, send_sems, recv_sems, k, b, ins[b].at[pl.ds(c * hr, hr)], me, c, (px, py, c)))
    for k, (px, py) in enumerate(peers):
        for b in range(nb):
            hr = ins[b].shape[0] // 2
            slot = 2 * px + py
            theirs = outs[b].at[slot, pl.ds((1 - c) * hr, hr)]
            _gather_copy(outs, send_sems, recv_sems, 3 + k, b, theirs, slot, 1 - c, (x, y, 1 - c)).wait_recv()
    for cp in sent:
        cp.wait_send()


def _scatter_chips(bufs, name):
    nb = len(bufs)

    def body(*refs):
        ins = refs[:nb]
        outs = refs[nb : 2 * nb]
        send_sems, recv_sems = refs[2 * nb :]
        _scatter_start(ins, outs, send_sems, recv_sems)
        _scatter_finish(ins, outs, send_sems, recv_sems)

    outs = pl.pallas_call(
        body,
        name=name,
        in_specs=[ANY] * nb,
        out_specs=[ANY] * nb,
        out_shape=[jax.ShapeDtypeStruct(b.shape, b.dtype) for b in bufs],
        scratch_shapes=_scatter_sems(nb),
    )(*bufs)
    return _scatter_own(outs, bufs)


def _scatter_sems(nb):
    return [pltpu.SemaphoreType.DMA((3 * nb,)), pltpu.SemaphoreType.DMA((3 * nb,))]


def _scatter_own(outs, bufs):
    me = 2 * lax.axis_index("x") + lax.axis_index("y")
    return [lax.dynamic_update_slice(o, lax.dynamic_slice_in_dim(b, me, 1, axis=0), (me, 0, 0)) for o, b in zip(outs, bufs)]


def _scatter_copy(ins, outs, send_sems, recv_sems, k, b, slot_from, slot_to, to):
    nb = len(ins)
    return pltpu.make_async_remote_copy(
        src_ref=ins[b].at[slot_from], dst_ref=outs[b].at[slot_to], send_sem=send_sems.at[k * nb + b],
        recv_sem=recv_sems.at[k * nb + b], device_id=to, device_id_type=MESH)


def _scatter_start(ins, outs, send_sems, recv_sems):
    x, y, c, me, peers = _mesh_place()
    for k, (px, py) in enumerate(peers):
        for b in range(len(ins)):
            _scatter_copy(ins, outs, send_sems, recv_sems, k, b, 2 * px + py, me, (px, py, c)).start()


def _scatter_finish(ins, outs, send_sems, recv_sems):
    x, y, c, me, peers = _mesh_place()
    for k, (px, py) in enumerate(peers):
        for b in range(len(ins)):
            _scatter_copy(ins, outs, send_sems, recv_sems, k, b, me, 2 * px + py, (px, py, c)).wait_recv()
    for k, (px, py) in enumerate(peers):
        for b in range(len(ins)):
            _scatter_copy(ins, outs, send_sems, recv_sems, k, b, 2 * px + py, me, (px, py, c)).wait_send()


def _split_cores(bufs, name):
    nb = len(bufs)

    def body(*refs):
        ins = refs[:nb]
        theirs = refs[nb : 2 * nb]
        send_sems, recv_sems = refs[2 * nb :]
        x, y, c, _, _ = _mesh_place()
        cps = []
        for b in range(nb):
            hr = ins[b].shape[1] // 2
            cp = pltpu.make_async_remote_copy(
                src_ref=ins[b].at[:, pl.ds((1 - c) * hr, hr)], dst_ref=theirs[b], send_sem=send_sems.at[b],
                recv_sem=recv_sems.at[b], device_id=(x, y, 1 - c), device_id_type=MESH)
            cp.start()
            cps.append(cp)
        for cp in cps:
            cp.wait()

    theirs = pl.pallas_call(
        body,
        name=name,
        in_specs=[ANY] * nb,
        out_specs=[ANY] * nb,
        out_shape=[jax.ShapeDtypeStruct((b.shape[0], b.shape[1] // 2, b.shape[2]), b.dtype) for b in bufs],
        scratch_shapes=[pltpu.SemaphoreType.DMA((nb,)), pltpu.SemaphoreType.DMA((nb,))],
    )(*bufs)
    c = lax.axis_index("c")
    mine = [lax.dynamic_slice_in_dim(b, c * (b.shape[1] // 2), b.shape[1] // 2, axis=1) for b in bufs]
    return mine, theirs


def _join_cores(bufs, name):
    nb = len(bufs)

    def body(*refs):
        ins = refs[:nb]
        outs = refs[nb : 2 * nb]
        send_sems, recv_sems = refs[2 * nb :]
        x, y, c, _, _ = _mesh_place()
        cps = []
        for b in range(nb):
            hr = ins[b].shape[0]
            cp = pltpu.make_async_remote_copy(
                src_ref=ins[b], dst_ref=outs[b].at[pl.ds(c * hr, hr)], send_sem=send_sems.at[b], recv_sem=recv_sems.at[b],
                device_id=(x, y, 1 - c), device_id_type=MESH)
            cp.start()
            cps.append(cp)
        for b, cp in enumerate(cps):
            hr = ins[b].shape[0]
            cp.wait_send()
            pltpu.make_async_remote_copy(
                src_ref=ins[b], dst_ref=outs[b].at[pl.ds((1 - c) * hr, hr)], send_sem=send_sems.at[b],
                recv_sem=recv_sems.at[b], device_id=(x, y, 1 - c), device_id_type=MESH).wait_recv()

    outs = pl.pallas_call(
        body,
        name=name,
        in_specs=[ANY] * nb,
        out_specs=[ANY] * nb,
        out_shape=[jax.ShapeDtypeStruct((2 * b.shape[0], b.shape[1]), b.dtype) for b in bufs],
        scratch_shapes=[pltpu.SemaphoreType.DMA((nb,)), pltpu.SemaphoreType.DMA((nb,))],
    )(*bufs)
    c = lax.axis_index("c")
    return [lax.dynamic_update_slice(o, b, (c * b.shape[0], 0)) for o, b in zip(outs, bufs)]


ROW_ALIGN = 1024


def _pack(pieces, dtype):
    flat = []
    for p in pieces:
        f = p.reshape(-1).astype(dtype)
        pad = (-f.shape[0]) % LANES
        if pad:
            f = jnp.pad(f, (0, pad))
        flat.append(f)
    tot = sum(f.shape[0] for f in flat)
    pad = (-tot) % (ROW_ALIGN * LANES)
    if pad:
        flat.append(jnp.zeros((pad,), dtype))
    return jnp.concatenate(flat).reshape(-1, LANES)


def _unpack(buf, shapes):
    lead = buf.shape[:-2]
    flat = buf.reshape(lead + (-1,))
    out = []
    off = 0
    for shp in shapes:
        n = 1
        for d in shp:
            n *= d
        out.append(flat[..., off : off + n].reshape(lead + tuple(shp)))
        off += n + ((-n) % LANES)
    return out


PARAMS = (
    ("meta_tokens", 1, "small"), ("ssd_norm", 1, "small"), ("ssd_w_in", 2, "big"), ("ssd_conv_w", 2, "small"),
    ("ssd_conv_b", 1, "small"), ("ssd_dt_bias", None, "rep"), ("ssd_a_log", None, "rep"), ("ssd_d_skip", None, "rep"),
    ("ssd_gate_norm", 1, "small"), ("ssd_w_out", 1, "big"), ("kv_norm", None, "rep"), ("w_kv", 1, "big"),
    ("sb_norm", None, "rep"), ("sb_w_q", 1, "big"), ("sb_w_o", 1, "big"), ("ffn_norm", None, "rep"),
    ("ffn_w_up", 2, "big"), ("ffn_conv_w", 2, "small"), ("ffn_conv_b", None, "rep"), ("ffn_w_down", 1, "big"),
    ("final_norm", None, "rep"),
)


def _head_cols(vec):
    return jnp.pad(vec.reshape(G, 1, E), ((0, 0), (0, 0), (0, LANES - E)))


def _head_rows(vec):
    return jnp.pad(vec.reshape(G, E, 1), ((0, 0), (0, 8 - E), (0, 0)))


def kernel(x, meta_tokens, ssd_norm, ssd_w_in, ssd_conv_w, ssd_conv_b, ssd_dt_bias, ssd_a_log, ssd_d_skip, ssd_gate_norm, ssd_w_out, kv_norm, w_kv, sb_norm, sb_w_q, sb_w_o, ffn_norm, ffn_w_up, ffn_conv_w, ffn_conv_b, ffn_w_down, final_norm, loss_target, m_meta_tokens, m_ssd_norm, m_ssd_w_in, m_ssd_conv_w, m_ssd_conv_b, m_ssd_dt_bias, m_ssd_a_log, m_ssd_d_skip, m_ssd_gate_norm, m_ssd_w_out, m_kv_norm, m_w_kv, m_sb_norm, m_sb_w_q, m_sb_w_o, m_ffn_norm, m_ffn_w_up, m_ffn_conv_w, m_ffn_conv_b, m_ffn_w_down, m_final_norm, v_meta_tokens, v_ssd_norm, v_ssd_w_in, v_ssd_conv_w, v_ssd_conv_b, v_ssd_dt_bias, v_ssd_a_log, v_ssd_d_skip, v_ssd_gate_norm, v_ssd_w_out, v_kv_norm, v_w_kv, v_sb_norm, v_sb_w_q, v_sb_w_o, v_ffn_norm, v_ffn_w_up, v_ffn_conv_w, v_ffn_conv_b, v_ffn_w_down, v_final_norm):
    local = dict(meta_tokens=meta_tokens, ssd_norm=ssd_norm, ssd_w_in=ssd_w_in, ssd_conv_w=ssd_conv_w, ssd_conv_b=ssd_conv_b, ssd_dt_bias=ssd_dt_bias, ssd_a_log=ssd_a_log, ssd_d_skip=ssd_d_skip, ssd_gate_norm=ssd_gate_norm, ssd_w_out=ssd_w_out, kv_norm=kv_norm, w_kv=w_kv, sb_norm=sb_norm, sb_w_q=sb_w_q, sb_w_o=sb_w_o, ffn_norm=ffn_norm, ffn_w_up=ffn_w_up, ffn_conv_w=ffn_conv_w, ffn_conv_b=ffn_conv_b, ffn_w_down=ffn_w_down, final_norm=final_norm)
    mom_m = dict(meta_tokens=m_meta_tokens, ssd_norm=m_ssd_norm, ssd_w_in=m_ssd_w_in, ssd_conv_w=m_ssd_conv_w, ssd_conv_b=m_ssd_conv_b, ssd_dt_bias=m_ssd_dt_bias, ssd_a_log=m_ssd_a_log, ssd_d_skip=m_ssd_d_skip, ssd_gate_norm=m_ssd_gate_norm, ssd_w_out=m_ssd_w_out, kv_norm=m_kv_norm, w_kv=m_w_kv, sb_norm=m_sb_norm, sb_w_q=m_sb_w_q, sb_w_o=m_sb_w_o, ffn_norm=m_ffn_norm, ffn_w_up=m_ffn_w_up, ffn_conv_w=m_ffn_conv_w, ffn_conv_b=m_ffn_conv_b, ffn_w_down=m_ffn_w_down, final_norm=m_final_norm)
    mom_v = dict(meta_tokens=v_meta_tokens, ssd_norm=v_ssd_norm, ssd_w_in=v_ssd_w_in, ssd_conv_w=v_ssd_conv_w, ssd_conv_b=v_ssd_conv_b, ssd_dt_bias=v_ssd_dt_bias, ssd_a_log=v_ssd_a_log, ssd_d_skip=v_ssd_d_skip, ssd_gate_norm=v_ssd_gate_norm, ssd_w_out=v_ssd_w_out, kv_norm=v_kv_norm, w_kv=v_w_kv, sb_norm=v_sb_norm, sb_w_q=v_sb_w_q, sb_w_o=v_sb_w_o, ffn_norm=v_ffn_norm, ffn_w_up=v_ffn_w_up, ffn_conv_w=v_ffn_conv_w, ffn_conv_b=v_ffn_conv_b, ffn_w_down=v_ffn_w_down, final_norm=v_final_norm)

    big_names = [n for n, _, kind in PARAMS if kind == "big"]
    small_names = [n for n, _, kind in PARAMS if kind == "small"]
    rep_names = [n for n, _, kind in PARAMS if kind == "rep"]
    axis_of = {n: ax for n, ax, _ in PARAMS}

    def rows2(a):
        return a.reshape(-1, a.shape[-1])

    first_big, later_big = big_names[:1], big_names[1:]
    full = {}

    def assemble(names, bufs):
        for n, buf in zip(names, bufs):
            p = buf.reshape((NCHIP,) + local[n].shape)
            full[n] = jnp.concatenate([p[s] for s in range(NCHIP)], axis=axis_of[n])

    small_own = _pack([local[n] for n in small_names], F32)
    gathered = _gather_chips([rows2(local[n]).astype(BF16) for n in first_big] + [small_own], "gather_first")
    assemble(first_big, gathered[:-1])
    for n, p in zip(small_names, _unpack(gathered[-1], [local[n].shape for n in small_names])):
        full[n] = jnp.concatenate([p[s] for s in range(NCHIP)], axis=axis_of[n])
    for n in rep_names:
        full[n] = local[n]

    w_in = full["ssd_w_in"][0]
    w_z, w_xbc = w_in[:, :DI], w_in[:, DI : DI + CD]
    w_dt = jnp.pad(w_in[:, DI + CD :], ((0, 0), (0, LANES - H)))
    fcw, fcb = full["ffn_conv_w"], full["ffn_conv_b"]
    scw, scb = full["ssd_conv_w"][0], full["ssd_conv_b"]
    bias_c, bias_r = _head_cols(full["ssd_dt_bias"][0]), _head_rows(full["ssd_dt_bias"][0])
    alog_c, alog_r = _head_cols(full["ssd_a_log"][0]), _head_rows(full["ssd_a_log"][0])
    dskip_c = _head_cols(full["ssd_d_skip"][0])
    kvn = full["kv_norm"].reshape(1, D)
    fin = full["final_norm"].reshape(1, D)

    h0 = jnp.concatenate([jnp.zeros((PF, D), F32), full["meta_tokens"], x[0]], axis=0)
    (u0,) = _rms_fwd(h0, [full["ssd_norm"]], "ssd_norm_fwd")
    z = _mm(u0, w_z, name="ssd_in_z")
    xr = _mm(u0, w_xbc, name="ssd_in_xbc")
    dt_raw = _mm(u0, w_dt, name="ssd_in_dt")
    xbc, xpre = _ssd_conv_fwd(xr, scw, scb, "ssd_conv_fwd")
    dth = dt_raw[:, :H].reshape(LP, G, E)
    dtc = jnp.pad(jnp.transpose(dth, (1, 0, 2)), ((0, 0), (0, 0), (0, LANES - E)))
    dtr = jnp.pad(jnp.transpose(dth, (1, 2, 0)), ((0, 0), (0, 8 - E), (0, 0)))
    later_own = [rows2(local[n]).astype(BF16) for n in later_big]
    y, states, later_all = _ssd_fwd(xbc, dtc, dtr, bias_c, bias_r, alog_c, alog_r, dskip_c, later_own, "ssd_scan_fwd")
    assemble(later_big, later_all)
    w_out = full["ssd_w_out"][0]
    wkv = full["w_kv"]
    w_q = full["sb_w_q"][0]
    w_o = full["sb_w_o"][0]
    w_up_g = [full["ffn_w_up"][l][:, :DFF] for l in range(2)]
    w_up_v = [full["ffn_w_up"][l][:, DFF:] for l in range(2)]
    w_down = [full["ffn_w_down"][l] for l in range(2)]
    hgn = _gate_fwd(y, z, full["ssd_gate_norm"], "ssd_gate_fwd")
    h1 = _mm(hgn, w_out, add=h0, mask_rows=True, name="ssd_out")

    def ffn_fwd(h, l, tag):
        (u,) = _rms_fwd(h, [full["ffn_norm"][l : l + 1]], f"ffn{tag}_norm_fwd")
        hg = _mm(u, w_up_g[l], name=f"ffn{tag}_up_g")
        hv = _mm(u, w_up_v[l], name=f"ffn{tag}_up_v")
        act, gpre, vpre = _ffn_act_fwd(hg, hv, fcw[l][:, :DFF], fcw[l][:, DFF:], fcb[l : l + 1, :DFF], fcb[l : l + 1, DFF:], f"ffn{tag}_act_fwd")
        hn = _mm(act, w_down[l], add=h, mask_rows=True, name=f"ffn{tag}_down")
        return hn, (u, hg, hv, act, gpre, vpre)

    h2, ffn0 = ffn_fwd(h1, 0, "0")
    ukv, uq = _rms_fwd(h2, [kvn, full["sb_norm"]], "attn_norm_fwd")
    kk = _mm(ukv, wkv[:, :D], out_dtype=BF16, name="attn_k")
    vv = _mm(ukv, wkv[:, D:], out_dtype=BF16, name="attn_v")
    qq = _mm(uq, w_q, out_dtype=BF16, scale=64.0**-0.5, name="attn_q")
    o = _attn_fwd(qq, kk, vv, "attn_fwd")
    h3 = _mm(o, w_o, add=h2, mask_rows=True, name="attn_out")
    h4, ffn1 = ffn_fwd(h3, 1, "1")
    dh, g_final, loss_rows = _loss_head(h4, fin, loss_target[0], "loss_head")
    loss = lax.psum(0.5 / D * jnp.sum(loss_rows), ("x", "y", "c"))

    grads = {"final_norm": g_final.reshape(D)}

    def ffn_bwd(dh, h, l, saved, tag):
        u, hg, hv, act, gpre, vpre = saved
        da = _mm(dh, w_down[l], tb=True, name=f"ffn{tag}_down_dx")
        gw_down = _mm(act, dh, ta=True, out_dtype=BF16, name=f"ffn{tag}_down_dw")
        dhg, dhv, dwg, dwv, dbg, dbv = _ffn_act_bwd(hg, hv, gpre, vpre, da, fcw[l][:, :DFF], fcw[l][:, DFF:], f"ffn{tag}_act_bwd")
        gw_up = jnp.concatenate([_mm(u, dhg, ta=True, out_dtype=BF16, name=f"ffn{tag}_up_g_dw"), _mm(u, dhv, ta=True, out_dtype=BF16, name=f"ffn{tag}_up_v_dw")], axis=1)
        du = _mm(dhg, w_up_g[l], tb=True, name=f"ffn{tag}_up_g_dx")
        du = _mm(dhv, w_up_v[l], tb=True, add=du, name=f"ffn{tag}_up_v_dx")
        dh_new, (gn,) = _rms_bwd(dh, h, [du], [full["ffn_norm"][l : l + 1]], f"ffn{tag}_norm_bwd")
        return dh_new, gw_down, gw_up, jnp.concatenate([dwg, dwv], axis=1), jnp.concatenate([dbg, dbv], axis=1), gn

    dh, gd1, gu1, gcw1, gcb1, gn1 = ffn_bwd(dh, h3, 1, ffn1, "1")
    do = _mm(dh, w_o, tb=True, name="attn_out_dx")
    grads["sb_w_o"] = _mm(o, dh, ta=True, out_dtype=BF16, name="attn_out_dw")[None]
    dq, dk, dv = _attn_bwd(qq, kk, vv, o, do, "attn_bwd")
    grads["sb_w_q"] = _mm(uq, dq, ta=True, out_dtype=BF16, scale=64.0**-0.5, name="attn_q_dw")[None]
    grads["w_kv"] = jnp.concatenate([_mm(ukv, dk, ta=True, out_dtype=BF16, name="attn_k_dw"), _mm(ukv, dv, ta=True, out_dtype=BF16, name="attn_v_dw")], axis=1)
    duq = _mm(dq, w_q, tb=True, scale=64.0**-0.5, name="attn_q_dx")
    dukv = _mm(dk, wkv[:, :D], tb=True, name="attn_k_dx")
    dukv = _mm(dv, wkv[:, D:], tb=True, add=dukv, name="attn_v_dx")
    dh, (g_kvn, g_sbn) = _rms_bwd(dh, h2, [dukv, duq], [kvn, full["sb_norm"]], "attn_norm_bwd")
    grads["kv_norm"] = g_kvn.reshape(D)
    grads["sb_norm"] = g_sbn
    dh, gd0, gu0, gcw0, gcb0, gn0 = ffn_bwd(dh, h1, 0, ffn0, "0")
    grads["ffn_w_down"] = jnp.stack([gd0, gd1])
    grads["ffn_w_up"] = jnp.stack([gu0, gu1])
    grads["ffn_conv_w"] = jnp.stack([gcw0, gcw1])
    grads["ffn_conv_b"] = jnp.concatenate([gcb0, gcb1], axis=0)
    grads["ffn_norm"] = jnp.concatenate([gn0, gn1], axis=0)
    dhgn = _mm(dh, w_out, tb=True, name="ssd_out_dx")
    grads["ssd_w_out"] = _mm(hgn, dh, ta=True, out_dtype=BF16, name="ssd_out_dw")[None]
    dy, dz, g_gate = _gate_bwd(dhgn, y, z, full["ssd_gate_norm"], "ssd_gate_bwd")
    grads["ssd_gate_norm"] = g_gate
    def slots(n):
        return jnp.stack([rows2(p) for p in jnp.split(grads[n], NCHIP, axis=axis_of[n])])

    mine, theirs = _split_cores([slots(n) for n in later_big], "split_cores_a")
    pair = [_add2(a, b, f"pair_sum_a{i}") for i, (a, b) in enumerate(zip(mine, theirs))]
    dxs, dB, dC, ddt_raw, g_bias, g_alog, g_dskip, got_a = _ssd_bwd(
        xbc, dy, states, dtc, dtr, bias_c, bias_r, alog_c, alog_r, dskip_c, pair, "ssd_scan_bwd")
    grads["ssd_dt_bias"] = g_bias[:, 0, :E].reshape(1, H)
    grads["ssd_a_log"] = g_alog[:, 0, :E].reshape(1, H)
    grads["ssd_d_skip"] = g_dskip[:, 0, :E].reshape(1, H)
    dxr, g_scw, g_scb = _ssd_conv_bwd(xr, xpre, jnp.concatenate([dxs, dB, dC], axis=1), scw, "ssd_conv_bwd")
    grads["ssd_conv_w"] = g_scw[None]
    grads["ssd_conv_b"] = g_scb
    ddt = jnp.pad(jnp.transpose(ddt_raw[:, :, :E], (1, 0, 2)).reshape(LP, H), ((0, 0), (0, LANES - H)))
    grads["ssd_w_in"] = jnp.concatenate(
        [_mm(u0, dz, ta=True, out_dtype=BF16, name="ssd_in_z_dw"), _mm(u0, dxr, ta=True, out_dtype=BF16, name="ssd_in_xbc_dw"), _mm(u0, ddt, ta=True, out_dtype=BF16, name="ssd_in_dt_dw")[:, :H]], axis=1)[None]
    du = _mm(dz, w_z, tb=True, name="ssd_in_z_dx")
    du = _mm(dxr, w_xbc, tb=True, add=du, name="ssd_in_xbc_dx")
    du = _mm(ddt, w_dt, tb=True, add=du, name="ssd_in_dt_dx")
    dh, (g_ssdn,) = _rms_bwd(dh, h0, [du], [full["ssd_norm"]], "ssd_norm_bwd")
    grads["ssd_norm"] = g_ssdn
    grads["meta_tokens"] = dh[PF : PF + N_META]
    grad_x = dh[PF + N_META :][None]

    def shard_pieces(names, s):
        out = []
        for n in names:
            ax = axis_of[n]
            out.append(grads[n] if ax is None else jnp.split(grads[n], NCHIP, axis=ax)[s])
        return out

    bufs = [slots(n) for n in first_big]
    bufs.append(jnp.stack([_pack(shard_pieces(small_names + rep_names, s), F32) for s in range(NCHIP)]))
    mine, theirs = _split_cores(bufs, "split_cores_b")
    pair = [_add2(a, b, f"pair_sum_b{i}") for i, (a, b) in enumerate(zip(mine, theirs))]
    got_b = _scatter_chips(pair, "scatter_grads")
    got = got_b[:-1] + got_a + got_b[-1:]
    sums = [_sum4(b, f"sum_chips_{i}") for i, b in enumerate(got)]
    gsum = _join_cores(sums, "join_cores")

    def rows(a):
        f = a.reshape(-1)
        pad = (-f.shape[0]) % LANES
        if pad:
            f = jnp.pad(f, (0, pad))
        return f.reshape(-1, LANES)

    order = [n for n, _, _ in PARAMS]
    res = {}
    for n, g2 in zip(big_names, gsum[:-1]):
        outs = _adamw(rows2(local[n]), g2, rows2(mom_m[n]), rows2(mom_v[n]), f"adamw_{n}")
        res[n] = [o_.reshape(local[n].shape) for o_ in outs]
    rest = small_names + rep_names
    for n, g1 in zip(rest, _unpack(gsum[-1], [local[n].shape for n in rest])):
        shp = local[n].shape
        cnt = 1
        for d in shp:
            cnt *= d
        outs = _adamw(rows(local[n]), rows(g1), rows(mom_m[n]), rows(mom_v[n]), f"adamw_{n}")
        res[n] = [o_.reshape(-1)[:cnt].reshape(shp) for o_ in outs]
    return (loss, grad_x, *[res[n][0] for n in order], *[res[n][1] for n in order], *[res[n][2] for n in order], *[res[n][3] for n in order])
```

```python
import functools

import jax
import jax.numpy as jnp
from jax import lax
from jax.experimental import pallas as pl
from jax.experimental.pallas import tpu as pltpu

D = 1024
SEQ = 8192
N_META = 16
EPS = 1e-6
P = 64
G = 4
N = 128
CONVW = 4
Q = 256
FC = 3
DFF = 256 * ((8 * D // 3 + 255) // 256)
DI = 2 * D
H = DI // P
E = H // G
GW = E * P
CD = DI + 2 * G * N
IN = DI + CD + H
SBH = D // 64
HP = 128
LANES = 128
PF = Q - N_META
LP = PF + N_META + SEQ
NC = LP // Q
TQ = 256
TK = 2 * TQ
NCHIP = 4
ADAM_LR, ADAM_B1, ADAM_B2, ADAM_EPS, ADAM_WD, ADAM_STEP = 0.001, 0.9, 0.999, 1e-08, 0.01, 10

F32 = jnp.float32
BF16 = jnp.bfloat16
HI = lax.Precision.HIGHEST
MESH = pl.DeviceIdType.MESH
VMEM_LIMIT = 48 * 1024 * 1024
MM_MAX_K = 3072
T_SKIP = 110.0


def _pick(n, cands):
    for c in cands:
        if n % c == 0:
            return c
    raise ValueError((n, cands))


def _cparams(sem):
    return pltpu.CompilerParams(dimension_semantics=sem, vmem_limit_bytes=VMEM_LIMIT)


def _valid_rows(block, rows):
    r = block * rows + lax.broadcasted_iota(jnp.int32, (rows, 1), 0)
    return r >= PF


def _sigmoid(x):
    return 1.0 / (1.0 + jnp.exp(-x))


def _softplus(x):
    return jnp.maximum(x, 0.0) + jnp.log(1.0 + jnp.exp(-jnp.abs(x)))


def _sum_all(x):
    return jnp.sum(jnp.sum(x, axis=1, keepdims=True), axis=0, keepdims=True)


def _dsilu(x):
    s = _sigmoid(x)
    return s * (1.0 + x * (1.0 - s))


def _mm(a, b, *, ta=False, tb=False, out_dtype=F32, add=None, mask_rows=False, scale=None, name):
    if ta:
        K, M = a.shape
    else:
        M, K = a.shape
    if tb:
        Nn, K2 = b.shape
    else:
        K2, Nn = b.shape
    assert K == K2, (a.shape, b.shape, ta, tb)
    tn = _pick(Nn, (1408, 1024, 768, 512, 256, 128))
    if ta:
        tm = _pick(M, (1408, 1024, 768, 512, 256, 128))
        tk = _pick(K, (768, 512, 256))
    else:
        tm = _pick(M, (768, 256))
        tk = K if K <= MM_MAX_K else _pick(K, (1024, 768, 512, 256, 128))
    nk = K // tk
    dims = (((0 if ta else 1,), (1 if tb else 0,)), ((), ()))

    def body(*refs):
        a_ref, b_ref = refs[0], refs[1]
        add_ref = refs[2] if add is not None else None
        o_ref = refs[3] if add is not None else refs[2]
        acc = refs[-1] if nk > 1 else None

        def finish(r):
            if scale is not None:
                r = r * scale
            if mask_rows:
                r = jnp.where(_valid_rows(pl.program_id(0), tm), r, 0.0)
            if add_ref is not None:
                r = r + add_ref[...]
            o_ref[...] = r.astype(out_dtype)

        part = lax.dot_general(a_ref[...].astype(BF16), b_ref[...].astype(BF16), dims, preferred_element_type=F32)
        if nk == 1:
            finish(part)
        else:
            k = pl.program_id(2)

            @pl.when(k == 0)
            def _():
                acc[...] = part

            @pl.when(k > 0)
            def _():
                acc[...] += part

            @pl.when(k == nk - 1)
            def _():
                finish(acc[...])

    a_spec = pl.BlockSpec((tk, tm), lambda i, j, k: (k, i)) if ta else pl.BlockSpec((tm, tk), lambda i, j, k: (i, k))
    b_spec = pl.BlockSpec((tn, tk), lambda i, j, k: (j, k)) if tb else pl.BlockSpec((tk, tn), lambda i, j, k: (k, j))
    o_spec = pl.BlockSpec((tm, tn), lambda i, j, k: (i, j))
    in_specs = [a_spec, b_spec] + ([o_spec] if add is not None else [])
    args = (a, b) + ((add,) if add is not None else ())
    return pl.pallas_call(
        body,
        name=name,
        grid=(M // tm, Nn // tn, nk),
        in_specs=in_specs,
        out_specs=o_spec,
        out_shape=jax.ShapeDtypeStruct((M, Nn), out_dtype),
        scratch_shapes=[pltpu.VMEM((tm, tn), F32)] if nk > 1 else [],
        compiler_params=_cparams(("parallel", "parallel", "arbitrary")),
    )(*args)


def _rms_fwd(h, gains, name):
    tr = _pick(LP, (768, 256))
    ng = len(gains)

    def body(*refs):
        h_ref = refs[0]
        g_refs = refs[1 : 1 + ng]
        o_refs = refs[1 + ng :]
        x = h_ref[...]
        xh = x * lax.rsqrt(jnp.mean(x * x, axis=-1, keepdims=True) + EPS)
        for g_ref, o_ref in zip(g_refs, o_refs):
            o_ref[...] = (xh * g_ref[...]).astype(BF16)

    row = pl.BlockSpec((tr, D), lambda i: (i, 0))
    vec = pl.BlockSpec((1, D), lambda i: (0, 0))
    outs = pl.pallas_call(
        body,
        name=name,
        grid=(LP // tr,),
        in_specs=[row] + [vec] * ng,
        out_specs=[row] * ng,
        out_shape=[jax.ShapeDtypeStruct((LP, D), BF16)] * ng,
        compiler_params=_cparams(("parallel",)),
    )(h, *gains)
    return outs


def _rms_bwd(dh_in, h, dus, gains, name):
    tr = _pick(LP, (256,))
    ng = len(gains)

    def body(*refs):
        dh_ref, h_ref = refs[0], refs[1]
        du_refs = refs[2 : 2 + ng]
        g_refs = refs[2 + ng : 2 + 2 * ng]
        o_ref = refs[2 + 2 * ng]
        dg_refs = refs[3 + 2 * ng :]
        i = pl.program_id(0)
        x = h_ref[...]
        r = lax.rsqrt(jnp.mean(x * x, axis=-1, keepdims=True) + EPS)
        xh = x * r
        tot = dh_ref[...]
        for du_ref, g_ref, dg_ref in zip(du_refs, g_refs, dg_refs):
            du = du_ref[...]
            dxh = du * g_ref[...]
            tot = tot + r * (dxh - xh * jnp.mean(dxh * xh, axis=-1, keepdims=True))

            @pl.when(i == 0)
            def _():
                dg_ref[...] = jnp.zeros_like(dg_ref)

            dg_ref[...] += jnp.sum(du * xh, axis=0, keepdims=True)
        o_ref[...] = jnp.where(_valid_rows(i, tr), tot, 0.0)

    row = pl.BlockSpec((tr, D), lambda i: (i, 0))
    vec = pl.BlockSpec((1, D), lambda i: (0, 0))
    outs = pl.pallas_call(
        body,
        name=name,
        grid=(LP // tr,),
        in_specs=[row, row] + [row] * ng + [vec] * ng,
        out_specs=[row] + [vec] * ng,
        out_shape=[jax.ShapeDtypeStruct((LP, D), F32)] + [jax.ShapeDtypeStruct((1, D), F32)] * ng,
        compiler_params=_cparams(("arbitrary",)),
    )(dh_in, h, *dus, *gains)
    return outs[0], outs[1:]


def _loss_head(h, gain, target, name):
    tr = Q

    def body(h_ref, g_ref, t_ref, dh_ref, dg_ref, ls_ref):
        i = pl.program_id(0)

        @pl.when(i == 0)
        def _():
            dg_ref[...] = jnp.zeros_like(dg_ref)
            ls_ref[...] = jnp.zeros_like(ls_ref)
            dh_ref[...] = jnp.zeros_like(dh_ref)

        @pl.when(i > 0)
        def _():
            x = h_ref[...]
            g = g_ref[...]
            r = lax.rsqrt(jnp.mean(x * x, axis=-1, keepdims=True) + EPS)
            xh = x * r
            e = xh * g - t_ref[...]
            ls_ref[...] += jnp.sum(e * e, axis=0, keepdims=True)
            dy = e * (1.0 / D)
            dg_ref[...] += jnp.sum(dy * xh, axis=0, keepdims=True)
            dxh = dy * g
            dh_ref[...] = r * (dxh - xh * jnp.mean(dxh * xh, axis=-1, keepdims=True))

    row = pl.BlockSpec((tr, D), lambda i: (i, 0))
    vec = pl.BlockSpec((1, D), lambda i: (0, 0))
    return pl.pallas_call(
        body,
        name=name,
        grid=(LP // tr,),
        in_specs=[row, vec, pl.BlockSpec((tr, D), lambda i: (jnp.maximum(i - 1, 0), 0))],
        out_specs=[row, vec, vec],
        out_shape=[jax.ShapeDtypeStruct((LP, D), F32), jax.ShapeDtypeStruct((1, D), F32), jax.ShapeDtypeStruct((1, D), F32)],
        compiler_params=_cparams(("arbitrary",)),
    )(h, gain, target)


HALO = 8
CONV_COLS = (1536, 1408, 768, 512, 256)


def _conv_rows(ext, w, b, width):
    n = ext.shape[0]
    acc = b + w[width - 1 : width, :] * ext[HALO:]
    for k in range(width - 1):
        acc = acc + w[k : k + 1, :] * pltpu.roll(ext, width - 1 - k, 0)[HALO:]
    return acc


def _conv_specs(tr, tn, col):
    per = tr // HALO
    last = LP // HALO - 1
    prev = pl.BlockSpec((HALO, tn), lambda j, i: (jnp.maximum(i * per - 1, 0), col(j)))
    cur = pl.BlockSpec((tr, tn), lambda j, i: (i, col(j)))
    nxt = pl.BlockSpec((HALO, tn), lambda j, i: (jnp.minimum((i + 1) * per, last), col(j)))
    return prev, cur, nxt


def _conv_bwd_core(ext, dpre, w, width, i, tr):
    rows = i * tr + lax.broadcasted_iota(jnp.int32, (tr + HALO, 1), 0)
    dpre = jnp.where((rows >= PF) & (rows < LP), dpre, 0.0)
    n = tr + HALO
    dx = w[width - 1 : width, :] * dpre[:tr]
    for k in range(width - 1):
        sh = width - 1 - k
        dx = dx + w[k : k + 1, :] * pltpu.roll(dpre, n - sh, 0)[:tr]
    dcur = dpre[:tr]
    dws = []
    for k in range(width):
        sh = width - 1 - k
        xs = ext[HALO:] if sh == 0 else pltpu.roll(ext, sh, 0)[HALO:]
        dws.append(jnp.sum(xs * dcur, axis=0, keepdims=True))
    db = jnp.sum(dcur, axis=0, keepdims=True)
    dx = jnp.where(_valid_rows(i, tr), dx, 0.0)
    return dx, dws, db


def _ssd_conv_fwd(xr, cw, cb, name):
    tr, tn = Q, _pick(CD, CONV_COLS)

    def body(p_ref, c_ref, w_ref, b_ref, o_ref, pre_ref):
        i = pl.program_id(1)
        ext = jnp.concatenate([jnp.where(i > 0, p_ref[...], 0.0), c_ref[...]], axis=0)
        pre = _conv_rows(ext, w_ref[...], b_ref[...], CONVW)
        pre_ref[...] = pre
        o_ref[...] = jnp.where(_valid_rows(i, tr), pre * _sigmoid(pre), 0.0)

    prev, cur, _ = _conv_specs(tr, tn, lambda j: j)
    return pl.pallas_call(
        body,
        name=name,
        grid=(CD // tn, LP // tr),
        in_specs=[prev, cur, pl.BlockSpec((CONVW, tn), lambda j, i: (0, j)), pl.BlockSpec((1, tn), lambda j, i: (0, j))],
        out_specs=[cur, cur],
        out_shape=[jax.ShapeDtypeStruct((LP, CD), F32)] * 2,
        compiler_params=_cparams(("parallel", "arbitrary")),
    )(xr, xr, cw, cb)


def _ssd_conv_bwd(xr, pre, dxbc, cw, name):
    tr, tn = Q, _pick(CD, CONV_COLS)

    def body(p_ref, c_ref, prc_ref, prn_ref, dc_ref, dn_ref, w_ref, dx_ref, dw_ref, db_ref):
        i = pl.program_id(1)
        ext = jnp.concatenate([jnp.where(i > 0, p_ref[...], 0.0), c_ref[...]], axis=0)
        dout = jnp.concatenate([dc_ref[...], dn_ref[...]], axis=0)
        prev_ = jnp.concatenate([prc_ref[...], prn_ref[...]], axis=0)
        dx, dws, db = _conv_bwd_core(ext, dout * _dsilu(prev_), w_ref[...], CONVW, i, tr)
        dx_ref[...] = dx

        @pl.when(i == 0)
        def _():
            dw_ref[...] = jnp.zeros_like(dw_ref)
            db_ref[...] = jnp.zeros_like(db_ref)

        for k in range(CONVW):
            dw_ref[k : k + 1, :] += dws[k]
        db_ref[...] += db

    prev, cur, nxt = _conv_specs(tr, tn, lambda j: j)
    wspec = pl.BlockSpec((CONVW, tn), lambda j, i: (0, j))
    bspec = pl.BlockSpec((1, tn), lambda j, i: (0, j))
    return pl.pallas_call(
        body,
        name=name,
        grid=(CD // tn, LP // tr),
        in_specs=[prev, cur, cur, nxt, cur, nxt, wspec],
        out_specs=[cur, wspec, bspec],
        out_shape=[jax.ShapeDtypeStruct((LP, CD), F32), jax.ShapeDtypeStruct((CONVW, CD), F32), jax.ShapeDtypeStruct((1, CD), F32)],
        compiler_params=_cparams(("parallel", "arbitrary")),
    )(xr, xr, pre, pre, dxbc, dxbc, cw)


def _ffn_act_fwd(hg, hv, cwg, cwv, cbg, cbv, name):
    tr, tn = Q, _pick(DFF, CONV_COLS)

    def body(pg, cg, pv, cv, wg, wv, bg, bv, o_ref, gate_ref, val_ref):
        i = pl.program_id(1)
        eg = jnp.concatenate([jnp.where(i > 0, pg[...], 0.0), cg[...]], axis=0)
        ev = jnp.concatenate([jnp.where(i > 0, pv[...], 0.0), cv[...]], axis=0)
        gate = _conv_rows(eg, wg[...], bg[...], FC)
        val = _conv_rows(ev, wv[...], bv[...], FC)
        gate_ref[...] = gate
        val_ref[...] = val
        o_ref[...] = (gate * _sigmoid(gate) * val).astype(BF16)

    prev, cur, _ = _conv_specs(tr, tn, lambda j: j)
    wspec = pl.BlockSpec((FC, tn), lambda j, i: (0, j))
    bspec = pl.BlockSpec((1, tn), lambda j, i: (0, j))
    return pl.pallas_call(
        body,
        name=name,
        grid=(DFF // tn, LP // tr),
        in_specs=[prev, cur, prev, cur, wspec, wspec, bspec, bspec],
        out_specs=[cur, cur, cur],
        out_shape=[jax.ShapeDtypeStruct((LP, DFF), BF16), jax.ShapeDtypeStruct((LP, DFF), F32), jax.ShapeDtypeStruct((LP, DFF), F32)],
        compiler_params=_cparams(("parallel", "arbitrary")),
    )(hg, hg, hv, hv, cwg, cwv, cbg, cbv)


def _ffn_act_bwd(hg, hv, gate_pre, val_pre, da, cwg, cwv, name):
    tr, tn = Q, _pick(DFF, CONV_COLS)

    def body(pg, cg, pv, cv, gc, gn, vc, vn, dc, dn, wg, wv, dg_ref, dv_ref, dwg, dwv, dbg, dbv):
        i = pl.program_id(1)
        eg = jnp.concatenate([jnp.where(i > 0, pg[...], 0.0), cg[...]], axis=0)
        ev = jnp.concatenate([jnp.where(i > 0, pv[...], 0.0), cv[...]], axis=0)
        dout = jnp.concatenate([dc[...], dn[...]], axis=0)
        gate = jnp.concatenate([gc[...], gn[...]], axis=0)
        val = jnp.concatenate([vc[...], vn[...]], axis=0)
        s = _sigmoid(gate)
        dxg, dwsg, dbgv = _conv_bwd_core(eg, dout * val * (s * (1.0 + gate * (1.0 - s))), wg[...], FC, i, tr)
        dxv, dwsv, dbvv = _conv_bwd_core(ev, dout * (gate * s), wv[...], FC, i, tr)
        dg_ref[...] = dxg
        dv_ref[...] = dxv

        @pl.when(i == 0)
        def _():
            dwg[...] = jnp.zeros_like(dwg)
            dwv[...] = jnp.zeros_like(dwv)
            dbg[...] = jnp.zeros_like(dbg)
            dbv[...] = jnp.zeros_like(dbv)

        for k in range(FC):
            dwg[k : k + 1, :] += dwsg[k]
            dwv[k : k + 1, :] += dwsv[k]
        dbg[...] += dbgv
        dbv[...] += dbvv

    prev, cur, nxt = _conv_specs(tr, tn, lambda j: j)
    wspec = pl.BlockSpec((FC, tn), lambda j, i: (0, j))
    bspec = pl.BlockSpec((1, tn), lambda j, i: (0, j))
    big = jax.ShapeDtypeStruct((LP, DFF), F32)
    wsh = jax.ShapeDtypeStruct((FC, DFF), F32)
    bsh = jax.ShapeDtypeStruct((1, DFF), F32)
    return pl.pallas_call(
        body,
        name=name,
        grid=(DFF // tn, LP // tr),
        in_specs=[prev, cur, prev, cur, cur, nxt, cur, nxt, cur, nxt, wspec, wspec],
        out_specs=[cur, cur, wspec, wspec, bspec, bspec],
        out_shape=[big, big, wsh, wsh, bsh, bsh],
        compiler_params=_cparams(("parallel", "arbitrary")),
    )(hg, hg, hv, hv, gate_pre, gate_pre, val_pre, val_pre, da, da, cwg, cwv)


def _gate_fwd(y, z, gg, name):
    tr = _pick(LP, (768, 256))

    def body(y_ref, z_ref, g_ref, o_ref):
        zv = z_ref[...]
        hg = y_ref[...] * zv * _sigmoid(zv)
        r = lax.rsqrt(jnp.mean(hg * hg, axis=-1, keepdims=True) + EPS)
        o_ref[...] = (hg * r * g_ref[...]).astype(BF16)

    blk = pl.BlockSpec((tr, GW), lambda i, g: (i, g))
    return pl.pallas_call(
        body,
        name=name,
        grid=(LP // tr, G),
        in_specs=[blk, blk, pl.BlockSpec((1, GW), lambda i, g: (0, g))],
        out_specs=blk,
        out_shape=jax.ShapeDtypeStruct((LP, DI), BF16),
        compiler_params=_cparams(("parallel", "parallel")),
    )(y, z, gg)


def _gate_bwd(dout, y, z, gg, name):
    tr = _pick(LP, (768, 256))

    def body(do_ref, y_ref, z_ref, g_ref, dy_ref, dz_ref, dg_ref):
        i = pl.program_id(1)
        zv = z_ref[...]
        yv = y_ref[...]
        sz = zv * _sigmoid(zv)
        hg = yv * sz
        r = lax.rsqrt(jnp.mean(hg * hg, axis=-1, keepdims=True) + EPS)
        hh = hg * r
        do = do_ref[...]
        dhh = do * g_ref[...]
        dhg = r * (dhh - hh * jnp.mean(dhh * hh, axis=-1, keepdims=True))
        dy_ref[...] = dhg * sz
        dz_ref[...] = dhg * yv * _dsilu(zv)

        @pl.when(i == 0)
        def _():
            dg_ref[...] = jnp.zeros_like(dg_ref)

        dg_ref[...] += jnp.sum(do * hh, axis=0, keepdims=True)

    blk = pl.BlockSpec((tr, GW), lambda g, i: (i, g))
    vec = pl.BlockSpec((1, GW), lambda g, i: (0, g))
    big = jax.ShapeDtypeStruct((LP, DI), F32)
    return pl.pallas_call(
        body,
        name=name,
        grid=(G, LP // tr),
        in_specs=[blk, blk, blk, vec],
        out_specs=[blk, blk, vec],
        out_shape=[big, big, jax.ShapeDtypeStruct((1, DI), F32)],
        compiler_params=_cparams(("parallel", "arbitrary")),
    )(dout, y, z, gg)


def _ssd_common(dtc_ref, dtr_ref, bc_ref, br_ref, ac_ref, ar_ref, c):
    rows = c * Q + lax.broadcasted_iota(jnp.int32, (Q, 1), 0)
    cols = c * Q + lax.broadcasted_iota(jnp.int32, (1, Q), 1)
    prec = dtc_ref[...] + bc_ref[...]
    prer = dtr_ref[...] + br_ref[...]
    dtc = jnp.where(rows >= PF, _softplus(prec), 0.0)
    dtr = jnp.where(cols >= PF, _softplus(prer), 0.0)
    a_c = -jnp.exp(ac_ref[...])
    a_r = -jnp.exp(ar_ref[...])
    li = lax.broadcasted_iota(jnp.int32, (Q, Q), 0)
    si = lax.broadcasted_iota(jnp.int32, (Q, Q), 1)
    tril = si <= li
    trif = tril.astype(F32)
    csc = jnp.dot(trif, dtc * a_c, precision=HI, preferred_element_type=F32)
    csr = lax.dot_general(dtr * a_r, trif, (((1,), (1,)), ((), ())), precision=HI, preferred_element_type=F32)
    return dict(rows=rows, prec=prec, dtc=dtc, a_c=a_c, tril=tril, trif=trif, csc=csc, csr=csr, li=li, si=si)


def _pair_expand(arr, h0, lane_lo):
    return jnp.where(lane_lo, arr[:, h0 : h0 + 1], arr[:, h0 + 1 : h0 + 2])


def _ssd_specs():
    nb = DI // N
    xs = pl.BlockSpec((Q, GW), lambda g, c: (c, g))
    bb = pl.BlockSpec((Q, N), lambda g, c: (c, nb + g))
    cc = pl.BlockSpec((Q, N), lambda g, c: (c, nb + G + g))
    dtc = pl.BlockSpec((None, Q, LANES), lambda g, c: (g, c, 0))
    dtr = pl.BlockSpec((None, 8, Q), lambda g, c: (g, 0, c))
    pc = pl.BlockSpec((None, 1, LANES), lambda g, c: (g, 0, 0))
    pr = pl.BlockSpec((None, 8, 1), lambda g, c: (g, 0, 0))
    return xs, bb, cc, dtc, dtr, pc, pr


def _ssd_fwd(xbc, dtc, dtr, bias_c, bias_r, alog_c, alog_r, dskip_c, gather, name):
    nb = len(gather)

    def body(*refs):
        xs_ref, b_ref, c_ref, dtc_ref, dtr_ref, bc_ref, br_ref, ac_ref, ar_ref, dk_ref = refs[:10]
        gin = refs[10 : 10 + nb]
        y_ref, st_ref = refs[10 + nb : 12 + nb]
        gout = refs[12 + nb : 12 + 2 * nb]
        state, send_sems, recv_sems = refs[12 + 2 * nb :]
        c = pl.program_id(1)
        first_step = (pl.program_id(0) == 0) & (c == 0)
        last_step = (pl.program_id(0) == G - 1) & (c == NC - 1)

        @pl.when(first_step)
        def _():
            _gather_start(gin, gout, send_sems, recv_sems)

        @pl.when(c == 0)
        def _():
            state[...] = jnp.zeros_like(state)

        st_ref[...] = state[...]
        cm = _ssd_common(dtc_ref, dtr_ref, bc_ref, br_ref, ac_ref, ar_ref, c)
        Bm = b_ref[...]
        Cm = c_ref[...]
        cb = lax.dot_general(Cm.astype(BF16), Bm.astype(BF16), (((1,), (1,)), ((), ())), preferred_element_type=F32)
        bt = Bm.T.astype(BF16)
        lane_lo = lax.broadcasted_iota(jnp.int32, (1, HP), 1) < P
        csc, csr, dtc_v = cm["csc"], cm["csr"], cm["dtc"]
        ecs = jnp.exp(csc)
        cs_end = csc[Q - 1 : Q, :]
        wdec = jnp.exp(cs_end - csc)
        eend = jnp.exp(cs_end)
        for pp in range(E // 2):
            h0 = 2 * pp
            sl = slice(pp * HP, (pp + 1) * HP)
            xp = xs_ref[:, sl]
            xdt = xp * _pair_expand(dtc_v, h0, lane_lo)
            yacc = xp * _pair_expand(dk_ref[...], h0, lane_lo)
            for e in range(2):
                h = h0 + e
                lm = jnp.where(cm["tril"], jnp.exp(jnp.minimum(csc[:, h : h + 1] - csr[h : h + 1, :], 0.0)), 0.0)
                m = (cb * lm).astype(BF16)
                xm = jnp.where(lane_lo if e == 0 else jnp.logical_not(lane_lo), xdt, 0.0).astype(BF16)
                yacc = yacc + jnp.dot(m, xm, preferred_element_type=F32)
            stp = state[:, sl]
            yoff = jnp.dot(Cm.astype(BF16), stp.astype(BF16), preferred_element_type=F32)
            y_ref[:, sl] = yacc + yoff * _pair_expand(ecs, h0, lane_lo)
            xw = (xdt * _pair_expand(wdec, h0, lane_lo)).astype(BF16)
            state[:, sl] = stp * _pair_expand(eend, h0, lane_lo) + jnp.dot(bt, xw, preferred_element_type=F32)

        @pl.when(last_step)
        def _():
            _gather_finish(gin, gout, send_sems, recv_sems)

    xs, bb, cc, dtcs, dtrs, pc, pr = _ssd_specs()
    outs = pl.pallas_call(
        body,
        name=name,
        grid=(G, NC),
        in_specs=[xs, bb, cc, dtcs, dtrs, pc, pr, pc, pr, pc] + [ANY] * nb,
        out_specs=[xs, pl.BlockSpec((None, None, N, GW), lambda g, c: (c, g, 0, 0))] + [ANY] * nb,
        out_shape=[jax.ShapeDtypeStruct((LP, DI), F32), jax.ShapeDtypeStruct((NC, G, N, GW), F32)] + _gather_shapes(gather),
        scratch_shapes=[pltpu.VMEM((N, GW), F32)] + _gather_sems(nb),
        compiler_params=_cparams(("arbitrary", "arbitrary")),
    )(xbc, xbc, xbc, dtc, dtr, bias_c, bias_r, alog_c, alog_r, dskip_c, *gather)
    return outs[0], outs[1], _gather_own(outs[2:], gather)


def _ssd_bwd(xbc, dy, states, dtc, dtr, bias_c, bias_r, alog_c, alog_r, dskip_c, scatter, name):
    nb = len(scatter)

    def body(*refs):
        xs_ref, b_ref, c_ref, dy_ref, st_ref, dtc_ref, dtr_ref, bc_ref, br_ref, ac_ref, ar_ref, dk_ref = refs[:12]
        sin = refs[12 : 12 + nb]
        dx_ref, db_ref, dc_ref, ddt_ref, dbias_ref, dalog_ref, ddk_ref = refs[12 + nb : 19 + nb]
        sout = refs[19 + nb : 19 + 2 * nb]
        dstate, send_sems, recv_sems = refs[19 + 2 * nb :]
        ci = pl.program_id(1)
        c = NC - 1 - ci

        @pl.when((pl.program_id(0) == 0) & (ci == 0))
        def _():
            _scatter_start(sin, sout, send_sems, recv_sems)

        @pl.when(ci == 0)
        def _():
            dstate[...] = jnp.zeros_like(dstate)
            dbias_ref[...] = jnp.zeros_like(dbias_ref)
            dalog_ref[...] = jnp.zeros_like(dalog_ref)
            ddk_ref[...] = jnp.zeros_like(ddk_ref)

        cm = _ssd_common(dtc_ref, dtr_ref, bc_ref, br_ref, ac_ref, ar_ref, c)
        Bm = b_ref[...]
        Cm = c_ref[...]
        Bb = Bm.astype(BF16)
        Cb = Cm.astype(BF16)
        nt = (((1,), (1,)), ((), ()))
        cb = lax.dot_general(Cb, Bb, nt, preferred_element_type=F32)
        ct = Cm.T.astype(BF16)
        lane_lo = lax.broadcasted_iota(jnp.int32, (1, HP), 1) < P
        lane_id = lax.broadcasted_iota(jnp.int32, (1, LANES), 1)
        csc, csr, dtc_v, a_c = cm["csc"], cm["csr"], cm["dtc"], cm["a_c"]
        triu = cm["si"] >= cm["li"]
        ecs = jnp.exp(csc)
        cs_end = csc[Q - 1 : Q, :]
        wdec = jnp.exp(cs_end - csc)
        eend = jnp.exp(cs_end)
        dcb = jnp.zeros((Q, Q), F32)
        dcs = jnp.zeros((Q, LANES), F32)
        dcs_end = jnp.zeros((1, LANES), F32)
        ddt = jnp.zeros((Q, LANES), F32)
        ddk = jnp.zeros((1, LANES), F32)
        dB = jnp.zeros((Q, N), F32)
        dC = jnp.zeros((Q, N), F32)
        for pp in range(E // 2):
            h0 = 2 * pp
            sl = slice(pp * HP, (pp + 1) * HP)
            xp = xs_ref[:, sl]
            dyp = dy_ref[:, sl]
            dtx = _pair_expand(dtc_v, h0, lane_lo)
            xdt = xp * dtx
            dxdt = jnp.zeros((Q, HP), F32)
            stp = st_ref[:, sl]
            stb = stp.astype(BF16)
            dsn = dstate[:, sl]
            dsnb = dsn.astype(BF16)
            ecsx = _pair_expand(ecs, h0, lane_lo)
            wdx = _pair_expand(wdec, h0, lane_lo)
            cs_ = jnp.dot(Cb, stb, preferred_element_type=F32)
            yo = cs_ * ecsx
            dyo = dyp * ecsx
            dyob = dyo.astype(BF16)
            dC = dC + lax.dot_general(dyob, stb, nt, preferred_element_type=F32)
            ds_from_y = jnp.dot(ct, dyob, preferred_element_type=F32)
            xw = xdt * wdx
            dB = dB + lax.dot_general(xw.astype(BF16), dsnb, nt, preferred_element_type=F32)
            dxw = jnp.dot(Bb, dsnb, preferred_element_type=F32)
            dxdt = dxdt + dxw * wdx
            w2 = dxw * xw
            rs = jnp.sum(dsn * stp, axis=0, keepdims=True) * _pair_expand(eend, h0, lane_lo)
            dstate[:, sl] = dsn * _pair_expand(eend, h0, lane_lo) + ds_from_y
            dyx = dyp * xp
            yd = jnp.zeros((Q, HP), F32)
            dxd = jnp.zeros((Q, HP), F32)
            for e in range(2):
                h = h0 + e
                msk = lane_lo if e == 0 else jnp.logical_not(lane_lo)
                col = csc[:, h : h + 1]
                row = csr[h : h + 1, :]
                lm = jnp.where(cm["tril"], jnp.exp(jnp.minimum(col - row, 0.0)), 0.0)
                dye = jnp.where(msk, dyp, 0.0).astype(BF16)
                xde = jnp.where(msk, xdt, 0.0).astype(BF16)
                gm = lax.dot_general(dye, xde, nt, preferred_element_type=F32)
                dcb = dcb + gm * lm
                mb = (cb * lm).astype(BF16)
                yd = yd + jnp.dot(mb, xde, preferred_element_type=F32)
                dxd = dxd + lax.dot_general(mb, dye, (((0,), (0,)), ((), ())), preferred_element_type=F32)
            dxdt = dxdt + dxd
            tt = dyp * yo - w2 + dyp.astype(BF16).astype(F32) * yd - xdt.astype(BF16).astype(F32) * dxd
            dx_ref[:, sl] = dxdt * dtx + dyp * _pair_expand(dk_ref[...], h0, lane_lo)
            dxx = dxdt * xp
            for e in range(2):
                h = h0 + e
                msk = lane_lo if e == 0 else jnp.logical_not(lane_lo)
                oh = (lane_id == h).astype(F32)
                dcs = dcs + jnp.sum(jnp.where(msk, tt, 0.0), axis=1, keepdims=True) * oh
                dcs_end = dcs_end + (_sum_all(jnp.where(msk, w2, 0.0)) + _sum_all(jnp.where(msk, rs, 0.0))) * oh
                ddk = ddk + _sum_all(jnp.where(msk, dyx, 0.0)) * oh
                ddt = ddt + jnp.sum(jnp.where(msk, dxx, 0.0), axis=1, keepdims=True) * oh
        dC = dC + jnp.dot(dcb.astype(BF16), Bb, preferred_element_type=F32)
        dB = dB + lax.dot_general(dcb.astype(BF16), Cb, (((0,), (0,)), ((), ())), preferred_element_type=F32)
        db_ref[...] = dB
        dc_ref[...] = dC
        last = (lax.broadcasted_iota(jnp.int32, (Q, 1), 0) == Q - 1).astype(F32)
        dcs = dcs + last * dcs_end
        dda = jnp.dot(triu.astype(F32), dcs, precision=HI, preferred_element_type=F32)
        ddt = ddt + dda * a_c
        da = jnp.sum(dda * dtc_v, axis=0, keepdims=True)
        draw = jnp.where(cm["rows"] >= PF, ddt * _sigmoid(cm["prec"]), 0.0)
        ddt_ref[...] = draw
        dbias_ref[...] += jnp.sum(draw, axis=0, keepdims=True)
        dalog_ref[...] += da * a_c
        ddk_ref[...] += ddk

        @pl.when((pl.program_id(0) == G - 1) & (ci == NC - 1))
        def _():
            _scatter_finish(sin, sout, send_sems, recv_sems)

    xs, bb, cc, dtcs, dtrs, pc, pr = _ssd_specs()

    def rev(spec_fn):
        return lambda g, ci: spec_fn(g, NC - 1 - ci)

    def rspec(spec):
        return pl.BlockSpec(spec.block_shape, rev(spec.index_map))

    xs_r, bb_r, cc_r, dtc_r, dtr_r = rspec(xs), rspec(bb), rspec(cc), rspec(dtcs), rspec(dtrs)
    st_r = pl.BlockSpec((None, None, N, GW), lambda g, ci: (NC - 1 - ci, g, 0, 0))
    gn = pl.BlockSpec((Q, N), lambda g, ci: (NC - 1 - ci, g))
    outs = pl.pallas_call(
        body,
        name=name,
        grid=(G, NC),
        in_specs=[xs_r, bb_r, cc_r, xs_r, st_r, dtc_r, dtr_r, pc, pr, pc, pr, pc] + [ANY] * nb,
        out_specs=[xs_r, gn, gn, dtc_r, pc, pc, pc] + [ANY] * nb,
        out_shape=[
            jax.ShapeDtypeStruct((LP, DI), F32),
            jax.ShapeDtypeStruct((LP, G * N), F32),
            jax.ShapeDtypeStruct((LP, G * N), F32),
            jax.ShapeDtypeStruct((G, LP, LANES), F32),
            jax.ShapeDtypeStruct((G, 1, LANES), F32),
            jax.ShapeDtypeStruct((G, 1, LANES), F32),
            jax.ShapeDtypeStruct((G, 1, LANES), F32),
        ]
        + [jax.ShapeDtypeStruct(b.shape, b.dtype) for b in scatter],
        scratch_shapes=[pltpu.VMEM((N, GW), F32)] + _scatter_sems(nb),
        compiler_params=_cparams(("arbitrary", "arbitrary")),
    )(xbc, xbc, xbc, dy, states, dtc, dtr, bias_c, bias_r, alog_c, alog_r, dskip_c, *scatter)
    return tuple(outs[:7]) + (_scatter_own(outs[7:], scatter),)


def _split_dot(x, u):
    hi = x.astype(BF16)
    lo = (x - hi.astype(F32)).astype(BF16)
    return jnp.dot(hi, u, preferred_element_type=F32) + jnp.dot(lo, u, preferred_element_type=F32)


def _suffix_sums(x, u):
    right = x[:, TQ:]
    return jnp.concatenate([_split_dot(x[:, :TQ], u) + jnp.sum(right, axis=1, keepdims=True), _split_dot(right, u)], axis=1)


def _mask_keys(x, vis):
    if vis is None:
        return x
    if vis.shape[1] == TQ:
        return jnp.concatenate([x[:, :TQ], jnp.where(vis, x[:, TQ:], 0.0)], axis=1)
    return jnp.where(vis, x, 0.0)


def _sb_block(qe, kblk, vis, a_run, u_gt):
    l = lax.dot_general(qe, kblk, (((1,), (1,)), ((), ())), preferred_element_type=F32)
    lk = jnp.minimum(-l, 0.0) - jnp.log(1.0 + jnp.exp(-jnp.abs(l)))
    lbeta = l + lk
    lk = _mask_keys(lk, vis)
    logw = lbeta + _suffix_sums(lk, u_gt) + a_run
    return lbeta, lk, logw


def _window(i, n):
    start = TQ * (i - 2 * n - 1)
    return pl.multiple_of(jnp.maximum(start, 0), TQ), start + TK


def _window_mask(i, n, t_idx, mode):
    if mode is None:
        return None
    if mode == "diag":
        return i * TQ + lax.broadcasted_iota(jnp.int32, (1, TQ), 1) < t_idx
    off, hi = _window(i, n)
    s_idx = off + lax.broadcasted_iota(jnp.int32, (1, TK), 1)
    return (s_idx < t_idx) & (s_idx >= PF) & (s_idx < hi)


def _descend(i, block, carry):
    def pack(n, c):
        return (n, jnp.max(jnp.maximum(c[0], c[1]))) + tuple(c)

    last = i // 2
    zero = jnp.int32(0)
    c0 = lax.cond(i >= 2, lambda c: tuple(block(zero, c, "diag")), lambda c: tuple(block(zero, c, "full")), tuple(carry))
    st = pack(jnp.int32(1), c0)
    st = lax.while_loop(lambda st: (st[0] < last) & (st[1] > -T_SKIP), lambda st: pack(st[0] + 1, block(st[0], st[2:], None)), st)
    st = lax.while_loop(lambda st: (st[0] == last) & (st[1] > -T_SKIP), lambda st: pack(st[0] + 1, block(last, st[2:], "full")), st)
    return st[2:]


def _attn_fwd(q, k, v, name):
    nq = LP // TQ

    def body(q_ref, k_ref, v_ref, o_ref):
        i = pl.program_id(1)
        qv = q_ref[...]
        lane_lo = lax.broadcasted_iota(jnp.int32, (1, HP), 1) < 64
        t_idx = i * TQ + lax.broadcasted_iota(jnp.int32, (TQ, 1), 0)
        ji = lax.broadcasted_iota(jnp.int32, (TQ, TQ), 0)
        si = lax.broadcasted_iota(jnp.int32, (TQ, TQ), 1)
        u_gt = (ji > si).astype(BF16)
        qs = [jnp.where(lane_lo, qv, jnp.zeros_like(qv)), jnp.where(lane_lo, jnp.zeros_like(qv), qv)]

        def block(n, carry, mode):
            a0, a1, acc = carry
            off, _ = _window(i, n)
            kblk = k_ref[pl.ds(off, TK), :]
            vblk = v_ref[pl.ds(off, TK), :]
            vis = _window_mask(i, n, t_idx, mode)
            new_a = []
            for e, a_run in enumerate((a0, a1)):
                _, lk, logw = _sb_block(qs[e], kblk, vis, a_run, u_gt)
                w = jnp.exp(logw)
                w = _mask_keys(w, vis)
                msk = lane_lo if e == 0 else jnp.logical_not(lane_lo)
                acc = acc + jnp.dot(w.astype(BF16), jnp.where(msk, vblk, jnp.zeros_like(vblk)), preferred_element_type=F32)
                new_a.append(a_run + jnp.sum(lk, axis=1, keepdims=True))
            return new_a[0], new_a[1], acc

        z1 = jnp.zeros((TQ, 1), F32)
        _, _, acc = _descend(i, block, (z1, z1, jnp.zeros((TQ, HP), F32)))
        o_ref[...] = acc

    return pl.pallas_call(
        body,
        name=name,
        grid=(D // HP, nq),
        in_specs=[
            pl.BlockSpec((TQ, HP), lambda j, i: (i, j)),
            pl.BlockSpec((LP, HP), lambda j, i: (0, j)),
            pl.BlockSpec((LP, HP), lambda j, i: (0, j)),
        ],
        out_specs=pl.BlockSpec((TQ, HP), lambda j, i: (i, j)),
        out_shape=jax.ShapeDtypeStruct((LP, D), F32),
        compiler_params=_cparams(("parallel", "arbitrary")),
    )(q, k, v)


def _attn_bwd(q, k, v, o, do, name):
    nq = LP // TQ

    def body(q_ref, k_ref, v_ref, o_ref, do_ref, dq_ref, dk_ref, dv_ref):
        i = pl.program_id(1)

        @pl.when(i == 0)
        def _():
            dk_ref[...] = jnp.zeros_like(dk_ref)
            dv_ref[...] = jnp.zeros_like(dv_ref)

        qv = q_ref[...]
        dov = do_ref[...]
        lane_lo = lax.broadcasted_iota(jnp.int32, (1, HP), 1) < 64
        t_idx = i * TQ + lax.broadcasted_iota(jnp.int32, (TQ, 1), 0)
        ji = lax.broadcasted_iota(jnp.int32, (TQ, TQ), 0)
        si = lax.broadcasted_iota(jnp.int32, (TQ, TQ), 1)
        u_gt = (ji > si).astype(BF16)
        u_ge = (ji >= si).astype(BF16)
        msks = [lane_lo, jnp.logical_not(lane_lo)]
        qs = [jnp.where(m, qv, jnp.zeros_like(qv)) for m in msks]
        dob = [jnp.where(m, dov, 0.0).astype(BF16) for m in msks]
        ov = o_ref[...]
        deltas = [jnp.sum(d.astype(F32) * ov, axis=1, keepdims=True) for d in dob]
        nt = (((1,), (1,)), ((), ()))
        tn = (((0,), (0,)), ((), ()))

        def block(n, carry, mode):
            a0, a1, p0, p1, dq = carry
            off, _ = _window(i, n)
            kblk = k_ref[pl.ds(off, TK), :]
            vblk = v_ref[pl.ds(off, TK), :]
            vis = _window_mask(i, n, t_idx, mode)
            new_a, new_p = [], []
            dk_acc = jnp.zeros((TK, HP), F32)
            dv_acc = jnp.zeros((TK, HP), F32)
            for e, (a_run, p_run) in enumerate(((a0, p0), (a1, p1))):
                lbeta, lk, logw = _sb_block(qs[e], kblk, vis, a_run, u_gt)
                sig = jnp.exp(lbeta)
                w = jnp.exp(logw)
                w = _mask_keys(w, vis)
                wb = w.astype(BF16)
                dw = lax.dot_general(dob[e], vblk, nt, preferred_element_type=F32)
                pm = wb.astype(F32) * dw
                cum_p = deltas[e] - (_suffix_sums(pm, u_ge) + p_run)
                dl = pm - (pm + cum_p) * sig
                dl = _mask_keys(dl, vis)
                dl = dl.astype(BF16)
                km = jnp.where(msks[e], kblk, jnp.zeros_like(kblk))
                dq = dq + jnp.dot(dl, km, preferred_element_type=F32)
                dk_acc = dk_acc + lax.dot_general(dl, qs[e], tn, preferred_element_type=F32)
                dv_acc = dv_acc + lax.dot_general(wb, dob[e], tn, preferred_element_type=F32)
                new_a.append(a_run + jnp.sum(lk, axis=1, keepdims=True))
                new_p.append(p_run + jnp.sum(pm, axis=1, keepdims=True))
            dk_ref[pl.ds(off, TK), :] += dk_acc
            dv_ref[pl.ds(off, TK), :] += dv_acc
            return new_a[0], new_a[1], new_p[0], new_p[1], dq

        z1 = jnp.zeros((TQ, 1), F32)
        carry = _descend(i, block, (z1, z1, z1, z1, jnp.zeros((TQ, HP), F32)))
        dq_ref[...] = carry[4]

    blk = pl.BlockSpec((TQ, HP), lambda j, i: (i, j))
    full = pl.BlockSpec((LP, HP), lambda j, i: (0, j))
    big = jax.ShapeDtypeStruct((LP, D), F32)
    return pl.pallas_call(
        body,
        name=name,
        grid=(D // HP, nq),
        in_specs=[blk, full, full, blk, blk],
        out_specs=[blk, full, full],
        out_shape=[big, big, big],
        compiler_params=_cparams(("parallel", "arbitrary")),
    )(q, k, v, o, do)


ROW_TILES = (2048, 1024, 512, 256, 128, 64, 32, 16, 8)


def _row_tile(rows, cols, budget):
    if rows % 8:
        return rows
    return _pick(rows, tuple(t for t in ROW_TILES if t * cols <= budget) or (8,))


def _adamw(w, g, m, v, name):
    R, C = w.shape
    tr = _row_tile(R, C, 128 * 1024)

    def body(w_ref, g_ref, m_ref, v_ref, g_out, d_out, m_out, v_out):
        g = g_ref[...]
        mn = ADAM_B1 * m_ref[...] + (1.0 - ADAM_B1) * g
        vn = ADAM_B2 * v_ref[...] + (1.0 - ADAM_B2) * (g * g)
        mh = mn / (1.0 - ADAM_B1**ADAM_STEP)
        vh = vn / (1.0 - ADAM_B2**ADAM_STEP)
        g_out[...] = g
        d_out[...] = -ADAM_LR * (mh / (jnp.sqrt(vh) + ADAM_EPS) + ADAM_WD * w_ref[...])
        m_out[...] = mn
        v_out[...] = vn

    blk = pl.BlockSpec((tr, C), lambda i: (i, 0))
    sh = jax.ShapeDtypeStruct((R, C), F32)
    return pl.pallas_call(
        body,
        name=name,
        grid=(R // tr,),
        in_specs=[blk] * 4,
        out_specs=[blk] * 4,
        out_shape=[sh] * 4,
        compiler_params=_cparams(("parallel",)),
    )(w, g, m, v)


def _sum4(buf, name):
    _, R, C = buf.shape
    tr = _row_tile(R, C, 128 * 1024)

    def body(b_ref, o_ref):
        acc = b_ref[0].astype(F32)
        for s in range(1, NCHIP):
            acc = acc + b_ref[s].astype(F32)
        o_ref[...] = acc

    return pl.pallas_call(
        body,
        name=name,
        grid=(R // tr,),
        in_specs=[pl.BlockSpec((NCHIP, tr, C), lambda i: (0, i, 0))],
        out_specs=pl.BlockSpec((tr, C), lambda i: (i, 0)),
        out_shape=jax.ShapeDtypeStruct((R, C), F32),
        compiler_params=_cparams(("parallel",)),
    )(buf)


def _add2(a, b, name):
    S, R, C = a.shape
    tr = _row_tile(R, C, 256 * 1024)

    def body(a_ref, b_ref, o_ref):
        o_ref[...] = (a_ref[...].astype(F32) + b_ref[...].astype(F32)).astype(o_ref.dtype)

    blk = pl.BlockSpec((None, tr, C), lambda s, i: (s, i, 0))
    return pl.pallas_call(
        body,
        name=name,
        grid=(S, R // tr),
        in_specs=[blk, blk],
        out_specs=blk,
        out_shape=jax.ShapeDtypeStruct(a.shape, a.dtype),
        compiler_params=_cparams(("parallel", "parallel")),
    )(a, b)


ANY = pl.BlockSpec(memory_space=pl.ANY)


def _mesh_place():
    x, y, c = lax.axis_index("x"), lax.axis_index("y"), lax.axis_index("c")
    return x, y, c, 2 * x + y, [(1 - x, y), (x, 1 - y), (1 - x, 1 - y)]


def _gather_chips(bufs, name):
    nb = len(bufs)

    def body(*refs):
        ins = refs[:nb]
        outs = refs[nb : 2 * nb]
        send_sems, recv_sems = refs[2 * nb :]
        _gather_start(ins, outs, send_sems, recv_sems)
        _gather_finish(ins, outs, send_sems, recv_sems)

    outs = pl.pallas_call(
        body,
        name=name,
        in_specs=[ANY] * nb,
        out_specs=[ANY] * nb,
        out_shape=_gather_shapes(bufs),
        scratch_shapes=_gather_sems(nb),
    )(*bufs)
    return _gather_own(outs, bufs)


def _gather_shapes(bufs):
    return [jax.ShapeDtypeStruct((NCHIP,) + tuple(b.shape), b.dtype) for b in bufs]


def _gather_sems(nb):
    return [pltpu.SemaphoreType.DMA((6 * nb,)), pltpu.SemaphoreType.DMA((6 * nb,))]


def _gather_own(outs, bufs):
    me = 2 * lax.axis_index("x") + lax.axis_index("y")
    return [lax.dynamic_update_slice(o, b[None], (me, 0, 0)) for o, b in zip(outs, bufs)]


def _gather_copy(outs, send_sems, recv_sems, k, b, src, slot, hc, to):
    nb = len(outs)
    hr = outs[b].shape[1] // 2
    return pltpu.make_async_remote_copy(
        src_ref=src, dst_ref=outs[b].at[slot, pl.ds(hc * hr, hr)], send_sem=send_sems.at[k * nb + b],
        recv_sem=recv_sems.at[k * nb + b], device_id=to, device_id_type=MESH)


def _gather_start(ins, outs, send_sems, recv_sems):
    x, y, c, me, peers = _mesh_place()
    for k, (px, py) in enumerate(peers):
        for b in range(len(ins)):
            hr = ins[b].shape[0] // 2
            _gather_copy(outs, send_sems, recv_sems, k, b, ins[b].at[pl.ds(c * hr, hr)], me, c, (px, py, c)).start()


def _gather_finish(ins, outs, send_sems, recv_sems):
    x, y, c, me, peers = _mesh_place()
    nb = len(ins)
    sent = []
    for k, (px, py) in enumerate(peers):
        for b in range(nb):
            hr = ins[b].shape[0] // 2
            slot = 2 * px + py
            landed = outs[b].at[slot, pl.ds(c * hr, hr)]
            _gather_copy(outs, send_sems, recv_sems, k, b, landed, slot, c, (px, py, c)).wait_recv()
            cp = _gather_copy(outs, send_sems, recv_sems, 3 + k, b, landed, slot, c, (x, y, 1 - c))
            cp.start()
            sent.append(cp)
            sent.append(_gather_copy(outs, send_sems, recv_sems, k, b, ins[b].at[pl.ds(c * hr, hr)], me, c, (px, py, c)))
    for k, (px, py) in enumerate(peers):
        for b in range(nb):
            hr = ins[b].shape[0] // 2
            slot = 2 * px + py
            theirs = outs[b].at[slot, pl.ds((1 - c) * hr, hr)]
            _gather_copy(outs, send_sems, recv_sems, 3 + k, b, theirs, slot, 1 - c, (x, y, 1 - c)).wait_recv()
    for cp in sent:
        cp.wait_send()


def _scatter_chips(bufs, name):
    nb = len(bufs)

    def body(*refs):
        ins = refs[:nb]
        outs = refs[nb : 2 * nb]
        send_sems, recv_sems = refs[2 * nb :]
        _scatter_start(ins, outs, send_sems, recv_sems)
        _scatter_finish(ins, outs, send_sems, recv_sems)

    outs = pl.pallas_call(
        body,
        name=name,
        in_specs=[ANY] * nb,
        out_specs=[ANY] * nb,
        out_shape=[jax.ShapeDtypeStruct(b.shape, b.dtype) for b in bufs],
        scratch_shapes=_scatter_sems(nb),
    )(*bufs)
    return _scatter_own(outs, bufs)


def _scatter_sems(nb):
    return [pltpu.SemaphoreType.DMA((3 * nb,)), pltpu.SemaphoreType.DMA((3 * nb,))]


def _scatter_own(outs, bufs):
    me = 2 * lax.axis_index("x") + lax.axis_index("y")
    return [lax.dynamic_update_slice(o, lax.dynamic_slice_in_dim(b, me, 1, axis=0), (me, 0, 0)) for o, b in zip(outs, bufs)]


def _scatter_copy(ins, outs, send_sems, recv_sems, k, b, slot_from, slot_to, to):
    nb = len(ins)
    return pltpu.make_async_remote_copy(
        src_ref=ins[b].at[slot_from], dst_ref=outs[b].at[slot_to], send_sem=send_sems.at[k * nb + b],
        recv_sem=recv_sems.at[k * nb + b], device_id=to, device_id_type=MESH)


def _scatter_start(ins, outs, send_sems, recv_sems):
    x, y, c, me, peers = _mesh_place()
    for k, (px, py) in enumerate(peers):
        for b in range(len(ins)):
            _scatter_copy(ins, outs, send_sems, recv_sems, k, b, 2 * px + py, me, (px, py, c)).start()


def _scatter_finish(ins, outs, send_sems, recv_sems):
    x, y, c, me, peers = _mesh_place()
    for k, (px, py) in enumerate(peers):
        for b in range(len(ins)):
            _scatter_copy(ins, outs, send_sems, recv_sems, k, b, me, 2 * px + py, (px, py, c)).wait_recv()
    for k, (px, py) in enumerate(peers):
        for b in range(len(ins)):
            _scatter_copy(ins, outs, send_sems, recv_sems, k, b, 2 * px + py, me, (px, py, c)).wait_send()


def _split_cores(bufs, name):
    nb = len(bufs)

    def body(*refs):
        ins = refs[:nb]
        theirs = refs[nb : 2 * nb]
        send_sems, recv_sems = refs[2 * nb :]
        x, y, c, _, _ = _mesh_place()
        cps = []
        for b in range(nb):
            hr = ins[b].shape[1] // 2
            cp = pltpu.make_async_remote_copy(
                src_ref=ins[b].at[:, pl.ds((1 - c) * hr, hr)], dst_ref=theirs[b], send_sem=send_sems.at[b],
                recv_sem=recv_sems.at[b], device_id=(x, y, 1 - c), device_id_type=MESH)
            cp.start()
            cps.append(cp)
        for cp in cps:
            cp.wait()

    theirs = pl.pallas_call(
        body,
        name=name,
        in_specs=[ANY] * nb,
        out_specs=[ANY] * nb,
        out_shape=[jax.ShapeDtypeStruct((b.shape[0], b.shape[1] // 2, b.shape[2]), b.dtype) for b in bufs],
        scratch_shapes=[pltpu.SemaphoreType.DMA((nb,)), pltpu.SemaphoreType.DMA((nb,))],
    )(*bufs)
    c = lax.axis_index("c")
    mine = [lax.dynamic_slice_in_dim(b, c * (b.shape[1] // 2), b.shape[1] // 2, axis=1) for b in bufs]
    return mine, theirs


def _join_cores(bufs, name):
    nb = len(bufs)

    def body(*refs):
        ins = refs[:nb]
        outs = refs[nb : 2 * nb]
        send_sems, recv_sems = refs[2 * nb :]
        x, y, c, _, _ = _mesh_place()
        cps = []
        for b in range(nb):
            hr = ins[b].shape[0]
            cp = pltpu.make_async_remote_copy(
                src_ref=ins[b], dst_ref=outs[b].at[pl.ds(c * hr, hr)], send_sem=send_sems.at[b], recv_sem=recv_sems.at[b],
                device_id=(x, y, 1 - c), device_id_type=MESH)
            cp.start()
            cps.append(cp)
        for b, cp in enumerate(cps):
            hr = ins[b].shape[0]
            cp.wait_send()
            pltpu.make_async_remote_copy(
                src_ref=ins[b], dst_ref=outs[b].at[pl.ds((1 - c) * hr, hr)], send_sem=send_sems.at[b],
                recv_sem=recv_sems.at[b], device_id=(x, y, 1 - c), device_id_type=MESH).wait_recv()

    outs = pl.pallas_call(
        body,
        name=name,
        in_specs=[ANY] * nb,
        out_specs=[ANY] * nb,
        out_shape=[jax.ShapeDtypeStruct((2 * b.shape[0], b.shape[1]), b.dtype) for b in bufs],
        scratch_shapes=[pltpu.SemaphoreType.DMA((nb,)), pltpu.SemaphoreType.DMA((nb,))],
    )(*bufs)
    c = lax.axis_index("c")
    return [lax.dynamic_update_slice(o, b, (c * b.shape[0], 0)) for o, b in zip(outs, bufs)]


ROW_ALIGN = 1024


def _pack(pieces, dtype):
    flat = []
    for p in pieces:
        f = p.reshape(-1).astype(dtype)
        pad = (-f.shape[0]) % LANES
        if pad:
            f = jnp.pad(f, (0, pad))
        flat.append(f)
    tot = sum(f.shape[0] for f in flat)
    pad = (-tot) % (ROW_ALIGN * LANES)
    if pad:
        flat.append(jnp.zeros((pad,), dtype))
    return jnp.concatenate(flat).reshape(-1, LANES)


def _unpack(buf, shapes):
    lead = buf.shape[:-2]
    flat = buf.reshape(lead + (-1,))
    out = []
    off = 0
    for shp in shapes:
        n = 1
        for d in shp:
            n *= d
        out.append(flat[..., off : off + n].reshape(lead + tuple(shp)))
        off += n + ((-n) % LANES)
    return out


PARAMS = (
    ("meta_tokens", 1, "small"), ("ssd_norm", 1, "small"), ("ssd_w_in", 2, "big"), ("ssd_conv_w", 2, "small"),
    ("ssd_conv_b", 1, "small"), ("ssd_dt_bias", None, "rep"), ("ssd_a_log", None, "rep"), ("ssd_d_skip", None, "rep"),
    ("ssd_gate_norm", 1, "small"), ("ssd_w_out", 1, "big"), ("kv_norm", None, "rep"), ("w_kv", 1, "big"),
    ("sb_norm", None, "rep"), ("sb_w_q", 1, "big"), ("sb_w_o", 1, "big"), ("ffn_norm", None, "rep"),
    ("ffn_w_up", 2, "big"), ("ffn_conv_w", 2, "small"), ("ffn_conv_b", None, "rep"), ("ffn_w_down", 1, "big"),
    ("final_norm", None, "rep"),
)


def _head_cols(vec):
    return jnp.pad(vec.reshape(G, 1, E), ((0, 0), (0, 0), (0, LANES - E)))


def _head_rows(vec):
    return jnp.pad(vec.reshape(G, E, 1), ((0, 0), (0, 8 - E), (0, 0)))


def kernel(x, meta_tokens, ssd_norm, ssd_w_in, ssd_conv_w, ssd_conv_b, ssd_dt_bias, ssd_a_log, ssd_d_skip, ssd_gate_norm, ssd_w_out, kv_norm, w_kv, sb_norm, sb_w_q, sb_w_o, ffn_norm, ffn_w_up, ffn_conv_w, ffn_conv_b, ffn_w_down, final_norm, loss_target, m_meta_tokens, m_ssd_norm, m_ssd_w_in, m_ssd_conv_w, m_ssd_conv_b, m_ssd_dt_bias, m_ssd_a_log, m_ssd_d_skip, m_ssd_gate_norm, m_ssd_w_out, m_kv_norm, m_w_kv, m_sb_norm, m_sb_w_q, m_sb_w_o, m_ffn_norm, m_ffn_w_up, m_ffn_conv_w, m_ffn_conv_b, m_ffn_w_down, m_final_norm, v_meta_tokens, v_ssd_norm, v_ssd_w_in, v_ssd_conv_w, v_ssd_conv_b, v_ssd_dt_bias, v_ssd_a_log, v_ssd_d_skip, v_ssd_gate_norm, v_ssd_w_out, v_kv_norm, v_w_kv, v_sb_norm, v_sb_w_q, v_sb_w_o, v_ffn_norm, v_ffn_w_up, v_ffn_conv_w, v_ffn_conv_b, v_ffn_w_down, v_final_norm):
    local = dict(meta_tokens=meta_tokens, ssd_norm=ssd_norm, ssd_w_in=ssd_w_in, ssd_conv_w=ssd_conv_w, ssd_conv_b=ssd_conv_b, ssd_dt_bias=ssd_dt_bias, ssd_a_log=ssd_a_log, ssd_d_skip=ssd_d_skip, ssd_gate_norm=ssd_gate_norm, ssd_w_out=ssd_w_out, kv_norm=kv_norm, w_kv=w_kv, sb_norm=sb_norm, sb_w_q=sb_w_q, sb_w_o=sb_w_o, ffn_norm=ffn_norm, ffn_w_up=ffn_w_up, ffn_conv_w=ffn_conv_w, ffn_conv_b=ffn_conv_b, ffn_w_down=ffn_w_down, final_norm=final_norm)
    mom_m = dict(meta_tokens=m_meta_tokens, ssd_norm=m_ssd_norm, ssd_w_in=m_ssd_w_in, ssd_conv_w=m_ssd_conv_w, ssd_conv_b=m_ssd_conv_b, ssd_dt_bias=m_ssd_dt_bias, ssd_a_log=m_ssd_a_log, ssd_d_skip=m_ssd_d_skip, ssd_gate_norm=m_ssd_gate_norm, ssd_w_out=m_ssd_w_out, kv_norm=m_kv_norm, w_kv=m_w_kv, sb_norm=m_sb_norm, sb_w_q=m_sb_w_q, sb_w_o=m_sb_w_o, ffn_norm=m_ffn_norm, ffn_w_up=m_ffn_w_up, ffn_conv_w=m_ffn_conv_w, ffn_conv_b=m_ffn_conv_b, ffn_w_down=m_ffn_w_down, final_norm=m_final_norm)
    mom_v = dict(meta_tokens=v_meta_tokens, ssd_norm=v_ssd_norm, ssd_w_in=v_ssd_w_in, ssd_conv_w=v_ssd_conv_w, ssd_conv_b=v_ssd_conv_b, ssd_dt_bias=v_ssd_dt_bias, ssd_a_log=v_ssd_a_log, ssd_d_skip=v_ssd_d_skip, ssd_gate_norm=v_ssd_gate_norm, ssd_w_out=v_ssd_w_out, kv_norm=v_kv_norm, w_kv=v_w_kv, sb_norm=v_sb_norm, sb_w_q=v_sb_w_q, sb_w_o=v_sb_w_o, ffn_norm=v_ffn_norm, ffn_w_up=v_ffn_w_up, ffn_conv_w=v_ffn_conv_w, ffn_conv_b=v_ffn_conv_b, ffn_w_down=v_ffn_w_down, final_norm=v_final_norm)

    big_names = [n for n, _, kind in PARAMS if kind == "big"]
    small_names = [n for n, _, kind in PARAMS if kind == "small"]
    rep_names = [n for n, _, kind in PARAMS if kind == "rep"]
    axis_of = {n: ax for n, ax, _ in PARAMS}

    def rows2(a):
        return a.reshape(-1, a.shape[-1])

    first_big, later_big = big_names[:1], big_names[1:]
    full = {}

    def assemble(names, bufs):
        for n, buf in zip(names, bufs):
            p = buf.reshape((NCHIP,) + local[n].shape)
            full[n] = jnp.concatenate([p[s] for s in range(NCHIP)], axis=axis_of[n])

    small_own = _pack([local[n] for n in small_names], F32)
    gathered = _gather_chips([rows2(local[n]).astype(BF16) for n in first_big] + [small_own], "gather_first")
    assemble(first_big, gathered[:-1])
    for n, p in zip(small_names, _unpack(gathered[-1], [local[n].shape for n in small_names])):
        full[n] = jnp.concatenate([p[s] for s in range(NCHIP)], axis=axis_of[n])
    for n in rep_names:
        full[n] = local[n]

    w_in = full["ssd_w_in"][0]
    w_z, w_xbc = w_in[:, :DI], w_in[:, DI : DI + CD]
    w_dt = jnp.pad(w_in[:, DI + CD :], ((0, 0), (0, LANES - H)))
    fcw, fcb = full["ffn_conv_w"], full["ffn_conv_b"]
    scw, scb = full["ssd_conv_w"][0], full["ssd_conv_b"]
    bias_c, bias_r = _head_cols(full["ssd_dt_bias"][0]), _head_rows(full["ssd_dt_bias"][0])
    alog_c, alog_r = _head_cols(full["ssd_a_log"][0]), _head_rows(full["ssd_a_log"][0])
    dskip_c = _head_cols(full["ssd_d_skip"][0])
    kvn = full["kv_norm"].reshape(1, D)
    fin = full["final_norm"].reshape(1, D)

    h0 = jnp.concatenate([jnp.zeros((PF, D), F32), full["meta_tokens"], x[0]], axis=0)
    (u0,) = _rms_fwd(h0, [full["ssd_norm"]], "ssd_norm_fwd")
    z = _mm(u0, w_z, name="ssd_in_z")
    xr = _mm(u0, w_xbc, name="ssd_in_xbc")
    dt_raw = _mm(u0, w_dt, name="ssd_in_dt")
    xbc, xpre = _ssd_conv_fwd(xr, scw, scb, "ssd_conv_fwd")
    dth = dt_raw[:, :H].reshape(LP, G, E)
    dtc = jnp.pad(jnp.transpose(dth, (1, 0, 2)), ((0, 0), (0, 0), (0, LANES - E)))
    dtr = jnp.pad(jnp.transpose(dth, (1, 2, 0)), ((0, 0), (0, 8 - E), (0, 0)))
    later_own = [rows2(local[n]).astype(BF16) for n in later_big]
    y, states, later_all = _ssd_fwd(xbc, dtc, dtr, bias_c, bias_r, alog_c, alog_r, dskip_c, later_own, "ssd_scan_fwd")
    assemble(later_big, later_all)
    w_out = full["ssd_w_out"][0]
    wkv = full["w_kv"]
    w_q = full["sb_w_q"][0]
    w_o = full["sb_w_o"][0]
    w_up_g = [full["ffn_w_up"][l][:, :DFF] for l in range(2)]
    w_up_v = [full["ffn_w_up"][l][:, DFF:] for l in range(2)]
    w_down = [full["ffn_w_down"][l] for l in range(2)]
    hgn = _gate_fwd(y, z, full["ssd_gate_norm"], "ssd_gate_fwd")
    h1 = _mm(hgn, w_out, add=h0, mask_rows=True, name="ssd_out")

    def ffn_fwd(h, l, tag):
        (u,) = _rms_fwd(h, [full["ffn_norm"][l : l + 1]], f"ffn{tag}_norm_fwd")
        hg = _mm(u, w_up_g[l], name=f"ffn{tag}_up_g")
        hv = _mm(u, w_up_v[l], name=f"ffn{tag}_up_v")
        act, gpre, vpre = _ffn_act_fwd(hg, hv, fcw[l][:, :DFF], fcw[l][:, DFF:], fcb[l : l + 1, :DFF], fcb[l : l + 1, DFF:], f"ffn{tag}_act_fwd")
        hn = _mm(act, w_down[l], add=h, mask_rows=True, name=f"ffn{tag}_down")
        return hn, (u, hg, hv, act, gpre, vpre)

    h2, ffn0 = ffn_fwd(h1, 0, "0")
    ukv, uq = _rms_fwd(h2, [kvn, full["sb_norm"]], "attn_norm_fwd")
    kk = _mm(ukv, wkv[:, :D], out_dtype=BF16, name="attn_k")
    vv = _mm(ukv, wkv[:, D:], out_dtype=BF16, name="attn_v")
    qq = _mm(uq, w_q, out_dtype=BF16, scale=64.0**-0.5, name="attn_q")
    o = _attn_fwd(qq, kk, vv, "attn_fwd")
    h3 = _mm(o, w_o, add=h2, mask_rows=True, name="attn_out")
    h4, ffn1 = ffn_fwd(h3, 1, "1")
    dh, g_final, loss_rows = _loss_head(h4, fin, loss_target[0], "loss_head")
    loss = lax.psum(0.5 / D * jnp.sum(loss_rows), ("x", "y", "c"))

    grads = {"final_norm": g_final.reshape(D)}

    def ffn_bwd(dh, h, l, saved, tag):
        u, hg, hv, act, gpre, vpre = saved
        da = _mm(dh, w_down[l], tb=True, name=f"ffn{tag}_down_dx")
        gw_down = _mm(act, dh, ta=True, out_dtype=BF16, name=f"ffn{tag}_down_dw")
        dhg, dhv, dwg, dwv, dbg, dbv = _ffn_act_bwd(hg, hv, gpre, vpre, da, fcw[l][:, :DFF], fcw[l][:, DFF:], f"ffn{tag}_act_bwd")
        gw_up = jnp.concatenate([_mm(u, dhg, ta=True, out_dtype=BF16, name=f"ffn{tag}_up_g_dw"), _mm(u, dhv, ta=True, out_dtype=BF16, name=f"ffn{tag}_up_v_dw")], axis=1)
        du = _mm(dhg, w_up_g[l], tb=True, name=f"ffn{tag}_up_g_dx")
        du = _mm(dhv, w_up_v[l], tb=True, add=du, name=f"ffn{tag}_up_v_dx")
        dh_new, (gn,) = _rms_bwd(dh, h, [du], [full["ffn_norm"][l : l + 1]], f"ffn{tag}_norm_bwd")
        return dh_new, gw_down, gw_up, jnp.concatenate([dwg, dwv], axis=1), jnp.concatenate([dbg, dbv], axis=1), gn

    dh, gd1, gu1, gcw1, gcb1, gn1 = ffn_bwd(dh, h3, 1, ffn1, "1")
    do = _mm(dh, w_o, tb=True, name="attn_out_dx")
    grads["sb_w_o"] = _mm(o, dh, ta=True, out_dtype=BF16, name="attn_out_dw")[None]
    dq, dk, dv = _attn_bwd(qq, kk, vv, o, do, "attn_bwd")
    grads["sb_w_q"] = _mm(uq, dq, ta=True, out_dtype=BF16, scale=64.0**-0.5, name="attn_q_dw")[None]
    grads["w_kv"] = jnp.concatenate([_mm(ukv, dk, ta=True, out_dtype=BF16, name="attn_k_dw"), _mm(ukv, dv, ta=True, out_dtype=BF16, name="attn_v_dw")], axis=1)
    duq = _mm(dq, w_q, tb=True, scale=64.0**-0.5, name="attn_q_dx")
    dukv = _mm(dk, wkv[:, :D], tb=True, name="attn_k_dx")
    dukv = _mm(dv, wkv[:, D:], tb=True, add=dukv, name="attn_v_dx")
    dh, (g_kvn, g_sbn) = _rms_bwd(dh, h2, [dukv, duq], [kvn, full["sb_norm"]], "attn_norm_bwd")
    grads["kv_norm"] = g_kvn.reshape(D)
    grads["sb_norm"] = g_sbn
    dh, gd0, gu0, gcw0, gcb0, gn0 = ffn_bwd(dh, h1, 0, ffn0, "0")
    grads["ffn_w_down"] = jnp.stack([gd0, gd1])
    grads["ffn_w_up"] = jnp.stack([gu0, gu1])
    grads["ffn_conv_w"] = jnp.stack([gcw0, gcw1])
    grads["ffn_conv_b"] = jnp.concatenate([gcb0, gcb1], axis=0)
    grads["ffn_norm"] = jnp.concatenate([gn0, gn1], axis=0)
    dhgn = _mm(dh, w_out, tb=True, name="ssd_out_dx")
    grads["ssd_w_out"] = _mm(hgn, dh, ta=True, out_dtype=BF16, name="ssd_out_dw")[None]
    dy, dz, g_gate = _gate_bwd(dhgn, y, z, full["ssd_gate_norm"], "ssd_gate_bwd")
    grads["ssd_gate_norm"] = g_gate
    def slots(n):
        return jnp.stack([rows2(p) for p in jnp.split(grads[n], NCHIP, axis=axis_of[n])])

    mine, theirs = _split_cores([slots(n) for n in later_big], "split_cores_a")
    pair = [_add2(a, b, f"pair_sum_a{i}") for i, (a, b) in enumerate(zip(mine, theirs))]
    dxs, dB, dC, ddt_raw, g_bias, g_alog, g_dskip, got_a = _ssd_bwd(
        xbc, dy, states, dtc, dtr, bias_c, bias_r, alog_c, alog_r, dskip_c, pair, "ssd_scan_bwd")
    grads["ssd_dt_bias"] = g_bias[:, 0, :E].reshape(1, H)
    grads["ssd_a_log"] = g_alog[:, 0, :E].reshape(1, H)
    grads["ssd_d_skip"] = g_dskip[:, 0, :E].reshape(1, H)
    dxr, g_scw, g_scb = _ssd_conv_bwd(xr, xpre, jnp.concatenate([dxs, dB, dC], axis=1), scw, "ssd_conv_bwd")
    grads["ssd_conv_w"] = g_scw[None]
    grads["ssd_conv_b"] = g_scb
    ddt = jnp.pad(jnp.transpose(ddt_raw[:, :, :E], (1, 0, 2)).reshape(LP, H), ((0, 0), (0, LANES - H)))
    grads["ssd_w_in"] = jnp.concatenate(
        [_mm(u0, dz, ta=True, out_dtype=BF16, name="ssd_in_z_dw"), _mm(u0, dxr, ta=True, out_dtype=BF16, name="ssd_in_xbc_dw"), _mm(u0, ddt, ta=True, out_dtype=BF16, name="ssd_in_dt_dw")[:, :H]], axis=1)[None]
    du = _mm(dz, w_z, tb=True, name="ssd_in_z_dx")
    du = _mm(dxr, w_xbc, tb=True, add=du, name="ssd_in_xbc_dx")
    du = _mm(ddt, w_dt, tb=True, add=du, name="ssd_in_dt_dx")
    dh, (g_ssdn,) = _rms_bwd(dh, h0, [du], [full["ssd_norm"]], "ssd_norm_bwd")
    grads["ssd_norm"] = g_ssdn
    grads["meta_tokens"] = dh[PF : PF + N_META]
    grad_x = dh[PF + N_META :][None]

    def shard_pieces(names, s):
        out = []
        for n in names:
            ax = axis_of[n]
            out.append(grads[n] if ax is None else jnp.split(grads[n], NCHIP, axis=ax)[s])
        return out

    bufs = [slots(n) for n in first_big]
    bufs.append(jnp.stack([_pack(shard_pieces(small_names + rep_names, s), F32) for s in range(NCHIP)]))
    mine, theirs = _split_cores(bufs, "split_cores_b")
    pair = [_add2(a, b, f"pair_sum_b{i}") for i, (a, b) in enumerate(zip(mine, theirs))]
    got_b = _scatter_chips(pair, "scatter_grads")
    got = got_b[:-1] + got_a + got_b[-1:]
    sums = [_sum4(b, f"sum_chips_{i}") for i, b in enumerate(got)]
    gsum = _join_cores(sums, "join_cores")

    def rows(a):
        f = a.reshape(-1)
        pad = (-f.shape[0]) % LANES
        if pad:
            f = jnp.pad(f, (0, pad))
        return f.reshape(-1, LANES)

    order = [n for n, _, _ in PARAMS]
    res = {}
    for n, g2 in zip(big_names, gsum[:-1]):
        outs = _adamw(rows2(local[n]), g2, rows2(mom_m[n]), rows2(mom_v[n]), f"adamw_{n}")
        res[n] = [o_.reshape(local[n].shape) for o_ in outs]
    rest = small_names + rep_names
    for n, g1 in zip(rest, _unpack(gsum[-1], [local[n].shape for n in rest])):
        shp = local[n].shape
        cnt = 1
        for d in shp:
            cnt *= d
        outs = _adamw(rows(local[n]), rows(g1), rows(mom_m[n]), rows(mom_v[n]), f"adamw_{n}")
        res[n] = [o_.reshape(-1)[:cnt].reshape(shp) for o_ in outs]
    return (loss, grad_x, *[res[n][0] for n in order], *[res[n][1] for n in order], *[res[n][2] for n in order], *[res[n][3] for n in order])
```

```python
import functools

import jax
import jax.numpy as jnp
from jax import lax
from jax.experimental import pallas as pl
from jax.experimental.pallas import tpu as pltpu

D = 1024
SEQ = 8192
N_META = 16
EPS = 1e-6
P = 64
G = 4
N = 128
CONVW = 4
Q = 256
FC = 3
DFF = 256 * ((8 * D // 3 + 255) // 256)
DI = 2 * D
H = DI // P
E = H // G
GW = E * P
CD = DI + 2 * G * N
IN = DI + CD + H
SBH = D // 64
HP = 128
LANES = 128
PF = Q - N_META
LP = PF + N_META + SEQ
NC = LP // Q
TQ = 256
TK = 2 * TQ
NCHIP = 4
ADAM_LR, ADAM_B1, ADAM_B2, ADAM_EPS, ADAM_WD, ADAM_STEP = 0.001, 0.9, 0.999, 1e-08, 0.01, 10

F32 = jnp.float32
BF16 = jnp.bfloat16
HI = lax.Precision.HIGHEST
MESH = pl.DeviceIdType.MESH
VMEM_LIMIT = 48 * 1024 * 1024
MM_MAX_K = 3072
T_SKIP = 110.0


def _pick(n, cands):
    for c in cands:
        if n % c == 0:
            return c
    raise ValueError((n, cands))


def _cparams(sem):
    return pltpu.CompilerParams(dimension_semantics=sem, vmem_limit_bytes=VMEM_LIMIT)


def _valid_rows(block, rows):
    r = block * rows + lax.broadcasted_iota(jnp.int32, (rows, 1), 0)
    return r >= PF


def _sigmoid(x):
    return 1.0 / (1.0 + jnp.exp(-x))


def _softplus(x):
    return jnp.maximum(x, 0.0) + jnp.log(1.0 + jnp.exp(-jnp.abs(x)))


def _sum_all(x):
    return jnp.sum(jnp.sum(x, axis=1, keepdims=True), axis=0, keepdims=True)


def _dsilu(x):
    s = _sigmoid(x)
    return s * (1.0 + x * (1.0 - s))


def _mm(a, b, *, ta=False, tb=False, out_dtype=F32, add=None, mask_rows=False, scale=None, name):
    if ta:
        K, M = a.shape
    else:
        M, K = a.shape
    if tb:
        Nn, K2 = b.shape
    else:
        K2, Nn = b.shape
    assert K == K2, (a.shape, b.shape, ta, tb)
    tn = _pick(Nn, (1408, 1024, 768, 512, 256, 128))
    if ta:
        tm = _pick(M, (1408, 1024, 768, 512, 256, 128))
        tk = _pick(K, (768, 512, 256))
    else:
        tm = _pick(M, (768, 256))
        tk = K if K <= MM_MAX_K else _pick(K, (1024, 768, 512, 256, 128))
    nk = K // tk
    dims = (((0 if ta else 1,), (1 if tb else 0,)), ((), ()))

    def body(*refs):
        a_ref, b_ref = refs[0], refs[1]
        add_ref = refs[2] if add is not None else None
        o_ref = refs[3] if add is not None else refs[2]
        acc = refs[-1] if nk > 1 else None

        def finish(r):
            if scale is not None:
                r = r * scale
            if mask_rows:
                r = jnp.where(_valid_rows(pl.program_id(0), tm), r, 0.0)
            if add_ref is not None:
                r = r + add_ref[...]
            o_ref[...] = r.astype(out_dtype)

        part = lax.dot_general(a_ref[...].astype(BF16), b_ref[...].astype(BF16), dims, preferred_element_type=F32)
        if nk == 1:
            finish(part)
        else:
            k = pl.program_id(2)

            @pl.when(k == 0)
            def _():
                acc[...] = part

            @pl.when(k > 0)
            def _():
                acc[...] += part

            @pl.when(k == nk - 1)
            def _():
                finish(acc[...])

    a_spec = pl.BlockSpec((tk, tm), lambda i, j, k: (k, i)) if ta else pl.BlockSpec((tm, tk), lambda i, j, k: (i, k))
    b_spec = pl.BlockSpec((tn, tk), lambda i, j, k: (j, k)) if tb else pl.BlockSpec((tk, tn), lambda i, j, k: (k, j))
    o_spec = pl.BlockSpec((tm, tn), lambda i, j, k: (i, j))
    in_specs = [a_spec, b_spec] + ([o_spec] if add is not None else [])
    args = (a, b) + ((add,) if add is not None else ())
    return pl.pallas_call(
        body,
        name=name,
        grid=(M // tm, Nn // tn, nk),
        in_specs=in_specs,
        out_specs=o_spec,
        out_shape=jax.ShapeDtypeStruct((M, Nn), out_dtype),
        scratch_shapes=[pltpu.VMEM((tm, tn), F32)] if nk > 1 else [],
        compiler_params=_cparams(("parallel", "parallel", "arbitrary")),
    )(*args)


def _rms_fwd(h, gains, name):
    tr = _pick(LP, (768, 256))
    ng = len(gains)

    def body(*refs):
        h_ref = refs[0]
        g_refs = refs[1 : 1 + ng]
        o_refs = refs[1 + ng :]
        x = h_ref[...]
        xh = x * lax.rsqrt(jnp.mean(x * x, axis=-1, keepdims=True) + EPS)
        for g_ref, o_ref in zip(g_refs, o_refs):
            o_ref[...] = (xh * g_ref[...]).astype(BF16)

    row = pl.BlockSpec((tr, D), lambda i: (i, 0))
    vec = pl.BlockSpec((1, D), lambda i: (0, 0))
    outs = pl.pallas_call(
        body,
        name=name,
        grid=(LP // tr,),
        in_specs=[row] + [vec] * ng,
        out_specs=[row] * ng,
        out_shape=[jax.ShapeDtypeStruct((LP, D), BF16)] * ng,
        compiler_params=_cparams(("parallel",)),
    )(h, *gains)
    return outs


def _rms_bwd(dh_in, h, dus, gains, name):
    tr = _pick(LP, (256,))
    ng = len(gains)

    def body(*refs):
        dh_ref, h_ref = refs[0], refs[1]
        du_refs = refs[2 : 2 + ng]
        g_refs = refs[2 + ng : 2 + 2 * ng]
        o_ref = refs[2 + 2 * ng]
        dg_refs = refs[3 + 2 * ng :]
        i = pl.program_id(0)
        x = h_ref[...]
        r = lax.rsqrt(jnp.mean(x * x, axis=-1, keepdims=True) + EPS)
        xh = x * r
        tot = dh_ref[...]
        for du_ref, g_ref, dg_ref in zip(du_refs, g_refs, dg_refs):
            du = du_ref[...]
            dxh = du * g_ref[...]
            tot = tot + r * (dxh - xh * jnp.mean(dxh * xh, axis=-1, keepdims=True))

            @pl.when(i == 0)
            def _():
                dg_ref[...] = jnp.zeros_like(dg_ref)

            dg_ref[...] += jnp.sum(du * xh, axis=0, keepdims=True)
        o_ref[...] = jnp.where(_valid_rows(i, tr), tot, 0.0)

    row = pl.BlockSpec((tr, D), lambda i: (i, 0))
    vec = pl.BlockSpec((1, D), lambda i: (0, 0))
    outs = pl.pallas_call(
        body,
        name=name,
        grid=(LP // tr,),
        in_specs=[row, row] + [row] * ng + [vec] * ng,
        out_specs=[row] + [vec] * ng,
        out_shape=[jax.ShapeDtypeStruct((LP, D), F32)] + [jax.ShapeDtypeStruct((1, D), F32)] * ng,
        compiler_params=_cparams(("arbitrary",)),
    )(dh_in, h, *dus, *gains)
    return outs[0], outs[1:]


def _loss_head(h, gain, target, name):
    tr = Q

    def body(h_ref, g_ref, t_ref, dh_ref, dg_ref, ls_ref):
        i = pl.program_id(0)

        @pl.when(i == 0)
        def _():
            dg_ref[...] = jnp.zeros_like(dg_ref)
            ls_ref[...] = jnp.zeros_like(ls_ref)
            dh_ref[...] = jnp.zeros_like(dh_ref)

        @pl.when(i > 0)
        def _():
            x = h_ref[...]
            g = g_ref[...]
            r = lax.rsqrt(jnp.mean(x * x, axis=-1, keepdims=True) + EPS)
            xh = x * r
            e = xh * g - t_ref[...]
            ls_ref[...] += jnp.sum(e * e, axis=0, keepdims=True)
            dy = e * (1.0 / D)
            dg_ref[...] += jnp.sum(dy * xh, axis=0, keepdims=True)
            dxh = dy * g
            dh_ref[...] = r * (dxh - xh * jnp.mean(dxh * xh, axis=-1, keepdims=True))

    row = pl.BlockSpec((tr, D), lambda i: (i, 0))
    vec = pl.BlockSpec((1, D), lambda i: (0, 0))
    return pl.pallas_call(
        body,
        name=name,
        grid=(LP // tr,),
        in_specs=[row, vec, pl.BlockSpec((tr, D), lambda i: (jnp.maximum(i - 1, 0), 0))],
        out_specs=[row, vec, vec],
        out_shape=[jax.ShapeDtypeStruct((LP, D), F32), jax.ShapeDtypeStruct((1, D), F32), jax.ShapeDtypeStruct((1, D), F32)],
        compiler_params=_cparams(("arbitrary",)),
    )(h, gain, target)


HALO = 8
CONV_COLS = (1536, 1408, 768, 512, 256)


def _conv_rows(ext, w, b, width):
    n = ext.shape[0]
    acc = b + w[width - 1 : width, :] * ext[HALO:]
    for k in range(width - 1):
        acc = acc + w[k : k + 1, :] * pltpu.roll(ext, width - 1 - k, 0)[HALO:]
    return acc


def _conv_specs(tr, tn, col):
    per = tr // HALO
    last = LP // HALO - 1
    prev = pl.BlockSpec((HALO, tn), lambda j, i: (jnp.maximum(i * per - 1, 0), col(j)))
    cur = pl.BlockSpec((tr, tn), lambda j, i: (i, col(j)))
    nxt = pl.BlockSpec((HALO, tn), lambda j, i: (jnp.minimum((i + 1) * per, last), col(j)))
    return prev, cur, nxt


def _conv_bwd_core(ext, dpre, w, width, i, tr):
    rows = i * tr + lax.broadcasted_iota(jnp.int32, (tr + HALO, 1), 0)
    dpre = jnp.where((rows >= PF) & (rows < LP), dpre, 0.0)
    n = tr + HALO
    dx = w[width - 1 : width, :] * dpre[:tr]
    for k in range(width - 1):
        sh = width - 1 - k
        dx = dx + w[k : k + 1, :] * pltpu.roll(dpre, n - sh, 0)[:tr]
    dcur = dpre[:tr]
    dws = []
    for k in range(width):
        sh = width - 1 - k
        xs = ext[HALO:] if sh == 0 else pltpu.roll(ext, sh, 0)[HALO:]
        dws.append(jnp.sum(xs * dcur, axis=0, keepdims=True))
    db = jnp.sum(dcur, axis=0, keepdims=True)
    dx = jnp.where(_valid_rows(i, tr), dx, 0.0)
    return dx, dws, db


def _ssd_conv_fwd(xr, cw, cb, name):
    tr, tn = Q, _pick(CD, CONV_COLS)

    def body(p_ref, c_ref, w_ref, b_ref, o_ref, pre_ref):
        i = pl.program_id(1)
        ext = jnp.concatenate([jnp.where(i > 0, p_ref[...], 0.0), c_ref[...]], axis=0)
        pre = _conv_rows(ext, w_ref[...], b_ref[...], CONVW)
        pre_ref[...] = pre
        o_ref[...] = jnp.where(_valid_rows(i, tr), pre * _sigmoid(pre), 0.0)

    prev, cur, _ = _conv_specs(tr, tn, lambda j: j)
    return pl.pallas_call(
        body,
        name=name,
        grid=(CD // tn, LP // tr),
        in_specs=[prev, cur, pl.BlockSpec((CONVW, tn), lambda j, i: (0, j)), pl.BlockSpec((1, tn), lambda j, i: (0, j))],
        out_specs=[cur, cur],
        out_shape=[jax.ShapeDtypeStruct((LP, CD), F32)] * 2,
        compiler_params=_cparams(("parallel", "arbitrary")),
    )(xr, xr, cw, cb)


def _ssd_conv_bwd(xr, pre, dxbc, cw, name):
    tr, tn = Q, _pick(CD, CONV_COLS)

    def body(p_ref, c_ref, prc_ref, prn_ref, dc_ref, dn_ref, w_ref, dx_ref, dw_ref, db_ref):
        i = pl.program_id(1)
        ext = jnp.concatenate([jnp.where(i > 0, p_ref[...], 0.0), c_ref[...]], axis=0)
        dout = jnp.concatenate([dc_ref[...], dn_ref[...]], axis=0)
        prev_ = jnp.concatenate([prc_ref[...], prn_ref[...]], axis=0)
        dx, dws, db = _conv_bwd_core(ext, dout * _dsilu(prev_), w_ref[...], CONVW, i, tr)
        dx_ref[...] = dx

        @pl.when(i == 0)
        def _():
            dw_ref[...] = jnp.zeros_like(dw_ref)
            db_ref[...] = jnp.zeros_like(db_ref)

        for k in range(CONVW):
            dw_ref[k : k + 1, :] += dws[k]
        db_ref[...] += db

    prev, cur, nxt = _conv_specs(tr, tn, lambda j: j)
    wspec = pl.BlockSpec((CONVW, tn), lambda j, i: (0, j))
    bspec = pl.BlockSpec((1, tn), lambda j, i: (0, j))
    return pl.pallas_call(
        body,
        name=name,
        grid=(CD // tn, LP // tr),
        in_specs=[prev, cur, cur, nxt, cur, nxt, wspec],
        out_specs=[cur, wspec, bspec],
        out_shape=[jax.ShapeDtypeStruct((LP, CD), F32), jax.ShapeDtypeStruct((CONVW, CD), F32), jax.ShapeDtypeStruct((1, CD), F32)],
        compiler_params=_cparams(("parallel", "arbitrary")),
    )(xr, xr, pre, pre, dxbc, dxbc, cw)


def _ffn_act_fwd(hg, hv, cwg, cwv, cbg, cbv, name):
    tr, tn = Q, _pick(DFF, CONV_COLS)

    def body(pg, cg, pv, cv, wg, wv, bg, bv, o_ref, gate_ref, val_ref):
        i = pl.program_id(1)
        eg = jnp.concatenate([jnp.where(i > 0, pg[...], 0.0), cg[...]], axis=0)
        ev = jnp.concatenate([jnp.where(i > 0, pv[...], 0.0), cv[...]], axis=0)
        gate = _conv_rows(eg, wg[...], bg[...], FC)
        val = _conv_rows(ev, wv[...], bv[...], FC)
        gate_ref[...] = gate
        val_ref[...] = val
        o_ref[...] = (gate * _sigmoid(gate) * val).astype(BF16)

    prev, cur, _ = _conv_specs(tr, tn, lambda j: j)
    wspec = pl.BlockSpec((FC, tn), lambda j, i: (0, j))
    bspec = pl.BlockSpec((1, tn), lambda j, i: (0, j))
    return pl.pallas_call(
        body,
        name=name,
        grid=(DFF // tn, LP // tr),
        in_specs=[prev, cur, prev, cur, wspec, wspec, bspec, bspec],
        out_specs=[cur, cur, cur],
        out_shape=[jax.ShapeDtypeStruct((LP, DFF), BF16), jax.ShapeDtypeStruct((LP, DFF), F32), jax.ShapeDtypeStruct((LP, DFF), F32)],
        compiler_params=_cparams(("parallel", "arbitrary")),
    )(hg, hg, hv, hv, cwg, cwv, cbg, cbv)


def _ffn_act_bwd(hg, hv, gate_pre, val_pre, da, cwg, cwv, name):
    tr, tn = Q, _pick(DFF, CONV_COLS)

    def body(pg, cg, pv, cv, gc, gn, vc, vn, dc, dn, wg, wv, dg_ref, dv_ref, dwg, dwv, dbg, dbv):
        i = pl.program_id(1)
        eg = jnp.concatenate([jnp.where(i > 0, pg[...], 0.0), cg[...]], axis=0)
        ev = jnp.concatenate([jnp.where(i > 0, pv[...], 0.0), cv[...]], axis=0)
        dout = jnp.concatenate([dc[...], dn[...]], axis=0)
        gate = jnp.concatenate([gc[...], gn[...]], axis=0)
        val = jnp.concatenate([vc[...], vn[...]], axis=0)
        s = _sigmoid(gate)
        dxg, dwsg, dbgv = _conv_bwd_core(eg, dout * val * (s * (1.0 + gate * (1.0 - s))), wg[...], FC, i, tr)
        dxv, dwsv, dbvv = _conv_bwd_core(ev, dout * (gate * s), wv[...], FC, i, tr)
        dg_ref[...] = dxg
        dv_ref[...] = dxv

        @pl.when(i == 0)
        def _():
            dwg[...] = jnp.zeros_like(dwg)
            dwv[...] = jnp.zeros_like(dwv)
            dbg[...] = jnp.zeros_like(dbg)
            dbv[...] = jnp.zeros_like(dbv)

        for k in range(FC):
            dwg[k : k + 1, :] += dwsg[k]
            dwv[k : k + 1, :] += dwsv[k]
        dbg[...] += dbgv
        dbv[...] += dbvv

    prev, cur, nxt = _conv_specs(tr, tn, lambda j: j)
    wspec = pl.BlockSpec((FC, tn), lambda j, i: (0, j))
    bspec = pl.BlockSpec((1, tn), lambda j, i: (0, j))
    big = jax.ShapeDtypeStruct((LP, DFF), F32)
    wsh = jax.ShapeDtypeStruct((FC, DFF), F32)
    bsh = jax.ShapeDtypeStruct((1, DFF), F32)
    return pl.pallas_call(
        body,
        name=name,
        grid=(DFF // tn, LP // tr),
        in_specs=[prev, cur, prev, cur, cur, nxt, cur, nxt, cur, nxt, wspec, wspec],
        out_specs=[cur, cur, wspec, wspec, bspec, bspec],
        out_shape=[big, big, wsh, wsh, bsh, bsh],
        compiler_params=_cparams(("parallel", "arbitrary")),
    )(hg, hg, hv, hv, gate_pre, gate_pre, val_pre, val_pre, da, da, cwg, cwv)


def _gate_fwd(y, z, gg, name):
    tr = _pick(LP, (768, 256))

    def body(y_ref, z_ref, g_ref, o_ref):
        zv = z_ref[...]
        hg = y_ref[...] * zv * _sigmoid(zv)
        r = lax.rsqrt(jnp.mean(hg * hg, axis=-1, keepdims=True) + EPS)
        o_ref[...] = (hg * r * g_ref[...]).astype(BF16)

    blk = pl.BlockSpec((tr, GW), lambda i, g: (i, g))
    return pl.pallas_call(
        body,
        name=name,
        grid=(LP // tr, G),
        in_specs=[blk, blk, pl.BlockSpec((1, GW), lambda i, g: (0, g))],
        out_specs=blk,
        out_shape=jax.ShapeDtypeStruct((LP, DI), BF16),
        compiler_params=_cparams(("parallel", "parallel")),
    )(y, z, gg)


def _gate_bwd(dout, y, z, gg, name):
    tr = _pick(LP, (768, 256))

    def body(do_ref, y_ref, z_ref, g_ref, dy_ref, dz_ref, dg_ref):
        i = pl.program_id(1)
        zv = z_ref[...]
        yv = y_ref[...]
        sz = zv * _sigmoid(zv)
        hg = yv * sz
        r = lax.rsqrt(jnp.mean(hg * hg, axis=-1, keepdims=True) + EPS)
        hh = hg * r
        do = do_ref[...]
        dhh = do * g_ref[...]
        dhg = r * (dhh - hh * jnp.mean(dhh * hh, axis=-1, keepdims=True))
        dy_ref[...] = dhg * sz
        dz_ref[...] = dhg * yv * _dsilu(zv)

        @pl.when(i == 0)
        def _():
            dg_ref[...] = jnp.zeros_like(dg_ref)

        dg_ref[...] += jnp.sum(do * hh, axis=0, keepdims=True)

    blk = pl.BlockSpec((tr, GW), lambda g, i: (i, g))
    vec = pl.BlockSpec((1, GW), lambda g, i: (0, g))
    big = jax.ShapeDtypeStruct((LP, DI), F32)
    return pl.pallas_call(
        body,
        name=name,
        grid=(G, LP // tr),
        in_specs=[blk, blk, blk, vec],
        out_specs=[blk, blk, vec],
        out_shape=[big, big, jax.ShapeDtypeStruct((1, DI), F32)],
        compiler_params=_cparams(("parallel", "arbitrary")),
    )(dout, y, z, gg)


def _ssd_common(dtc_ref, dtr_ref, bc_ref, br_ref, ac_ref, ar_ref, c):
    rows = c * Q + lax.broadcasted_iota(jnp.int32, (Q, 1), 0)
    cols = c * Q + lax.broadcasted_iota(jnp.int32, (1, Q), 1)
    prec = dtc_ref[...] + bc_ref[...]
    prer = dtr_ref[...] + br_ref[...]
    dtc = jnp.where(rows >= PF, _softplus(prec), 0.0)
    dtr = jnp.where(cols >= PF, _softplus(prer), 0.0)
    a_c = -jnp.exp(ac_ref[...])
    a_r = -jnp.exp(ar_ref[...])
    li = lax.broadcasted_iota(jnp.int32, (Q, Q), 0)
    si = lax.broadcasted_iota(jnp.int32, (Q, Q), 1)
    tril = si <= li
    trif = tril.astype(F32)
    csc = jnp.dot(trif, dtc * a_c, precision=HI, preferred_element_type=F32)
    csr = lax.dot_general(dtr * a_r, trif, (((1,), (1,)), ((), ())), precision=HI, preferred_element_type=F32)
    return dict(rows=rows, prec=prec, dtc=dtc, a_c=a_c, tril=tril, trif=trif, csc=csc, csr=csr, li=li, si=si)


def _pair_expand(arr, h0, lane_lo):
    return jnp.where(lane_lo, arr[:, h0 : h0 + 1], arr[:, h0 + 1 : h0 + 2])


def _ssd_specs():
    nb = DI // N
    xs = pl.BlockSpec((Q, GW), lambda g, c: (c, g))
    bb = pl.BlockSpec((Q, N), lambda g, c: (c, nb + g))
    cc = pl.BlockSpec((Q, N), lambda g, c: (c, nb + G + g))
    dtc = pl.BlockSpec((None, Q, LANES), lambda g, c: (g, c, 0))
    dtr = pl.BlockSpec((None, 8, Q), lambda g, c: (g, 0, c))
    pc = pl.BlockSpec((None, 1, LANES), lambda g, c: (g, 0, 0))
    pr = pl.BlockSpec((None, 8, 1), lambda g, c: (g, 0, 0))
    return xs, bb, cc, dtc, dtr, pc, pr


def _ssd_fwd(xbc, dtc, dtr, bias_c, bias_r, alog_c, alog_r, dskip_c, gather, name):
    nb = len(gather)

    def body(*refs):
        xs_ref, b_ref, c_ref, dtc_ref, dtr_ref, bc_ref, br_ref, ac_ref, ar_ref, dk_ref = refs[:10]
        gin = refs[10 : 10 + nb]
        y_ref, st_ref = refs[10 + nb : 12 + nb]
        gout = refs[12 + nb : 12 + 2 * nb]
        state, send_sems, recv_sems = refs[12 + 2 * nb :]
        c = pl.program_id(1)
        first_step = (pl.program_id(0) == 0) & (c == 0)
        last_step = (pl.program_id(0) == G - 1) & (c == NC - 1)

        @pl.when(first_step)
        def _():
            _gather_start(gin, gout, send_sems, recv_sems)

        @pl.when(c == 0)
        def _():
            state[...] = jnp.zeros_like(state)

        st_ref[...] = state[...]
        cm = _ssd_common(dtc_ref, dtr_ref, bc_ref, br_ref, ac_ref, ar_ref, c)
        Bm = b_ref[...]
        Cm = c_ref[...]
        cb = lax.dot_general(Cm.astype(BF16), Bm.astype(BF16), (((1,), (1,)), ((), ())), preferred_element_type=F32)
        bt = Bm.T.astype(BF16)
        lane_lo = lax.broadcasted_iota(jnp.int32, (1, HP), 1) < P
        csc, csr, dtc_v = cm["csc"], cm["csr"], cm["dtc"]
        ecs = jnp.exp(csc)
        cs_end = csc[Q - 1 : Q, :]
        wdec = jnp.exp(cs_end - csc)
        eend = jnp.exp(cs_end)
        for pp in range(E // 2):
            h0 = 2 * pp
            sl = slice(pp * HP, (pp + 1) * HP)
            xp = xs_ref[:, sl]
            xdt = xp * _pair_expand(dtc_v, h0, lane_lo)
            yacc = xp * _pair_expand(dk_ref[...], h0, lane_lo)
            for e in range(2):
                h = h0 + e
                lm = jnp.where(cm["tril"], jnp.exp(jnp.minimum(csc[:, h : h + 1] - csr[h : h + 1, :], 0.0)), 0.0)
                m = (cb * lm).astype(BF16)
                xm = jnp.where(lane_lo if e == 0 else jnp.logical_not(lane_lo), xdt, 0.0).astype(BF16)
                yacc = yacc + jnp.dot(m, xm, preferred_element_type=F32)
            stp = state[:, sl]
            yoff = jnp.dot(Cm.astype(BF16), stp.astype(BF16), preferred_element_type=F32)
            y_ref[:, sl] = yacc + yoff * _pair_expand(ecs, h0, lane_lo)
            xw = (xdt * _pair_expand(wdec, h0, lane_lo)).astype(BF16)
            state[:, sl] = stp * _pair_expand(eend, h0, lane_lo) + jnp.dot(bt, xw, preferred_element_type=F32)

        @pl.when(last_step)
        def _():
            _gather_finish(gin, gout, send_sems, recv_sems)

    xs, bb, cc, dtcs, dtrs, pc, pr = _ssd_specs()
    outs = pl.pallas_call(
        body,
        name=name,
        grid=(G, NC),
        in_specs=[xs, bb, cc, dtcs, dtrs, pc, pr, pc, pr, pc] + [ANY] * nb,
        out_specs=[xs, pl.BlockSpec((None, None, N, GW), lambda g, c: (c, g, 0, 0))] + [ANY] * nb,
        out_shape=[jax.ShapeDtypeStruct((LP, DI), F32), jax.ShapeDtypeStruct((NC, G, N, GW), F32)] + _gather_shapes(gather),
        scratch_shapes=[pltpu.VMEM((N, GW), F32)] + _gather_sems(nb),
        compiler_params=_cparams(("arbitrary", "arbitrary")),
    )(xbc, xbc, xbc, dtc, dtr, bias_c, bias_r, alog_c, alog_r, dskip_c, *gather)
    return outs[0], outs[1], _gather_own(outs[2:], gather)


def _ssd_bwd(xbc, dy, states, dtc, dtr, bias_c, bias_r, alog_c, alog_r, dskip_c, scatter, name):
    nb = len(scatter)

    def body(*refs):
        xs_ref, b_ref, c_ref, dy_ref, st_ref, dtc_ref, dtr_ref, bc_ref, br_ref, ac_ref, ar_ref, dk_ref = refs[:12]
        sin = refs[12 : 12 + nb]
        dx_ref, db_ref, dc_ref, ddt_ref, dbias_ref, dalog_ref, ddk_ref = refs[12 + nb : 19 + nb]
        sout = refs[19 + nb : 19 + 2 * nb]
        dstate, send_sems, recv_sems = refs[19 + 2 * nb :]
        ci = pl.program_id(1)
        c = NC - 1 - ci

        @pl.when((pl.program_id(0) == 0) & (ci == 0))
        def _():
            _scatter_start(sin, sout, send_sems, recv_sems)

        @pl.when(ci == 0)
        def _():
            dstate[...] = jnp.zeros_like(dstate)
            dbias_ref[...] = jnp.zeros_like(dbias_ref)
            dalog_ref[...] = jnp.zeros_like(dalog_ref)
            ddk_ref[...] = jnp.zeros_like(ddk_ref)

        cm = _ssd_common(dtc_ref, dtr_ref, bc_ref, br_ref, ac_ref, ar_ref, c)
        Bm = b_ref[...]
        Cm = c_ref[...]
        Bb = Bm.astype(BF16)
        Cb = Cm.astype(BF16)
        nt = (((1,), (1,)), ((), ()))
        cb = lax.dot_general(Cb, Bb, nt, preferred_element_type=F32)
        ct = Cm.T.astype(BF16)
        lane_lo = lax.broadcasted_iota(jnp.int32, (1, HP), 1) < P
        lane_id = lax.broadcasted_iota(jnp.int32, (1, LANES), 1)
        csc, csr, dtc_v, a_c = cm["csc"], cm["csr"], cm["dtc"], cm["a_c"]
        triu = cm["si"] >= cm["li"]
        ecs = jnp.exp(csc)
        cs_end = csc[Q - 1 : Q, :]
        wdec = jnp.exp(cs_end - csc)
        eend = jnp.exp(cs_end)
        dcb = jnp.zeros((Q, Q), F32)
        dcs = jnp.zeros((Q, LANES), F32)
        dcs_end = jnp.zeros((1, LANES), F32)
        ddt = jnp.zeros((Q, LANES), F32)
        ddk = jnp.zeros((1, LANES), F32)
        dB = jnp.zeros((Q, N), F32)
        dC = jnp.zeros((Q, N), F32)
        for pp in range(E // 2):
            h0 = 2 * pp
            sl = slice(pp * HP, (pp + 1) * HP)
            xp = xs_ref[:, sl]
            dyp = dy_ref[:, sl]
            dtx = _pair_expand(dtc_v, h0, lane_lo)
            xdt = xp * dtx
            dxdt = jnp.zeros((Q, HP), F32)
            stp = st_ref[:, sl]
            stb = stp.astype(BF16)
            dsn = dstate[:, sl]
            dsnb = dsn.astype(BF16)
            ecsx = _pair_expand(ecs, h0, lane_lo)
            wdx = _pair_expand(wdec, h0, lane_lo)
            cs_ = jnp.dot(Cb, stb, preferred_element_type=F32)
            yo = cs_ * ecsx
            dyo = dyp * ecsx
            dyob = dyo.astype(BF16)
            dC = dC + lax.dot_general(dyob, stb, nt, preferred_element_type=F32)
            ds_from_y = jnp.dot(ct, dyob, preferred_element_type=F32)
            xw = xdt * wdx
            dB = dB + lax.dot_general(xw.astype(BF16), dsnb, nt, preferred_element_type=F32)
            dxw = jnp.dot(Bb, dsnb, preferred_element_type=F32)
            dxdt = dxdt + dxw * wdx
            w2 = dxw * xw
            rs = jnp.sum(dsn * stp, axis=0, keepdims=True) * _pair_expand(eend, h0, lane_lo)
            dstate[:, sl] = dsn * _pair_expand(eend, h0, lane_lo) + ds_from_y
            dyx = dyp * xp
            yd = jnp.zeros((Q, HP), F32)
            dxd = jnp.zeros((Q, HP), F32)
            for e in range(2):
                h = h0 + e
                msk = lane_lo if e == 0 else jnp.logical_not(lane_lo)
                col = csc[:, h : h + 1]
                row = csr[h : h + 1, :]
                lm = jnp.where(cm["tril"], jnp.exp(jnp.minimum(col - row, 0.0)), 0.0)
                dye = jnp.where(msk, dyp, 0.0).astype(BF16)
                xde = jnp.where(msk, xdt, 0.0).astype(BF16)
                gm = lax.dot_general(dye, xde, nt, preferred_element_type=F32)
                dcb = dcb + gm * lm
                mb = (cb * lm).astype(BF16)
                yd = yd + jnp.dot(mb, xde, preferred_element_type=F32)
                dxd = dxd + lax.dot_general(mb, dye, (((0,), (0,)), ((), ())), preferred_element_type=F32)
            dxdt = dxdt + dxd
            tt = dyp * yo - w2 + dyp.astype(BF16).astype(F32) * yd - xdt.astype(BF16).astype(F32) * dxd
            dx_ref[:, sl] = dxdt * dtx + dyp * _pair_expand(dk_ref[...], h0, lane_lo)
            dxx = dxdt * xp
            for e in range(2):
                h = h0 + e
                msk = lane_lo if e == 0 else jnp.logical_not(lane_lo)
                oh = (lane_id == h).astype(F32)
                dcs = dcs + jnp.sum(jnp.where(msk, tt, 0.0), axis=1, keepdims=True) * oh
                dcs_end = dcs_end + (_sum_all(jnp.where(msk, w2, 0.0)) + _sum_all(jnp.where(msk, rs, 0.0))) * oh
                ddk = ddk + _sum_all(jnp.where(msk, dyx, 0.0)) * oh
                ddt = ddt + jnp.sum(jnp.where(msk, dxx, 0.0), axis=1, keepdims=True) * oh
        dC = dC + jnp.dot(dcb.astype(BF16), Bb, preferred_element_type=F32)
        dB = dB + lax.dot_general(dcb.astype(BF16), Cb, (((0,), (0,)), ((), ())), preferred_element_type=F32)
        db_ref[...] = dB
        dc_ref[...] = dC
        last = (lax.broadcasted_iota(jnp.int32, (Q, 1), 0) == Q - 1).astype(F32)
        dcs = dcs + last * dcs_end
        dda = jnp.dot(triu.astype(F32), dcs, precision=HI, preferred_element_type=F32)
        ddt = ddt + dda * a_c
        da = jnp.sum(dda * dtc_v, axis=0, keepdims=True)
        draw = jnp.where(cm["rows"] >= PF, ddt * _sigmoid(cm["prec"]), 0.0)
        ddt_ref[...] = draw
        dbias_ref[...] += jnp.sum(draw, axis=0, keepdims=True)
        dalog_ref[...] += da * a_c
        ddk_ref[...] += ddk

        @pl.when((pl.program_id(0) == G - 1) & (ci == NC - 1))
        def _():
            _scatter_finish(sin, sout, send_sems, recv_sems)

    xs, bb, cc, dtcs, dtrs, pc, pr = _ssd_specs()

    def rev(spec_fn):
        return lambda g, ci: spec_fn(g, NC - 1 - ci)

    def rspec(spec):
        return pl.BlockSpec(spec.block_shape, rev(spec.index_map))

    xs_r, bb_r, cc_r, dtc_r, dtr_r = rspec(xs), rspec(bb), rspec(cc), rspec(dtcs), rspec(dtrs)
    st_r = pl.BlockSpec((None, None, N, GW), lambda g, ci: (NC - 1 - ci, g, 0, 0))
    gn = pl.BlockSpec((Q, N), lambda g, ci: (NC - 1 - ci, g))
    outs = pl.pallas_call(
        body,
        name=name,
        grid=(G, NC),
        in_specs=[xs_r, bb_r, cc_r, xs_r, st_r, dtc_r, dtr_r, pc, pr, pc, pr, pc] + [ANY] * nb,
        out_specs=[xs_r, gn, gn, dtc_r, pc, pc, pc] + [ANY] * nb,
        out_shape=[
            jax.ShapeDtypeStruct((LP, DI), F32),
            jax.ShapeDtypeStruct((LP, G * N), F32),
            jax.ShapeDtypeStruct((LP, G * N), F32),
            jax.ShapeDtypeStruct((G, LP, LANES), F32),
            jax.ShapeDtypeStruct((G, 1, LANES), F32),
            jax.ShapeDtypeStruct((G, 1, LANES), F32),
            jax.ShapeDtypeStruct((G, 1, LANES), F32),
        ]
        + [jax.ShapeDtypeStruct(b.shape, b.dtype) for b in scatter],
        scratch_shapes=[pltpu.VMEM((N, GW), F32)] + _scatter_sems(nb),
        compiler_params=_cparams(("arbitrary", "arbitrary")),
    )(xbc, xbc, xbc, dy, states, dtc, dtr, bias_c, bias_r, alog_c, alog_r, dskip_c, *scatter)
    return tuple(outs[:7]) + (_scatter_own(outs[7:], scatter),)


def _split_dot(x, u):
    hi = x.astype(BF16)
    lo = (x - hi.astype(F32)).astype(BF16)
    return jnp.dot(hi, u, preferred_element_type=F32) + jnp.dot(lo, u, preferred_element_type=F32)


def _suffix_sums(x, u, split=True):
    right = x[:, TQ:]
    dot = _split_dot if split else (lambda a, b: jnp.dot(a.astype(BF16), b, preferred_element_type=F32))
    return jnp.concatenate([dot(x[:, :TQ], u) + jnp.sum(right, axis=1, keepdims=True), dot(right, u)], axis=1)


def _sb_block(qe, kblk, vis, a_run, u_gt):
    l = lax.dot_general(qe, kblk, (((1,), (1,)), ((), ())), preferred_element_type=F32)
    lk = jnp.minimum(-l, 0.0) - jnp.log(1.0 + jnp.exp(-jnp.abs(l)))
    lbeta = l + lk
    if vis is not None:
        lk = jnp.where(vis, lk, 0.0)
    logw = lbeta + _suffix_sums(lk, u_gt, split=False) + a_run
    return lbeta, lk, logw


def _window(i, n):
    start = TQ * (i - 2 * n - 1)
    return pl.multiple_of(jnp.maximum(start, 0), TQ), start + TK


def _window_mask(i, n, t_idx):
    off, hi = _window(i, n)
    s_idx = off + lax.broadcasted_iota(jnp.int32, (1, TK), 1)
    return (s_idx < t_idx) & (s_idx >= PF) & (s_idx < hi)


def _descend(i, block, carry):
    def pack(n, c):
        return (n, jnp.max(jnp.maximum(c[0], c[1]))) + tuple(c)

    last = i // 2
    st = pack(jnp.int32(1), block(jnp.int32(0), carry, True))
    st = lax.while_loop(lambda st: (st[0] < last) & (st[1] > -T_SKIP), lambda st: pack(st[0] + 1, block(st[0], st[2:], False)), st)
    st = lax.while_loop(lambda st: (st[0] == last) & (st[1] > -T_SKIP), lambda st: pack(st[0] + 1, block(last, st[2:], True)), st)
    return st[2:]


def _attn_fwd(q, k, v, name):
    nq = LP // TQ

    def body(q_ref, k_ref, v_ref, o_ref):
        i = pl.program_id(1)
        qv = q_ref[...]
        lane_lo = lax.broadcasted_iota(jnp.int32, (1, HP), 1) < 64
        t_idx = i * TQ + lax.broadcasted_iota(jnp.int32, (TQ, 1), 0)
        ji = lax.broadcasted_iota(jnp.int32, (TQ, TQ), 0)
        si = lax.broadcasted_iota(jnp.int32, (TQ, TQ), 1)
        u_gt = (ji > si).astype(BF16)
        qs = [jnp.where(lane_lo, qv, jnp.zeros_like(qv)), jnp.where(lane_lo, jnp.zeros_like(qv), qv)]

        def block(n, carry, masked):
            a0, a1, acc = carry
            off, _ = _window(i, n)
            kblk = k_ref[pl.ds(off, TK), :]
            vblk = v_ref[pl.ds(off, TK), :]
            vis = _window_mask(i, n, t_idx) if masked else None
            new_a = []
            for e, a_run in enumerate((a0, a1)):
                _, lk, logw = _sb_block(qs[e], kblk, vis, a_run, u_gt)
                w = jnp.exp(logw)
                if masked:
                    w = jnp.where(vis, w, 0.0)
                msk = lane_lo if e == 0 else jnp.logical_not(lane_lo)
                acc = acc + jnp.dot(w.astype(BF16), jnp.where(msk, vblk, jnp.zeros_like(vblk)), preferred_element_type=F32)
                new_a.append(a_run + jnp.sum(lk, axis=1, keepdims=True))
            return new_a[0], new_a[1], acc

        z1 = jnp.zeros((TQ, 1), F32)
        _, _, acc = _descend(i, block, (z1, z1, jnp.zeros((TQ, HP), F32)))
        o_ref[...] = acc

    return pl.pallas_call(
        body,
        name=name,
        grid=(D // HP, nq),
        in_specs=[
            pl.BlockSpec((TQ, HP), lambda j, i: (i, j)),
            pl.BlockSpec((LP, HP), lambda j, i: (0, j)),
            pl.BlockSpec((LP, HP), lambda j, i: (0, j)),
        ],
        out_specs=pl.BlockSpec((TQ, HP), lambda j, i: (i, j)),
        out_shape=jax.ShapeDtypeStruct((LP, D), F32),
        compiler_params=_cparams(("parallel", "arbitrary")),
    )(q, k, v)


def _attn_bwd(q, k, v, o, do, name):
    nq = LP // TQ

    def body(q_ref, k_ref, v_ref, o_ref, do_ref, dq_ref, dk_ref, dv_ref):
        i = pl.program_id(1)

        @pl.when(i == 0)
        def _():
            dk_ref[...] = jnp.zeros_like(dk_ref)
            dv_ref[...] = jnp.zeros_like(dv_ref)

        qv = q_ref[...]
        dov = do_ref[...]
        lane_lo = lax.broadcasted_iota(jnp.int32, (1, HP), 1) < 64
        t_idx = i * TQ + lax.broadcasted_iota(jnp.int32, (TQ, 1), 0)
        ji = lax.broadcasted_iota(jnp.int32, (TQ, TQ), 0)
        si = lax.broadcasted_iota(jnp.int32, (TQ, TQ), 1)
        u_gt = (ji > si).astype(BF16)
        u_ge = (ji >= si).astype(BF16)
        msks = [lane_lo, jnp.logical_not(lane_lo)]
        qs = [jnp.where(m, qv, jnp.zeros_like(qv)) for m in msks]
        dob = [jnp.where(m, dov, 0.0).astype(BF16) for m in msks]
        ov = o_ref[...]
        deltas = [jnp.sum(d.astype(F32) * ov, axis=1, keepdims=True) for d in dob]
        nt = (((1,), (1,)), ((), ()))
        tn = (((0,), (0,)), ((), ()))

        def block(n, carry, masked):
            a0, a1, p0, p1, dq = carry
            off, _ = _window(i, n)
            kblk = k_ref[pl.ds(off, TK), :]
            vblk = v_ref[pl.ds(off, TK), :]
            vis = _window_mask(i, n, t_idx) if masked else None
            new_a, new_p = [], []
            dk_acc = jnp.zeros((TK, HP), F32)
            dv_acc = jnp.zeros((TK, HP), F32)
            for e, (a_run, p_run) in enumerate(((a0, p0), (a1, p1))):
                lbeta, lk, logw = _sb_block(qs[e], kblk, vis, a_run, u_gt)
                sig = jnp.exp(lbeta)
                w = jnp.exp(logw)
                if masked:
                    w = jnp.where(vis, w, 0.0)
                wb = w.astype(BF16)
                dw = lax.dot_general(dob[e], vblk, nt, preferred_element_type=F32)
                pm = wb.astype(F32) * dw
                cum_p = deltas[e] - (_suffix_sums(pm, u_ge) + p_run)
                dl = pm - (pm + cum_p) * sig
                if masked:
                    dl = jnp.where(vis, dl, 0.0)
                dl = dl.astype(BF16)
                km = jnp.where(msks[e], kblk, jnp.zeros_like(kblk))
                dq = dq + jnp.dot(dl, km, preferred_element_type=F32)
                dk_acc = dk_acc + lax.dot_general(dl, qs[e], tn, preferred_element_type=F32)
                dv_acc = dv_acc + lax.dot_general(wb, dob[e], tn, preferred_element_type=F32)
                new_a.append(a_run + jnp.sum(lk, axis=1, keepdims=True))
                new_p.append(p_run + jnp.sum(pm, axis=1, keepdims=True))
            dk_ref[pl.ds(off, TK), :] += dk_acc
            dv_ref[pl.ds(off, TK), :] += dv_acc
            return new_a[0], new_a[1], new_p[0], new_p[1], dq

        z1 = jnp.zeros((TQ, 1), F32)
        carry = _descend(i, block, (z1, z1, z1, z1, jnp.zeros((TQ, HP), F32)))
        dq_ref[...] = carry[4]

    blk = pl.BlockSpec((TQ, HP), lambda j, i: (i, j))
    full = pl.BlockSpec((LP, HP), lambda j, i: (0, j))
    big = jax.ShapeDtypeStruct((LP, D), F32)
    return pl.pallas_call(
        body,
        name=name,
        grid=(D // HP, nq),
        in_specs=[blk, full, full, blk, blk],
        out_specs=[blk, full, full],
        out_shape=[big, big, big],
        compiler_params=_cparams(("parallel", "arbitrary")),
    )(q, k, v, o, do)


ROW_TILES = (2048, 1024, 512, 256, 128, 64, 32, 16, 8)


def _row_tile(rows, cols, budget):
    if rows % 8:
        return rows
    return _pick(rows, tuple(t for t in ROW_TILES if t * cols <= budget) or (8,))


def _adamw(w, g, m, v, name):
    R, C = w.shape
    tr = _row_tile(R, C, 128 * 1024)

    def body(w_ref, g_ref, m_ref, v_ref, g_out, d_out, m_out, v_out):
        g = g_ref[...]
        mn = ADAM_B1 * m_ref[...] + (1.0 - ADAM_B1) * g
        vn = ADAM_B2 * v_ref[...] + (1.0 - ADAM_B2) * (g * g)
        mh = mn / (1.0 - ADAM_B1**ADAM_STEP)
        vh = vn / (1.0 - ADAM_B2**ADAM_STEP)
        g_out[...] = g
        d_out[...] = -ADAM_LR * (mh / (jnp.sqrt(vh) + ADAM_EPS) + ADAM_WD * w_ref[...])
        m_out[...] = mn
        v_out[...] = vn

    blk = pl.BlockSpec((tr, C), lambda i: (i, 0))
    sh = jax.ShapeDtypeStruct((R, C), F32)
    return pl.pallas_call(
        body,
        name=name,
        grid=(R // tr,),
        in_specs=[blk] * 4,
        out_specs=[blk] * 4,
        out_shape=[sh] * 4,
        compiler_params=_cparams(("parallel",)),
    )(w, g, m, v)


def _sum4(buf, name):
    _, R, C = buf.shape
    tr = _row_tile(R, C, 128 * 1024)

    def body(b_ref, o_ref):
        acc = b_ref[0].astype(F32)
        for s in range(1, NCHIP):
            acc = acc + b_ref[s].astype(F32)
        o_ref[...] = acc

    return pl.pallas_call(
        body,
        name=name,
        grid=(R // tr,),
        in_specs=[pl.BlockSpec((NCHIP, tr, C), lambda i: (0, i, 0))],
        out_specs=pl.BlockSpec((tr, C), lambda i: (i, 0)),
        out_shape=jax.ShapeDtypeStruct((R, C), F32),
        compiler_params=_cparams(("parallel",)),
    )(buf)


def _add2(a, b, name):
    S, R, C = a.shape
    tr = _row_tile(R, C, 256 * 1024)

    def body(a_ref, b_ref, o_ref):
        o_ref[...] = (a_ref[...].astype(F32) + b_ref[...].astype(F32)).astype(o_ref.dtype)

    blk = pl.BlockSpec((None, tr, C), lambda s, i: (s, i, 0))
    return pl.pallas_call(
        body,
        name=name,
        grid=(S, R // tr),
        in_specs=[blk, blk],
        out_specs=blk,
        out_shape=jax.ShapeDtypeStruct(a.shape, a.dtype),
        compiler_params=_cparams(("parallel", "parallel")),
    )(a, b)


ANY = pl.BlockSpec(memory_space=pl.ANY)


def _mesh_place():
    x, y, c = lax.axis_index("x"), lax.axis_index("y"), lax.axis_index("c")
    return x, y, c, 2 * x + y, [(1 - x, y), (x, 1 - y), (1 - x, 1 - y)]


def _gather_chips(bufs, name):
    nb = len(bufs)

    def body(*refs):
        ins = refs[:nb]
        outs = refs[nb : 2 * nb]
        send_sems, recv_sems = refs[2 * nb :]
        _gather_start(ins, outs, send_sems, recv_sems)
        _gather_finish(ins, outs, send_sems, recv_sems)

    outs = pl.pallas_call(
        body,
        name=name,
        in_specs=[ANY] * nb,
        out_specs=[ANY] * nb,
        out_shape=_gather_shapes(bufs),
        scratch_shapes=_gather_sems(nb),
    )(*bufs)
    return _gather_own(outs, bufs)


def _gather_shapes(bufs):
    return [jax.ShapeDtypeStruct((NCHIP,) + tuple(b.shape), b.dtype) for b in bufs]


def _gather_sems(nb):
    return [pltpu.SemaphoreType.DMA((6 * nb,)), pltpu.SemaphoreType.DMA((6 * nb,))]


def _gather_own(outs, bufs):
    me = 2 * lax.axis_index("x") + lax.axis_index("y")
    return [lax.dynamic_update_slice(o, b[None], (me, 0, 0)) for o, b in zip(outs, bufs)]


def _gather_copy(outs, send_sems, recv_sems, k, b, src, slot, hc, to):
    nb = len(outs)
    hr = outs[b].shape[1] // 2
    return pltpu.make_async_remote_copy(
        src_ref=src, dst_ref=outs[b].at[slot, pl.ds(hc * hr, hr)], send_sem=send_sems.at[k * nb + b],
        recv_sem=recv_sems.at[k * nb + b], device_id=to, device_id_type=MESH)


def _gather_start(ins, outs, send_sems, recv_sems):
    x, y, c, me, peers = _mesh_place()
    for k, (px, py) in enumerate(peers):
        for b in range(len(ins)):
            hr = ins[b].shape[0] // 2
            _gather_copy(outs, send_sems, recv_sems, k, b, ins[b].at[pl.ds(c * hr, hr)], me, c, (px, py, c)).start()


def _gather_finish(ins, outs, send_sems, recv_sems):
    x, y, c, me, peers = _mesh_place()
    nb = len(ins)
    sent = []
    for k, (px, py) in enumerate(peers):
        for b in range(nb):
            hr = ins[b].shape[0] // 2
            slot = 2 * px + py
            landed = outs[b].at[slot, pl.ds(c * hr, hr)]
            _gather_copy(outs, send_sems, recv_sems, k, b, landed, slot, c, (px, py, c)).wait_recv()
            cp = _gather_copy(outs, send_sems, recv_sems, 3 + k, b, landed, slot, c, (x, y, 1 - c))
            cp.start()
            sent.append(cp)
            sent.append(_gather_copy(outs, send_sems, recv_sems, k, b, ins[b].at[pl.ds(c * hr, hr)], me, c, (px, py, c)))
    for k, (px, py) in enumerate(peers):
        for b in range(nb):
            hr = ins[b].shape[0] // 2
            slot = 2 * px + py
            theirs = outs[b].at[slot, pl.ds((1 - c) * hr, hr)]
            _gather_copy(outs, send_sems, recv_sems, 3 + k, b, theirs, slot, 1 - c, (x, y, 1 - c)).wait_recv()
    for cp in sent:
        cp.wait_send()


def _scatter_chips(bufs, name):
    nb = len(bufs)

    def body(*refs):
        ins = refs[:nb]
        outs = refs[nb : 2 * nb]
        send_sems, recv_sems = refs[2 * nb :]
        _scatter_start(ins, outs, send_sems, recv_sems)
        _scatter_finish(ins, outs, send_sems, recv_sems)

    outs = pl.pallas_call(
        body,
        name=name,
        in_specs=[ANY] * nb,
        out_specs=[ANY] * nb,
        out_shape=[jax.ShapeDtypeStruct(b.shape, b.dtype) for b in bufs],
        scratch_shapes=_scatter_sems(nb),
    )(*bufs)
    return _scatter_own(outs, bufs)


def _scatter_sems(nb):
    return [pltpu.SemaphoreType.DMA((3 * nb,)), pltpu.SemaphoreType.DMA((3 * nb,))]


def _scatter_own(outs, bufs):
    me = 2 * lax.axis_index("x") + lax.axis_index("y")
    return [lax.dynamic_update_slice(o, lax.dynamic_slice_in_dim(b, me, 1, axis=0), (me, 0, 0)) for o, b in zip(outs, bufs)]


def _scatter_copy(ins, outs, send_sems, recv_sems, k, b, slot_from, slot_to, to):
    nb = len(ins)
    return pltpu.make_async_remote_copy(
        src_ref=ins[b].at[slot_from], dst_ref=outs[b].at[slot_to], send_sem=send_sems.at[k * nb + b],
        recv_sem=recv_sems.at[k * nb + b], device_id=to, device_id_type=MESH)


def _scatter_start(ins, outs, send_sems, recv_sems):
    x, y, c, me, peers = _mesh_place()
    for k, (px, py) in enumerate(peers):
        for b in range(len(ins)):
            _scatter_copy(ins, outs, send_sems, recv_sems, k, b, 2 * px + py, me, (px, py, c)).start()


def _scatter_finish(ins, outs, send_sems, recv_sems):
    x, y, c, me, peers = _mesh_place()
    for k, (px, py) in enumerate(peers):
        for b in range(len(ins)):
            _scatter_copy(ins, outs, send_sems, recv_sems, k, b, me, 2 * px + py, (px, py, c)).wait_recv()
    for k, (px, py) in enumerate(peers):
        for b in range(len(ins)):
            _scatter_copy(ins, outs, send_sems, recv_sems, k, b, 2 * px + py, me, (px, py, c)).wait_send()


def _split_cores(bufs, name):
    nb = len(bufs)

    def body(*refs):
        ins = refs[:nb]
        theirs = refs[nb : 2 * nb]
        send_sems, recv_sems = refs[2 * nb :]
        x, y, c, _, _ = _mesh_place()
        cps = []
        for b in range(nb):
            hr = ins[b].shape[1] // 2
            cp = pltpu.make_async_remote_copy(
                src_ref=ins[b].at[:, pl.ds((1 - c) * hr, hr)], dst_ref=theirs[b], send_sem=send_sems.at[b],
                recv_sem=recv_sems.at[b], device_id=(x, y, 1 - c), device_id_type=MESH)
            cp.start()
            cps.append(cp)
        for cp in cps:
            cp.wait()

    theirs = pl.pallas_call(
        body,
        name=name,
        in_specs=[ANY] * nb,
        out_specs=[ANY] * nb,
        out_shape=[jax.ShapeDtypeStruct((b.shape[0], b.shape[1] // 2, b.shape[2]), b.dtype) for b in bufs],
        scratch_shapes=[pltpu.SemaphoreType.DMA((nb,)), pltpu.SemaphoreType.DMA((nb,))],
    )(*bufs)
    c = lax.axis_index("c")
    mine = [lax.dynamic_slice_in_dim(b, c * (b.shape[1] // 2), b.shape[1] // 2, axis=1) for b in bufs]
    return mine, theirs


def _join_cores(bufs, name):
    nb = len(bufs)

    def body(*refs):
        ins = refs[:nb]
        outs = refs[nb : 2 * nb]
        send_sems, recv_sems = refs[2 * nb :]
        x, y, c, _, _ = _mesh_place()
        cps = []
        for b in range(nb):
            hr = ins[b].shape[0]
            cp = pltpu.make_async_remote_copy(
                src_ref=ins[b], dst_ref=outs[b].at[pl.ds(c * hr, hr)], send_sem=send_sems.at[b], recv_sem=recv_sems.at[b],
                device_id=(x, y, 1 - c), device_id_type=MESH)
            cp.start()
            cps.append(cp)
        for b, cp in enumerate(cps):
            hr = ins[b].shape[0]
            cp.wait_send()
            pltpu.make_async_remote_copy(
                src_ref=ins[b], dst_ref=outs[b].at[pl.ds((1 - c) * hr, hr)], send_sem=send_sems.at[b],
                recv_sem=recv_sems.at[b], device_id=(x, y, 1 - c), device_id_type=MESH).wait_recv()

    outs = pl.pallas_call(
        body,
        name=name,
        in_specs=[ANY] * nb,
        out_specs=[ANY] * nb,
        out_shape=[jax.ShapeDtypeStruct((2 * b.shape[0], b.shape[1]), b.dtype) for b in bufs],
        scratch_shapes=[pltpu.SemaphoreType.DMA((nb,)), pltpu.SemaphoreType.DMA((nb,))],
    )(*bufs)
    c = lax.axis_index("c")
    return [lax.dynamic_update_slice(o, b, (c * b.shape[0], 0)) for o, b in zip(outs, bufs)]


ROW_ALIGN = 1024


def _pack(pieces, dtype):
    flat = []
    for p in pieces:
        f = p.reshape(-1).astype(dtype)
        pad = (-f.shape[0]) % LANES
        if pad:
            f = jnp.pad(f, (0, pad))
        flat.append(f)
    tot = sum(f.shape[0] for f in flat)
    pad = (-tot) % (ROW_ALIGN * LANES)
    if pad:
        flat.append(jnp.zeros((pad,), dtype))
    return jnp.concatenate(flat).reshape(-1, LANES)


def _unpack(buf, shapes):
    lead = buf.shape[:-2]
    flat = buf.reshape(lead + (-1,))
    out = []
    off = 0
    for shp in shapes:
        n = 1
        for d in shp:
            n *= d
        out.append(flat[..., off : off + n].reshape(lead + tuple(shp)))
        off += n + ((-n) % LANES)
    return out


PARAMS = (
    ("meta_tokens", 1, "small"), ("ssd_norm", 1, "small"), ("ssd_w_in", 2, "big"), ("ssd_conv_w", 2, "small"),
    ("ssd_conv_b", 1, "small"), ("ssd_dt_bias", None, "rep"), ("ssd_a_log", None, "rep"), ("ssd_d_skip", None, "rep"),
    ("ssd_gate_norm", 1, "small"), ("ssd_w_out", 1, "big"), ("kv_norm", None, "rep"), ("w_kv", 1, "big"),
    ("sb_norm", None, "rep"), ("sb_w_q", 1, "big"), ("sb_w_o", 1, "big"), ("ffn_norm", None, "rep"),
    ("ffn_w_up", 2, "big"), ("ffn_conv_w", 2, "small"), ("ffn_conv_b", None, "rep"), ("ffn_w_down", 1, "big"),
    ("final_norm", None, "rep"),
)


def _head_cols(vec):
    return jnp.pad(vec.reshape(G, 1, E), ((0, 0), (0, 0), (0, LANES - E)))


def _head_rows(vec):
    return jnp.pad(vec.reshape(G, E, 1), ((0, 0), (0, 8 - E), (0, 0)))


def kernel(x, meta_tokens, ssd_norm, ssd_w_in, ssd_conv_w, ssd_conv_b, ssd_dt_bias, ssd_a_log, ssd_d_skip, ssd_gate_norm, ssd_w_out, kv_norm, w_kv, sb_norm, sb_w_q, sb_w_o, ffn_norm, ffn_w_up, ffn_conv_w, ffn_conv_b, ffn_w_down, final_norm, loss_target, m_meta_tokens, m_ssd_norm, m_ssd_w_in, m_ssd_conv_w, m_ssd_conv_b, m_ssd_dt_bias, m_ssd_a_log, m_ssd_d_skip, m_ssd_gate_norm, m_ssd_w_out, m_kv_norm, m_w_kv, m_sb_norm, m_sb_w_q, m_sb_w_o, m_ffn_norm, m_ffn_w_up, m_ffn_conv_w, m_ffn_conv_b, m_ffn_w_down, m_final_norm, v_meta_tokens, v_ssd_norm, v_ssd_w_in, v_ssd_conv_w, v_ssd_conv_b, v_ssd_dt_bias, v_ssd_a_log, v_ssd_d_skip, v_ssd_gate_norm, v_ssd_w_out, v_kv_norm, v_w_kv, v_sb_norm, v_sb_w_q, v_sb_w_o, v_ffn_norm, v_ffn_w_up, v_ffn_conv_w, v_ffn_conv_b, v_ffn_w_down, v_final_norm):
    local = dict(meta_tokens=meta_tokens, ssd_norm=ssd_norm, ssd_w_in=ssd_w_in, ssd_conv_w=ssd_conv_w, ssd_conv_b=ssd_conv_b, ssd_dt_bias=ssd_dt_bias, ssd_a_log=ssd_a_log, ssd_d_skip=ssd_d_skip, ssd_gate_norm=ssd_gate_norm, ssd_w_out=ssd_w_out, kv_norm=kv_norm, w_kv=w_kv, sb_norm=sb_norm, sb_w_q=sb_w_q, sb_w_o=sb_w_o, ffn_norm=ffn_norm, ffn_w_up=ffn_w_up, ffn_conv_w=ffn_conv_w, ffn_conv_b=ffn_conv_b, ffn_w_down=ffn_w_down, final_norm=final_norm)
    mom_m = dict(meta_tokens=m_meta_tokens, ssd_norm=m_ssd_norm, ssd_w_in=m_ssd_w_in, ssd_conv_w=m_ssd_conv_w, ssd_conv_b=m_ssd_conv_b, ssd_dt_bias=m_ssd_dt_bias, ssd_a_log=m_ssd_a_log, ssd_d_skip=m_ssd_d_skip, ssd_gate_norm=m_ssd_gate_norm, ssd_w_out=m_ssd_w_out, kv_norm=m_kv_norm, w_kv=m_w_kv, sb_norm=m_sb_norm, sb_w_q=m_sb_w_q, sb_w_o=m_sb_w_o, ffn_norm=m_ffn_norm, ffn_w_up=m_ffn_w_up, ffn_conv_w=m_ffn_conv_w, ffn_conv_b=m_ffn_conv_b, ffn_w_down=m_ffn_w_down, final_norm=m_final_norm)
    mom_v = dict(meta_tokens=v_meta_tokens, ssd_norm=v_ssd_norm, ssd_w_in=v_ssd_w_in, ssd_conv_w=v_ssd_conv_w, ssd_conv_b=v_ssd_conv_b, ssd_dt_bias=v_ssd_dt_bias, ssd_a_log=v_ssd_a_log, ssd_d_skip=v_ssd_d_skip, ssd_gate_norm=v_ssd_gate_norm, ssd_w_out=v_ssd_w_out, kv_norm=v_kv_norm, w_kv=v_w_kv, sb_norm=v_sb_norm, sb_w_q=v_sb_w_q, sb_w_o=v_sb_w_o, ffn_norm=v_ffn_norm, ffn_w_up=v_ffn_w_up, ffn_conv_w=v_ffn_conv_w, ffn_conv_b=v_ffn_conv_b, ffn_w_down=v_ffn_w_down, final_norm=v_final_norm)

    big_names = [n for n, _, kind in PARAMS if kind == "big"]
    small_names = [n for n, _, kind in PARAMS if kind == "small"]
    rep_names = [n for n, _, kind in PARAMS if kind == "rep"]
    axis_of = {n: ax for n, ax, _ in PARAMS}

    def rows2(a):
        return a.reshape(-1, a.shape[-1])

    first_big, later_big = big_names[:1], big_names[1:]
    full = {}

    def assemble(names, bufs):
        for n, buf in zip(names, bufs):
            p = buf.reshape((NCHIP,) + local[n].shape)
            full[n] = jnp.concatenate([p[s] for s in range(NCHIP)], axis=axis_of[n])

    small_own = _pack([local[n] for n in small_names], F32)
    gathered = _gather_chips([rows2(local[n]).astype(BF16) for n in first_big] + [small_own], "gather_first")
    assemble(first_big, gathered[:-1])
    for n, p in zip(small_names, _unpack(gathered[-1], [local[n].shape for n in small_names])):
        full[n] = jnp.concatenate([p[s] for s in range(NCHIP)], axis=axis_of[n])
    for n in rep_names:
        full[n] = local[n]

    w_in = full["ssd_w_in"][0]
    w_z, w_xbc = w_in[:, :DI], w_in[:, DI : DI + CD]
    w_dt = jnp.pad(w_in[:, DI + CD :], ((0, 0), (0, LANES - H)))
    fcw, fcb = full["ffn_conv_w"], full["ffn_conv_b"]
    scw, scb = full["ssd_conv_w"][0], full["ssd_conv_b"]
    bias_c, bias_r = _head_cols(full["ssd_dt_bias"][0]), _head_rows(full["ssd_dt_bias"][0])
    alog_c, alog_r = _head_cols(full["ssd_a_log"][0]), _head_rows(full["ssd_a_log"][0])
    dskip_c = _head_cols(full["ssd_d_skip"][0])
    kvn = full["kv_norm"].reshape(1, D)
    fin = full["final_norm"].reshape(1, D)

    h0 = jnp.concatenate([jnp.zeros((PF, D), F32), full["meta_tokens"], x[0]], axis=0)
    (u0,) = _rms_fwd(h0, [full["ssd_norm"]], "ssd_norm_fwd")
    z = _mm(u0, w_z, name="ssd_in_z")
    xr = _mm(u0, w_xbc, name="ssd_in_xbc")
    dt_raw = _mm(u0, w_dt, name="ssd_in_dt")
    xbc, xpre = _ssd_conv_fwd(xr, scw, scb, "ssd_conv_fwd")
    dth = dt_raw[:, :H].reshape(LP, G, E)
    dtc = jnp.pad(jnp.transpose(dth, (1, 0, 2)), ((0, 0), (0, 0), (0, LANES - E)))
    dtr = jnp.pad(jnp.transpose(dth, (1, 2, 0)), ((0, 0), (0, 8 - E), (0, 0)))
    later_own = [rows2(local[n]).astype(BF16) for n in later_big]
    y, states, later_all = _ssd_fwd(xbc, dtc, dtr, bias_c, bias_r, alog_c, alog_r, dskip_c, later_own, "ssd_scan_fwd")
    assemble(later_big, later_all)
    w_out = full["ssd_w_out"][0]
    wkv = full["w_kv"]
    w_q = full["sb_w_q"][0]
    w_o = full["sb_w_o"][0]
    w_up_g = [full["ffn_w_up"][l][:, :DFF] for l in range(2)]
    w_up_v = [full["ffn_w_up"][l][:, DFF:] for l in range(2)]
    w_down = [full["ffn_w_down"][l] for l in range(2)]
    hgn = _gate_fwd(y, z, full["ssd_gate_norm"], "ssd_gate_fwd")
    h1 = _mm(hgn, w_out, add=h0, mask_rows=True, name="ssd_out")

    def ffn_fwd(h, l, tag):
        (u,) = _rms_fwd(h, [full["ffn_norm"][l : l + 1]], f"ffn{tag}_norm_fwd")
        hg = _mm(u, w_up_g[l], name=f"ffn{tag}_up_g")
        hv = _mm(u, w_up_v[l], name=f"ffn{tag}_up_v")
        act, gpre, vpre = _ffn_act_fwd(hg, hv, fcw[l][:, :DFF], fcw[l][:, DFF:], fcb[l : l + 1, :DFF], fcb[l : l + 1, DFF:], f"ffn{tag}_act_fwd")
        hn = _mm(act, w_down[l], add=h, mask_rows=True, name=f"ffn{tag}_down")
        return hn, (u, hg, hv, act, gpre, vpre)

    h2, ffn0 = ffn_fwd(h1, 0, "0")
    ukv, uq = _rms_fwd(h2, [kvn, full["sb_norm"]], "attn_norm_fwd")
    kk = _mm(ukv, wkv[:, :D], out_dtype=BF16, name="attn_k")
    vv = _mm(ukv, wkv[:, D:], out_dtype=BF16, name="attn_v")
    qq = _mm(uq, w_q, out_dtype=BF16, scale=64.0**-0.5, name="attn_q")
    o = _attn_fwd(qq, kk, vv, "attn_fwd")
    h3 = _mm(o, w_o, add=h2, mask_rows=True, name="attn_out")
    h4, ffn1 = ffn_fwd(h3, 1, "1")
    dh, g_final, loss_rows = _loss_head(h4, fin, loss_target[0], "loss_head")
    loss = lax.psum(0.5 / D * jnp.sum(loss_rows), ("x", "y", "c"))

    grads = {"final_norm": g_final.reshape(D)}

    def ffn_bwd(dh, h, l, saved, tag):
        u, hg, hv, act, gpre, vpre = saved
        da = _mm(dh, w_down[l], tb=True, name=f"ffn{tag}_down_dx")
        gw_down = _mm(act, dh, ta=True, out_dtype=BF16, name=f"ffn{tag}_down_dw")
        dhg, dhv, dwg, dwv, dbg, dbv = _ffn_act_bwd(hg, hv, gpre, vpre, da, fcw[l][:, :DFF], fcw[l][:, DFF:], f"ffn{tag}_act_bwd")
        gw_up = jnp.concatenate([_mm(u, dhg, ta=True, out_dtype=BF16, name=f"ffn{tag}_up_g_dw"), _mm(u, dhv, ta=True, out_dtype=BF16, name=f"ffn{tag}_up_v_dw")], axis=1)
        du = _mm(dhg, w_up_g[l], tb=True, name=f"ffn{tag}_up_g_dx")
        du = _mm(dhv, w_up_v[l], tb=True, add=du, name=f"ffn{tag}_up_v_dx")
        dh_new, (gn,) = _rms_bwd(dh, h, [du], [full["ffn_norm"][l : l + 1]], f"ffn{tag}_norm_bwd")
        return dh_new, gw_down, gw_up, jnp.concatenate([dwg, dwv], axis=1), jnp.concatenate([dbg, dbv], axis=1), gn

    dh, gd1, gu1, gcw1, gcb1, gn1 = ffn_bwd(dh, h3, 1, ffn1, "1")
    do = _mm(dh, w_o, tb=True, name="attn_out_dx")
    grads["sb_w_o"] = _mm(o, dh, ta=True, out_dtype=BF16, name="attn_out_dw")[None]
    dq, dk, dv = _attn_bwd(qq, kk, vv, o, do, "attn_bwd")
    grads["sb_w_q"] = _mm(uq, dq, ta=True, out_dtype=BF16, scale=64.0**-0.5, name="attn_q_dw")[None]
    grads["w_kv"] = jnp.concatenate([_mm(ukv, dk, ta=True, out_dtype=BF16, name="attn_k_dw"), _mm(ukv, dv, ta=True, out_dtype=BF16, name="attn_v_dw")], axis=1)
    duq = _mm(dq, w_q, tb=True, scale=64.0**-0.5, name="attn_q_dx")
    dukv = _mm(dk, wkv[:, :D], tb=True, name="attn_k_dx")
    dukv = _mm(dv, wkv[:, D:], tb=True, add=dukv, name="attn_v_dx")
    dh, (g_kvn, g_sbn) = _rms_bwd(dh, h2, [dukv, duq], [kvn, full["sb_norm"]], "attn_norm_bwd")
    grads["kv_norm"] = g_kvn.reshape(D)
    grads["sb_norm"] = g_sbn
    dh, gd0, gu0, gcw0, gcb0, gn0 = ffn_bwd(dh, h1, 0, ffn0, "0")
    grads["ffn_w_down"] = jnp.stack([gd0, gd1])
    grads["ffn_w_up"] = jnp.stack([gu0, gu1])
    grads["ffn_conv_w"] = jnp.stack([gcw0, gcw1])
    grads["ffn_conv_b"] = jnp.concatenate([gcb0, gcb1], axis=0)
    grads["ffn_norm"] = jnp.concatenate([gn0, gn1], axis=0)
    dhgn = _mm(dh, w_out, tb=True, name="ssd_out_dx")
    grads["ssd_w_out"] = _mm(hgn, dh, ta=True, out_dtype=BF16, name="ssd_out_dw")[None]
    dy, dz, g_gate = _gate_bwd(dhgn, y, z, full["ssd_gate_norm"], "ssd_gate_bwd")
    grads["ssd_gate_norm"] = g_gate
    def slots(n):
        return jnp.stack([rows2(p) for p in jnp.split(grads[n], NCHIP, axis=axis_of[n])])

    mine, theirs = _split_cores([slots(n) for n in later_big], "split_cores_a")
    pair = [_add2(a, b, f"pair_sum_a{i}") for i, (a, b) in enumerate(zip(mine, theirs))]
    dxs, dB, dC, ddt_raw, g_bias, g_alog, g_dskip, got_a = _ssd_bwd(
        xbc, dy, states, dtc, dtr, bias_c, bias_r, alog_c, alog_r, dskip_c, pair, "ssd_scan_bwd")
    grads["ssd_dt_bias"] = g_bias[:, 0, :E].reshape(1, H)
    grads["ssd_a_log"] = g_alog[:, 0, :E].reshape(1, H)
    grads["ssd_d_skip"] = g_dskip[:, 0, :E].reshape(1, H)
    dxr, g_scw, g_scb = _ssd_conv_bwd(xr, xpre, jnp.concatenate([dxs, dB, dC], axis=1), scw, "ssd_conv_bwd")
    grads["ssd_conv_w"] = g_scw[None]
    grads["ssd_conv_b"] = g_scb
    ddt = jnp.pad(jnp.transpose(ddt_raw[:, :, :E], (1, 0, 2)).reshape(LP, H), ((0, 0), (0, LANES - H)))
    grads["ssd_w_in"] = jnp.concatenate(
        [_mm(u0, dz, ta=True, out_dtype=BF16, name="ssd_in_z_dw"), _mm(u0, dxr, ta=True, out_dtype=BF16, name="ssd_in_xbc_dw"), _mm(u0, ddt, ta=True, out_dtype=BF16, name="ssd_in_dt_dw")[:, :H]], axis=1)[None]
    du = _mm(dz, w_z, tb=True, name="ssd_in_z_dx")
    du = _mm(dxr, w_xbc, tb=True, add=du, name="ssd_in_xbc_dx")
    du = _mm(ddt, w_dt, tb=True, add=du, name="ssd_in_dt_dx")
    dh, (g_ssdn,) = _rms_bwd(dh, h0, [du], [full["ssd_norm"]], "ssd_norm_bwd")
    grads["ssd_norm"] = g_ssdn
    grads["meta_tokens"] = dh[PF : PF + N_META]
    grad_x = dh[PF + N_META :][None]

    def shard_pieces(names, s):
        out = []
        for n in names:
            ax = axis_of[n]
            out.append(grads[n] if ax is None else jnp.split(grads[n], NCHIP, axis=ax)[s])
        return out

    bufs = [slots(n) for n in first_big]
    bufs.append(jnp.stack([_pack(shard_pieces(small_names + rep_names, s), F32) for s in range(NCHIP)]))
    mine, theirs = _split_cores(bufs, "split_cores_b")
    pair = [_add2(a, b, f"pair_sum_b{i}") for i, (a, b) in enumerate(zip(mine, theirs))]
    got_b = _scatter_chips(pair, "scatter_grads")
    got = got_b[:-1] + got_a + got_b[-1:]
    sums = [_sum4(b, f"sum_chips_{i}") for i, b in enumerate(got)]
    gsum = _join_cores(sums, "join_cores")

    def rows(a):
        f = a.reshape(-1)
        pad = (-f.shape[0]) % LANES
        if pad:
            f = jnp.pad(f, (0, pad))
        return f.reshape(-1, LANES)

    order = [n for n, _, _ in PARAMS]
    res = {}
    for n, g2 in zip(big_names, gsum[:-1]):
        outs = _adamw(rows2(local[n]), g2, rows2(mom_m[n]), rows2(mom_v[n]), f"adamw_{n}")
        res[n] = [o_.reshape(local[n].shape) for o_ in outs]
    rest = small_names + rep_names
    for n, g1 in zip(rest, _unpack(gsum[-1], [local[n].shape for n in rest])):
        shp = local[n].shape
        cnt = 1
        for d in shp:
            cnt *= d
        outs = _adamw(rows(local[n]), rows(g1), rows(mom_m[n]), rows(mom_v[n]), f"adamw_{n}")
        res[n] = [o_.reshape(-1)[:cnt].reshape(shp) for o_ in outs]
    return (loss, grad_x, *[res[n][0] for n in order], *[res[n][1] for n in order], *[res[n][2] for n in order], *[res[n][3] for n in order])
```
